```python
import jax, jax.numpy as jnp
from jax import lax
import numpy as np

D_MODEL = 1024
BATCH = 4
SEQ = 4096
DEPTH = 1

D_MIX = D_MODEL
D_ATTN = D_MIX // 2
D_REC = D_MIX - D_ATTN
HEAD_DIM = 64
N_HEADS = D_ATTN // HEAD_DIM
DILATED_PATTERNS = ((128, 1), (512, 4), (2048, 16))
REC_BLOCKS = 8
REC_BLOCK_W = D_REC // REC_BLOCKS
CONV_WIDTH = 4
RG_C = 8.0
N_EXPERTS = 32
TOP_K = 4
D_EXPERT = D_MODEL
SWIGLU_LIMIT = 7.0
SWIGLU_ALPHA = 1.702
EXPERT_BLOCK = 128
NORM_EPS = 1e-6
NEG_INF = -1e30
IN_COLS = 3 * D_ATTN + 2 * D_REC

kernel_name = "hymba_rglru_dilated_attn_moe_block"


def _rmsnorm(x, g):
    xf = x.astype(jnp.float32)
    y = xf * lax.rsqrt(jnp.mean(xf * xf, axis=-1, keepdims=True) + NORM_EPS)
    return (y * g.astype(jnp.float32)).astype(x.dtype)


def _modulate(h, shift, scale):
    return h * (1.0 + scale[:, None, :]) + shift[:, None, :]


def _dilated_branch(q, k, v, window, dilation):
    bsz, s, nh, e = q.shape
    span = window // dilation
    sub_len = s // dilation
    nb = -(-sub_len // span)
    lp = nb * span

    def to_sub(t):
        t = t.reshape(bsz, sub_len, dilation, nh, e).transpose(0, 2, 3, 1, 4)
        t = jnp.pad(t, ((0, 0), (0, 0), (0, 0), (0, lp - sub_len), (0, 0)))
        return t.reshape(bsz, dilation, nh, nb, span, e)

    def with_prev(t):
        prev = jnp.pad(t[:, :, :, :-1], ((0, 0), (0, 0), (0, 0), (1, 0), (0, 0), (0, 0)))
        return jnp.concatenate([prev, t], axis=4)

    qb = to_sub(q)
    kc = with_prev(to_sub(k))
    vc = with_prev(to_sub(v))
    scores = jnp.einsum('bdhnqe,bdhnke->bdhnqk', qb, kc) * (e ** -0.5)
    blk = jnp.arange(nb)[:, None, None] * span
    qpos = blk + jnp.arange(span)[None, :, None]
    kpos = blk - span + jnp.arange(2 * span)[None, None, :]
    rel = qpos - kpos
    mask = (rel >= 0) & (rel <= span) & (kpos >= 0)
    scores = jnp.where(mask, scores, NEG_INF)
    lse = jax.nn.logsumexp(scores, axis=-1)
    p = jnp.exp(scores - lse[..., None])
    o = jnp.einsum('bdhnqk,bdhnke->bdhnqe', p, vc)
    o = o.reshape(bsz, dilation, nh, lp, e)[:, :, :, :sub_len]
    o = o.transpose(0, 3, 1, 2, 4).reshape(bsz, s, nh, e)
    lse = lse.reshape(bsz, dilation, nh, lp)[:, :, :, :sub_len]
    lse = lse.transpose(0, 3, 1, 2).reshape(bsz, s, nh)
    return o, lse


def _dilated_attention(q, k, v):
    dt = q.dtype
    qf, kf, vf = q.astype(jnp.float32), k.astype(jnp.float32), v.astype(jnp.float32)
    outs, lses = [], []
    for window, dilation in DILATED_PATTERNS:
        o, lse = _dilated_branch(qf, kf, vf, window, dilation)
        outs.append(o)
        lses.append(lse)
    w = jax.nn.softmax(jnp.stack(lses, axis=0), axis=0)
    o = jnp.sum(w[..., None] * jnp.stack(outs, axis=0), axis=0)
    return o.astype(dt)


def _rglru_branch(xr, gr, conv_w, conv_b, w_a, b_a, w_x, b_x, lam):
    bsz, s, ch = xr.shape
    xc = lax.conv_general_dilated(
        xr, conv_w[:, None, :].astype(xr.dtype), window_strides=(1,),
        padding=[(CONV_WIDTH - 1, 0)], dimension_numbers=('NWC', 'WIO', 'NWC'),
        feature_group_count=ch) + conv_b
    xf = xc.astype(jnp.float32)
    xh = xf.reshape(bsz, s, REC_BLOCKS, REC_BLOCK_W)
    r = jax.nn.sigmoid(jnp.einsum('bsnh,nhk->bsnk', xh, w_a.astype(jnp.float32)).reshape(bsz, s, ch) + b_a)
    i = jax.nn.sigmoid(jnp.einsum('bsnh,nhk->bsnk', xh, w_x.astype(jnp.float32)).reshape(bsz, s, ch) + b_x)
    log_a = -RG_C * r * jax.nn.softplus(-lam.astype(jnp.float32))
    a = jnp.exp(log_a)
    mult = jnp.sqrt(-jnp.expm1(2.0 * log_a))
    u = mult * (i * xf)

    def combine(left, right):
        a1, b1 = left
        a2, b2 = right
        return a1 * a2, a2 * b1 + b2

    _, h = lax.associative_scan(combine, (a, u), axis=1)
    out = h * jax.nn.gelu(gr.astype(jnp.float32), approximate=True)
    return out.astype(xr.dtype)


def _moe(h, w_router, b_router, w_gate_up, b_gate_up, w_down, b_down):
    bsz, s, d = h.shape
    t = bsz * s
    ht = h.reshape(t, d)
    logits = (ht @ w_router + b_router).astype(jnp.float32)
    top_val, top_idx = lax.top_k(logits, TOP_K)
    gates = jax.nn.softmax(top_val, axis=-1)
    n_assign = t * TOP_K
    e_flat = top_idx.reshape(n_assign)
    tok_flat = jnp.arange(n_assign, dtype=jnp.int32) // TOP_K
    g_flat = gates.reshape(n_assign)
    order = jnp.argsort(e_flat)
    e_sorted = e_flat[order]
    tok_sorted = tok_flat[order]
    g_sorted = g_flat[order]
    counts = jnp.bincount(e_flat, length=N_EXPERTS)
    padded = ((counts + EXPERT_BLOCK - 1) // EXPERT_BLOCK) * EXPERT_BLOCK
    start = jnp.cumsum(counts) - counts
    pend = jnp.cumsum(padded)
    pstart = pend - padded
    rank = jnp.arange(n_assign, dtype=jnp.int32) - start[e_sorted]
    dest = pstart[e_sorted] + rank
    buf_len = (-(-n_assign // EXPERT_BLOCK) + N_EXPERTS) * EXPERT_BLOCK
    n_blocks = buf_len // EXPERT_BLOCK
    buf_tok = jnp.full((buf_len,), t, dtype=jnp.int32).at[dest].set(tok_sorted)
    ht_pad = jnp.concatenate([ht, jnp.zeros((1, d), ht.dtype)], axis=0)
    x_buf = ht_pad[buf_tok].reshape(n_blocks, EXPERT_BLOCK, d)
    blk_start = jnp.arange(n_blocks, dtype=pend.dtype) * EXPERT_BLOCK
    blk_exp = jnp.minimum(jnp.searchsorted(pend, blk_start, side='right'), N_EXPERTS - 1)

    def expert_block(args):
        xb, e = args
        gu = xb @ w_gate_up[e] + b_gate_up[e]
        gate, up = gu[:, :D_EXPERT], gu[:, D_EXPERT:]
        gate = jnp.minimum(gate, SWIGLU_LIMIT)
        up = jnp.clip(up, -SWIGLU_LIMIT, SWIGLU_LIMIT)
        glu = gate * jax.nn.sigmoid(gate * SWIGLU_ALPHA)
        return ((up + 1.0) * glu) @ w_down[e] + b_down[e]

    y_buf = lax.map(expert_block, (x_buf, blk_exp)).reshape(buf_len, d)
    y = jnp.zeros((t, d), jnp.float32).at[tok_sorted].add(
        y_buf[dest].astype(jnp.float32) * g_sorted[:, None])
    return y.reshape(bsz, s, d).astype(h.dtype)


def setup_inputs(seed: int = 0) -> dict:
    key = jax.random.key(seed)
    ks = jax.random.split(key, 24)
    nrm = jax.random.normal
    D = D_MODEL
    a0 = jax.random.uniform(ks[12], (DEPTH, D_REC), minval=0.9, maxval=0.999)
    sig = a0 ** (1.0 / RG_C)
    lam = jnp.log(sig) - jnp.log1p(-sig)
    return {
        'x': nrm(ks[0], (BATCH, SEQ, D), jnp.float32),
        'c': nrm(ks[1], (BATCH, D), jnp.float32),
        'w_ada': nrm(ks[2], (DEPTH, D, 6 * D)) * (0.5 * D ** -0.5),
        'b_ada': 0.01 * nrm(ks[3], (DEPTH, 6 * D)),
        'g_mix': 1.0 + 0.05 * nrm(ks[4], (DEPTH, D)),
        'w_in': nrm(ks[5], (DEPTH, D, IN_COLS)) * D ** -0.5,
        'conv_w': nrm(ks[6], (DEPTH, CONV_WIDTH, D_REC)) * CONV_WIDTH ** -0.5,
        'conv_b': 0.01 * nrm(ks[7], (DEPTH, D_REC)),
        'w_rg_a': nrm(ks[8], (DEPTH, REC_BLOCKS, REC_BLOCK_W, REC_BLOCK_W)) * REC_BLOCK_W ** -0.5,
        'b_rg_a': 0.01 * nrm(ks[9], (DEPTH, D_REC)),
        'w_rg_x': nrm(ks[10], (DEPTH, REC_BLOCKS, REC_BLOCK_W, REC_BLOCK_W)) * REC_BLOCK_W ** -0.5,
        'b_rg_x': 0.01 * nrm(ks[11], (DEPTH, D_REC)),
        'lam': lam,
        'w_out': nrm(ks[13], (DEPTH, D_MIX, D)) * D_MIX ** -0.5,
        'g_ffn': 1.0 + 0.05 * nrm(ks[14], (DEPTH, D)),
        'w_router': nrm(ks[15], (DEPTH, D, N_EXPERTS)) * D ** -0.5,
        'b_router': 0.01 * nrm(ks[16], (DEPTH, N_EXPERTS)),
        'w_gate_up': nrm(ks[17], (DEPTH, N_EXPERTS, D, 2 * D_EXPERT)) * D ** -0.5,
        'b_gate_up': 0.01 * nrm(ks[18], (DEPTH, N_EXPERTS, 2 * D_EXPERT)),
        'w_down': nrm(ks[19], (DEPTH, N_EXPERTS, D_EXPERT, D)) * D_EXPERT ** -0.5,
        'b_down': 0.01 * nrm(ks[20], (DEPTH, N_EXPERTS, D)),
        'g_final': 1.0 + 0.05 * nrm(ks[21], (D,)),
    }


def reference(x, c, w_ada, b_ada, g_mix, w_in, conv_w, conv_b, w_rg_a, b_rg_a, w_rg_x, b_rg_x,
              lam, w_out, g_ffn, w_router, b_router, w_gate_up, b_gate_up, w_down, b_down, g_final):
    bsz, s, _ = x.shape
    for l in range(DEPTH):
        mod = c @ w_ada[l] + b_ada[l]
        shift_m, scale_m, gate_m, shift_f, scale_f, gate_f = jnp.split(mod, 6, axis=-1)
        h = _modulate(_rmsnorm(x, g_mix[l]), shift_m, scale_m)
        z = h @ w_in[l]
        q, k, v, xr, gr = jnp.split(
            z, [D_ATTN, 2 * D_ATTN, 3 * D_ATTN, 3 * D_ATTN + D_REC], axis=-1)
        hd = (bsz, s, N_HEADS, HEAD_DIM)
        attn = _dilated_attention(q.reshape(hd), k.reshape(hd), v.reshape(hd)).reshape(bsz, s, D_ATTN)
        rec = _rglru_branch(xr, gr, conv_w[l], conv_b[l], w_rg_a[l], b_rg_a[l],
                            w_rg_x[l], b_rg_x[l], lam[l])
        y = jnp.concatenate([attn, rec], axis=-1) @ w_out[l]
        x = x + gate_m[:, None, :] * y
        h = _modulate(_rmsnorm(x, g_ffn[l]), shift_f, scale_f)
        x = x + gate_f[:, None, :] * _moe(h, w_router[l], b_router[l], w_gate_up[l],
                                          b_gate_up[l], w_down[l], b_down[l])
    return _rmsnorm(x, g_final)
```

```python
import functools

import jax
import jax.numpy as jnp
from jax import lax
from jax.experimental import pallas as pl
from jax.experimental.pallas import tpu as pltpu

F32 = jnp.float32
BF16 = jnp.bfloat16
I32 = jnp.int32

HEAD_DIM = 64
DILATED_PATTERNS = ((128, 1), (512, 4), (2048, 16))
REC_BLOCKS = 8
CONV_WIDTH = 4
RG_C = 8.0
N_EXPERTS = 32
TOP_K = 4
SWIGLU_LIMIT = 7.0
SWIGLU_ALPHA = 1.702
NORM_EPS = 1e-6
NEG_INF = -1e30

LANES = 128
SUBLANES = 8
ROW_CHUNKS = 8
VMEM_LIMIT = 48 * 1024 * 1024

INPROJ_ROWS = 512
RGLRU_ROWS = 512
OUTPROJ_ROWS = 256
ATTN_BLOCK = 128
EXPERT_ROWS = 256
DISPATCH_ROWS = 256
COMBINE_ROWS = 256


def _params(sem, vmem=VMEM_LIMIT):
    return pltpu.CompilerParams(dimension_semantics=sem, vmem_limit_bytes=vmem)


def _ada_kernel(c_ref, w_ref, b_ref, o_ref):
    o_ref[...] = jnp.dot(c_ref[...], w_ref[...], precision=lax.Precision.HIGHEST,
                         preferred_element_type=F32) + b_ref[...]


def _ada(c, w, b):
    bsz, d = c.shape
    n = w.shape[1]
    return pl.pallas_call(
        _ada_kernel,
        grid=(n // d,),
        in_specs=[pl.BlockSpec((bsz, d), lambda j: (0, 0)),
                  pl.BlockSpec((d, d), lambda j: (0, j)),
                  pl.BlockSpec((1, d), lambda j: (0, j))],
        out_specs=pl.BlockSpec((bsz, d), lambda j: (0, j)),
        out_shape=jax.ShapeDtypeStruct((bsz, n), F32),
        compiler_params=_params(("arbitrary",)),
        name="ada",
    )(c, w, b.reshape(1, n))


def _rms_modulate(x, g, shift, scale):
    y = x * lax.rsqrt(jnp.mean(x * x, axis=-1, keepdims=True) + NORM_EPS)
    return (y * g) * (1.0 + scale) + shift


def _inproj_kernel(x_ref, g_ref, sh_ref, sc_ref, w_ref, q_ref, k_ref, v_ref, xr_ref, gr_ref):
    h = _rms_modulate(x_ref[...], g_ref[...], sh_ref[...], sc_ref[...]).astype(BF16)
    outs = (q_ref, k_ref, v_ref, xr_ref, gr_ref)
    width = q_ref.shape[-1]
    for j, o_ref in enumerate(outs):
        z = jnp.dot(h, w_ref[:, j * width:(j + 1) * width], preferred_element_type=F32)
        o_ref[...] = z.astype(o_ref.dtype)


def _inproj(x, g, shift, scale, w_bf):
    bsz, s, d = x.shape
    n = w_bf.shape[1]
    width = n // 5
    tm = INPROJ_ROWS
    row = pl.BlockSpec((None, 1, d), lambda b, i: (b, 0, 0))
    out_blk = pl.BlockSpec((None, tm, width), lambda b, i: (b, i, 0))
    shp = lambda dt: jax.ShapeDtypeStruct((bsz, s, width), dt)
    return pl.pallas_call(
        _inproj_kernel,
        grid=(bsz, s // tm),
        in_specs=[pl.BlockSpec((None, tm, d), lambda b, i: (b, i, 0)),
                  pl.BlockSpec((1, d), lambda b, i: (0, 0)),
                  row, row,
                  pl.BlockSpec((d, n), lambda b, i: (0, 0))],
        out_specs=[out_blk] * 5,
        out_shape=[shp(BF16), shp(BF16), shp(BF16), shp(F32), shp(F32)],
        compiler_params=_params(("arbitrary", "arbitrary")),
        name="inproj",
    )(x, g.reshape(1, d), shift, scale, w_bf)


def _gelu_tanh(x):
    return 0.5 * x * (1.0 + jnp.tanh(0.7978845608028654 * (x + 0.044715 * (x * x * x))))


def _rglru_kernel(xr_ref, gr_ref, cw_ref, cb_ref, wa_ref, ba_ref, wx_ref, bx_ref, lam_ref,
                  o_ref, xe_ref, a_ref, u_ref, tail_ref, h_ref):
    ts, ch = xr_ref.shape
    pad = SUBLANES

    @pl.when(pl.program_id(1) == 0)
    def _():
        tail_ref[...] = jnp.zeros_like(tail_ref)
        h_ref[...] = jnp.zeros_like(h_ref)

    x = xr_ref[...]
    xe_ref[0:pad, :] = tail_ref[...]
    xe_ref[pad:pad + ts, :] = x
    tail_ref[...] = x[ts - pad:ts, :]
    xc = cb_ref[...] + jnp.zeros((ts, ch), F32)
    for j in range(CONV_WIDTH):
        off = pad - (CONV_WIDTH - 1) + j
        xc = xc + cw_ref[j:j + 1, :] * xe_ref[off:off + ts, :]

    xb = xc.astype(BF16)
    r = jax.nn.sigmoid(jnp.dot(xb, wa_ref[...], preferred_element_type=F32) + ba_ref[...])
    i = jax.nn.sigmoid(jnp.dot(xb, wx_ref[...], preferred_element_type=F32) + bx_ref[...])
    nl = -lam_ref[...]
    softplus = jnp.maximum(nl, 0.0) + jnp.log(1.0 + jnp.exp(-jnp.abs(nl)))
    log_a = (-RG_C) * r * softplus
    a = jnp.exp(log_a)
    mult = jnp.sqrt(1.0 - jnp.exp(2.0 * log_a))
    u = mult * (i * xc)

    ng = ts // SUBLANES
    a3 = a.reshape(ng, SUBLANES, ch)
    u3 = u.reshape(ng, SUBLANES, ch)
    rid = lax.broadcasted_iota(I32, (ng, SUBLANES, ch), 1)
    for sft in (1, 2, 4):
        a_s = pltpu.roll(a3, sft, 1)
        u_s = pltpu.roll(u3, sft, 1)
        keep = rid >= sft
        u3 = jnp.where(keep, a3 * u_s + u3, u3)
        a3 = jnp.where(keep, a3 * a_s, a3)
    a_ref[...] = a3.reshape(ts, ch)
    u_ref[...] = u3.reshape(ts, ch)

    inner = 8

    def body(gi, hprev):
        for jj in range(inner):
            base = pl.multiple_of((gi * inner + jj) * SUBLANES, SUBLANES)
            hcur = u_ref[pl.ds(base, SUBLANES), :] + a_ref[pl.ds(base, SUBLANES), :] * hprev
            u_ref[pl.ds(base, SUBLANES), :] = hcur
            hprev = hcur[SUBLANES - 1:SUBLANES, :]
        return hprev

    h_ref[...] = lax.fori_loop(0, ng // inner, body, h_ref[...])
    o_ref[...] = (u_ref[...] * _gelu_tanh(gr_ref[...])).astype(o_ref.dtype)


def _block_diag(w):
    nb, hin, hout = w.shape
    eye = jnp.eye(nb, dtype=w.dtype)
    return (eye[:, None, :, None] * w[:, :, None, :]).reshape(nb * hin, nb * hout)


def _rglru(xr, gr, conv_w, conv_b, w_a, b_a, w_x, b_x, lam):
    bsz, s, ch = xr.shape
    ts = RGLRU_ROWS
    blk = pl.BlockSpec((None, ts, ch), lambda b, t: (b, t, 0))
    full = lambda r, c: pl.BlockSpec((r, c), lambda b, t: (0, 0))
    return pl.pallas_call(
        _rglru_kernel,
        grid=(bsz, s // ts),
        in_specs=[blk, blk, full(CONV_WIDTH, ch), full(1, ch), full(ch, ch), full(1, ch),
                  full(ch, ch), full(1, ch), full(1, ch)],
        out_specs=blk,
        out_shape=jax.ShapeDtypeStruct((bsz, s, ch), BF16),
        scratch_shapes=[pltpu.VMEM((ts + SUBLANES, ch), F32),
                        pltpu.VMEM((ts, ch), F32),
                        pltpu.VMEM((ts, ch), F32),
                        pltpu.VMEM((SUBLANES, ch), F32),
                        pltpu.VMEM((1, ch), F32)],
        compiler_params=_params(("arbitrary", "arbitrary")),
        name="rglru",
    )(xr, gr, conv_w, conv_b.reshape(1, ch), _block_diag(w_a).astype(BF16), b_a.reshape(1, ch),
      _block_diag(w_x).astype(BF16), b_x.reshape(1, ch), lam.reshape(1, ch))


def _attn_kernel(q_ref, k_ref, v_ref, o_ref, qf_ref, kf_ref, vf_ref, op_ref, lp_ref):
    s = q_ref.shape[0]
    blk = ATTN_BLOCK
    qf_ref[...] = q_ref[...].astype(F32) * (HEAD_DIM ** -0.5)
    kf_ref[...] = k_ref[...].astype(F32)
    vf_ref[...] = v_ref[...].astype(F32)

    lane = lax.broadcasted_iota(I32, (blk, LANES), 1)
    head_lanes = (lane < HEAD_DIM, lane >= HEAD_DIM)
    row = lax.broadcasted_iota(I32, (blk, blk), 0)
    col = lax.broadcasted_iota(I32, (blk, blk), 1)
    cur_ok = col <= row
    prev_ok = col >= row
    dims = (((1,), (1,)), ((), ()))

    for p, (window, dil) in enumerate(DILATED_PATTERNS):
        assert window // dil == blk
        per_res = (s // dil) // blk

        def rows(start, dil=dil):
            if dil == 1:
                return pl.ds(pl.multiple_of(start, blk), blk)
            return pl.ds(start, blk, stride=dil)

        def body(j, carry, p=p, dil=dil, per_res=per_res, rows=rows):
            res = j // per_res
            nblk = j % per_res
            start = res + nblk * (dil * blk)
            has_prev = nblk > 0
            pstart = jnp.where(has_prev, start - dil * blk, start)
            qb = qf_ref[rows(start), :]
            kc = kf_ref[rows(start), :].astype(BF16)
            kp = kf_ref[rows(pstart), :].astype(BF16)
            vc = vf_ref[rows(start), :]
            vp = vf_ref[rows(pstart), :]
            no_prev = jnp.where(has_prev, 0.0, NEG_INF)
            o_acc = jnp.zeros((blk, LANES), F32)
            lse_t = jnp.zeros((blk, LANES), F32)
            for hl in head_lanes:
                qh = jnp.where(hl, qb, 0.0).astype(BF16)
                s_c = jnp.where(cur_ok, lax.dot_general(qh, kc, dims, preferred_element_type=F32), NEG_INF)
                s_p = jnp.where(prev_ok, lax.dot_general(qh, kp, dims, preferred_element_type=F32),
                                NEG_INF) + no_prev
                m = jnp.maximum(jnp.max(s_c, axis=-1, keepdims=True), jnp.max(s_p, axis=-1, keepdims=True))
                p_c = jnp.exp(s_c - m)
                p_p = jnp.exp(s_p - m)
                l = jnp.sum(p_c, axis=-1, keepdims=True) + jnp.sum(p_p, axis=-1, keepdims=True)
                vch = jnp.where(hl, vc, 0.0).astype(BF16)
                vph = jnp.where(hl, vp, 0.0).astype(BF16)
                o_h = (jnp.dot(p_c.astype(BF16), vch, preferred_element_type=F32)
                       + jnp.dot(p_p.astype(BF16), vph, preferred_element_type=F32))
                o_acc = o_acc + o_h / l
                lse_t = jnp.where(hl, m + jnp.log(l), lse_t)
            op_ref[p, rows(start), :] = o_acc
            lp_ref[p, rows(start), :] = lse_t
            return carry

        lax.fori_loop(0, s // blk, body, 0)

    chunk = 512
    for c0 in range(0, s, chunk):
        sl = pl.ds(c0, chunk)
        l0, l1, l2 = lp_ref[0, sl, :], lp_ref[1, sl, :], lp_ref[2, sl, :]
        m = jnp.maximum(jnp.maximum(l0, l1), l2)
        w0, w1, w2 = jnp.exp(l0 - m), jnp.exp(l1 - m), jnp.exp(l2 - m)
        num = w0 * op_ref[0, sl, :] + w1 * op_ref[1, sl, :] + w2 * op_ref[2, sl, :]
        o_ref[sl, :] = (num / (w0 + w1 + w2)).astype(o_ref.dtype)


def _attention(q, k, v):
    bsz, s, da = q.shape
    blk = pl.BlockSpec((None, s, LANES), lambda b, h: (b, 0, h))
    n_pat = len(DILATED_PATTERNS)
    return pl.pallas_call(
        _attn_kernel,
        grid=(bsz, da // LANES),
        in_specs=[blk, blk, blk],
        out_specs=blk,
        out_shape=jax.ShapeDtypeStruct((bsz, s, da), BF16),
        scratch_shapes=[pltpu.VMEM((s, LANES), F32)] * 3
                       + [pltpu.VMEM((n_pat, s, LANES), F32)] * 2,
        compiler_params=_params(("arbitrary", "arbitrary")),
        name="attn",
    )(q, k, v)


def _outproj_kernel(attn_ref, rec_ref, w_ref, x_ref, gm_ref, g_ref, sh_ref, sc_ref, wr_ref, br_ref,
                    x1_ref, h2_ref, idx_ref, gate_ref, rank_ref, cnt_ref, base_ref):
    tm = x_ref.shape[0]
    da = attn_ref.shape[1]

    @pl.when(jnp.logical_and(pl.program_id(0) == 0, pl.program_id(1) == 0))
    def _():
        base_ref[...] = jnp.zeros_like(base_ref)

    y = (jnp.dot(attn_ref[...], w_ref[0:da, :], preferred_element_type=F32)
         + jnp.dot(rec_ref[...], w_ref[da:, :], preferred_element_type=F32))
    x1 = x_ref[...] + gm_ref[...] * y
    x1_ref[...] = x1
    h2 = _rms_modulate(x1, g_ref[...], sh_ref[...], sc_ref[...])
    for c in range(ROW_CHUNKS):
        h2_ref[pl.ds(c, tm, stride=ROW_CHUNKS), :] = h2[:, c * LANES:(c + 1) * LANES]

    work = lax.dot_general(wr_ref[...], h2, (((1,), (1,)), ((), ())), precision=lax.Precision.HIGHEST,
                           preferred_element_type=F32) + br_ref[...]
    n_exp = work.shape[0]
    eid = lax.broadcasted_iota(I32, (n_exp, tm), 0)
    vals, hots = [], []
    for kk in range(TOP_K):
        m = jnp.max(work, axis=0, keepdims=True)
        sel = jnp.min(jnp.where(work == m, eid, n_exp), axis=0, keepdims=True)
        hot = eid == sel
        idx_ref[kk:kk + 1, :] = sel
        vals.append(m)
        hots.append(hot)
        work = jnp.where(hot, -jnp.inf, work)
    exps = [jnp.exp(v - vals[0]) for v in vals]
    den = exps[0] + exps[1] + exps[2] + exps[3]
    for kk in range(TOP_K):
        gate_ref[kk:kk + 1, :] = exps[kk] / den

    cnt = jnp.zeros((n_exp, tm), F32)
    for hot in hots:
        cnt = cnt + hot.astype(F32)
    tr = lax.broadcasted_iota(I32, (tm, tm), 0)
    tc = lax.broadcasted_iota(I32, (tm, tm), 1)
    before = jnp.where(tr < tc, 1.0, 0.0).astype(BF16)
    prior = jnp.dot(cnt.astype(BF16), before, preferred_element_type=F32) + base_ref[:, 0:1]
    for kk in range(TOP_K):
        rank_ref[kk:kk + 1, :] = jnp.sum(jnp.where(hots[kk], prior, 0.0), axis=0,
                                         keepdims=True).astype(I32)
    base_ref[...] = base_ref[...] + jnp.sum(cnt, axis=1, keepdims=True)
    cnt_ref[...] = base_ref[...].astype(I32)


def _outproj(attn, rec, w_bf, x, gate_m, g_ffn, shift_f, scale_f, w_router, b_router):
    bsz, s, d = x.shape
    da = attn.shape[-1]
    n_exp = w_router.shape[1]
    tm = OUTPROJ_ROWS
    nt = s // tm
    t = bsz * s
    row = pl.BlockSpec((None, 1, d), lambda b, i: (b, 0, 0))
    small = pl.BlockSpec((TOP_K, tm), lambda b, i: (0, b * nt + i))
    return pl.pallas_call(
        _outproj_kernel,
        grid=(bsz, nt),
        in_specs=[pl.BlockSpec((None, tm, da), lambda b, i: (b, i, 0)),
                  pl.BlockSpec((None, tm, da), lambda b, i: (b, i, 0)),
                  pl.BlockSpec((d, d), lambda b, i: (0, 0)),
                  pl.BlockSpec((None, tm, d), lambda b, i: (b, i, 0)),
                  row,
                  pl.BlockSpec((1, d), lambda b, i: (0, 0)),
                  row, row,
                  pl.BlockSpec((n_exp, d), lambda b, i: (0, 0)),
                  pl.BlockSpec((n_exp, 1), lambda b, i: (0, 0))],
        out_specs=[pl.BlockSpec((None, tm, d), lambda b, i: (b, i, 0)),
                   pl.BlockSpec((tm * ROW_CHUNKS, LANES), lambda b, i: (b * nt + i, 0)),
                   small, small, small,
                   pl.BlockSpec((n_exp, LANES), lambda b, i: (0, 0))],
        out_shape=[jax.ShapeDtypeStruct((bsz, s, d), F32),
                   jax.ShapeDtypeStruct((t * ROW_CHUNKS, LANES), F32),
                   jax.ShapeDtypeStruct((TOP_K, t), I32),
                   jax.ShapeDtypeStruct((TOP_K, t), F32),
                   jax.ShapeDtypeStruct((TOP_K, t), I32),
                   jax.ShapeDtypeStruct((n_exp, LANES), I32)],
        scratch_shapes=[pltpu.VMEM((n_exp, LANES), F32)],
        compiler_params=_params(("arbitrary", "arbitrary")),
        name="outproj_router",
    )(attn, rec, w_bf, x, gate_m, g_ffn.reshape(1, d), shift_f, scale_f, w_router.T,
      b_router.reshape(n_exp, 1))


def _row_slice(ref, row):
    if isinstance(row, int):
        return ref.at[pl.ds(row * ROW_CHUNKS, ROW_CHUNKS), :]
    return ref.at[pl.ds(pl.multiple_of(row * ROW_CHUNKS, ROW_CHUNKS), ROW_CHUNKS), :]


def _row_copy(src_ref, src_row, dst_ref, dst_row, sem):
    return pltpu.make_async_copy(_row_slice(src_ref, src_row), _row_slice(dst_ref, dst_row), sem)


def _dispatch_kernel(fill_ref, dest_ref, h2_ref, xbuf_ref, zero_ref, sem, zsem):
    tm = dest_ref.shape[1]
    step = pl.program_id(0)

    @pl.when(step == 0)
    def _():
        zero_ref[...] = jnp.zeros_like(zero_ref)
        n_fill = fill_ref.shape[0]

        def zstart(i, c):
            _row_copy(zero_ref, 0, xbuf_ref, fill_ref[i], zsem).start()
            return c

        lax.fori_loop(0, n_fill, zstart, 0)

        def zwait(i, c):
            _row_copy(zero_ref, 0, xbuf_ref, 0, zsem).wait()
            return c

        lax.fori_loop(0, n_fill, zwait, 0)

    def start(i, c):
        for kk in range(TOP_K):
            _row_copy(h2_ref, i, xbuf_ref, dest_ref[kk, i], sem).start()
        return c

    lax.fori_loop(0, tm, start, 0)

    def wait(i, c):
        for kk in range(TOP_K):
            _row_copy(h2_ref, 0, xbuf_ref, 0, sem).wait()
        return c

    lax.fori_loop(0, tm, wait, 0)


def _dispatch(h2rows, dest3, fill, n_rows):
    nt, _, tm = dest3.shape
    return pl.pallas_call(
        _dispatch_kernel,
        grid_spec=pltpu.PrefetchScalarGridSpec(
            num_scalar_prefetch=1,
            grid=(nt,),
            in_specs=[pl.BlockSpec((None, TOP_K, tm), lambda i, f: (i, 0, 0), memory_space=pltpu.SMEM),
                      pl.BlockSpec((tm * ROW_CHUNKS, LANES), lambda i, f: (i, 0))],
            out_specs=pl.BlockSpec(memory_space=pl.ANY),
            scratch_shapes=[pltpu.VMEM((ROW_CHUNKS, LANES), F32),
                            pltpu.SemaphoreType.DMA(()),
                            pltpu.SemaphoreType.DMA(())]),
        out_shape=jax.ShapeDtypeStruct((n_rows * ROW_CHUNKS, LANES), F32),
        compiler_params=_params(("arbitrary",)),
        name="dispatch",
    )(fill, dest3, h2rows)


def _experts_kernel(bexp_ref, nused_ref, x_ref, wgu_ref, bgu_ref, wd_ref, bd_ref, y_ref,
                    xs_ref, wgu_bf, wd_bf):
    b = pl.program_id(0)
    rows = xs_ref.shape[0]
    f = wd_ref.shape[0]
    used = b < nused_ref[0]
    changed = jnp.logical_or(b == 0, bexp_ref[b] != bexp_ref[jnp.maximum(b - 1, 0)])

    @pl.when(jnp.logical_and(used, changed))
    def _():
        wgu_bf[...] = wgu_ref[...].astype(BF16)
        wd_bf[...] = wd_ref[...].astype(BF16)

    @pl.when(used)
    def _():
        for c in range(ROW_CHUNKS):
            xs_ref[:, c * LANES:(c + 1) * LANES] = x_ref[pl.ds(c, rows, stride=ROW_CHUNKS), :].astype(BF16)
        gu = jnp.dot(xs_ref[...], wgu_bf[...], preferred_element_type=F32) + bgu_ref[...]
        gate = jnp.minimum(gu[:, :f], SWIGLU_LIMIT)
        up = jnp.clip(gu[:, f:], -SWIGLU_LIMIT, SWIGLU_LIMIT)
        glu = gate * jax.nn.sigmoid(gate * SWIGLU_ALPHA)
        act = ((up + 1.0) * glu).astype(BF16)
        y = jnp.dot(act, wd_bf[...], preferred_element_type=F32) + bd_ref[...]
        for c in range(ROW_CHUNKS):
            y_ref[pl.ds(c, rows, stride=ROW_CHUNKS), :] = y[:, c * LANES:(c + 1) * LANES]

    @pl.when(jnp.logical_not(used))
    def _():
        y_ref[...] = jnp.zeros_like(y_ref)


def _experts(xbuf, blk_exp, n_used, w_gate_up, b_gate_up, w_down, b_down):
    n_exp, d, f2 = w_gate_up.shape
    f = w_down.shape[1]
    rows = EXPERT_ROWS
    n_blocks = xbuf.shape[0] // (rows * ROW_CHUNKS)
    return pl.pallas_call(
        _experts_kernel,
        grid_spec=pltpu.PrefetchScalarGridSpec(
            num_scalar_prefetch=2,
            grid=(n_blocks,),
            in_specs=[pl.BlockSpec((rows * ROW_CHUNKS, LANES),
                                   lambda b, e, n: (jnp.minimum(b, n[0] - 1), 0)),
                      pl.BlockSpec((None, d, f2), lambda b, e, n: (e[b], 0, 0)),
                      pl.BlockSpec((None, 1, f2), lambda b, e, n: (e[b], 0, 0)),
                      pl.BlockSpec((None, f, d), lambda b, e, n: (e[b], 0, 0)),
                      pl.BlockSpec((None, 1, d), lambda b, e, n: (e[b], 0, 0))],
            out_specs=pl.BlockSpec((rows * ROW_CHUNKS, LANES), lambda b, e, n: (b, 0)),
            scratch_shapes=[pltpu.VMEM((rows, d), BF16),
                            pltpu.VMEM((d, f2), BF16),
                            pltpu.VMEM((f, d), BF16)]),
        out_shape=jax.ShapeDtypeStruct(xbuf.shape, F32),
        compiler_params=_params(("arbitrary",), 56 * 1024 * 1024),
        name="experts",
    )(blk_exp, n_used, xbuf, w_gate_up, b_gate_up.reshape(n_exp, 1, f2), w_down,
      b_down.reshape(n_exp, 1, d))


def _combine_kernel(dest_ref, ybuf_ref, gates_ref, x1_ref, gf_ref, g_ref, o_ref, rows_ref, sem):
    tm = x1_ref.shape[0]

    def start(i, c):
        for kk in range(TOP_K):
            _row_copy(ybuf_ref, dest_ref[kk, i], rows_ref, kk * tm + i, sem).start()
        return c

    lax.fori_loop(0, tm, start, 0)

    def wait(i, c):
        for kk in range(TOP_K):
            _row_copy(ybuf_ref, 0, rows_ref, 0, sem).wait()
        return c

    lax.fori_loop(0, tm, wait, 0)

    gates = gates_ref[...]
    chunks = []
    for c in range(ROW_CHUNKS):
        acc = jnp.zeros((tm, LANES), F32)
        for kk in range(TOP_K):
            yk = rows_ref[pl.ds(kk * tm * ROW_CHUNKS + c, tm, stride=ROW_CHUNKS), :]
            acc = acc + yk * gates[:, kk:kk + 1]
        chunks.append(acc)
    moe = jnp.concatenate(chunks, axis=-1)
    x2 = x1_ref[...] + gf_ref[...] * moe
    o_ref[...] = (x2 * lax.rsqrt(jnp.mean(x2 * x2, axis=-1, keepdims=True) + NORM_EPS)) * g_ref[...]


def _combine(ybuf, dest3, gates_t, x1, gate_f, g_final):
    bsz, s, d = x1.shape
    nt, _, tm = dest3.shape
    per_b = s // tm
    return pl.pallas_call(
        _combine_kernel,
        grid=(nt,),
        in_specs=[pl.BlockSpec((None, TOP_K, tm), lambda i: (i, 0, 0), memory_space=pltpu.SMEM),
                  pl.BlockSpec(memory_space=pl.ANY),
                  pl.BlockSpec((tm, TOP_K), lambda i: (i, 0)),
                  pl.BlockSpec((None, tm, d), lambda i: (i // per_b, i % per_b, 0)),
                  pl.BlockSpec((None, 1, d), lambda i: (i // per_b, 0, 0)),
                  pl.BlockSpec((1, d), lambda i: (0, 0))],
        out_specs=pl.BlockSpec((None, tm, d), lambda i: (i // per_b, i % per_b, 0)),
        out_shape=jax.ShapeDtypeStruct((bsz, s, d), F32),
        scratch_shapes=[pltpu.VMEM((TOP_K * tm * ROW_CHUNKS, LANES), F32),
                        pltpu.SemaphoreType.DMA(())],
        compiler_params=_params(("arbitrary",)),
        name="combine",
    )(dest3, ybuf, gates_t, x1, gate_f, g_final.reshape(1, d))


def _routing_plan(idx, rank, counts, n_rows):
    rows = EXPERT_ROWS
    padded = ((counts + rows - 1) // rows) * rows
    pend = jnp.cumsum(padded)
    pstart = pend - padded
    dest = pstart[idx] + rank
    n_blocks = n_rows // rows
    blk_start = jnp.arange(n_blocks, dtype=I32) * rows
    blk_exp = jnp.minimum(jnp.searchsorted(pend, blk_start, side='right'), N_EXPERTS - 1).astype(I32)
    n_used = (pend[-1] // rows).astype(I32).reshape(1)
    pad = padded - counts
    pad_end = jnp.cumsum(pad)
    j = jnp.arange(n_rows - idx.size, dtype=I32)
    pe = jnp.minimum(jnp.searchsorted(pad_end, j, side='right'), N_EXPERTS - 1)
    in_pad = pstart[pe] + counts[pe] + (j - (pad_end[pe] - pad[pe]))
    fill = jnp.where(j < pad_end[-1], in_pad, pend[-1] + (j - pad_end[-1])).astype(I32)
    return dest.astype(I32), blk_exp, n_used, fill


def kernel(x, c, w_ada, b_ada, g_mix, w_in, conv_w, conv_b, w_rg_a, b_rg_a, w_rg_x, b_rg_x, lam, w_out,
           g_ffn, w_router, b_router, w_gate_up, b_gate_up, w_down, b_down, g_final):
    bsz, s, d = x.shape
    t = bsz * s
    depth = w_ada.shape[0]
    assert depth == 1, "the combine kernel applies the final norm; one layer only"
    for l in range(depth):
        mod = _ada(c, w_ada[l], b_ada[l]).reshape(bsz, 6, 1, d)
        shift_m, scale_m, gate_m, shift_f, scale_f, gate_f = (mod[:, j] for j in range(6))
        q, k, v, xr, gr = _inproj(x, g_mix[l], shift_m, scale_m, w_in[l].astype(BF16))
        attn = _attention(q, k, v)
        rec = _rglru(xr, gr, conv_w[l], conv_b[l], w_rg_a[l], b_rg_a[l], w_rg_x[l], b_rg_x[l], lam[l])
        x1, h2rows, idx, gates, rank, cnt = _outproj(
            attn, rec, w_out[l].astype(BF16), x, gate_m, g_ffn[l], shift_f, scale_f, w_router[l], b_router[l])
        n_rows = t * TOP_K + N_EXPERTS * EXPERT_ROWS
        dest, blk_exp, n_used, fill = _routing_plan(idx, rank, cnt[:, 0], n_rows)
        nt = t // DISPATCH_ROWS
        dest3 = dest.reshape(TOP_K, nt, DISPATCH_ROWS).transpose(1, 0, 2)
        xbuf = _dispatch(h2rows, dest3, fill, n_rows)
        ybuf = _experts(xbuf, blk_exp, n_used, w_gate_up[l], b_gate_up[l], w_down[l], b_down[l])
        x = _combine(ybuf, dest3, gates.T, x1, gate_f, g_final)
    return x
```

```python
import functools

import jax
import jax.numpy as jnp
from jax import lax
from jax.experimental import pallas as pl
from jax.experimental.pallas import tpu as pltpu

F32 = jnp.float32
BF16 = jnp.bfloat16
I32 = jnp.int32

HEAD_DIM = 64
DILATED_PATTERNS = ((128, 1), (512, 4), (2048, 16))
REC_BLOCKS = 8
CONV_WIDTH = 4
RG_C = 8.0
N_EXPERTS = 32
TOP_K = 4
SWIGLU_LIMIT = 7.0
SWIGLU_ALPHA = 1.702
NORM_EPS = 1e-6
NEG_INF = -1e30
LOG2_E = 1.4426950408889634

LANES = 128
SUBLANES = 8
ROW_CHUNKS = 8
VMEM_LIMIT = 48 * 1024 * 1024

INPROJ_ROWS = 512
RGLRU_ROWS = 512
OUTPROJ_ROWS = 256
ATTN_BLOCK = 128
EXPERT_ROWS = 256
DISPATCH_ROWS = 256
COMBINE_ROWS = 256


def _params(sem, vmem=VMEM_LIMIT):
    return pltpu.CompilerParams(dimension_semantics=sem, vmem_limit_bytes=vmem)


def _ada_kernel(c_ref, w_ref, b_ref, o_ref):
    o_ref[...] = jnp.dot(c_ref[...], w_ref[...], precision=lax.Precision.HIGHEST,
                         preferred_element_type=F32) + b_ref[...]


def _ada(c, w, b):
    bsz, d = c.shape
    n = w.shape[1]
    return pl.pallas_call(
        _ada_kernel,
        grid=(n // d,),
        in_specs=[pl.BlockSpec((bsz, d), lambda j: (0, 0)),
                  pl.BlockSpec((d, d), lambda j: (0, j)),
                  pl.BlockSpec((1, d), lambda j: (0, j))],
        out_specs=pl.BlockSpec((bsz, d), lambda j: (0, j)),
        out_shape=jax.ShapeDtypeStruct((bsz, n), F32),
        compiler_params=_params(("arbitrary",)),
        name="ada",
    )(c, w, b.reshape(1, n))


def _rms_modulate(x, g, shift, scale):
    y = x * lax.rsqrt(jnp.mean(x * x, axis=-1, keepdims=True) + NORM_EPS)
    return (y * g) * (1.0 + scale) + shift


def _inproj_kernel(x_ref, g_ref, sh_ref, sc_ref, w_ref, q_ref, k_ref, v_ref, xr_ref, gr_ref):
    h = _rms_modulate(x_ref[...], g_ref[...], sh_ref[...], sc_ref[...]).astype(BF16)
    outs = (q_ref, k_ref, v_ref, xr_ref, gr_ref)
    width = q_ref.shape[-1]
    for j, o_ref in enumerate(outs):
        z = jnp.dot(h, w_ref[:, j * width:(j + 1) * width], preferred_element_type=F32)
        o_ref[...] = z.astype(o_ref.dtype)


def _inproj(x, g, shift, scale, w_bf):
    bsz, s, d = x.shape
    n = w_bf.shape[1]
    width = n // 5
    tm = INPROJ_ROWS
    row = pl.BlockSpec((None, 1, d), lambda b, i: (b, 0, 0))
    out_blk = pl.BlockSpec((None, tm, width), lambda b, i: (b, i, 0))
    shp = lambda dt: jax.ShapeDtypeStruct((bsz, s, width), dt)
    return pl.pallas_call(
        _inproj_kernel,
        grid=(bsz, s // tm),
        in_specs=[pl.BlockSpec((None, tm, d), lambda b, i: (b, i, 0)),
                  pl.BlockSpec((1, d), lambda b, i: (0, 0)),
                  row, row,
                  pl.BlockSpec((d, n), lambda b, i: (0, 0))],
        out_specs=[out_blk] * 5,
        out_shape=[shp(BF16), shp(BF16), shp(BF16), shp(F32), shp(F32)],
        compiler_params=_params(("arbitrary", "arbitrary")),
        name="inproj",
    )(x, g.reshape(1, d), shift, scale, w_bf)


def _gelu_tanh(x):
    return 0.5 * x * (1.0 + jnp.tanh(0.7978845608028654 * (x + 0.044715 * (x * x * x))))


def _rglru_kernel(xr_ref, gr_ref, cw_ref, cb_ref, wa_ref, ba_ref, wx_ref, bx_ref, lam_ref,
                  o_ref, xe_ref, a_ref, u_ref, tail_ref, h_ref):
    ts, ch = xr_ref.shape
    pad = SUBLANES

    @pl.when(pl.program_id(1) == 0)
    def _():
        tail_ref[...] = jnp.zeros_like(tail_ref)
        h_ref[...] = jnp.zeros_like(h_ref)

    x = xr_ref[...]
    xe_ref[0:pad, :] = tail_ref[...]
    xe_ref[pad:pad + ts, :] = x
    tail_ref[...] = x[ts - pad:ts, :]
    xc = cb_ref[...] + jnp.zeros((ts, ch), F32)
    for j in range(CONV_WIDTH):
        off = pad - (CONV_WIDTH - 1) + j
        xc = xc + cw_ref[j:j + 1, :] * xe_ref[off:off + ts, :]

    xb = xc.astype(BF16)
    r = jax.nn.sigmoid(jnp.dot(xb, wa_ref[...], preferred_element_type=F32) + ba_ref[...])
    i = jax.nn.sigmoid(jnp.dot(xb, wx_ref[...], preferred_element_type=F32) + bx_ref[...])
    nl = -lam_ref[...]
    softplus = jnp.maximum(nl, 0.0) + jnp.log(1.0 + jnp.exp(-jnp.abs(nl)))
    log_a = (-RG_C) * r * softplus
    a = jnp.exp(log_a)
    mult = jnp.sqrt(1.0 - jnp.exp(2.0 * log_a))
    u = mult * (i * xc)

    ng = ts // SUBLANES
    a3 = a.reshape(ng, SUBLANES, ch)
    u3 = u.reshape(ng, SUBLANES, ch)
    rid = lax.broadcasted_iota(I32, (ng, SUBLANES, ch), 1)
    for sft in (1, 2, 4):
        a_s = pltpu.roll(a3, sft, 1)
        u_s = pltpu.roll(u3, sft, 1)
        keep = rid >= sft
        u3 = jnp.where(keep, a3 * u_s + u3, u3)
        a3 = jnp.where(keep, a3 * a_s, a3)
    a_ref[...] = a3.reshape(ts, ch)
    u_ref[...] = u3.reshape(ts, ch)

    inner = 8

    def body(gi, hprev):
        for jj in range(inner):
            base = pl.multiple_of((gi * inner + jj) * SUBLANES, SUBLANES)
            hcur = u_ref[pl.ds(base, SUBLANES), :] + a_ref[pl.ds(base, SUBLANES), :] * hprev
            u_ref[pl.ds(base, SUBLANES), :] = hcur
            hprev = hcur[SUBLANES - 1:SUBLANES, :]
        return hprev

    h_ref[...] = lax.fori_loop(0, ng // inner, body, h_ref[...])
    o_ref[...] = (u_ref[...] * _gelu_tanh(gr_ref[...])).astype(o_ref.dtype)


def _block_diag(w):
    nb, hin, hout = w.shape
    eye = jnp.eye(nb, dtype=w.dtype)
    return (eye[:, None, :, None] * w[:, :, None, :]).reshape(nb * hin, nb * hout)


def _rglru(xr, gr, conv_w, conv_b, w_a, b_a, w_x, b_x, lam):
    bsz, s, ch = xr.shape
    ts = RGLRU_ROWS
    blk = pl.BlockSpec((None, ts, ch), lambda b, t: (b, t, 0))
    full = lambda r, c: pl.BlockSpec((r, c), lambda b, t: (0, 0))
    return pl.pallas_call(
        _rglru_kernel,
        grid=(bsz, s // ts),
        in_specs=[blk, blk, full(CONV_WIDTH, ch), full(1, ch), full(ch, ch), full(1, ch),
                  full(ch, ch), full(1, ch), full(1, ch)],
        out_specs=blk,
        out_shape=jax.ShapeDtypeStruct((bsz, s, ch), BF16),
        scratch_shapes=[pltpu.VMEM((ts + SUBLANES, ch), F32),
                        pltpu.VMEM((ts, ch), F32),
                        pltpu.VMEM((ts, ch), F32),
                        pltpu.VMEM((SUBLANES, ch), F32),
                        pltpu.VMEM((1, ch), F32)],
        compiler_params=_params(("arbitrary", "arbitrary")),
        name="rglru",
    )(xr, gr, conv_w, conv_b.reshape(1, ch), _block_diag(w_a).astype(BF16), b_a.reshape(1, ch),
      _block_diag(w_x).astype(BF16), b_x.reshape(1, ch), lam.reshape(1, ch))


def _attn_kernel(q_ref, k_ref, v_ref, o_ref, f_ref, qs_ref, kp_ref, vp_ref, op_ref, lp_ref):
    s = q_ref.shape[0]
    blk = ATTN_BLOCK
    n_blocks = s // blk
    unroll = 8

    lane = lax.broadcasted_iota(I32, (blk, LANES), 1)
    head0 = lane < HEAD_DIM
    head0_wide = lax.broadcasted_iota(I32, (blk, 2 * LANES), 1) % LANES < HEAD_DIM
    qi = lax.broadcasted_iota(I32, (2 * blk, blk), 0) % blk
    kj = lax.broadcasted_iota(I32, (2 * blk, blk), 1)
    prev_ok = kj >= qi
    cur_ok = kj <= qi
    dims = (((1,), (1,)), ((), ()))

    def block_rows(j, dil):
        per_res = (s // dil) // blk
        start = j // per_res + (j % per_res) * (dil * blk)
        if dil == 1:
            return pl.ds(pl.multiple_of(start, blk), blk)
        return pl.ds(start, blk, stride=dil)

    def permute(src_ref, store):
        f_ref[...] = src_ref[...].astype(F32)
        for p, (_, dil) in enumerate(DILATED_PATTERNS):
            def body(j, c, p=p, dil=dil):
                store(p, j, f_ref[block_rows(j, dil), :])
                return c
            lax.fori_loop(0, n_blocks, body, 0)

    def store_q(p, j, x):
        x = x * (HEAD_DIM ** -0.5 * LOG2_E)
        base = pl.multiple_of(j * (2 * blk), 2 * blk)
        qs_ref[p, pl.ds(base, blk), :] = jnp.where(head0, x, 0.0).astype(BF16)
        qs_ref[p, pl.ds(base + blk, blk), :] = jnp.where(head0, 0.0, x).astype(BF16)

    def store_kv(dst_ref):
        def store(p, j, x):
            dst_ref[p, pl.ds(pl.multiple_of((j + 1) * blk, blk), blk), 0:LANES] = x.astype(BF16)
        return store

    for p in range(len(DILATED_PATTERNS)):
        kp_ref[p, 0:blk, :] = jnp.zeros((blk, LANES), BF16)
        vp_ref[p, 0:blk, 0:LANES] = jnp.zeros((blk, LANES), BF16)
        vp_ref[p, :, LANES:] = jnp.ones((s + blk, LANES), BF16)
    permute(q_ref, store_q)
    permute(k_ref, store_kv(kp_ref))
    permute(v_ref, store_kv(vp_ref))

    for p, (window, dil) in enumerate(DILATED_PATTERNS):
        assert window // dil == blk
        per_res = (s // dil) // blk

        def body(it, carry, p=p, dil=dil, per_res=per_res):
            for u in range(unroll):
                j = it * unroll + u
                no_prev = jnp.where(j % per_res > 0, 0.0, NEG_INF)
                qs = qs_ref[p, pl.ds(pl.multiple_of(j * (2 * blk), 2 * blk), 2 * blk), :]
                kv_rows = pl.ds(pl.multiple_of(j * blk, blk), 2 * blk)
                sc = lax.dot_general(qs, kp_ref[p, kv_rows, :], dims, preferred_element_type=F32)
                s_prev = jnp.where(prev_ok, sc[:, :blk], NEG_INF) + no_prev
                s_cur = jnp.where(cur_ok, sc[:, blk:], NEG_INF)
                m = jnp.max(jnp.maximum(s_prev, s_cur), axis=-1, keepdims=True)
                pcat = jnp.concatenate([jnp.exp2(s_prev - m), jnp.exp2(s_cur - m)], axis=-1).astype(BF16)
                rl = jnp.dot(pcat, vp_ref[p, kv_rows, :], preferred_element_type=F32)
                rl = jnp.where(head0_wide, rl[:blk], rl[blk:])
                l = rl[:, LANES:]
                rows = block_rows(j, dil)
                op_ref[p, rows, :] = rl[:, :LANES] / l
                lp_ref[p, rows, :] = jnp.where(head0, m[:blk], m[blk:]) + jnp.log2(l)
            return carry

        lax.fori_loop(0, n_blocks // unroll, body, 0)

    chunk = 512
    for c0 in range(0, s, chunk):
        sl = pl.ds(c0, chunk)
        l0, l1, l2 = lp_ref[0, sl, :], lp_ref[1, sl, :], lp_ref[2, sl, :]
        m = jnp.maximum(jnp.maximum(l0, l1), l2)
        w0, w1, w2 = jnp.exp2(l0 - m), jnp.exp2(l1 - m), jnp.exp2(l2 - m)
        num = w0 * op_ref[0, sl, :] + w1 * op_ref[1, sl, :] + w2 * op_ref[2, sl, :]
        o_ref[sl, :] = (num / (w0 + w1 + w2)).astype(o_ref.dtype)


def _attention(q, k, v):
    bsz, s, da = q.shape
    blk = pl.BlockSpec((None, s, LANES), lambda b, h: (b, 0, h))
    n_pat = len(DILATED_PATTERNS)
    return pl.pallas_call(
        _attn_kernel,
        grid=(bsz, da // LANES),
        in_specs=[blk, blk, blk],
        out_specs=blk,
        out_shape=jax.ShapeDtypeStruct((bsz, s, da), BF16),
        scratch_shapes=[pltpu.VMEM((s, LANES), F32),
                        pltpu.VMEM((n_pat, 2 * s, LANES), BF16),
                        pltpu.VMEM((n_pat, s + ATTN_BLOCK, LANES), BF16),
                        pltpu.VMEM((n_pat, s + ATTN_BLOCK, 2 * LANES), BF16),
                        pltpu.VMEM((n_pat, s, LANES), F32),
                        pltpu.VMEM((n_pat, s, LANES), F32)],
        compiler_params=_params(("arbitrary", "arbitrary")),
        name="attn",
    )(q, k, v)


def _outproj_kernel(attn_ref, rec_ref, w_ref, x_ref, gm_ref, g_ref, sh_ref, sc_ref, wr_ref, br_ref,
                    x1_ref, h2_ref, idx_ref, gate_ref, rank_ref, cnt_ref, base_ref):
    tm = x_ref.shape[0]
    da = attn_ref.shape[1]

    @pl.when(jnp.logical_and(pl.program_id(0) == 0, pl.program_id(1) == 0))
    def _():
        base_ref[...] = jnp.zeros_like(base_ref)

    y = (jnp.dot(attn_ref[...], w_ref[0:da, :], preferred_element_type=F32)
         + jnp.dot(rec_ref[...], w_ref[da:, :], preferred_element_type=F32))
    x1 = x_ref[...] + gm_ref[...] * y
    x1_ref[...] = x1
    h2 = _rms_modulate(x1, g_ref[...], sh_ref[...], sc_ref[...])
    for c in range(ROW_CHUNKS):
        h2_ref[pl.ds(c, tm, stride=ROW_CHUNKS), :] = h2[:, c * LANES:(c + 1) * LANES]

    work = lax.dot_general(wr_ref[...], h2, (((1,), (1,)), ((), ())), precision=lax.Precision.HIGHEST,
                           preferred_element_type=F32) + br_ref[...]
    n_exp = work.shape[0]
    eid = lax.broadcasted_iota(I32, (n_exp, tm), 0)
    vals, hots = [], []
    for kk in range(TOP_K):
        m = jnp.max(work, axis=0, keepdims=True)
        sel = jnp.min(jnp.where(work == m, eid, n_exp), axis=0, keepdims=True)
        hot = eid == sel
        idx_ref[kk:kk + 1, :] = sel
        vals.append(m)
        hots.append(hot)
        work = jnp.where(hot, -jnp.inf, work)
    exps = [jnp.exp(v - vals[0]) for v in vals]
    den = exps[0] + exps[1] + exps[2] + exps[3]
    for kk in range(TOP_K):
        gate_ref[kk:kk + 1, :] = exps[kk] / den

    cnt = jnp.zeros((n_exp, tm), F32)
    for hot in hots:
        cnt = cnt + hot.astype(F32)
    tr = lax.broadcasted_iota(I32, (tm, tm), 0)
    tc = lax.broadcasted_iota(I32, (tm, tm), 1)
    before = jnp.where(tr < tc, 1.0, 0.0).astype(BF16)
    prior = jnp.dot(cnt.astype(BF16), before, preferred_element_type=F32) + base_ref[:, 0:1]
    for kk in range(TOP_K):
        rank_ref[kk:kk + 1, :] = jnp.sum(jnp.where(hots[kk], prior, 0.0), axis=0,
                                         keepdims=True).astype(I32)
    base_ref[...] = base_ref[...] + jnp.sum(cnt, axis=1, keepdims=True)
    cnt_ref[...] = base_ref[...].astype(I32)


def _outproj(attn, rec, w_bf, x, gate_m, g_ffn, shift_f, scale_f, w_router, b_router):
    bsz, s, d = x.shape
    da = attn.shape[-1]
    n_exp = w_router.shape[1]
    tm = OUTPROJ_ROWS
    nt = s // tm
    t = bsz * s
    row = pl.BlockSpec((None, 1, d), lambda b, i: (b, 0, 0))
    small = pl.BlockSpec((TOP_K, tm), lambda b, i: (0, b * nt + i))
    return pl.pallas_call(
        _outproj_kernel,
        grid=(bsz, nt),
        in_specs=[pl.BlockSpec((None, tm, da), lambda b, i: (b, i, 0)),
                  pl.BlockSpec((None, tm, da), lambda b, i: (b, i, 0)),
                  pl.BlockSpec((d, d), lambda b, i: (0, 0)),
                  pl.BlockSpec((None, tm, d), lambda b, i: (b, i, 0)),
                  row,
                  pl.BlockSpec((1, d), lambda b, i: (0, 0)),
                  row, row,
                  pl.BlockSpec((n_exp, d), lambda b, i: (0, 0)),
                  pl.BlockSpec((n_exp, 1), lambda b, i: (0, 0))],
        out_specs=[pl.BlockSpec((None, tm, d), lambda b, i: (b, i, 0)),
                   pl.BlockSpec((tm * ROW_CHUNKS, LANES), lambda b, i: (b * nt + i, 0)),
                   small, small, small,
                   pl.BlockSpec((n_exp, LANES), lambda b, i: (0, 0))],
        out_shape=[jax.ShapeDtypeStruct((bsz, s, d), F32),
                   jax.ShapeDtypeStruct((t * ROW_CHUNKS, LANES), F32),
                   jax.ShapeDtypeStruct((TOP_K, t), I32),
                   jax.ShapeDtypeStruct((TOP_K, t), F32),
                   jax.ShapeDtypeStruct((TOP_K, t), I32),
                   jax.ShapeDtypeStruct((n_exp, LANES), I32)],
        scratch_shapes=[pltpu.VMEM((n_exp, LANES), F32)],
        compiler_params=_params(("arbitrary", "arbitrary")),
        name="outproj_router",
    )(attn, rec, w_bf, x, gate_m, g_ffn.reshape(1, d), shift_f, scale_f, w_router.T,
      b_router.reshape(n_exp, 1))


def _row_slice(ref, row):
    if isinstance(row, int):
        return ref.at[pl.ds(row * ROW_CHUNKS, ROW_CHUNKS), :]
    return ref.at[pl.ds(pl.multiple_of(row * ROW_CHUNKS, ROW_CHUNKS), ROW_CHUNKS), :]


def _row_copy(src_ref, src_row, dst_ref, dst_row, sem):
    return pltpu.make_async_copy(_row_slice(src_ref, src_row), _row_slice(dst_ref, dst_row), sem)


def _dispatch_kernel(fill_ref, dest_ref, h2_ref, xbuf_ref, zero_ref, sem, zsem):
    tm = dest_ref.shape[1]
    step = pl.program_id(0)

    @pl.when(step == 0)
    def _():
        zero_ref[...] = jnp.zeros_like(zero_ref)
        n_fill = fill_ref.shape[0]

        def zstart(i, c):
            _row_copy(zero_ref, 0, xbuf_ref, fill_ref[i], zsem).start()
            return c

        lax.fori_loop(0, n_fill, zstart, 0)

        def zwait(i, c):
            _row_copy(zero_ref, 0, xbuf_ref, 0, zsem).wait()
            return c

        lax.fori_loop(0, n_fill, zwait, 0)

    def start(i, c):
        for kk in range(TOP_K):
            _row_copy(h2_ref, i, xbuf_ref, dest_ref[kk, i], sem).start()
        return c

    lax.fori_loop(0, tm, start, 0)

    def wait(i, c):
        for kk in range(TOP_K):
            _row_copy(h2_ref, 0, xbuf_ref, 0, sem).wait()
        return c

    lax.fori_loop(0, tm, wait, 0)


def _dispatch(h2rows, dest3, fill, n_rows):
    nt, _, tm = dest3.shape
    return pl.pallas_call(
        _dispatch_kernel,
        grid_spec=pltpu.PrefetchScalarGridSpec(
            num_scalar_prefetch=1,
            grid=(nt,),
            in_specs=[pl.BlockSpec((None, TOP_K, tm), lambda i, f: (i, 0, 0), memory_space=pltpu.SMEM),
                      pl.BlockSpec((tm * ROW_CHUNKS, LANES), lambda i, f: (i, 0))],
            out_specs=pl.BlockSpec(memory_space=pl.ANY),
            scratch_shapes=[pltpu.VMEM((ROW_CHUNKS, LANES), F32),
                            pltpu.SemaphoreType.DMA(()),
                            pltpu.SemaphoreType.DMA(())]),
        out_shape=jax.ShapeDtypeStruct((n_rows * ROW_CHUNKS, LANES), F32),
        compiler_params=_params(("arbitrary",)),
        name="dispatch",
    )(fill, dest3, h2rows)


def _experts_kernel(bexp_ref, nused_ref, x_ref, wgu_ref, bgu_ref, wd_ref, bd_ref, y_ref,
                    xs_ref, wgu_bf, wd_bf):
    b = pl.program_id(0)
    rows = xs_ref.shape[0]
    f = wd_ref.shape[0]
    used = b < nused_ref[0]
    changed = jnp.logical_or(b == 0, bexp_ref[b] != bexp_ref[jnp.maximum(b - 1, 0)])

    @pl.when(jnp.logical_and(used, changed))
    def _():
        wgu_bf[...] = wgu_ref[...].astype(BF16)
        wd_bf[...] = wd_ref[...].astype(BF16)

    @pl.when(used)
    def _():
        for c in range(ROW_CHUNKS):
            xs_ref[:, c * LANES:(c + 1) * LANES] = x_ref[pl.ds(c, rows, stride=ROW_CHUNKS), :].astype(BF16)
        gu = jnp.dot(xs_ref[...], wgu_bf[...], preferred_element_type=F32) + bgu_ref[...]
        gate = jnp.minimum(gu[:, :f], SWIGLU_LIMIT)
        up = jnp.clip(gu[:, f:], -SWIGLU_LIMIT, SWIGLU_LIMIT)
        glu = gate * jax.nn.sigmoid(gate * SWIGLU_ALPHA)
        act = ((up + 1.0) * glu).astype(BF16)
        y = jnp.dot(act, wd_bf[...], preferred_element_type=F32) + bd_ref[...]
        for c in range(ROW_CHUNKS):
            y_ref[pl.ds(c, rows, stride=ROW_CHUNKS), :] = y[:, c * LANES:(c + 1) * LANES]

    @pl.when(jnp.logical_not(used))
    def _():
        y_ref[...] = jnp.zeros_like(y_ref)


def _experts(xbuf, blk_exp, n_used, w_gate_up, b_gate_up, w_down, b_down):
    n_exp, d, f2 = w_gate_up.shape
    f = w_down.shape[1]
    rows = EXPERT_ROWS
    n_blocks = xbuf.shape[0] // (rows * ROW_CHUNKS)
    return pl.pallas_call(
        _experts_kernel,
        grid_spec=pltpu.PrefetchScalarGridSpec(
            num_scalar_prefetch=2,
            grid=(n_blocks,),
            in_specs=[pl.BlockSpec((rows * ROW_CHUNKS, LANES),
                                   lambda b, e, n: (jnp.minimum(b, n[0] - 1), 0)),
                      pl.BlockSpec((None, d, f2), lambda b, e, n: (e[b], 0, 0)),
                      pl.BlockSpec((None, 1, f2), lambda b, e, n: (e[b], 0, 0)),
                      pl.BlockSpec((None, f, d), lambda b, e, n: (e[b], 0, 0)),
                      pl.BlockSpec((None, 1, d), lambda b, e, n: (e[b], 0, 0))],
            out_specs=pl.BlockSpec((rows * ROW_CHUNKS, LANES), lambda b, e, n: (b, 0)),
            scratch_shapes=[pltpu.VMEM((rows, d), BF16),
                            pltpu.VMEM((d, f2), BF16),
                            pltpu.VMEM((f, d), BF16)]),
        out_shape=jax.ShapeDtypeStruct(xbuf.shape, F32),
        compiler_params=_params(("arbitrary",), 56 * 1024 * 1024),
        name="experts",
    )(blk_exp, n_used, xbuf, w_gate_up, b_gate_up.reshape(n_exp, 1, f2), w_down,
      b_down.reshape(n_exp, 1, d))


def _combine_kernel(dest_ref, ybuf_ref, gates_ref, x1_ref, gf_ref, g_ref, o_ref, rows_ref, sem):
    tm = x1_ref.shape[0]

    def start(i, c):
        for kk in range(TOP_K):
            _row_copy(ybuf_ref, dest_ref[kk, i], rows_ref, kk * tm + i, sem).start()
        return c

    lax.fori_loop(0, tm, start, 0)

    def wait(i, c):
        for kk in range(TOP_K):
            _row_copy(ybuf_ref, 0, rows_ref, 0, sem).wait()
        return c

    lax.fori_loop(0, tm, wait, 0)

    gates = gates_ref[...]
    chunks = []
    for c in range(ROW_CHUNKS):
        acc = jnp.zeros((tm, LANES), F32)
        for kk in range(TOP_K):
            yk = rows_ref[pl.ds(kk * tm * ROW_CHUNKS + c, tm, stride=ROW_CHUNKS), :]
            acc = acc + yk * gates[:, kk:kk + 1]
        chunks.append(acc)
    moe = jnp.concatenate(chunks, axis=-1)
    x2 = x1_ref[...] + gf_ref[...] * moe
    o_ref[...] = (x2 * lax.rsqrt(jnp.mean(x2 * x2, axis=-1, keepdims=True) + NORM_EPS)) * g_ref[...]


def _combine(ybuf, dest3, gates_t, x1, gate_f, g_final):
    bsz, s, d = x1.shape
    nt, _, tm = dest3.shape
    per_b = s // tm
    return pl.pallas_call(
        _combine_kernel,
        grid=(nt,),
        in_specs=[pl.BlockSpec((None, TOP_K, tm), lambda i: (i, 0, 0), memory_space=pltpu.SMEM),
                  pl.BlockSpec(memory_space=pl.ANY),
                  pl.BlockSpec((tm, TOP_K), lambda i: (i, 0)),
                  pl.BlockSpec((None, tm, d), lambda i: (i // per_b, i % per_b, 0)),
                  pl.BlockSpec((None, 1, d), lambda i: (i // per_b, 0, 0)),
                  pl.BlockSpec((1, d), lambda i: (0, 0))],
        out_specs=pl.BlockSpec((None, tm, d), lambda i: (i // per_b, i % per_b, 0)),
        out_shape=jax.ShapeDtypeStruct((bsz, s, d), F32),
        scratch_shapes=[pltpu.VMEM((TOP_K * tm * ROW_CHUNKS, LANES), F32),
                        pltpu.SemaphoreType.DMA(())],
        compiler_params=_params(("arbitrary",)),
        name="combine",
    )(dest3, ybuf, gates_t, x1, gate_f, g_final.reshape(1, d))


def _routing_plan(idx, rank, counts, n_rows):
    rows = EXPERT_ROWS
    experts = jnp.arange(N_EXPERTS, dtype=I32)

    def lookup(table, e):
        return jnp.sum(jnp.where(e[..., None] == experts, table, 0), axis=-1)

    def segment_of(ends, pos):
        return jnp.minimum(jnp.sum((pos[..., None] >= ends).astype(I32), axis=-1), N_EXPERTS - 1)

    padded = ((counts + rows - 1) // rows) * rows
    pend = jnp.cumsum(padded)
    pstart = pend - padded
    dest = lookup(pstart, idx) + rank
    n_blocks = n_rows // rows
    blk_exp = segment_of(pend, jnp.arange(n_blocks, dtype=I32) * rows)
    n_used = (pend[-1] // rows).astype(I32).reshape(1)
    pad = padded - counts
    pad_end = jnp.cumsum(pad)
    j = jnp.arange(n_rows - idx.size, dtype=I32)
    pe = segment_of(pad_end, j)
    in_pad = lookup(pstart + counts - (pad_end - pad), pe) + j
    fill = jnp.where(j < pad_end[-1], in_pad, pend[-1] + (j - pad_end[-1])).astype(I32)
    return dest.astype(I32), blk_exp.astype(I32), n_used, fill


def kernel(x, c, w_ada, b_ada, g_mix, w_in, conv_w, conv_b, w_rg_a, b_rg_a, w_rg_x, b_rg_x, lam, w_out,
           g_ffn, w_router, b_router, w_gate_up, b_gate_up, w_down, b_down, g_final):
    bsz, s, d = x.shape
    t = bsz * s
    depth = w_ada.shape[0]
    assert depth == 1, "the combine kernel applies the final norm; one layer only"
    for l in range(depth):
        mod = _ada(c, w_ada[l], b_ada[l]).reshape(bsz, 6, 1, d)
        shift_m, scale_m, gate_m, shift_f, scale_f, gate_f = (mod[:, j] for j in range(6))
        q, k, v, xr, gr = _inproj(x, g_mix[l], shift_m, scale_m, w_in[l].astype(BF16))
        attn = _attention(q, k, v)
        rec = _rglru(xr, gr, conv_w[l], conv_b[l], w_rg_a[l], b_rg_a[l], w_rg_x[l], b_rg_x[l], lam[l])
        x1, h2rows, idx, gates, rank, cnt = _outproj(
            attn, rec, w_out[l].astype(BF16), x, gate_m, g_ffn[l], shift_f, scale_f, w_router[l], b_router[l])
        n_rows = t * TOP_K + N_EXPERTS * EXPERT_ROWS
        dest, blk_exp, n_used, fill = _routing_plan(idx, rank, cnt[:, 0], n_rows)
        nt = t // DISPATCH_ROWS
        dest3 = dest.reshape(TOP_K, nt, DISPATCH_ROWS).transpose(1, 0, 2)
        xbuf = _dispatch(h2rows, dest3, fill, n_rows)
        ybuf = _experts(xbuf, blk_exp, n_used, w_gate_up[l], b_gate_up[l], w_down[l], b_down[l])
        x = _combine(ybuf, dest3, gates.T, x1, gate_f, g_final)
    return x
```

```python
import functools

import jax
import jax.numpy as jnp
from jax import lax
from jax.experimental import pallas as pl
from jax.experimental.pallas import tpu as pltpu

F32 = jnp.float32
BF16 = jnp.bfloat16
I32 = jnp.int32

HEAD_DIM = 64
DILATED_PATTERNS = ((128, 1), (512, 4), (2048, 16))
REC_BLOCKS = 8
CONV_WIDTH = 4
RG_C = 8.0
N_EXPERTS = 32
TOP_K = 4
SWIGLU_LIMIT = 7.0
SWIGLU_ALPHA = 1.702
NORM_EPS = 1e-6
NEG_INF = -1e30
LOG2_E = 1.4426950408889634

LANES = 128
SUBLANES = 8
ROW_CHUNKS = 8
VMEM_LIMIT = 48 * 1024 * 1024

INPROJ_ROWS = 512
RGLRU_ROWS = 512
OUTPROJ_ROWS = 256
ATTN_BLOCK = 128
EXPERT_ROWS = 256
DISPATCH_ROWS = 256
COMBINE_ROWS = 256


def _params(sem, vmem=VMEM_LIMIT):
    return pltpu.CompilerParams(dimension_semantics=sem, vmem_limit_bytes=vmem)


def _ada_kernel(c_ref, w_ref, b_ref, o_ref):
    o_ref[...] = jnp.dot(c_ref[...], w_ref[...], precision=lax.Precision.HIGHEST,
                         preferred_element_type=F32) + b_ref[...]


def _ada(c, w, b):
    bsz, d = c.shape
    n = w.shape[1]
    return pl.pallas_call(
        _ada_kernel,
        grid=(n // d,),
        in_specs=[pl.BlockSpec((bsz, d), lambda j: (0, 0)),
                  pl.BlockSpec((d, d), lambda j: (0, j)),
                  pl.BlockSpec((1, d), lambda j: (0, j))],
        out_specs=pl.BlockSpec((bsz, d), lambda j: (0, j)),
        out_shape=jax.ShapeDtypeStruct((bsz, n), F32),
        compiler_params=_params(("arbitrary",)),
        name="ada",
    )(c, w, b.reshape(1, n))


def _rms_modulate(x, g, shift, scale):
    y = x * lax.rsqrt(jnp.mean(x * x, axis=-1, keepdims=True) + NORM_EPS)
    return (y * g) * (1.0 + scale) + shift


def _inproj_kernel(x_ref, g_ref, sh_ref, sc_ref, w_ref, q_ref, k_ref, v_ref, xr_ref, gr_ref):
    h = _rms_modulate(x_ref[...], g_ref[...], sh_ref[...], sc_ref[...]).astype(BF16)
    outs = (q_ref, k_ref, v_ref, xr_ref, gr_ref)
    width = q_ref.shape[-1]
    for j, o_ref in enumerate(outs):
        z = jnp.dot(h, w_ref[:, j * width:(j + 1) * width], preferred_element_type=F32)
        o_ref[...] = z.astype(o_ref.dtype)


def _inproj(x, g, shift, scale, w_bf):
    bsz, s, d = x.shape
    n = w_bf.shape[1]
    width = n // 5
    tm = INPROJ_ROWS
    row = pl.BlockSpec((None, 1, d), lambda b, i: (b, 0, 0))
    out_blk = pl.BlockSpec((None, tm, width), lambda b, i: (b, i, 0))
    shp = lambda dt: jax.ShapeDtypeStruct((bsz, s, width), dt)
    return pl.pallas_call(
        _inproj_kernel,
        grid=(bsz, s // tm),
        in_specs=[pl.BlockSpec((None, tm, d), lambda b, i: (b, i, 0)),
                  pl.BlockSpec((1, d), lambda b, i: (0, 0)),
                  row, row,
                  pl.BlockSpec((d, n), lambda b, i: (0, 0))],
        out_specs=[out_blk] * 5,
        out_shape=[shp(BF16), shp(BF16), shp(BF16), shp(F32), shp(F32)],
        compiler_params=_params(("arbitrary", "arbitrary")),
        name="inproj",
    )(x, g.reshape(1, d), shift, scale, w_bf)


def _gelu_tanh(x):
    return 0.5 * x * (1.0 + jnp.tanh(0.7978845608028654 * (x + 0.044715 * (x * x * x))))


def _rglru_kernel(xr_ref, gr_ref, cw_ref, cb_ref, wa_ref, ba_ref, wx_ref, bx_ref, lam_ref,
                  o_ref, xe_ref, a_ref, u_ref, tail_ref, h_ref):
    ts, ch = xr_ref.shape
    pad = SUBLANES

    @pl.when(pl.program_id(1) == 0)
    def _():
        tail_ref[...] = jnp.zeros_like(tail_ref)
        h_ref[...] = jnp.zeros_like(h_ref)

    x = xr_ref[...]
    xe_ref[0:pad, :] = tail_ref[...]
    xe_ref[pad:pad + ts, :] = x
    tail_ref[...] = x[ts - pad:ts, :]
    xc = cb_ref[...] + jnp.zeros((ts, ch), F32)
    for j in range(CONV_WIDTH):
        off = pad - (CONV_WIDTH - 1) + j
        xc = xc + cw_ref[j:j + 1, :] * xe_ref[off:off + ts, :]

    xb = xc.astype(BF16)
    r = jax.nn.sigmoid(jnp.dot(xb, wa_ref[...], preferred_element_type=F32) + ba_ref[...])
    i = jax.nn.sigmoid(jnp.dot(xb, wx_ref[...], preferred_element_type=F32) + bx_ref[...])
    nl = -lam_ref[...]
    softplus = jnp.maximum(nl, 0.0) + jnp.log(1.0 + jnp.exp(-jnp.abs(nl)))
    log_a = (-RG_C) * r * softplus
    a = jnp.exp(log_a)
    mult = jnp.sqrt(1.0 - jnp.exp(2.0 * log_a))
    u = mult * (i * xc)

    ng = ts // SUBLANES
    a3 = a.reshape(ng, SUBLANES, ch)
    u3 = u.reshape(ng, SUBLANES, ch)
    rid = lax.broadcasted_iota(I32, (ng, SUBLANES, ch), 1)
    for sft in (1, 2, 4):
        a_s = pltpu.roll(a3, sft, 1)
        u_s = pltpu.roll(u3, sft, 1)
        keep = rid >= sft
        u3 = jnp.where(keep, a3 * u_s + u3, u3)
        a3 = jnp.where(keep, a3 * a_s, a3)
    a_ref[...] = a3.reshape(ts, ch)
    u_ref[...] = u3.reshape(ts, ch)

    inner = 8

    def body(gi, hprev):
        for jj in range(inner):
            base = pl.multiple_of((gi * inner + jj) * SUBLANES, SUBLANES)
            hcur = u_ref[pl.ds(base, SUBLANES), :] + a_ref[pl.ds(base, SUBLANES), :] * hprev
            u_ref[pl.ds(base, SUBLANES), :] = hcur
            hprev = hcur[SUBLANES - 1:SUBLANES, :]
        return hprev

    h_ref[...] = lax.fori_loop(0, ng // inner, body, h_ref[...])
    o_ref[...] = (u_ref[...] * _gelu_tanh(gr_ref[...])).astype(o_ref.dtype)


def _block_diag(w):
    nb, hin, hout = w.shape
    eye = jnp.eye(nb, dtype=w.dtype)
    return (eye[:, None, :, None] * w[:, :, None, :]).reshape(nb * hin, nb * hout)


def _rglru(xr, gr, conv_w, conv_b, w_a, b_a, w_x, b_x, lam):
    bsz, s, ch = xr.shape
    ts = RGLRU_ROWS
    blk = pl.BlockSpec((None, ts, ch), lambda b, t: (b, t, 0))
    full = lambda r, c: pl.BlockSpec((r, c), lambda b, t: (0, 0))
    return pl.pallas_call(
        _rglru_kernel,
        grid=(bsz, s // ts),
        in_specs=[blk, blk, full(CONV_WIDTH, ch), full(1, ch), full(ch, ch), full(1, ch),
                  full(ch, ch), full(1, ch), full(1, ch)],
        out_specs=blk,
        out_shape=jax.ShapeDtypeStruct((bsz, s, ch), BF16),
        scratch_shapes=[pltpu.VMEM((ts + SUBLANES, ch), F32),
                        pltpu.VMEM((ts, ch), F32),
                        pltpu.VMEM((ts, ch), F32),
                        pltpu.VMEM((SUBLANES, ch), F32),
                        pltpu.VMEM((1, ch), F32)],
        compiler_params=_params(("arbitrary", "arbitrary")),
        name="rglru",
    )(xr, gr, conv_w, conv_b.reshape(1, ch), _block_diag(w_a).astype(BF16), b_a.reshape(1, ch),
      _block_diag(w_x).astype(BF16), b_x.reshape(1, ch), lam.reshape(1, ch))


def _attn_kernel(q_ref, k_ref, v_ref, o_ref, f_ref, qs_ref, kp_ref, vp_ref, op_ref, lp_ref):
    s = q_ref.shape[0]
    blk = ATTN_BLOCK
    n_blocks = s // blk
    unroll = 8

    lane = lax.broadcasted_iota(I32, (blk, LANES), 1)
    head0 = lane < HEAD_DIM
    head0_wide = lax.broadcasted_iota(I32, (blk, 2 * LANES), 1) % LANES < HEAD_DIM
    qi = lax.broadcasted_iota(I32, (2 * blk, blk), 0) % blk
    kj = lax.broadcasted_iota(I32, (2 * blk, blk), 1)
    prev_ok = kj >= qi
    cur_ok = kj <= qi
    dims = (((1,), (1,)), ((), ()))

    def block_rows(j, dil):
        per_res = (s // dil) // blk
        start = j // per_res + (j % per_res) * (dil * blk)
        if dil == 1:
            return pl.ds(pl.multiple_of(start, blk), blk)
        return pl.ds(start, blk, stride=dil)

    def permute(src_ref, store):
        f_ref[...] = src_ref[...].astype(F32)
        for p, (_, dil) in enumerate(DILATED_PATTERNS):
            def body(j, c, p=p, dil=dil):
                store(p, j, f_ref[block_rows(j, dil), :])
                return c
            lax.fori_loop(0, n_blocks, body, 0)

    def store_q(p, j, x):
        x = x * (HEAD_DIM ** -0.5 * LOG2_E)
        base = pl.multiple_of(j * (2 * blk), 2 * blk)
        qs_ref[p, pl.ds(base, blk), :] = jnp.where(head0, x, 0.0).astype(BF16)
        qs_ref[p, pl.ds(base + blk, blk), :] = jnp.where(head0, 0.0, x).astype(BF16)

    def store_kv(dst_ref):
        def store(p, j, x):
            dst_ref[p, pl.ds(pl.multiple_of((j + 1) * blk, blk), blk), 0:LANES] = x.astype(BF16)
        return store

    for p in range(len(DILATED_PATTERNS)):
        kp_ref[p, 0:blk, :] = jnp.zeros((blk, LANES), BF16)
        vp_ref[p, 0:blk, 0:LANES] = jnp.zeros((blk, LANES), BF16)
        vp_ref[p, :, LANES:] = jnp.ones((s + blk, LANES), BF16)
    permute(q_ref, store_q)
    permute(k_ref, store_kv(kp_ref))
    permute(v_ref, store_kv(vp_ref))

    for p, (window, dil) in enumerate(DILATED_PATTERNS):
        assert window // dil == blk
        per_res = (s // dil) // blk

        def body(it, carry, p=p, dil=dil, per_res=per_res):
            for u in range(unroll):
                j = it * unroll + u
                no_prev = jnp.where(j % per_res > 0, 0.0, NEG_INF)
                qs = qs_ref[p, pl.ds(pl.multiple_of(j * (2 * blk), 2 * blk), 2 * blk), :]
                kv_rows = pl.ds(pl.multiple_of(j * blk, blk), 2 * blk)
                sc = lax.dot_general(qs, kp_ref[p, kv_rows, :], dims, preferred_element_type=F32)
                s_prev = jnp.where(prev_ok, sc[:, :blk], NEG_INF) + no_prev
                s_cur = jnp.where(cur_ok, sc[:, blk:], NEG_INF)
                m = jnp.max(jnp.maximum(s_prev, s_cur), axis=-1, keepdims=True)
                pcat = jnp.concatenate([jnp.exp2(s_prev - m), jnp.exp2(s_cur - m)], axis=-1).astype(BF16)
                rl = jnp.dot(pcat, vp_ref[p, kv_rows, :], preferred_element_type=F32)
                rl = jnp.where(head0_wide, rl[:blk], rl[blk:])
                l = rl[:, LANES:]
                rows = block_rows(j, dil)
                op_ref[p, rows, :] = rl[:, :LANES] / l
                lp_ref[p, rows, :] = jnp.where(head0, m[:blk], m[blk:]) + jnp.log2(l)
            return carry

        lax.fori_loop(0, n_blocks // unroll, body, 0)

    chunk = 512
    for c0 in range(0, s, chunk):
        sl = pl.ds(c0, chunk)
        l0, l1, l2 = lp_ref[0, sl, :], lp_ref[1, sl, :], lp_ref[2, sl, :]
        m = jnp.maximum(jnp.maximum(l0, l1), l2)
        w0, w1, w2 = jnp.exp2(l0 - m), jnp.exp2(l1 - m), jnp.exp2(l2 - m)
        num = w0 * op_ref[0, sl, :] + w1 * op_ref[1, sl, :] + w2 * op_ref[2, sl, :]
        o_ref[sl, :] = (num / (w0 + w1 + w2)).astype(o_ref.dtype)


def _attention(q, k, v):
    bsz, s, da = q.shape
    blk = pl.BlockSpec((None, s, LANES), lambda b, h: (b, 0, h))
    n_pat = len(DILATED_PATTERNS)
    return pl.pallas_call(
        _attn_kernel,
        grid=(bsz, da // LANES),
        in_specs=[blk, blk, blk],
        out_specs=blk,
        out_shape=jax.ShapeDtypeStruct((bsz, s, da), BF16),
        scratch_shapes=[pltpu.VMEM((s, LANES), F32),
                        pltpu.VMEM((n_pat, 2 * s, LANES), BF16),
                        pltpu.VMEM((n_pat, s + ATTN_BLOCK, LANES), BF16),
                        pltpu.VMEM((n_pat, s + ATTN_BLOCK, 2 * LANES), BF16),
                        pltpu.VMEM((n_pat, s, LANES), F32),
                        pltpu.VMEM((n_pat, s, LANES), F32)],
        compiler_params=_params(("arbitrary", "arbitrary")),
        name="attn",
    )(q, k, v)


def _outproj_kernel(attn_ref, rec_ref, w_ref, x_ref, gm_ref, g_ref, sh_ref, sc_ref, wr_ref, br_ref,
                    x1_ref, h2_ref, idx_ref, gate_ref, rank_ref, cnt_ref, base_ref):
    tm = x_ref.shape[0]
    da = attn_ref.shape[1]

    @pl.when(jnp.logical_and(pl.program_id(0) == 0, pl.program_id(1) == 0))
    def _():
        base_ref[...] = jnp.zeros_like(base_ref)

    y = (jnp.dot(attn_ref[...], w_ref[0:da, :], preferred_element_type=F32)
         + jnp.dot(rec_ref[...], w_ref[da:, :], preferred_element_type=F32))
    x1 = x_ref[...] + gm_ref[...] * y
    x1_ref[...] = x1
    h2 = _rms_modulate(x1, g_ref[...], sh_ref[...], sc_ref[...])
    for c in range(ROW_CHUNKS):
        h2_ref[pl.ds(c, tm, stride=ROW_CHUNKS), :] = h2[:, c * LANES:(c + 1) * LANES]

    work = lax.dot_general(wr_ref[...], h2, (((1,), (1,)), ((), ())), precision=lax.Precision.HIGHEST,
                           preferred_element_type=F32) + br_ref[...]
    n_exp = work.shape[0]
    eid = lax.broadcasted_iota(I32, (n_exp, tm), 0)
    vals, hots = [], []
    for kk in range(TOP_K):
        m = jnp.max(work, axis=0, keepdims=True)
        sel = jnp.min(jnp.where(work == m, eid, n_exp), axis=0, keepdims=True)
        hot = eid == sel
        idx_ref[kk:kk + 1, :] = sel
        vals.append(m)
        hots.append(hot)
        work = jnp.where(hot, -jnp.inf, work)
    exps = [jnp.exp(v - vals[0]) for v in vals]
    den = exps[0] + exps[1] + exps[2] + exps[3]
    for kk in range(TOP_K):
        gate_ref[kk:kk + 1, :] = exps[kk] / den

    cnt = jnp.zeros((n_exp, tm), F32)
    for hot in hots:
        cnt = cnt + hot.astype(F32)
    tr = lax.broadcasted_iota(I32, (tm, tm), 0)
    tc = lax.broadcasted_iota(I32, (tm, tm), 1)
    before = jnp.where(tr < tc, 1.0, 0.0).astype(BF16)
    prior = jnp.dot(cnt.astype(BF16), before, preferred_element_type=F32) + base_ref[:, 0:1]
    for kk in range(TOP_K):
        rank_ref[kk:kk + 1, :] = jnp.sum(jnp.where(hots[kk], prior, 0.0), axis=0,
                                         keepdims=True).astype(I32)
    base_ref[...] = base_ref[...] + jnp.sum(cnt, axis=1, keepdims=True)
    cnt_ref[...] = base_ref[...].astype(I32)


def _outproj(attn, rec, w_bf, x, gate_m, g_ffn, shift_f, scale_f, w_router, b_router):
    bsz, s, d = x.shape
    da = attn.shape[-1]
    n_exp = w_router.shape[1]
    tm = OUTPROJ_ROWS
    nt = s // tm
    t = bsz * s
    row = pl.BlockSpec((None, 1, d), lambda b, i: (b, 0, 0))
    small = pl.BlockSpec((TOP_K, tm), lambda b, i: (0, b * nt + i))
    return pl.pallas_call(
        _outproj_kernel,
        grid=(bsz, nt),
        in_specs=[pl.BlockSpec((None, tm, da), lambda b, i: (b, i, 0)),
                  pl.BlockSpec((None, tm, da), lambda b, i: (b, i, 0)),
                  pl.BlockSpec((d, d), lambda b, i: (0, 0)),
                  pl.BlockSpec((None, tm, d), lambda b, i: (b, i, 0)),
                  row,
                  pl.BlockSpec((1, d), lambda b, i: (0, 0)),
                  row, row,
                  pl.BlockSpec((n_exp, d), lambda b, i: (0, 0)),
                  pl.BlockSpec((n_exp, 1), lambda b, i: (0, 0))],
        out_specs=[pl.BlockSpec((None, tm, d), lambda b, i: (b, i, 0)),
                   pl.BlockSpec((tm * ROW_CHUNKS, LANES), lambda b, i: (b * nt + i, 0)),
                   small, small, small,
                   pl.BlockSpec((n_exp, LANES), lambda b, i: (0, 0))],
        out_shape=[jax.ShapeDtypeStruct((bsz, s, d), F32),
                   jax.ShapeDtypeStruct((t * ROW_CHUNKS, LANES), F32),
                   jax.ShapeDtypeStruct((TOP_K, t), I32),
                   jax.ShapeDtypeStruct((TOP_K, t), F32),
                   jax.ShapeDtypeStruct((TOP_K, t), I32),
                   jax.ShapeDtypeStruct((n_exp, LANES), I32)],
        scratch_shapes=[pltpu.VMEM((n_exp, LANES), F32)],
        compiler_params=_params(("arbitrary", "arbitrary")),
        name="outproj_router",
    )(attn, rec, w_bf, x, gate_m, g_ffn.reshape(1, d), shift_f, scale_f, w_router.T,
      b_router.reshape(n_exp, 1))


def _row_slice(ref, row):
    return ref.at[pl.ds(pl.multiple_of(row * ROW_CHUNKS, ROW_CHUNKS), ROW_CHUNKS), :]


def _slots_kernel(dest_ref, fill_ref, slot_ref):
    n_assign = dest_ref.shape[0]
    n_fill = fill_ref.shape[0]
    lead = EXPERT_ROWS
    unroll = 8

    def lead_body(i, c):
        slot_ref[i] = n_assign + n_fill + i
        return c

    lax.fori_loop(0, lead, lead_body, 0)

    def assign_body(i, c):
        for u in range(unroll):
            slot_ref[lead + dest_ref[i * unroll + u]] = i * unroll + u
        return c

    lax.fori_loop(0, n_assign // unroll, assign_body, 0)

    def fill_body(i, c):
        for u in range(unroll):
            slot_ref[lead + fill_ref[i * unroll + u]] = n_assign + i * unroll + u
        return c

    lax.fori_loop(0, n_fill // unroll, fill_body, 0)


def _slots(dest_flat, fill):
    smem = pl.BlockSpec(memory_space=pltpu.SMEM)
    n = EXPERT_ROWS + dest_flat.shape[0] + fill.shape[0]
    return pl.pallas_call(
        _slots_kernel,
        in_specs=[smem, smem],
        out_specs=smem,
        out_shape=jax.ShapeDtypeStruct((n,), I32),
        name="slots",
    )(dest_flat, fill)


GATHER_BUFS = 3
SCATTER_BUFS = 2


def _experts_kernel(bexp_ref, nused_ref, slot_ref, h2_ref, wgu_ref, bgu_ref, wd_ref, bd_ref, y_ref,
                    xg_ref, ys_ref, xs_ref, wgu_bf, wd_bf, gsem, ssem):
    b = pl.program_id(0)
    nb = pl.num_programs(0)
    rows = xs_ref.shape[0]
    f = wd_ref.shape[0]
    n_tok = h2_ref.shape[0] // ROW_CHUNKS
    assert n_tok & (n_tok - 1) == 0
    used = b < nused_ref[0]
    changed = jnp.logical_or(b == 0, bexp_ref[b] != bexp_ref[jnp.maximum(b - 1, 0)])

    def buf_rows(ref, buf, i):
        if isinstance(i, int):
            return ref.at[buf, pl.ds(i * ROW_CHUNKS, ROW_CHUNKS), :]
        return ref.at[buf, pl.ds(pl.multiple_of(i * ROW_CHUNKS, ROW_CHUNKS), ROW_CHUNKS), :]

    def fetch_row(base, buf, i):
        tok = slot_ref[base + i] & (n_tok - 1)
        return pltpu.make_async_copy(_row_slice(h2_ref, tok), buf_rows(xg_ref, buf, i), gsem.at[buf])

    def send_row(base, buf, i):
        return pltpu.make_async_copy(buf_rows(ys_ref, buf, i), _row_slice(y_ref, slot_ref[base + i]),
                                     ssem.at[buf])

    def start_rows(make, blk, buf, rolled=False):
        base = (blk + 1) * rows
        if rolled:
            def body(i, c):
                make(base, buf, i).start()
                return c
            lax.fori_loop(0, rows, body, 0)
        else:
            for i in range(rows):
                make(base, buf, i).start()

    def wait_rows(ref, buf, sem):
        pltpu.make_async_copy(ref.at[buf], ref.at[buf], sem.at[buf]).wait()

    @pl.when(b == 0)
    def _():
        ys_ref[...] = jnp.zeros_like(ys_ref)
        start_rows(fetch_row, 0, 0, rolled=True)
        start_rows(fetch_row, 1, 1, rolled=True)

    gbuf = b % GATHER_BUFS
    sbuf = b % SCATTER_BUFS
    wait_rows(xg_ref, gbuf, gsem)

    @pl.when(b >= 1)
    def _():
        wait_rows(ys_ref, sbuf, ssem)

    def issue_neighbours():
        start_rows(fetch_row, jnp.minimum(b + 2, nb - 1), (b + 2) % GATHER_BUFS)
        start_rows(send_row, b - 1, (b + 1) % SCATTER_BUFS)

    @pl.when(jnp.logical_and(used, changed))
    def _():
        wgu_bf[...] = wgu_ref[...].astype(BF16)
        wd_bf[...] = wd_ref[...].astype(BF16)

    @pl.when(used)
    def _():
        issue_neighbours()
        for c in range(ROW_CHUNKS):
            xs_ref[:, c * LANES:(c + 1) * LANES] = (
                xg_ref[gbuf, pl.ds(c, rows, stride=ROW_CHUNKS), :].astype(BF16))
        gu = jnp.dot(xs_ref[...], wgu_bf[...], preferred_element_type=F32) + bgu_ref[...]
        gate = jnp.minimum(gu[:, :f], SWIGLU_LIMIT)
        up = jnp.clip(gu[:, f:], -SWIGLU_LIMIT, SWIGLU_LIMIT)
        glu = gate * jax.nn.sigmoid(gate * SWIGLU_ALPHA)
        act = ((up + 1.0) * glu).astype(BF16)
        y = jnp.dot(act, wd_bf[...], preferred_element_type=F32) + bd_ref[...]
        for c in range(ROW_CHUNKS):
            ys_ref[sbuf, pl.ds(c, rows, stride=ROW_CHUNKS), :] = y[:, c * LANES:(c + 1) * LANES]

    @pl.when(jnp.logical_not(used))
    def _():
        issue_neighbours()
        ys_ref[sbuf] = jnp.zeros(ys_ref.shape[1:], F32)

    @pl.when(b == nb - 1)
    def _():
        start_rows(send_row, b, sbuf, rolled=True)
        wait_rows(ys_ref, (b + 1) % SCATTER_BUFS, ssem)
        wait_rows(ys_ref, sbuf, ssem)
        wait_rows(xg_ref, (b + 1) % GATHER_BUFS, gsem)
        wait_rows(xg_ref, (b + 2) % GATHER_BUFS, gsem)


def _experts(h2rows, slots, blk_exp, n_used, w_gate_up, b_gate_up, w_down, b_down):
    n_exp, d, f2 = w_gate_up.shape
    f = w_down.shape[1]
    rows = EXPERT_ROWS
    n_blocks = slots.shape[0] // rows - 1
    buf_shape = (rows * ROW_CHUNKS, LANES)
    return pl.pallas_call(
        _experts_kernel,
        grid_spec=pltpu.PrefetchScalarGridSpec(
            num_scalar_prefetch=3,
            grid=(n_blocks,),
            in_specs=[pl.BlockSpec(memory_space=pl.ANY),
                      pl.BlockSpec((None, d, f2), lambda b, e, n, s: (e[b], 0, 0)),
                      pl.BlockSpec((None, 1, f2), lambda b, e, n, s: (e[b], 0, 0)),
                      pl.BlockSpec((None, f, d), lambda b, e, n, s: (e[b], 0, 0)),
                      pl.BlockSpec((None, 1, d), lambda b, e, n, s: (e[b], 0, 0))],
            out_specs=pl.BlockSpec(memory_space=pl.ANY),
            scratch_shapes=[pltpu.VMEM((GATHER_BUFS,) + buf_shape, F32),
                            pltpu.VMEM((SCATTER_BUFS,) + buf_shape, F32),
                            pltpu.VMEM((rows, d), BF16),
                            pltpu.VMEM((d, f2), BF16),
                            pltpu.VMEM((f, d), BF16),
                            pltpu.SemaphoreType.DMA((GATHER_BUFS,)),
                            pltpu.SemaphoreType.DMA((SCATTER_BUFS,))]),
        out_shape=jax.ShapeDtypeStruct((slots.shape[0] * ROW_CHUNKS, LANES), F32),
        compiler_params=_params(("arbitrary",), 56 * 1024 * 1024),
        name="experts",
    )(blk_exp, n_used, slots, h2rows, w_gate_up, b_gate_up.reshape(n_exp, 1, f2), w_down,
      b_down.reshape(n_exp, 1, d))


def _combine_kernel(y0_ref, y1_ref, y2_ref, y3_ref, gates_ref, x1_ref, gf_ref, g_ref, o_ref):
    tm = x1_ref.shape[0]
    gates = gates_ref[...]
    chunks = []
    for c in range(ROW_CHUNKS):
        acc = jnp.zeros((tm, LANES), F32)
        for kk, y_ref in enumerate((y0_ref, y1_ref, y2_ref, y3_ref)):
            acc = acc + y_ref[pl.ds(c, tm, stride=ROW_CHUNKS), :] * gates[:, kk:kk + 1]
        chunks.append(acc)
    moe = jnp.concatenate(chunks, axis=-1)
    x2 = x1_ref[...] + gf_ref[...] * moe
    o_ref[...] = (x2 * lax.rsqrt(jnp.mean(x2 * x2, axis=-1, keepdims=True) + NORM_EPS)) * g_ref[...]


def _combine(y, gates_t, x1, gate_f, g_final):
    bsz, s, d = x1.shape
    tm = COMBINE_ROWS
    nt = bsz * s // tm
    per_b = s // tm
    planes = [pl.BlockSpec((tm * ROW_CHUNKS, LANES), lambda i, kk=kk: (kk * nt + i, 0))
              for kk in range(TOP_K)]
    return pl.pallas_call(
        _combine_kernel,
        grid=(nt,),
        in_specs=planes + [
                  pl.BlockSpec((tm, TOP_K), lambda i: (i, 0)),
                  pl.BlockSpec((None, tm, d), lambda i: (i // per_b, i % per_b, 0)),
                  pl.BlockSpec((None, 1, d), lambda i: (i // per_b, 0, 0)),
                  pl.BlockSpec((1, d), lambda i: (0, 0))],
        out_specs=pl.BlockSpec((None, tm, d), lambda i: (i // per_b, i % per_b, 0)),
        out_shape=jax.ShapeDtypeStruct((bsz, s, d), F32),
        compiler_params=_params(("arbitrary",)),
        name="combine",
    )(y, y, y, y, gates_t, x1, gate_f, g_final.reshape(1, d))


def _routing_plan(idx, rank, counts, n_rows):
    rows = EXPERT_ROWS
    experts = jnp.arange(N_EXPERTS, dtype=I32)

    def lookup(table, e):
        return jnp.sum(jnp.where(e[..., None] == experts, table, 0), axis=-1)

    def segment_of(ends, pos):
        return jnp.minimum(jnp.sum((pos[..., None] >= ends).astype(I32), axis=-1), N_EXPERTS - 1)

    padded = ((counts + rows - 1) // rows) * rows
    pend = jnp.cumsum(padded)
    pstart = pend - padded
    dest = lookup(pstart, idx) + rank
    n_blocks = n_rows // rows
    blk_exp = segment_of(pend, jnp.arange(n_blocks, dtype=I32) * rows)
    n_used = (pend[-1] // rows).astype(I32).reshape(1)
    pad = padded - counts
    pad_end = jnp.cumsum(pad)
    j = jnp.arange(n_rows - idx.size, dtype=I32)
    pe = segment_of(pad_end, j)
    in_pad = lookup(pstart + counts - (pad_end - pad), pe) + j
    fill = jnp.where(j < pad_end[-1], in_pad, pend[-1] + (j - pad_end[-1])).astype(I32)
    return dest.astype(I32), blk_exp.astype(I32), n_used, fill


def kernel(x, c, w_ada, b_ada, g_mix, w_in, conv_w, conv_b, w_rg_a, b_rg_a, w_rg_x, b_rg_x, lam, w_out,
           g_ffn, w_router, b_router, w_gate_up, b_gate_up, w_down, b_down, g_final):
    bsz, s, d = x.shape
    t = bsz * s
    depth = w_ada.shape[0]
    assert depth == 1, "the combine kernel applies the final norm; one layer only"
    for l in range(depth):
        mod = _ada(c, w_ada[l], b_ada[l]).reshape(bsz, 6, 1, d)
        shift_m, scale_m, gate_m, shift_f, scale_f, gate_f = (mod[:, j] for j in range(6))
        q, k, v, xr, gr = _inproj(x, g_mix[l], shift_m, scale_m, w_in[l].astype(BF16))
        attn = _attention(q, k, v)
        rec = _rglru(xr, gr, conv_w[l], conv_b[l], w_rg_a[l], b_rg_a[l], w_rg_x[l], b_rg_x[l], lam[l])
        x1, h2rows, idx, gates, rank, cnt = _outproj(
            attn, rec, w_out[l].astype(BF16), x, gate_m, g_ffn[l], shift_f, scale_f, w_router[l], b_router[l])
        n_rows = t * TOP_K + N_EXPERTS * EXPERT_ROWS
        dest, blk_exp, n_used, fill = _routing_plan(idx, rank, cnt[:, 0], n_rows)
        slots = _slots(dest.reshape(-1), fill)
        y = _experts(h2rows, slots, blk_exp, n_used, w_gate_up[l], b_gate_up[l], w_down[l], b_down[l])
        x = _combine(y, gates.T, x1, gate_f, g_final)
    return x
```

```python
import functools

import jax
import jax.numpy as jnp
from jax import lax
from jax.experimental import pallas as pl
from jax.experimental.pallas import tpu as pltpu

F32 = jnp.float32
BF16 = jnp.bfloat16
I32 = jnp.int32

HEAD_DIM = 64
DILATED_PATTERNS = ((128, 1), (512, 4), (2048, 16))
REC_BLOCKS = 8
CONV_WIDTH = 4
RG_C = 8.0
N_EXPERTS = 32
TOP_K = 4
SWIGLU_LIMIT = 7.0
SWIGLU_ALPHA = 1.702
NORM_EPS = 1e-6
NEG_INF = -1e30
LOG2_E = 1.4426950408889634

LANES = 128
SUBLANES = 8
ROW_CHUNKS = 8
VMEM_LIMIT = 48 * 1024 * 1024

INPROJ_ROWS = 512
RGLRU_ROWS = 512
OUTPROJ_ROWS = 512
ATTN_BLOCK = 128
EXPERT_ROWS = 256
DISPATCH_ROWS = 256
COMBINE_ROWS = 256


def _params(sem, vmem=VMEM_LIMIT):
    return pltpu.CompilerParams(dimension_semantics=sem, vmem_limit_bytes=vmem)


def _ada_kernel(c_ref, w_ref, b_ref, o_ref):
    o_ref[...] = jnp.dot(c_ref[...], w_ref[...], precision=lax.Precision.HIGHEST,
                         preferred_element_type=F32) + b_ref[...]


def _ada(c, w, b):
    bsz, d = c.shape
    n = w.shape[1]
    return pl.pallas_call(
        _ada_kernel,
        grid=(n // d,),
        in_specs=[pl.BlockSpec((bsz, d), lambda j: (0, 0)),
                  pl.BlockSpec((d, d), lambda j: (0, j)),
                  pl.BlockSpec((1, d), lambda j: (0, j))],
        out_specs=pl.BlockSpec((bsz, d), lambda j: (0, j)),
        out_shape=jax.ShapeDtypeStruct((bsz, n), F32),
        compiler_params=_params(("arbitrary",)),
        name="ada",
    )(c, w, b.reshape(1, n))


def _rms_modulate(x, g, shift, scale):
    y = x * lax.rsqrt(jnp.mean(x * x, axis=-1, keepdims=True) + NORM_EPS)
    return (y * g) * (1.0 + scale) + shift


def _inproj_kernel(x_ref, g_ref, sh_ref, sc_ref, w_ref, q_ref, k_ref, v_ref, xr_ref, gr_ref):
    h = _rms_modulate(x_ref[...], g_ref[...], sh_ref[...], sc_ref[...]).astype(BF16)
    outs = (q_ref, k_ref, v_ref, xr_ref, gr_ref)
    width = q_ref.shape[-1]
    for j, o_ref in enumerate(outs):
        z = jnp.dot(h, w_ref[:, j * width:(j + 1) * width], preferred_element_type=F32)
        o_ref[...] = z.astype(o_ref.dtype)


def _inproj(x, g, shift, scale, w_bf):
    bsz, s, d = x.shape
    n = w_bf.shape[1]
    width = n // 5
    tm = INPROJ_ROWS
    row = pl.BlockSpec((None, 1, d), lambda b, i: (b, 0, 0))
    out_blk = pl.BlockSpec((None, tm, width), lambda b, i: (b, i, 0))
    shp = lambda dt: jax.ShapeDtypeStruct((bsz, s, width), dt)
    return pl.pallas_call(
        _inproj_kernel,
        grid=(bsz, s // tm),
        in_specs=[pl.BlockSpec((None, tm, d), lambda b, i: (b, i, 0)),
                  pl.BlockSpec((1, d), lambda b, i: (0, 0)),
                  row, row,
                  pl.BlockSpec((d, n), lambda b, i: (0, 0))],
        out_specs=[out_blk] * 5,
        out_shape=[shp(BF16), shp(BF16), shp(BF16), shp(F32), shp(F32)],
        compiler_params=_params(("arbitrary", "arbitrary")),
        name="inproj",
    )(x, g.reshape(1, d), shift, scale, w_bf)


def _gelu_tanh(x):
    return 0.5 * x * (1.0 + jnp.tanh(0.7978845608028654 * (x + 0.044715 * (x * x * x))))


def _rglru_kernel(xr_ref, gr_ref, cw_ref, cb_ref, wa_ref, ba_ref, wx_ref, bx_ref, lam_ref,
                  o_ref, xe_ref, a_ref, u_ref, tail_ref, h_ref):
    ts, ch = xr_ref.shape
    pad = SUBLANES

    @pl.when(pl.program_id(1) == 0)
    def _():
        tail_ref[...] = jnp.zeros_like(tail_ref)
        h_ref[...] = jnp.zeros_like(h_ref)

    x = xr_ref[...]
    xe_ref[0:pad, :] = tail_ref[...]
    xe_ref[pad:pad + ts, :] = x
    tail_ref[...] = x[ts - pad:ts, :]
    xc = cb_ref[...] + jnp.zeros((ts, ch), F32)
    for j in range(CONV_WIDTH):
        off = pad - (CONV_WIDTH - 1) + j
        xc = xc + cw_ref[j:j + 1, :] * xe_ref[off:off + ts, :]

    xb = xc.astype(BF16)
    r = jax.nn.sigmoid(jnp.dot(xb, wa_ref[...], preferred_element_type=F32) + ba_ref[...])
    i = jax.nn.sigmoid(jnp.dot(xb, wx_ref[...], preferred_element_type=F32) + bx_ref[...])
    nl = -lam_ref[...]
    softplus = jnp.maximum(nl, 0.0) + jnp.log(1.0 + jnp.exp(-jnp.abs(nl)))
    log_a = (-RG_C) * r * softplus
    a = jnp.exp(log_a)
    mult = jnp.sqrt(1.0 - jnp.exp(2.0 * log_a))
    u = mult * (i * xc)

    ng = ts // SUBLANES
    a3 = a.reshape(ng, SUBLANES, ch)
    u3 = u.reshape(ng, SUBLANES, ch)
    rid = lax.broadcasted_iota(I32, (ng, SUBLANES, ch), 1)
    for sft in (1, 2, 4):
        a_s = pltpu.roll(a3, sft, 1)
        u_s = pltpu.roll(u3, sft, 1)
        keep = rid >= sft
        u3 = jnp.where(keep, a3 * u_s + u3, u3)
        a3 = jnp.where(keep, a3 * a_s, a3)
    a_ref[...] = a3.reshape(ts, ch)
    u_ref[...] = u3.reshape(ts, ch)

    inner = 8

    def body(gi, hprev):
        for jj in range(inner):
            base = pl.multiple_of((gi * inner + jj) * SUBLANES, SUBLANES)
            hcur = u_ref[pl.ds(base, SUBLANES), :] + a_ref[pl.ds(base, SUBLANES), :] * hprev
            u_ref[pl.ds(base, SUBLANES), :] = hcur
            hprev = hcur[SUBLANES - 1:SUBLANES, :]
        return hprev

    h_ref[...] = lax.fori_loop(0, ng // inner, body, h_ref[...])
    o_ref[...] = (u_ref[...] * _gelu_tanh(gr_ref[...])).astype(o_ref.dtype)


def _block_diag(w):
    nb, hin, hout = w.shape
    eye = jnp.eye(nb, dtype=w.dtype)
    return (eye[:, None, :, None] * w[:, :, None, :]).reshape(nb * hin, nb * hout)


def _rglru(xr, gr, conv_w, conv_b, w_a, b_a, w_x, b_x, lam):
    bsz, s, ch = xr.shape
    ts = RGLRU_ROWS
    blk = pl.BlockSpec((None, ts, ch), lambda b, t: (b, t, 0))
    full = lambda r, c: pl.BlockSpec((r, c), lambda b, t: (0, 0))
    return pl.pallas_call(
        _rglru_kernel,
        grid=(bsz, s // ts),
        in_specs=[blk, blk, full(CONV_WIDTH, ch), full(1, ch), full(ch, ch), full(1, ch),
                  full(ch, ch), full(1, ch), full(1, ch)],
        out_specs=blk,
        out_shape=jax.ShapeDtypeStruct((bsz, s, ch), BF16),
        scratch_shapes=[pltpu.VMEM((ts + SUBLANES, ch), F32),
                        pltpu.VMEM((ts, ch), F32),
                        pltpu.VMEM((ts, ch), F32),
                        pltpu.VMEM((SUBLANES, ch), F32),
                        pltpu.VMEM((1, ch), F32)],
        compiler_params=_params(("arbitrary", "arbitrary")),
        name="rglru",
    )(xr, gr, conv_w, conv_b.reshape(1, ch), _block_diag(w_a).astype(BF16), b_a.reshape(1, ch),
      _block_diag(w_x).astype(BF16), b_x.reshape(1, ch), lam.reshape(1, ch))


def _attn_kernel(q_ref, k_ref, v_ref, o_ref, f_ref, qs_ref, kp_ref, vp_ref, op_ref, lp_ref):
    s = q_ref.shape[0]
    blk = ATTN_BLOCK
    n_blocks = s // blk
    unroll = 8

    lane = lax.broadcasted_iota(I32, (blk, LANES), 1)
    head0 = lane < HEAD_DIM
    head0_wide = lax.broadcasted_iota(I32, (blk, 2 * LANES), 1) % LANES < HEAD_DIM
    qi = lax.broadcasted_iota(I32, (2 * blk, blk), 0) % blk
    kj = lax.broadcasted_iota(I32, (2 * blk, blk), 1)
    prev_ok = kj >= qi
    cur_ok = kj <= qi
    dims = (((1,), (1,)), ((), ()))

    def block_rows(j, dil):
        per_res = (s // dil) // blk
        start = j // per_res + (j % per_res) * (dil * blk)
        if dil == 1:
            return pl.ds(pl.multiple_of(start, blk), blk)
        return pl.ds(start, blk, stride=dil)

    def permute(src_ref, store):
        f_ref[...] = src_ref[...].astype(F32)
        for p, (_, dil) in enumerate(DILATED_PATTERNS):
            def body(j, c, p=p, dil=dil):
                store(p, j, f_ref[block_rows(j, dil), :])
                return c
            lax.fori_loop(0, n_blocks, body, 0)

    def store_q(p, j, x):
        x = x * (HEAD_DIM ** -0.5 * LOG2_E)
        base = pl.multiple_of(j * (2 * blk), 2 * blk)
        qs_ref[p, pl.ds(base, blk), :] = jnp.where(head0, x, 0.0).astype(BF16)
        qs_ref[p, pl.ds(base + blk, blk), :] = jnp.where(head0, 0.0, x).astype(BF16)

    def store_kv(dst_ref):
        def store(p, j, x):
            dst_ref[p, pl.ds(pl.multiple_of((j + 1) * blk, blk), blk), 0:LANES] = x.astype(BF16)
        return store

    for p in range(len(DILATED_PATTERNS)):
        kp_ref[p, 0:blk, :] = jnp.zeros((blk, LANES), BF16)
        vp_ref[p, 0:blk, 0:LANES] = jnp.zeros((blk, LANES), BF16)
        vp_ref[p, :, LANES:] = jnp.ones((s + blk, LANES), BF16)
    permute(q_ref, store_q)
    permute(k_ref, store_kv(kp_ref))
    permute(v_ref, store_kv(vp_ref))

    for p, (window, dil) in enumerate(DILATED_PATTERNS):
        assert window // dil == blk
        per_res = (s // dil) // blk

        def body(it, carry, p=p, dil=dil, per_res=per_res):
            for u in range(unroll):
                j = it * unroll + u
                no_prev = jnp.where(j % per_res > 0, 0.0, NEG_INF)
                qs = qs_ref[p, pl.ds(pl.multiple_of(j * (2 * blk), 2 * blk), 2 * blk), :]
                kv_rows = pl.ds(pl.multiple_of(j * blk, blk), 2 * blk)
                sc = lax.dot_general(qs, kp_ref[p, kv_rows, :], dims, preferred_element_type=F32)
                s_prev = jnp.where(prev_ok, sc[:, :blk], NEG_INF) + no_prev
                s_cur = jnp.where(cur_ok, sc[:, blk:], NEG_INF)
                m = jnp.max(jnp.maximum(s_prev, s_cur), axis=-1, keepdims=True)
                pcat = jnp.concatenate([jnp.exp2(s_prev - m), jnp.exp2(s_cur - m)], axis=-1).astype(BF16)
                rl = jnp.dot(pcat, vp_ref[p, kv_rows, :], preferred_element_type=F32)
                rl = jnp.where(head0_wide, rl[:blk], rl[blk:])
                l = rl[:, LANES:]
                rows = block_rows(j, dil)
                op_ref[p, rows, :] = rl[:, :LANES] / l
                lp_ref[p, rows, :] = jnp.where(head0, m[:blk], m[blk:]) + jnp.log2(l)
            return carry

        lax.fori_loop(0, n_blocks // unroll, body, 0)

    chunk = 512
    for c0 in range(0, s, chunk):
        sl = pl.ds(c0, chunk)
        l0, l1, l2 = lp_ref[0, sl, :], lp_ref[1, sl, :], lp_ref[2, sl, :]
        m = jnp.maximum(jnp.maximum(l0, l1), l2)
        w0, w1, w2 = jnp.exp2(l0 - m), jnp.exp2(l1 - m), jnp.exp2(l2 - m)
        num = w0 * op_ref[0, sl, :] + w1 * op_ref[1, sl, :] + w2 * op_ref[2, sl, :]
        o_ref[sl, :] = (num / (w0 + w1 + w2)).astype(o_ref.dtype)


def _attention(q, k, v):
    bsz, s, da = q.shape
    blk = pl.BlockSpec((None, s, LANES), lambda b, h: (b, 0, h))
    n_pat = len(DILATED_PATTERNS)
    return pl.pallas_call(
        _attn_kernel,
        grid=(bsz, da // LANES),
        in_specs=[blk, blk, blk],
        out_specs=blk,
        out_shape=jax.ShapeDtypeStruct((bsz, s, da), BF16),
        scratch_shapes=[pltpu.VMEM((s, LANES), F32),
                        pltpu.VMEM((n_pat, 2 * s, LANES), BF16),
                        pltpu.VMEM((n_pat, s + ATTN_BLOCK, LANES), BF16),
                        pltpu.VMEM((n_pat, s + ATTN_BLOCK, 2 * LANES), BF16),
                        pltpu.VMEM((n_pat, s, LANES), F32),
                        pltpu.VMEM((n_pat, s, LANES), F32)],
        compiler_params=_params(("arbitrary", "arbitrary")),
        name="attn",
    )(q, k, v)


def _outproj_kernel(attn_ref, rec_ref, w_ref, x_ref, gm_ref, g_ref, sh_ref, sc_ref, wr_ref, br_ref,
                    x1_ref, h2_ref, idx_ref, gate_ref, rank_ref, cnt_ref, base_ref):
    tm = x_ref.shape[0]
    da = attn_ref.shape[1]

    @pl.when(jnp.logical_and(pl.program_id(0) == 0, pl.program_id(1) == 0))
    def _():
        base_ref[...] = jnp.zeros_like(base_ref)

    y = (jnp.dot(attn_ref[...], w_ref[0:da, :], preferred_element_type=F32)
         + jnp.dot(rec_ref[...], w_ref[da:, :], preferred_element_type=F32))
    x1 = x_ref[...] + gm_ref[...] * y
    x1_ref[...] = x1
    h2 = _rms_modulate(x1, g_ref[...], sh_ref[...], sc_ref[...])
    for c in range(ROW_CHUNKS):
        h2_ref[pl.ds(c, tm, stride=ROW_CHUNKS), :] = h2[:, c * LANES:(c + 1) * LANES]

    work = lax.dot_general(wr_ref[...], h2, (((1,), (1,)), ((), ())), precision=lax.Precision.HIGHEST,
                           preferred_element_type=F32) + br_ref[...]
    n_exp = work.shape[0]
    eid = lax.broadcasted_iota(I32, (n_exp, tm), 0)
    vals, hots = [], []
    for kk in range(TOP_K):
        m = jnp.max(work, axis=0, keepdims=True)
        sel = jnp.min(jnp.where(work == m, eid, n_exp), axis=0, keepdims=True)
        hot = eid == sel
        idx_ref[kk:kk + 1, :] = sel
        vals.append(m)
        hots.append(hot)
        work = jnp.where(hot, -jnp.inf, work)
    exps = [jnp.exp(v - vals[0]) for v in vals]
    den = exps[0] + exps[1] + exps[2] + exps[3]
    for kk in range(TOP_K):
        gate_ref[kk:kk + 1, :] = exps[kk] / den

    cnt = jnp.zeros((n_exp, tm), F32)
    for hot in hots:
        cnt = cnt + hot.astype(F32)
    tr = lax.broadcasted_iota(I32, (tm, tm), 0)
    tc = lax.broadcasted_iota(I32, (tm, tm), 1)
    before = jnp.where(tr < tc, 1.0, 0.0).astype(BF16)
    prior = jnp.dot(cnt.astype(BF16), before, preferred_element_type=F32) + base_ref[:, 0:1]
    for kk in range(TOP_K):
        rank_ref[kk:kk + 1, :] = jnp.sum(jnp.where(hots[kk], prior, 0.0), axis=0,
                                         keepdims=True).astype(I32)
    base_ref[...] = base_ref[...] + jnp.sum(cnt, axis=1, keepdims=True)
    cnt_ref[...] = base_ref[...].astype(I32)


def _outproj(attn, rec, w_bf, x, gate_m, g_ffn, shift_f, scale_f, w_router, b_router):
    bsz, s, d = x.shape
    da = attn.shape[-1]
    n_exp = w_router.shape[1]
    tm = OUTPROJ_ROWS
    nt = s // tm
    t = bsz * s
    row = pl.BlockSpec((None, 1, d), lambda b, i: (b, 0, 0))
    small = pl.BlockSpec((TOP_K, tm), lambda b, i: (0, b * nt + i))
    return pl.pallas_call(
        _outproj_kernel,
        grid=(bsz, nt),
        in_specs=[pl.BlockSpec((None, tm, da), lambda b, i: (b, i, 0)),
                  pl.BlockSpec((None, tm, da), lambda b, i: (b, i, 0)),
                  pl.BlockSpec((d, d), lambda b, i: (0, 0)),
                  pl.BlockSpec((None, tm, d), lambda b, i: (b, i, 0)),
                  row,
                  pl.BlockSpec((1, d), lambda b, i: (0, 0)),
                  row, row,
                  pl.BlockSpec((n_exp, d), lambda b, i: (0, 0)),
                  pl.BlockSpec((n_exp, 1), lambda b, i: (0, 0))],
        out_specs=[pl.BlockSpec((None, tm, d), lambda b, i: (b, i, 0)),
                   pl.BlockSpec((tm * ROW_CHUNKS, LANES), lambda b, i: (b * nt + i, 0)),
                   small, small, small,
                   pl.BlockSpec((n_exp, LANES), lambda b, i: (0, 0))],
        out_shape=[jax.ShapeDtypeStruct((bsz, s, d), F32),
                   jax.ShapeDtypeStruct((t * ROW_CHUNKS, LANES), F32),
                   jax.ShapeDtypeStruct((TOP_K, t), I32),
                   jax.ShapeDtypeStruct((TOP_K, t), F32),
                   jax.ShapeDtypeStruct((TOP_K, t), I32),
                   jax.ShapeDtypeStruct((n_exp, LANES), I32)],
        scratch_shapes=[pltpu.VMEM((n_exp, LANES), F32)],
        compiler_params=_params(("arbitrary", "arbitrary")),
        name="outproj_router",
    )(attn, rec, w_bf, x, gate_m, g_ffn.reshape(1, d), shift_f, scale_f, w_router.T,
      b_router.reshape(n_exp, 1))


def _row_slice(ref, row):
    return ref.at[pl.ds(pl.multiple_of(row * ROW_CHUNKS, ROW_CHUNKS), ROW_CHUNKS), :]


def _slots_kernel(dest_ref, fill_ref, slot_ref):
    n_assign = dest_ref.shape[0]
    n_fill = fill_ref.shape[0]
    lead = EXPERT_ROWS
    unroll = 32

    def lead_body(i, c):
        slot_ref[i] = n_assign + n_fill + i
        return c

    lax.fori_loop(0, lead, lead_body, 0)

    def assign_body(i, c):
        for u in range(unroll):
            slot_ref[lead + dest_ref[i * unroll + u]] = i * unroll + u
        return c

    lax.fori_loop(0, n_assign // unroll, assign_body, 0)

    def fill_body(i, c):
        for u in range(unroll):
            slot_ref[lead + fill_ref[i * unroll + u]] = n_assign + i * unroll + u
        return c

    lax.fori_loop(0, n_fill // unroll, fill_body, 0)


def _slots(dest_flat, fill):
    smem = pl.BlockSpec(memory_space=pltpu.SMEM)
    n = EXPERT_ROWS + dest_flat.shape[0] + fill.shape[0]
    return pl.pallas_call(
        _slots_kernel,
        in_specs=[smem, smem],
        out_specs=smem,
        out_shape=jax.ShapeDtypeStruct((n,), I32),
        name="slots",
    )(dest_flat, fill)


GATHER_BUFS = 3
SCATTER_BUFS = 3
FETCH_PRIORITY = 0
SEND_PRIORITY = 1


def _experts_kernel(bexp_ref, nused_ref, slot_ref, h2_ref, wgu_ref, bgu_ref, wd_ref, bd_ref, y_ref,
                    xg_ref, ys_ref, xs_ref, wgu_bf, wd_bf, gsem, ssem):
    b = pl.program_id(0)
    nb = pl.num_programs(0)
    rows = xs_ref.shape[0]
    f = wd_ref.shape[0]
    n_tok = h2_ref.shape[0] // ROW_CHUNKS
    assert n_tok & (n_tok - 1) == 0
    used = b < nused_ref[0]
    changed = jnp.logical_or(b == 0, bexp_ref[b] != bexp_ref[jnp.maximum(b - 1, 0)])

    def buf_rows(ref, buf, i):
        if isinstance(i, int):
            return ref.at[buf, pl.ds(i * ROW_CHUNKS, ROW_CHUNKS), :]
        return ref.at[buf, pl.ds(pl.multiple_of(i * ROW_CHUNKS, ROW_CHUNKS), ROW_CHUNKS), :]

    def fetch_row(base, buf, i):
        tok = slot_ref[base + i] & (n_tok - 1)
        return pltpu.make_async_copy(_row_slice(h2_ref, tok), buf_rows(xg_ref, buf, i), gsem.at[buf])

    def send_row(base, buf, i):
        return pltpu.make_async_copy(buf_rows(ys_ref, buf, i), _row_slice(y_ref, slot_ref[base + i]),
                                     ssem.at[buf])

    def start_rows(make, blk, buf, priority, rolled=False):
        base = (blk + 1) * rows
        if rolled:
            def body(i, c):
                make(base, buf, i).start(priority=priority)
                return c
            lax.fori_loop(0, rows, body, 0)
        else:
            for i in range(rows):
                make(base, buf, i).start(priority=priority)

    def wait_rows(ref, buf, sem):
        pltpu.make_async_copy(ref.at[buf], ref.at[buf], sem.at[buf]).wait()

    @pl.when(b == 0)
    def _():
        ys_ref[...] = jnp.zeros_like(ys_ref)
        start_rows(fetch_row, 0, 0, FETCH_PRIORITY, rolled=True)
        start_rows(fetch_row, 1, 1, FETCH_PRIORITY, rolled=True)

    gbuf = b % GATHER_BUFS
    sbuf = b % SCATTER_BUFS
    wait_rows(xg_ref, gbuf, gsem)

    @pl.when(b >= 2)
    def _():
        wait_rows(ys_ref, sbuf, ssem)

    def issue_neighbours():
        start_rows(fetch_row, jnp.minimum(b + 2, nb - 1), (b + 2) % GATHER_BUFS, FETCH_PRIORITY)
        start_rows(send_row, b - 1, (b + 2) % SCATTER_BUFS, SEND_PRIORITY)

    @pl.when(jnp.logical_and(used, changed))
    def _():
        wgu_bf[...] = wgu_ref[...].astype(BF16)
        wd_bf[...] = wd_ref[...].astype(BF16)

    @pl.when(used)
    def _():
        for c in range(ROW_CHUNKS):
            xs_ref[:, c * LANES:(c + 1) * LANES] = (
                xg_ref[gbuf, pl.ds(c, rows, stride=ROW_CHUNKS), :].astype(BF16))
        issue_neighbours()
        gu = jnp.dot(xs_ref[...], wgu_bf[...], preferred_element_type=F32) + bgu_ref[...]
        gate = jnp.minimum(gu[:, :f], SWIGLU_LIMIT)
        up = jnp.clip(gu[:, f:], -SWIGLU_LIMIT, SWIGLU_LIMIT)
        glu = gate * jax.nn.sigmoid(gate * SWIGLU_ALPHA)
        act = ((up + 1.0) * glu).astype(BF16)
        y = jnp.dot(act, wd_bf[...], preferred_element_type=F32) + bd_ref[...]
        for c in range(ROW_CHUNKS):
            ys_ref[sbuf, pl.ds(c, rows, stride=ROW_CHUNKS), :] = y[:, c * LANES:(c + 1) * LANES]

    @pl.when(jnp.logical_not(used))
    def _():
        issue_neighbours()
        ys_ref[sbuf] = jnp.zeros(ys_ref.shape[1:], F32)

    @pl.when(b == nb - 1)
    def _():
        assert SCATTER_BUFS == 3 and GATHER_BUFS == 3
        start_rows(send_row, b, sbuf, SEND_PRIORITY, rolled=True)
        for k in range(SCATTER_BUFS):
            wait_rows(ys_ref, k, ssem)
        wait_rows(xg_ref, (b + 1) % GATHER_BUFS, gsem)
        wait_rows(xg_ref, (b + 2) % GATHER_BUFS, gsem)


def _experts(h2rows, slots, blk_exp, n_used, w_gate_up, b_gate_up, w_down, b_down):
    n_exp, d, f2 = w_gate_up.shape
    f = w_down.shape[1]
    rows = EXPERT_ROWS
    n_blocks = slots.shape[0] // rows - 1
    assert n_blocks >= SCATTER_BUFS
    buf_shape = (rows * ROW_CHUNKS, LANES)
    return pl.pallas_call(
        _experts_kernel,
        grid_spec=pltpu.PrefetchScalarGridSpec(
            num_scalar_prefetch=3,
            grid=(n_blocks,),
            in_specs=[pl.BlockSpec(memory_space=pl.ANY),
                      pl.BlockSpec((None, d, f2), lambda b, e, n, s: (e[b], 0, 0)),
                      pl.BlockSpec((None, 1, f2), lambda b, e, n, s: (e[b], 0, 0)),
                      pl.BlockSpec((None, f, d), lambda b, e, n, s: (e[b], 0, 0)),
                      pl.BlockSpec((None, 1, d), lambda b, e, n, s: (e[b], 0, 0))],
            out_specs=pl.BlockSpec(memory_space=pl.ANY),
            scratch_shapes=[pltpu.VMEM((GATHER_BUFS,) + buf_shape, F32),
                            pltpu.VMEM((SCATTER_BUFS,) + buf_shape, F32),
                            pltpu.VMEM((rows, d), BF16),
                            pltpu.VMEM((d, f2), BF16),
                            pltpu.VMEM((f, d), BF16),
                            pltpu.SemaphoreType.DMA((GATHER_BUFS,)),
                            pltpu.SemaphoreType.DMA((SCATTER_BUFS,))]),
        out_shape=jax.ShapeDtypeStruct((slots.shape[0] * ROW_CHUNKS, LANES), F32),
        compiler_params=_params(("arbitrary",), 56 * 1024 * 1024),
        name="experts",
    )(blk_exp, n_used, slots, h2rows, w_gate_up, b_gate_up.reshape(n_exp, 1, f2), w_down,
      b_down.reshape(n_exp, 1, d))


def _combine_kernel(y0_ref, y1_ref, y2_ref, y3_ref, gates_ref, x1_ref, gf_ref, g_ref, o_ref):
    tm = x1_ref.shape[0]
    gates = gates_ref[...]
    chunks = []
    for c in range(ROW_CHUNKS):
        acc = jnp.zeros((tm, LANES), F32)
        for kk, y_ref in enumerate((y0_ref, y1_ref, y2_ref, y3_ref)):
            acc = acc + y_ref[pl.ds(c, tm, stride=ROW_CHUNKS), :] * gates[:, kk:kk + 1]
        chunks.append(acc)
    moe = jnp.concatenate(chunks, axis=-1)
    x2 = x1_ref[...] + gf_ref[...] * moe
    o_ref[...] = (x2 * lax.rsqrt(jnp.mean(x2 * x2, axis=-1, keepdims=True) + NORM_EPS)) * g_ref[...]


def _combine(y, gates_t, x1, gate_f, g_final):
    bsz, s, d = x1.shape
    tm = COMBINE_ROWS
    nt = bsz * s // tm
    per_b = s // tm
    planes = [pl.BlockSpec((tm * ROW_CHUNKS, LANES), lambda i, kk=kk: (kk * nt + i, 0))
              for kk in range(TOP_K)]
    return pl.pallas_call(
        _combine_kernel,
        grid=(nt,),
        in_specs=planes + [
                  pl.BlockSpec((tm, TOP_K), lambda i: (i, 0)),
                  pl.BlockSpec((None, tm, d), lambda i: (i // per_b, i % per_b, 0)),
                  pl.BlockSpec((None, 1, d), lambda i: (i // per_b, 0, 0)),
                  pl.BlockSpec((1, d), lambda i: (0, 0))],
        out_specs=pl.BlockSpec((None, tm, d), lambda i: (i // per_b, i % per_b, 0)),
        out_shape=jax.ShapeDtypeStruct((bsz, s, d), F32),
        compiler_params=_params(("arbitrary",)),
        name="combine",
    )(y, y, y, y, gates_t, x1, gate_f, g_final.reshape(1, d))


def _routing_plan(idx, rank, counts, n_rows):
    rows = EXPERT_ROWS
    experts = jnp.arange(N_EXPERTS, dtype=I32)

    def lookup(table, e):
        return jnp.sum(jnp.where(e[..., None] == experts, table, 0), axis=-1)

    def segment_of(ends, pos):
        return jnp.minimum(jnp.sum((pos[..., None] >= ends).astype(I32), axis=-1), N_EXPERTS - 1)

    padded = ((counts + rows - 1) // rows) * rows
    pend = jnp.cumsum(padded)
    pstart = pend - padded
    dest = lookup(pstart, idx) + rank
    n_blocks = n_rows // rows
    blk_exp = segment_of(pend, jnp.arange(n_blocks, dtype=I32) * rows)
    n_used = (pend[-1] // rows).astype(I32).reshape(1)
    pad = padded - counts
    pad_end = jnp.cumsum(pad)
    j = jnp.arange(n_rows - idx.size, dtype=I32)
    pe = segment_of(pad_end, j)
    in_pad = lookup(pstart + counts - (pad_end - pad), pe) + j
    fill = jnp.where(j < pad_end[-1], in_pad, pend[-1] + (j - pad_end[-1])).astype(I32)
    return dest.astype(I32), blk_exp.astype(I32), n_used, fill


def kernel(x, c, w_ada, b_ada, g_mix, w_in, conv_w, conv_b, w_rg_a, b_rg_a, w_rg_x, b_rg_x, lam, w_out,
           g_ffn, w_router, b_router, w_gate_up, b_gate_up, w_down, b_down, g_final):
    bsz, s, d = x.shape
    t = bsz * s
    depth = w_ada.shape[0]
    assert depth == 1, "the combine kernel applies the final norm; one layer only"
    for l in range(depth):
        mod = _ada(c, w_ada[l], b_ada[l]).reshape(bsz, 6, 1, d)
        shift_m, scale_m, gate_m, shift_f, scale_f, gate_f = (mod[:, j] for j in range(6))
        q, k, v, xr, gr = _inproj(x, g_mix[l], shift_m, scale_m, w_in[l].astype(BF16))
        attn = _attention(q, k, v)
        rec = _rglru(xr, gr, conv_w[l], conv_b[l], w_rg_a[l], b_rg_a[l], w_rg_x[l], b_rg_x[l], lam[l])
        x1, h2rows, idx, gates, rank, cnt = _outproj(
            attn, rec, w_out[l].astype(BF16), x, gate_m, g_ffn[l], shift_f, scale_f, w_router[l], b_router[l])
        n_rows = t * TOP_K + N_EXPERTS * EXPERT_ROWS
        dest, blk_exp, n_used, fill = _routing_plan(idx, rank, cnt[:, 0], n_rows)
        slots = _slots(dest.reshape(-1), fill)
        y = _experts(h2rows, slots, blk_exp, n_used, w_gate_up[l], b_gate_up[l], w_down[l], b_down[l])
        x = _combine(y, gates.T, x1, gate_f, g_final)
    return x
```

```python
import functools

import jax
import jax.numpy as jnp
from jax import lax
from jax.experimental import pallas as pl
from jax.experimental.pallas import tpu as pltpu

F32 = jnp.float32
BF16 = jnp.bfloat16
I32 = jnp.int32

HEAD_DIM = 64
DILATED_PATTERNS = ((128, 1), (512, 4), (2048, 16))
REC_BLOCKS = 8
CONV_WIDTH = 4
RG_C = 8.0
N_EXPERTS = 32
TOP_K = 4
SWIGLU_LIMIT = 7.0
SWIGLU_ALPHA = 1.702
NORM_EPS = 1e-6
NEG_INF = -1e30
LOG2_E = 1.4426950408889634

LANES = 128
SUBLANES = 8
ROW_CHUNKS = 8
VMEM_LIMIT = 48 * 1024 * 1024

INPROJ_ROWS = 512
RGLRU_ROWS = 512
OUTPROJ_ROWS = 512
ATTN_BLOCK = 128
ATTN_UNROLL = 8
EXPERT_ROWS = 256
DISPATCH_ROWS = 256
COMBINE_ROWS = 256


def _params(sem, vmem=VMEM_LIMIT):
    return pltpu.CompilerParams(dimension_semantics=sem, vmem_limit_bytes=vmem)


def _ada_kernel(c_ref, w_ref, b_ref, o_ref):
    o_ref[...] = jnp.dot(c_ref[...], w_ref[...], precision=lax.Precision.HIGHEST,
                         preferred_element_type=F32) + b_ref[...]


def _ada(c, w, b):
    bsz, d = c.shape
    n = w.shape[1]
    return pl.pallas_call(
        _ada_kernel,
        grid=(n // d,),
        in_specs=[pl.BlockSpec((bsz, d), lambda j: (0, 0)),
                  pl.BlockSpec((d, d), lambda j: (0, j)),
                  pl.BlockSpec((1, d), lambda j: (0, j))],
        out_specs=pl.BlockSpec((bsz, d), lambda j: (0, j)),
        out_shape=jax.ShapeDtypeStruct((bsz, n), F32),
        compiler_params=_params(("arbitrary",)),
        name="ada",
    )(c, w, b.reshape(1, n))


def _rms_modulate(x, g, shift, scale):
    y = x * lax.rsqrt(jnp.mean(x * x, axis=-1, keepdims=True) + NORM_EPS)
    return (y * g) * (1.0 + scale) + shift


def _inproj_kernel(x_ref, g_ref, sh_ref, sc_ref, w_ref, q_ref, k_ref, v_ref, xr_ref, gr_ref):
    h = _rms_modulate(x_ref[...], g_ref[...], sh_ref[...], sc_ref[...]).astype(BF16)
    outs = (q_ref, k_ref, v_ref, xr_ref, gr_ref)
    width = q_ref.shape[-1]
    for j, o_ref in enumerate(outs):
        z = jnp.dot(h, w_ref[:, j * width:(j + 1) * width], preferred_element_type=F32)
        o_ref[...] = z.astype(o_ref.dtype)


def _inproj(x, g, shift, scale, w_bf):
    bsz, s, d = x.shape
    n = w_bf.shape[1]
    width = n // 5
    tm = INPROJ_ROWS
    row = pl.BlockSpec((None, 1, d), lambda b, i: (b, 0, 0))
    out_blk = pl.BlockSpec((None, tm, width), lambda b, i: (b, i, 0))
    shp = lambda dt: jax.ShapeDtypeStruct((bsz, s, width), dt)
    return pl.pallas_call(
        _inproj_kernel,
        grid=(bsz, s // tm),
        in_specs=[pl.BlockSpec((None, tm, d), lambda b, i: (b, i, 0)),
                  pl.BlockSpec((1, d), lambda b, i: (0, 0)),
                  row, row,
                  pl.BlockSpec((d, n), lambda b, i: (0, 0))],
        out_specs=[out_blk] * 5,
        out_shape=[shp(BF16), shp(BF16), shp(BF16), shp(F32), shp(F32)],
        compiler_params=_params(("arbitrary", "arbitrary")),
        name="inproj",
    )(x, g.reshape(1, d), shift, scale, w_bf)


def _sigmoid(x):
    return 0.5 * jnp.tanh(0.5 * x) + 0.5


def _gelu_tanh(x):
    return 0.5 * x * (1.0 + jnp.tanh(0.7978845608028654 * (x + 0.044715 * (x * x * x))))


def _rglru_kernel(xr_ref, gr_ref, cw_ref, cb_ref, wa_ref, ba_ref, wx_ref, bx_ref, lam_ref,
                  o_ref, xe_ref, a_ref, u_ref, tail_ref, h_ref):
    ts, ch = xr_ref.shape
    pad = SUBLANES

    @pl.when(pl.program_id(1) == 0)
    def _():
        tail_ref[...] = jnp.zeros_like(tail_ref)
        h_ref[...] = jnp.zeros_like(h_ref)

    x = xr_ref[...]
    xe_ref[0:pad, :] = tail_ref[...]
    xe_ref[pad:pad + ts, :] = x
    tail_ref[...] = x[ts - pad:ts, :]
    xc = cb_ref[...] + jnp.zeros((ts, ch), F32)
    for j in range(CONV_WIDTH):
        off = pad - (CONV_WIDTH - 1) + j
        xc = xc + cw_ref[j:j + 1, :] * xe_ref[off:off + ts, :]

    xb = xc.astype(BF16)
    r = _sigmoid(jnp.dot(xb, wa_ref[...], preferred_element_type=F32) + ba_ref[...])
    i = _sigmoid(jnp.dot(xb, wx_ref[...], preferred_element_type=F32) + bx_ref[...])
    nl = -lam_ref[...]
    softplus = jnp.maximum(nl, 0.0) + jnp.log(1.0 + jnp.exp(-jnp.abs(nl)))
    log_a = (-RG_C) * r * softplus
    a = jnp.exp(log_a)
    mult = jnp.sqrt(1.0 - a * a)
    u = mult * (i * xc)

    ng = ts // SUBLANES
    a3 = a.reshape(ng, SUBLANES, ch)
    u3 = u.reshape(ng, SUBLANES, ch)
    rid = lax.broadcasted_iota(I32, (ng, SUBLANES, ch), 1)
    for sft in (1, 2, 4):
        a_s = pltpu.roll(a3, sft, 1)
        u_s = pltpu.roll(u3, sft, 1)
        keep = rid >= sft
        u3 = jnp.where(keep, a3 * u_s + u3, u3)
        a3 = jnp.where(keep, a3 * a_s, a3)
    a_ref[...] = a3.reshape(ts, ch)
    u_ref[...] = u3.reshape(ts, ch)

    inner = 8

    def body(gi, hprev):
        for jj in range(inner):
            base = pl.multiple_of((gi * inner + jj) * SUBLANES, SUBLANES)
            hcur = u_ref[pl.ds(base, SUBLANES), :] + a_ref[pl.ds(base, SUBLANES), :] * hprev
            u_ref[pl.ds(base, SUBLANES), :] = hcur
            hprev = hcur[SUBLANES - 1:SUBLANES, :]
        return hprev

    h_ref[...] = lax.fori_loop(0, ng // inner, body, h_ref[...])
    o_ref[...] = (u_ref[...] * _gelu_tanh(gr_ref[...])).astype(o_ref.dtype)


def _block_diag(w):
    nb, hin, hout = w.shape
    eye = jnp.eye(nb, dtype=w.dtype)
    return (eye[:, None, :, None] * w[:, :, None, :]).reshape(nb * hin, nb * hout)


def _rglru(xr, gr, conv_w, conv_b, w_a, b_a, w_x, b_x, lam):
    bsz, s, ch = xr.shape
    ts = RGLRU_ROWS
    blk = pl.BlockSpec((None, ts, ch), lambda b, t: (b, t, 0))
    full = lambda r, c: pl.BlockSpec((r, c), lambda b, t: (0, 0))
    return pl.pallas_call(
        _rglru_kernel,
        grid=(bsz, s // ts),
        in_specs=[blk, blk, full(CONV_WIDTH, ch), full(1, ch), full(ch, ch), full(1, ch),
                  full(ch, ch), full(1, ch), full(1, ch)],
        out_specs=blk,
        out_shape=jax.ShapeDtypeStruct((bsz, s, ch), BF16),
        scratch_shapes=[pltpu.VMEM((ts + SUBLANES, ch), F32),
                        pltpu.VMEM((ts, ch), F32),
                        pltpu.VMEM((ts, ch), F32),
                        pltpu.VMEM((SUBLANES, ch), F32),
                        pltpu.VMEM((1, ch), F32)],
        compiler_params=_params(("arbitrary", "arbitrary")),
        name="rglru",
    )(xr, gr, conv_w, conv_b.reshape(1, ch), _block_diag(w_a).astype(BF16), b_a.reshape(1, ch),
      _block_diag(w_x).astype(BF16), b_x.reshape(1, ch), lam.reshape(1, ch))


def _attn_kernel(q_ref, k_ref, v_ref, o_ref, f_ref, f4_ref, qs_ref, kp_ref, vp_ref, op_ref, lp_ref,
                 pc_ref, ml_ref):
    s = q_ref.shape[0]
    blk = ATTN_BLOCK
    n_blocks = s // blk
    unroll = ATTN_UNROLL

    lane = lax.broadcasted_iota(I32, (blk, LANES), 1)
    head0 = lane < HEAD_DIM
    head0_wide = lax.broadcasted_iota(I32, (blk, 2 * LANES), 1) % LANES < HEAD_DIM
    qi = lax.broadcasted_iota(I32, (2 * blk, blk), 0) % blk
    kj = lax.broadcasted_iota(I32, (2 * blk, blk), 1)
    prev_ok = kj >= qi
    cur_ok = kj <= qi
    dims = (((1,), (1,)), ((), ()))

    def block_rows(j, dil):
        per_res = (s // dil) // blk
        start = j // per_res + (j % per_res) * (dil * blk)
        if dil == 1:
            return pl.ds(pl.multiple_of(start, blk), blk)
        return pl.ds(start, blk, stride=dil)

    dils = tuple(d for _, d in DILATED_PATTERNS)
    assert dils == (1, 4, 16)

    def permute(src_ref, store):
        f_ref[...] = src_ref[...].astype(F32)

        def body0(j, c):
            store(0, j, f_ref[block_rows(j, 1), :])
            return c

        def body1(j, c):
            x = f_ref[block_rows(j, 4), :]
            f4_ref[pl.ds(pl.multiple_of(j * blk, blk), blk), :] = x
            store(1, j, x)
            return c

        def body2(j, c):
            per_res = (s // 16) // blk
            res, nblk = j // per_res, j % per_res
            start = (res % 4) * (s // 4) + res // 4 + nblk * (4 * blk)
            store(2, j, f4_ref[pl.ds(start, blk, stride=4), :])
            return c

        for body in (body0, body1, body2):
            lax.fori_loop(0, n_blocks, body, 0)

    def store_q(p, j, x):
        x = x * (HEAD_DIM ** -0.5 * LOG2_E)
        base = pl.multiple_of(j * (2 * blk), 2 * blk)
        qs_ref[p, pl.ds(base, blk), :] = jnp.where(head0, x, 0.0).astype(BF16)
        qs_ref[p, pl.ds(base + blk, blk), :] = jnp.where(head0, 0.0, x).astype(BF16)

    def store_kv(dst_ref):
        def store(p, j, x):
            dst_ref[p, pl.ds(pl.multiple_of((j + 1) * blk, blk), blk), 0:LANES] = x.astype(BF16)
        return store

    for p in range(len(DILATED_PATTERNS)):
        kp_ref[p, 0:blk, :] = jnp.zeros((blk, LANES), BF16)
        vp_ref[p, 0:blk, 0:LANES] = jnp.zeros((blk, LANES), BF16)
        vp_ref[p, :, LANES:] = jnp.ones((s + blk, LANES), BF16)
    permute(q_ref, store_q)
    permute(k_ref, store_kv(kp_ref))
    permute(v_ref, store_kv(vp_ref))

    def kv_rows(j):
        return pl.ds(pl.multiple_of(j * blk, blk), 2 * blk)

    def scores(p, g):
        per_res = (s // dils[p]) // blk
        for u in range(unroll):
            j = g * unroll + u
            no_prev = jnp.where(j % per_res > 0, 0.0, NEG_INF)
            qs = qs_ref[p, pl.ds(pl.multiple_of(j * (2 * blk), 2 * blk), 2 * blk), :]
            sc = lax.dot_general(qs, kp_ref[p, kv_rows(j), :], dims, preferred_element_type=F32)
            s_prev = jnp.where(prev_ok, sc[:, :blk], NEG_INF) + no_prev
            s_cur = jnp.where(cur_ok, sc[:, blk:], NEG_INF)
            m = jnp.max(jnp.maximum(s_prev, s_cur), axis=-1, keepdims=True)
            pc_ref[g % 2, u] = jnp.concatenate(
                [jnp.exp2(s_prev - m), jnp.exp2(s_cur - m)], axis=-1).astype(BF16)
            ml_ref[g % 2, u] = jnp.where(head0, m[:blk], m[blk:])

    def outputs(p, g):
        for u in range(unroll):
            j = g * unroll + u
            rl = jnp.dot(pc_ref[g % 2, u], vp_ref[p, kv_rows(j), :], preferred_element_type=F32)
            rl = jnp.where(head0_wide, rl[:blk], rl[blk:])
            l = rl[:, LANES:]
            rows = block_rows(j, dils[p])
            op_ref[p, rows, :] = rl[:, :LANES] / l
            lp_ref[p, rows, :] = ml_ref[g % 2, u] + jnp.log2(l)

    n_groups = n_blocks // unroll
    assert n_groups % 2 == 0
    n_pat = len(DILATED_PATTERNS)
    scores(0, 0)
    for p, (window, dil) in enumerate(DILATED_PATTERNS):
        assert window // dil == blk

        def body(g, carry, p=p):
            outputs(p, g - 1)
            scores(p, g)
            return carry

        lax.fori_loop(1, n_groups, body, 0)
        outputs(p, n_groups - 1)
        if p + 1 < n_pat:
            scores(p + 1, 0)

    chunk = 512
    for c0 in range(0, s, chunk):
        sl = pl.ds(c0, chunk)
        l0, l1, l2 = lp_ref[0, sl, :], lp_ref[1, sl, :], lp_ref[2, sl, :]
        m = jnp.maximum(jnp.maximum(l0, l1), l2)
        w0, w1, w2 = jnp.exp2(l0 - m), jnp.exp2(l1 - m), jnp.exp2(l2 - m)
        num = w0 * op_ref[0, sl, :] + w1 * op_ref[1, sl, :] + w2 * op_ref[2, sl, :]
        o_ref[sl, :] = (num / (w0 + w1 + w2)).astype(o_ref.dtype)


def _attention(q, k, v):
    bsz, s, da = q.shape
    blk = pl.BlockSpec((None, s, LANES), lambda b, h: (b, 0, h))
    n_pat = len(DILATED_PATTERNS)
    return pl.pallas_call(
        _attn_kernel,
        grid=(bsz, da // LANES),
        in_specs=[blk, blk, blk],
        out_specs=blk,
        out_shape=jax.ShapeDtypeStruct((bsz, s, da), BF16),
        scratch_shapes=[pltpu.VMEM((s, LANES), F32),
                        pltpu.VMEM((s, LANES), F32),
                        pltpu.VMEM((n_pat, 2 * s, LANES), BF16),
                        pltpu.VMEM((n_pat, s + ATTN_BLOCK, LANES), BF16),
                        pltpu.VMEM((n_pat, s + ATTN_BLOCK, 2 * LANES), BF16),
                        pltpu.VMEM((n_pat, s, LANES), F32),
                        pltpu.VMEM((n_pat, s, LANES), F32),
                        pltpu.VMEM((2, ATTN_UNROLL, 2 * ATTN_BLOCK, 2 * ATTN_BLOCK), BF16),
                        pltpu.VMEM((2, ATTN_UNROLL, ATTN_BLOCK, LANES), F32)],
        compiler_params=_params(("arbitrary", "arbitrary"), 56 * 1024 * 1024),
        name="attn",
    )(q, k, v)


def _outproj_kernel(attn_ref, rec_ref, w_ref, x_ref, gm_ref, g_ref, sh_ref, sc_ref, wr_ref, br_ref,
                    x1_ref, h2_ref, idx_ref, gate_ref, rank_ref, cnt_ref, base_ref):
    tm = x_ref.shape[0]
    da = attn_ref.shape[1]

    @pl.when(jnp.logical_and(pl.program_id(0) == 0, pl.program_id(1) == 0))
    def _():
        base_ref[...] = jnp.zeros_like(base_ref)

    y = (jnp.dot(attn_ref[...], w_ref[0:da, :], preferred_element_type=F32)
         + jnp.dot(rec_ref[...], w_ref[da:, :], preferred_element_type=F32))
    x1 = x_ref[...] + gm_ref[...] * y
    x1_ref[...] = x1
    h2 = _rms_modulate(x1, g_ref[...], sh_ref[...], sc_ref[...])
    for c in range(ROW_CHUNKS):
        h2_ref[pl.ds(c, tm, stride=ROW_CHUNKS), :] = h2[:, c * LANES:(c + 1) * LANES]

    work = lax.dot_general(wr_ref[...], h2, (((1,), (1,)), ((), ())), precision=lax.Precision.HIGHEST,
                           preferred_element_type=F32) + br_ref[...]
    n_exp = work.shape[0]
    eid = lax.broadcasted_iota(I32, (n_exp, tm), 0)
    vals, hots = [], []
    for kk in range(TOP_K):
        m = jnp.max(work, axis=0, keepdims=True)
        sel = jnp.min(jnp.where(work == m, eid, n_exp), axis=0, keepdims=True)
        hot = eid == sel
        idx_ref[kk:kk + 1, :] = sel
        vals.append(m)
        hots.append(hot)
        work = jnp.where(hot, -jnp.inf, work)
    exps = [jnp.exp(v - vals[0]) for v in vals]
    den = exps[0] + exps[1] + exps[2] + exps[3]
    for kk in range(TOP_K):
        gate_ref[kk:kk + 1, :] = exps[kk] / den

    cnt = jnp.zeros((n_exp, tm), F32)
    for hot in hots:
        cnt = cnt + hot.astype(F32)
    tr = lax.broadcasted_iota(I32, (tm, tm), 0)
    tc = lax.broadcasted_iota(I32, (tm, tm), 1)
    before = jnp.where(tr < tc, 1.0, 0.0).astype(BF16)
    prior = jnp.dot(cnt.astype(BF16), before, preferred_element_type=F32) + base_ref[:, 0:1]
    for kk in range(TOP_K):
        rank_ref[kk:kk + 1, :] = jnp.sum(jnp.where(hots[kk], prior, 0.0), axis=0,
                                         keepdims=True).astype(I32)
    base_ref[...] = base_ref[...] + jnp.sum(cnt, axis=1, keepdims=True)
    cnt_ref[...] = base_ref[...].astype(I32)


def _outproj(attn, rec, w_bf, x, gate_m, g_ffn, shift_f, scale_f, w_router, b_router):
    bsz, s, d = x.shape
    da = attn.shape[-1]
    n_exp = w_router.shape[1]
    tm = OUTPROJ_ROWS
    nt = s // tm
    t = bsz * s
    row = pl.BlockSpec((None, 1, d), lambda b, i: (b, 0, 0))
    small = pl.BlockSpec((TOP_K, tm), lambda b, i: (0, b * nt + i))
    return pl.pallas_call(
        _outproj_kernel,
        grid=(bsz, nt),
        in_specs=[pl.BlockSpec((None, tm, da), lambda b, i: (b, i, 0)),
                  pl.BlockSpec((None, tm, da), lambda b, i: (b, i, 0)),
                  pl.BlockSpec((d, d), lambda b, i: (0, 0)),
                  pl.BlockSpec((None, tm, d), lambda b, i: (b, i, 0)),
                  row,
                  pl.BlockSpec((1, d), lambda b, i: (0, 0)),
                  row, row,
                  pl.BlockSpec((n_exp, d), lambda b, i: (0, 0)),
                  pl.BlockSpec((n_exp, 1), lambda b, i: (0, 0))],
        out_specs=[pl.BlockSpec((None, tm, d), lambda b, i: (b, i, 0)),
                   pl.BlockSpec((tm * ROW_CHUNKS, LANES), lambda b, i: (b * nt + i, 0)),
                   small, small, small,
                   pl.BlockSpec((n_exp, LANES), lambda b, i: (0, 0))],
        out_shape=[jax.ShapeDtypeStruct((bsz, s, d), F32),
                   jax.ShapeDtypeStruct((t * ROW_CHUNKS, LANES), F32),
                   jax.ShapeDtypeStruct((TOP_K, t), I32),
                   jax.ShapeDtypeStruct((TOP_K, t), F32),
                   jax.ShapeDtypeStruct((TOP_K, t), I32),
                   jax.ShapeDtypeStruct((n_exp, LANES), I32)],
        scratch_shapes=[pltpu.VMEM((n_exp, LANES), F32)],
        compiler_params=_params(("arbitrary", "arbitrary")),
        name="outproj_router",
    )(attn, rec, w_bf, x, gate_m, g_ffn.reshape(1, d), shift_f, scale_f, w_router.T,
      b_router.reshape(n_exp, 1))


def _row_slice(ref, row):
    return ref.at[pl.ds(pl.multiple_of(row * ROW_CHUNKS, ROW_CHUNKS), ROW_CHUNKS), :]


def _slots_kernel(dest_ref, fill_ref, slot_ref):
    n_assign = dest_ref.shape[0]
    n_fill = fill_ref.shape[0]
    lead = EXPERT_ROWS
    unroll = 32

    def lead_body(i, c):
        slot_ref[i] = n_assign + n_fill + i
        return c

    lax.fori_loop(0, lead, lead_body, 0)

    def assign_body(i, c):
        for u in range(unroll):
            slot_ref[lead + dest_ref[i * unroll + u]] = i * unroll + u
        return c

    lax.fori_loop(0, n_assign // unroll, assign_body, 0)

    def fill_body(i, c):
        for u in range(unroll):
            slot_ref[lead + fill_ref[i * unroll + u]] = n_assign + i * unroll + u
        return c

    lax.fori_loop(0, n_fill // unroll, fill_body, 0)


def _slots(dest_flat, fill):
    smem = pl.BlockSpec(memory_space=pltpu.SMEM)
    n = EXPERT_ROWS + dest_flat.shape[0] + fill.shape[0]
    return pl.pallas_call(
        _slots_kernel,
        in_specs=[smem, smem],
        out_specs=smem,
        out_shape=jax.ShapeDtypeStruct((n,), I32),
        name="slots",
    )(dest_flat, fill)


GATHER_BUFS = 3
SCATTER_BUFS = 3
FETCH_PRIORITY = 0
SEND_PRIORITY = 1


def _experts_kernel(bexp_ref, nused_ref, slot_ref, h2_ref, wgu_ref, bgu_ref, wd_ref, bd_ref, y_ref,
                    xg_ref, ys_ref, xs_ref, wgu_bf, wd_bf, gsem, ssem):
    b = pl.program_id(0)
    nb = pl.num_programs(0)
    rows = xs_ref.shape[0]
    f = wd_ref.shape[0]
    n_tok = h2_ref.shape[0] // ROW_CHUNKS
    assert n_tok & (n_tok - 1) == 0
    used = b < nused_ref[0]
    changed = jnp.logical_or(b == 0, bexp_ref[b] != bexp_ref[jnp.maximum(b - 1, 0)])

    def buf_rows(ref, buf, i):
        if isinstance(i, int):
            return ref.at[buf, pl.ds(i * ROW_CHUNKS, ROW_CHUNKS), :]
        return ref.at[buf, pl.ds(pl.multiple_of(i * ROW_CHUNKS, ROW_CHUNKS), ROW_CHUNKS), :]

    def fetch_row(base, buf, i):
        tok = slot_ref[base + i] & (n_tok - 1)
        return pltpu.make_async_copy(_row_slice(h2_ref, tok), buf_rows(xg_ref, buf, i), gsem.at[buf])

    def send_row(base, buf, i):
        return pltpu.make_async_copy(buf_rows(ys_ref, buf, i), _row_slice(y_ref, slot_ref[base + i]),
                                     ssem.at[buf])

    def start_rows(make, blk, buf, priority, rolled=False):
        base = (blk + 1) * rows
        if rolled:
            def body(i, c):
                make(base, buf, i).start(priority=priority)
                return c
            lax.fori_loop(0, rows, body, 0)
        else:
            for i in range(rows):
                make(base, buf, i).start(priority=priority)

    def wait_rows(ref, buf, sem):
        pltpu.make_async_copy(ref.at[buf], ref.at[buf], sem.at[buf]).wait()

    @pl.when(b == 0)
    def _():
        ys_ref[...] = jnp.zeros_like(ys_ref)
        start_rows(fetch_row, 0, 0, FETCH_PRIORITY, rolled=True)
        start_rows(fetch_row, 1, 1, FETCH_PRIORITY, rolled=True)

    gbuf = b % GATHER_BUFS
    sbuf = b % SCATTER_BUFS
    wait_rows(xg_ref, gbuf, gsem)

    @pl.when(b >= 2)
    def _():
        wait_rows(ys_ref, sbuf, ssem)

    def issue_neighbours():
        start_rows(fetch_row, jnp.minimum(b + 2, nb - 1), (b + 2) % GATHER_BUFS, FETCH_PRIORITY)
        start_rows(send_row, b - 1, (b + 2) % SCATTER_BUFS, SEND_PRIORITY)

    @pl.when(jnp.logical_and(used, changed))
    def _():
        wgu_bf[...] = wgu_ref[...].astype(BF16)
        wd_bf[...] = wd_ref[...].astype(BF16)

    @pl.when(used)
    def _():
        for c in range(ROW_CHUNKS):
            xs_ref[:, c * LANES:(c + 1) * LANES] = (
                xg_ref[gbuf, pl.ds(c, rows, stride=ROW_CHUNKS), :].astype(BF16))
        issue_neighbours()
        gu = jnp.dot(xs_ref[...], wgu_bf[...], preferred_element_type=F32) + bgu_ref[...]
        gate = jnp.minimum(gu[:, :f], SWIGLU_LIMIT)
        up = jnp.clip(gu[:, f:], -SWIGLU_LIMIT, SWIGLU_LIMIT)
        glu = gate * jax.nn.sigmoid(gate * SWIGLU_ALPHA)
        act = ((up + 1.0) * glu).astype(BF16)
        y = jnp.dot(act, wd_bf[...], preferred_element_type=F32) + bd_ref[...]
        for c in range(ROW_CHUNKS):
            ys_ref[sbuf, pl.ds(c, rows, stride=ROW_CHUNKS), :] = y[:, c * LANES:(c + 1) * LANES]

    @pl.when(jnp.logical_not(used))
    def _():
        issue_neighbours()
        ys_ref[sbuf] = jnp.zeros(ys_ref.shape[1:], F32)

    @pl.when(b == nb - 1)
    def _():
        assert SCATTER_BUFS == 3 and GATHER_BUFS == 3
        start_rows(send_row, b, sbuf, SEND_PRIORITY, rolled=True)
        for k in range(SCATTER_BUFS):
            wait_rows(ys_ref, k, ssem)
        wait_rows(xg_ref, (b + 1) % GATHER_BUFS, gsem)
        wait_rows(xg_ref, (b + 2) % GATHER_BUFS, gsem)


def _experts(h2rows, slots, blk_exp, n_used, w_gate_up, b_gate_up, w_down, b_down):
    n_exp, d, f2 = w_gate_up.shape
    f = w_down.shape[1]
    rows = EXPERT_ROWS
    n_blocks = slots.shape[0] // rows - 1
    assert n_blocks >= SCATTER_BUFS
    buf_shape = (rows * ROW_CHUNKS, LANES)
    return pl.pallas_call(
        _experts_kernel,
        grid_spec=pltpu.PrefetchScalarGridSpec(
            num_scalar_prefetch=3,
            grid=(n_blocks,),
            in_specs=[pl.BlockSpec(memory_space=pl.ANY),
                      pl.BlockSpec((None, d, f2), lambda b, e, n, s: (e[b], 0, 0)),
                      pl.BlockSpec((None, 1, f2), lambda b, e, n, s: (e[b], 0, 0)),
                      pl.BlockSpec((None, f, d), lambda b, e, n, s: (e[b], 0, 0)),
                      pl.BlockSpec((None, 1, d), lambda b, e, n, s: (e[b], 0, 0))],
            out_specs=pl.BlockSpec(memory_space=pl.ANY),
            scratch_shapes=[pltpu.VMEM((GATHER_BUFS,) + buf_shape, F32),
                            pltpu.VMEM((SCATTER_BUFS,) + buf_shape, F32),
                            pltpu.VMEM((rows, d), BF16),
                            pltpu.VMEM((d, f2), BF16),
                            pltpu.VMEM((f, d), BF16),
                            pltpu.SemaphoreType.DMA((GATHER_BUFS,)),
                            pltpu.SemaphoreType.DMA((SCATTER_BUFS,))]),
        out_shape=jax.ShapeDtypeStruct((slots.shape[0] * ROW_CHUNKS, LANES), F32),
        compiler_params=_params(("arbitrary",), 56 * 1024 * 1024),
        name="experts",
    )(blk_exp, n_used, slots, h2rows, w_gate_up, b_gate_up.reshape(n_exp, 1, f2), w_down,
      b_down.reshape(n_exp, 1, d))


def _combine_kernel(y0_ref, y1_ref, y2_ref, y3_ref, gates_ref, x1_ref, gf_ref, g_ref, o_ref):
    tm = x1_ref.shape[0]
    gates = gates_ref[...]
    chunks = []
    for c in range(ROW_CHUNKS):
        acc = jnp.zeros((tm, LANES), F32)
        for kk, y_ref in enumerate((y0_ref, y1_ref, y2_ref, y3_ref)):
            acc = acc + y_ref[pl.ds(c, tm, stride=ROW_CHUNKS), :] * gates[:, kk:kk + 1]
        chunks.append(acc)
    moe = jnp.concatenate(chunks, axis=-1)
    x2 = x1_ref[...] + gf_ref[...] * moe
    o_ref[...] = (x2 * lax.rsqrt(jnp.mean(x2 * x2, axis=-1, keepdims=True) + NORM_EPS)) * g_ref[...]


def _combine(y, gates_t, x1, gate_f, g_final):
    bsz, s, d = x1.shape
    tm = COMBINE_ROWS
    nt = bsz * s // tm
    per_b = s // tm
    planes = [pl.BlockSpec((tm * ROW_CHUNKS, LANES), lambda i, kk=kk: (kk * nt + i, 0))
              for kk in range(TOP_K)]
    return pl.pallas_call(
        _combine_kernel,
        grid=(nt,),
        in_specs=planes + [
                  pl.BlockSpec((tm, TOP_K), lambda i: (i, 0)),
                  pl.BlockSpec((None, tm, d), lambda i: (i // per_b, i % per_b, 0)),
                  pl.BlockSpec((None, 1, d), lambda i: (i // per_b, 0, 0)),
                  pl.BlockSpec((1, d), lambda i: (0, 0))],
        out_specs=pl.BlockSpec((None, tm, d), lambda i: (i // per_b, i % per_b, 0)),
        out_shape=jax.ShapeDtypeStruct((bsz, s, d), F32),
        compiler_params=_params(("arbitrary",)),
        name="combine",
    )(y, y, y, y, gates_t, x1, gate_f, g_final.reshape(1, d))


def _routing_plan(idx, rank, counts, n_rows):
    rows = EXPERT_ROWS
    experts = jnp.arange(N_EXPERTS, dtype=I32)

    def lookup(table, e):
        return jnp.sum(jnp.where(e[..., None] == experts, table, 0), axis=-1)

    def segment_of(ends, pos):
        return jnp.minimum(jnp.sum((pos[..., None] >= ends).astype(I32), axis=-1), N_EXPERTS - 1)

    padded = ((counts + rows - 1) // rows) * rows
    pend = jnp.cumsum(padded)
    pstart = pend - padded
    dest = lookup(pstart, idx) + rank
    n_blocks = n_rows // rows
    blk_exp = segment_of(pend, jnp.arange(n_blocks, dtype=I32) * rows)
    n_used = (pend[-1] // rows).astype(I32).reshape(1)
    pad = padded - counts
    pad_end = jnp.cumsum(pad)
    j = jnp.arange(n_rows - idx.size, dtype=I32)
    pe = segment_of(pad_end, j)
    in_pad = lookup(pstart + counts - (pad_end - pad), pe) + j
    fill = jnp.where(j < pad_end[-1], in_pad, pend[-1] + (j - pad_end[-1])).astype(I32)
    return dest.astype(I32), blk_exp.astype(I32), n_used, fill


def kernel(x, c, w_ada, b_ada, g_mix, w_in, conv_w, conv_b, w_rg_a, b_rg_a, w_rg_x, b_rg_x, lam, w_out,
           g_ffn, w_router, b_router, w_gate_up, b_gate_up, w_down, b_down, g_final):
    bsz, s, d = x.shape
    t = bsz * s
    depth = w_ada.shape[0]
    assert depth == 1, "the combine kernel applies the final norm; one layer only"
    for l in range(depth):
        mod = _ada(c, w_ada[l], b_ada[l]).reshape(bsz, 6, 1, d)
        shift_m, scale_m, gate_m, shift_f, scale_f, gate_f = (mod[:, j] for j in range(6))
        q, k, v, xr, gr = _inproj(x, g_mix[l], shift_m, scale_m, w_in[l].astype(BF16))
        attn = _attention(q, k, v)
        rec = _rglru(xr, gr, conv_w[l], conv_b[l], w_rg_a[l], b_rg_a[l], w_rg_x[l], b_rg_x[l], lam[l])
        x1, h2rows, idx, gates, rank, cnt = _outproj(
            attn, rec, w_out[l].astype(BF16), x, gate_m, g_ffn[l], shift_f, scale_f, w_router[l], b_router[l])
        n_rows = t * TOP_K + N_EXPERTS * EXPERT_ROWS
        dest, blk_exp, n_used, fill = _routing_plan(idx, rank, cnt[:, 0], n_rows)
        slots = _slots(dest.reshape(-1), fill)
        y = _experts(h2rows, slots, blk_exp, n_used, w_gate_up[l], b_gate_up[l], w_down[l], b_down[l])
        x = _combine(y, gates.T, x1, gate_f, g_final)
    return x
```

```python
import functools

import jax
import jax.numpy as jnp
from jax import lax
from jax.experimental import pallas as pl
from jax.experimental.pallas import tpu as pltpu

F32 = jnp.float32
BF16 = jnp.bfloat16
I32 = jnp.int32

HEAD_DIM = 64
DILATED_PATTERNS = ((128, 1), (512, 4), (2048, 16))
REC_BLOCKS = 8
CONV_WIDTH = 4
RG_C = 8.0
N_EXPERTS = 32
TOP_K = 4
SWIGLU_LIMIT = 7.0
SWIGLU_ALPHA = 1.702
NORM_EPS = 1e-6
NEG_INF = -1e30
LOG2_E = 1.4426950408889634

LANES = 128
SUBLANES = 8
ROW_CHUNKS = 8
VMEM_LIMIT = 48 * 1024 * 1024

INPROJ_ROWS = 512
RGLRU_ROWS = 512
OUTPROJ_ROWS = 512
ATTN_BLOCK = 128
ATTN_UNROLL = 8
EXPERT_ROWS = 256
DISPATCH_ROWS = 256
COMBINE_ROWS = 256


def _params(sem, vmem=VMEM_LIMIT):
    return pltpu.CompilerParams(dimension_semantics=sem, vmem_limit_bytes=vmem)


def _ada_kernel(c_ref, w_ref, b_ref, o_ref):
    o_ref[...] = jnp.dot(c_ref[...], w_ref[...], precision=lax.Precision.HIGHEST,
                         preferred_element_type=F32) + b_ref[...]


def _ada(c, w, b):
    bsz, d = c.shape
    n = w.shape[1]
    return pl.pallas_call(
        _ada_kernel,
        grid=(n // d,),
        in_specs=[pl.BlockSpec((bsz, d), lambda j: (0, 0)),
                  pl.BlockSpec((d, d), lambda j: (0, j)),
                  pl.BlockSpec((1, d), lambda j: (0, j))],
        out_specs=pl.BlockSpec((bsz, d), lambda j: (0, j)),
        out_shape=jax.ShapeDtypeStruct((bsz, n), F32),
        compiler_params=_params(("arbitrary",)),
        name="ada",
    )(c, w, b.reshape(1, n))


def _rms_modulate(x, g, shift, scale):
    y = x * lax.rsqrt(jnp.mean(x * x, axis=-1, keepdims=True) + NORM_EPS)
    return (y * g) * (1.0 + scale) + shift


def _inproj_kernel(x_ref, g_ref, sh_ref, sc_ref, w_ref, q_ref, k_ref, v_ref, xr_ref, gr_ref):
    h = _rms_modulate(x_ref[...], g_ref[...], sh_ref[...], sc_ref[...]).astype(BF16)
    outs = (q_ref, k_ref, v_ref, xr_ref, gr_ref)
    width = q_ref.shape[-1]
    for j, o_ref in enumerate(outs):
        z = jnp.dot(h, w_ref[:, j * width:(j + 1) * width], preferred_element_type=F32)
        o_ref[...] = z.astype(o_ref.dtype)


def _inproj(x, g, shift, scale, w_bf):
    bsz, s, d = x.shape
    n = w_bf.shape[1]
    width = n // 5
    tm = INPROJ_ROWS
    row = pl.BlockSpec((None, 1, d), lambda b, i: (b, 0, 0))
    out_blk = pl.BlockSpec((None, tm, width), lambda b, i: (b, i, 0))
    shp = lambda dt: jax.ShapeDtypeStruct((bsz, s, width), dt)
    return pl.pallas_call(
        _inproj_kernel,
        grid=(bsz, s // tm),
        in_specs=[pl.BlockSpec((None, tm, d), lambda b, i: (b, i, 0)),
                  pl.BlockSpec((1, d), lambda b, i: (0, 0)),
                  row, row,
                  pl.BlockSpec((d, n), lambda b, i: (0, 0))],
        out_specs=[out_blk] * 5,
        out_shape=[shp(BF16), shp(BF16), shp(BF16), shp(F32), shp(F32)],
        compiler_params=_params(("arbitrary", "arbitrary")),
        name="inproj",
    )(x, g.reshape(1, d), shift, scale, w_bf)


def _sigmoid(x):
    return 0.5 * jnp.tanh(0.5 * x) + 0.5


def _gelu_tanh(x):
    return 0.5 * x * (1.0 + jnp.tanh(0.7978845608028654 * (x + 0.044715 * (x * x * x))))


def _rglru_kernel(xr_ref, gr_ref, cw_ref, cb_ref, wa_ref, ba_ref, wx_ref, bx_ref, lam_ref,
                  o_ref, xe_ref, a_ref, u_ref, tail_ref, h_ref):
    ts, ch = xr_ref.shape
    pad = SUBLANES

    @pl.when(pl.program_id(1) == 0)
    def _():
        tail_ref[...] = jnp.zeros_like(tail_ref)
        h_ref[...] = jnp.zeros_like(h_ref)

    x = xr_ref[...]
    xe_ref[0:pad, :] = tail_ref[...]
    xe_ref[pad:pad + ts, :] = x
    tail_ref[...] = x[ts - pad:ts, :]
    xc = cb_ref[...] + jnp.zeros((ts, ch), F32)
    for j in range(CONV_WIDTH):
        off = pad - (CONV_WIDTH - 1) + j
        xc = xc + cw_ref[j:j + 1, :] * xe_ref[off:off + ts, :]

    xb = xc.astype(BF16)
    r = _sigmoid(jnp.dot(xb, wa_ref[...], preferred_element_type=F32) + ba_ref[...])
    i = _sigmoid(jnp.dot(xb, wx_ref[...], preferred_element_type=F32) + bx_ref[...])
    nl = -lam_ref[...]
    softplus = jnp.maximum(nl, 0.0) + jnp.log(1.0 + jnp.exp(-jnp.abs(nl)))
    log_a = (-RG_C) * r * softplus
    a = jnp.exp(log_a)
    mult = jnp.sqrt(1.0 - a * a)
    u = mult * (i * xc)

    ng = ts // SUBLANES
    a3 = a.reshape(ng, SUBLANES, ch)
    u3 = u.reshape(ng, SUBLANES, ch)
    rid = lax.broadcasted_iota(I32, (ng, SUBLANES, ch), 1)
    for sft in (1, 2, 4):
        a_s = pltpu.roll(a3, sft, 1)
        u_s = pltpu.roll(u3, sft, 1)
        keep = rid >= sft
        u3 = jnp.where(keep, a3 * u_s + u3, u3)
        a3 = jnp.where(keep, a3 * a_s, a3)
    a_ref[...] = a3.reshape(ts, ch)
    u_ref[...] = u3.reshape(ts, ch)

    inner = 8

    def body(gi, hprev):
        for jj in range(inner):
            base = pl.multiple_of((gi * inner + jj) * SUBLANES, SUBLANES)
            hcur = u_ref[pl.ds(base, SUBLANES), :] + a_ref[pl.ds(base, SUBLANES), :] * hprev
            u_ref[pl.ds(base, SUBLANES), :] = hcur
            hprev = hcur[SUBLANES - 1:SUBLANES, :]
        return hprev

    h_ref[...] = lax.fori_loop(0, ng // inner, body, h_ref[...])
    o_ref[...] = (u_ref[...] * _gelu_tanh(gr_ref[...])).astype(o_ref.dtype)


def _block_diag(w):
    nb, hin, hout = w.shape
    eye = jnp.eye(nb, dtype=w.dtype)
    return (eye[:, None, :, None] * w[:, :, None, :]).reshape(nb * hin, nb * hout)


def _rglru(xr, gr, conv_w, conv_b, w_a, b_a, w_x, b_x, lam):
    bsz, s, ch = xr.shape
    ts = RGLRU_ROWS
    blk = pl.BlockSpec((None, ts, ch), lambda b, t: (b, t, 0))
    full = lambda r, c: pl.BlockSpec((r, c), lambda b, t: (0, 0))
    return pl.pallas_call(
        _rglru_kernel,
        grid=(bsz, s // ts),
        in_specs=[blk, blk, full(CONV_WIDTH, ch), full(1, ch), full(ch, ch), full(1, ch),
                  full(ch, ch), full(1, ch), full(1, ch)],
        out_specs=blk,
        out_shape=jax.ShapeDtypeStruct((bsz, s, ch), BF16),
        scratch_shapes=[pltpu.VMEM((ts + SUBLANES, ch), F32),
                        pltpu.VMEM((ts, ch), F32),
                        pltpu.VMEM((ts, ch), F32),
                        pltpu.VMEM((SUBLANES, ch), F32),
                        pltpu.VMEM((1, ch), F32)],
        compiler_params=_params(("arbitrary", "arbitrary")),
        name="rglru",
    )(xr, gr, conv_w, conv_b.reshape(1, ch), _block_diag(w_a).astype(BF16), b_a.reshape(1, ch),
      _block_diag(w_x).astype(BF16), b_x.reshape(1, ch), lam.reshape(1, ch))


def _attn_kernel(q_ref, k_ref, v_ref, o_ref, f_ref, f4_ref, qs_ref, kp_ref, vp_ref, op_ref, lp_ref,
                 pc_ref, ml_ref):
    s = q_ref.shape[0]
    blk = ATTN_BLOCK
    n_blocks = s // blk
    unroll = ATTN_UNROLL

    lane = lax.broadcasted_iota(I32, (blk, LANES), 1)
    head0 = lane < HEAD_DIM
    head0_wide = lax.broadcasted_iota(I32, (blk, 2 * LANES), 1) % LANES < HEAD_DIM
    qi = lax.broadcasted_iota(I32, (2 * blk, blk), 0) % blk
    kj = lax.broadcasted_iota(I32, (2 * blk, blk), 1)
    prev_ok = kj >= qi
    cur_ok = kj <= qi
    dims = (((1,), (1,)), ((), ()))

    def block_rows(j, dil):
        per_res = (s // dil) // blk
        start = j // per_res + (j % per_res) * (dil * blk)
        if dil == 1:
            return pl.ds(pl.multiple_of(start, blk), blk)
        return pl.ds(start, blk, stride=dil)

    dils = tuple(d for _, d in DILATED_PATTERNS)
    assert dils == (1, 4, 16)

    def permute(src_ref, store):
        f_ref[...] = src_ref[...].astype(F32)

        def body0(j, c):
            store(0, j, f_ref[block_rows(j, 1), :])
            return c

        def body1(j, c):
            x = f_ref[block_rows(j, 4), :]
            f4_ref[pl.ds(pl.multiple_of(j * blk, blk), blk), :] = x
            store(1, j, x)
            return c

        def body2(j, c):
            per_res = (s // 16) // blk
            res, nblk = j // per_res, j % per_res
            start = (res % 4) * (s // 4) + res // 4 + nblk * (4 * blk)
            store(2, j, f4_ref[pl.ds(start, blk, stride=4), :])
            return c

        for body in (body0, body1, body2):
            lax.fori_loop(0, n_blocks, body, 0)

    def store_q(p, j, x):
        x = x * (HEAD_DIM ** -0.5 * LOG2_E)
        base = pl.multiple_of(j * (2 * blk), 2 * blk)
        qs_ref[p, pl.ds(base, blk), :] = jnp.where(head0, x, 0.0).astype(BF16)
        qs_ref[p, pl.ds(base + blk, blk), :] = jnp.where(head0, 0.0, x).astype(BF16)

    def store_kv(dst_ref):
        def store(p, j, x):
            dst_ref[p, pl.ds(pl.multiple_of((j + 1) * blk, blk), blk), 0:LANES] = x.astype(BF16)
        return store

    for p in range(len(DILATED_PATTERNS)):
        kp_ref[p, 0:blk, :] = jnp.zeros((blk, LANES), BF16)
        vp_ref[p, 0:blk, 0:LANES] = jnp.zeros((blk, LANES), BF16)
        vp_ref[p, :, LANES:] = jnp.ones((s + blk, LANES), BF16)
    permute(q_ref, store_q)
    permute(k_ref, store_kv(kp_ref))
    permute(v_ref, store_kv(vp_ref))

    def kv_rows(j):
        return pl.ds(pl.multiple_of(j * blk, blk), 2 * blk)

    def scores(p, g):
        per_res = (s // dils[p]) // blk
        for u in range(unroll):
            j = g * unroll + u
            no_prev = jnp.where(j % per_res > 0, 0.0, NEG_INF)
            qs = qs_ref[p, pl.ds(pl.multiple_of(j * (2 * blk), 2 * blk), 2 * blk), :]
            sc = lax.dot_general(qs, kp_ref[p, kv_rows(j), :], dims, preferred_element_type=F32)
            s_prev = jnp.where(prev_ok, sc[:, :blk], NEG_INF) + no_prev
            s_cur = jnp.where(cur_ok, sc[:, blk:], NEG_INF)
            m = jnp.max(jnp.maximum(s_prev, s_cur), axis=-1, keepdims=True)
            pc_ref[g % 2, u] = jnp.concatenate(
                [jnp.exp2(s_prev - m), jnp.exp2(s_cur - m)], axis=-1).astype(BF16)
            ml_ref[g % 2, u] = jnp.where(head0, m[:blk], m[blk:])

    def outputs(p, g):
        for u in range(unroll):
            j = g * unroll + u
            rl = jnp.dot(pc_ref[g % 2, u], vp_ref[p, kv_rows(j), :], preferred_element_type=F32)
            rl = jnp.where(head0_wide, rl[:blk], rl[blk:])
            l = rl[:, LANES:]
            rows = block_rows(j, dils[p])
            op_ref[p, rows, :] = rl[:, :LANES] / l
            lp_ref[p, rows, :] = ml_ref[g % 2, u] + jnp.log2(l)

    n_groups = n_blocks // unroll
    assert n_groups % 2 == 0
    n_pat = len(DILATED_PATTERNS)
    scores(0, 0)
    for p, (window, dil) in enumerate(DILATED_PATTERNS):
        assert window // dil == blk

        def body(g, carry, p=p):
            outputs(p, g - 1)
            scores(p, g)
            return carry

        lax.fori_loop(1, n_groups, body, 0)
        outputs(p, n_groups - 1)
        if p + 1 < n_pat:
            scores(p + 1, 0)

    chunk = 512
    for c0 in range(0, s, chunk):
        sl = pl.ds(c0, chunk)
        l0, l1, l2 = lp_ref[0, sl, :], lp_ref[1, sl, :], lp_ref[2, sl, :]
        m = jnp.maximum(jnp.maximum(l0, l1), l2)
        w0, w1, w2 = jnp.exp2(l0 - m), jnp.exp2(l1 - m), jnp.exp2(l2 - m)
        num = w0 * op_ref[0, sl, :] + w1 * op_ref[1, sl, :] + w2 * op_ref[2, sl, :]
        o_ref[sl, :] = (num / (w0 + w1 + w2)).astype(o_ref.dtype)


def _attention(q, k, v):
    bsz, s, da = q.shape
    blk = pl.BlockSpec((None, s, LANES), lambda b, h: (b, 0, h))
    n_pat = len(DILATED_PATTERNS)
    return pl.pallas_call(
        _attn_kernel,
        grid=(bsz, da // LANES),
        in_specs=[blk, blk, blk],
        out_specs=blk,
        out_shape=jax.ShapeDtypeStruct((bsz, s, da), BF16),
        scratch_shapes=[pltpu.VMEM((s, LANES), F32),
                        pltpu.VMEM((s, LANES), F32),
                        pltpu.VMEM((n_pat, 2 * s, LANES), BF16),
                        pltpu.VMEM((n_pat, s + ATTN_BLOCK, LANES), BF16),
                        pltpu.VMEM((n_pat, s + ATTN_BLOCK, 2 * LANES), BF16),
                        pltpu.VMEM((n_pat, s, LANES), F32),
                        pltpu.VMEM((n_pat, s, LANES), F32),
                        pltpu.VMEM((2, ATTN_UNROLL, 2 * ATTN_BLOCK, 2 * ATTN_BLOCK), BF16),
                        pltpu.VMEM((2, ATTN_UNROLL, ATTN_BLOCK, LANES), F32)],
        compiler_params=_params(("arbitrary", "arbitrary"), 56 * 1024 * 1024),
        name="attn",
    )(q, k, v)


def _outproj_kernel(attn_ref, rec_ref, w_ref, x_ref, gm_ref, g_ref, sh_ref, sc_ref, wr_ref, br_ref,
                    x1_ref, h2_ref, idx_ref, gate_ref, rank_ref, cnt_ref, base_ref):
    tm = x_ref.shape[0]
    da = attn_ref.shape[1]

    @pl.when(jnp.logical_and(pl.program_id(0) == 0, pl.program_id(1) == 0))
    def _():
        base_ref[...] = jnp.zeros_like(base_ref)

    y = (jnp.dot(attn_ref[...], w_ref[0:da, :], preferred_element_type=F32)
         + jnp.dot(rec_ref[...], w_ref[da:, :], preferred_element_type=F32))
    x1 = x_ref[...] + gm_ref[...] * y
    x1_ref[...] = x1
    h2 = _rms_modulate(x1, g_ref[...], sh_ref[...], sc_ref[...])
    for c in range(ROW_CHUNKS):
        h2_ref[pl.ds(c, tm, stride=ROW_CHUNKS), :] = h2[:, c * LANES:(c + 1) * LANES]

    work = lax.dot_general(wr_ref[...], h2, (((1,), (1,)), ((), ())), precision=lax.Precision.HIGHEST,
                           preferred_element_type=F32) + br_ref[...]
    n_exp = work.shape[0]
    eid = lax.broadcasted_iota(I32, (n_exp, tm), 0)
    vals, hots = [], []
    for kk in range(TOP_K):
        m = jnp.max(work, axis=0, keepdims=True)
        sel = jnp.min(jnp.where(work == m, eid, n_exp), axis=0, keepdims=True)
        hot = eid == sel
        idx_ref[kk:kk + 1, :] = sel
        vals.append(m)
        hots.append(hot)
        work = jnp.where(hot, -jnp.inf, work)
    exps = [jnp.exp(v - vals[0]) for v in vals]
    den = exps[0] + exps[1] + exps[2] + exps[3]
    for kk in range(TOP_K):
        gate_ref[kk:kk + 1, :] = exps[kk] / den

    cnt = jnp.zeros((n_exp, tm), F32)
    for hot in hots:
        cnt = cnt + hot.astype(F32)
    tr = lax.broadcasted_iota(I32, (tm, tm), 0)
    tc = lax.broadcasted_iota(I32, (tm, tm), 1)
    before = jnp.where(tr < tc, 1.0, 0.0).astype(BF16)
    prior = jnp.dot(cnt.astype(BF16), before, preferred_element_type=F32) + base_ref[:, 0:1]
    for kk in range(TOP_K):
        rank_ref[kk:kk + 1, :] = jnp.sum(jnp.where(hots[kk], prior, 0.0), axis=0,
                                         keepdims=True).astype(I32)
    base_ref[...] = base_ref[...] + jnp.sum(cnt, axis=1, keepdims=True)
    cnt_ref[...] = base_ref[...].astype(I32)


def _outproj(attn, rec, w_bf, x, gate_m, g_ffn, shift_f, scale_f, w_router, b_router):
    bsz, s, d = x.shape
    da = attn.shape[-1]
    n_exp = w_router.shape[1]
    tm = OUTPROJ_ROWS
    nt = s // tm
    t = bsz * s
    row = pl.BlockSpec((None, 1, d), lambda b, i: (b, 0, 0))
    small = pl.BlockSpec((TOP_K, tm), lambda b, i: (0, b * nt + i))
    return pl.pallas_call(
        _outproj_kernel,
        grid=(bsz, nt),
        in_specs=[pl.BlockSpec((None, tm, da), lambda b, i: (b, i, 0)),
                  pl.BlockSpec((None, tm, da), lambda b, i: (b, i, 0)),
                  pl.BlockSpec((d, d), lambda b, i: (0, 0)),
                  pl.BlockSpec((None, tm, d), lambda b, i: (b, i, 0)),
                  row,
                  pl.BlockSpec((1, d), lambda b, i: (0, 0)),
                  row, row,
                  pl.BlockSpec((n_exp, d), lambda b, i: (0, 0)),
                  pl.BlockSpec((n_exp, 1), lambda b, i: (0, 0))],
        out_specs=[pl.BlockSpec((None, tm, d), lambda b, i: (b, i, 0)),
                   pl.BlockSpec((tm * ROW_CHUNKS, LANES), lambda b, i: (b * nt + i, 0)),
                   small, small, small,
                   pl.BlockSpec((n_exp, LANES), lambda b, i: (0, 0))],
        out_shape=[jax.ShapeDtypeStruct((bsz, s, d), F32),
                   jax.ShapeDtypeStruct((t * ROW_CHUNKS, LANES), F32),
                   jax.ShapeDtypeStruct((TOP_K, t), I32),
                   jax.ShapeDtypeStruct((TOP_K, t), F32),
                   jax.ShapeDtypeStruct((TOP_K, t), I32),
                   jax.ShapeDtypeStruct((n_exp, LANES), I32)],
        scratch_shapes=[pltpu.VMEM((n_exp, LANES), F32)],
        compiler_params=_params(("arbitrary", "arbitrary")),
        name="outproj_router",
    )(attn, rec, w_bf, x, gate_m, g_ffn.reshape(1, d), shift_f, scale_f, w_router.T,
      b_router.reshape(n_exp, 1))


def _row_slice(ref, row):
    return ref.at[pl.ds(pl.multiple_of(row * ROW_CHUNKS, ROW_CHUNKS), ROW_CHUNKS), :]


def _slots_kernel(dest_ref, fill_ref, slot_ref):
    n_assign = dest_ref.shape[0]
    n_fill = fill_ref.shape[0]
    unroll = 32

    def assign_body(i, c):
        for u in range(unroll):
            slot_ref[dest_ref[i * unroll + u]] = i * unroll + u
        return c

    lax.fori_loop(0, n_assign // unroll, assign_body, 0)

    def fill_body(i, c):
        for u in range(unroll):
            slot_ref[fill_ref[i * unroll + u]] = n_assign + i * unroll + u
        return c

    lax.fori_loop(0, n_fill // unroll, fill_body, 0)


def _slots(dest_flat, fill):
    smem = pl.BlockSpec(memory_space=pltpu.SMEM)
    n = dest_flat.shape[0] + fill.shape[0]
    return pl.pallas_call(
        _slots_kernel,
        in_specs=[smem, smem],
        out_specs=smem,
        out_shape=jax.ShapeDtypeStruct((n,), I32),
        name="slots",
    )(dest_flat, fill)


GATHER_BUFS = 3


def _experts_kernel(bexp_ref, nused_ref, slot_ref, h2_ref, wgu_ref, bgu_ref, wd_ref, bd_ref, y_ref,
                    xg_ref, xs_ref, wgu_bf, wd_bf, gsem):
    b = pl.program_id(0)
    nb = pl.num_programs(0)
    rows = xs_ref.shape[0]
    f = wd_ref.shape[0]
    n_tok = h2_ref.shape[0] // ROW_CHUNKS
    assert n_tok & (n_tok - 1) == 0
    used = b < nused_ref[0]
    changed = jnp.logical_or(b == 0, bexp_ref[b] != bexp_ref[jnp.maximum(b - 1, 0)])

    def buf_rows(ref, buf, i):
        if isinstance(i, int):
            return ref.at[buf, pl.ds(i * ROW_CHUNKS, ROW_CHUNKS), :]
        return ref.at[buf, pl.ds(pl.multiple_of(i * ROW_CHUNKS, ROW_CHUNKS), ROW_CHUNKS), :]

    def fetch_row(base, buf, i):
        tok = slot_ref[base + i] & (n_tok - 1)
        return pltpu.make_async_copy(_row_slice(h2_ref, tok), buf_rows(xg_ref, buf, i), gsem.at[buf])

    def fetch_block(blk, buf, rolled=False):
        base = blk * rows
        if rolled:
            def body(i, c):
                fetch_row(base, buf, i).start()
                return c
            lax.fori_loop(0, rows, body, 0)
        else:
            for i in range(rows):
                fetch_row(base, buf, i).start()

    def wait_block(buf):
        pltpu.make_async_copy(xg_ref.at[buf], xg_ref.at[buf], gsem.at[buf]).wait()

    @pl.when(b == 0)
    def _():
        fetch_block(0, 0, rolled=True)
        fetch_block(1, 1, rolled=True)

    gbuf = b % GATHER_BUFS
    wait_block(gbuf)

    def fetch_ahead():
        fetch_block(jnp.minimum(b + 2, nb - 1), (b + 2) % GATHER_BUFS)

    @pl.when(jnp.logical_and(used, changed))
    def _():
        wgu_bf[...] = wgu_ref[...].astype(BF16)
        wd_bf[...] = wd_ref[...].astype(BF16)

    @pl.when(used)
    def _():
        for c in range(ROW_CHUNKS):
            xs_ref[:, c * LANES:(c + 1) * LANES] = (
                xg_ref[gbuf, pl.ds(c, rows, stride=ROW_CHUNKS), :].astype(BF16))
        fetch_ahead()
        gu = jnp.dot(xs_ref[...], wgu_bf[...], preferred_element_type=F32) + bgu_ref[...]
        gate = jnp.minimum(gu[:, :f], SWIGLU_LIMIT)
        up = jnp.clip(gu[:, f:], -SWIGLU_LIMIT, SWIGLU_LIMIT)
        glu = gate * jax.nn.sigmoid(gate * SWIGLU_ALPHA)
        act = ((up + 1.0) * glu).astype(BF16)
        y = jnp.dot(act, wd_bf[...], preferred_element_type=F32) + bd_ref[...]
        for c in range(ROW_CHUNKS):
            y_ref[pl.ds(c, rows, stride=ROW_CHUNKS), :] = y[:, c * LANES:(c + 1) * LANES]

    @pl.when(jnp.logical_not(used))
    def _():
        fetch_ahead()
        y_ref[...] = jnp.zeros_like(y_ref)

    @pl.when(b == nb - 1)
    def _():
        wait_block((b + 1) % GATHER_BUFS)
        wait_block((b + 2) % GATHER_BUFS)


def _experts(h2rows, slots, blk_exp, n_used, w_gate_up, b_gate_up, w_down, b_down):
    n_exp, d, f2 = w_gate_up.shape
    f = w_down.shape[1]
    rows = EXPERT_ROWS
    n_blocks = slots.shape[0] // rows
    assert n_blocks >= GATHER_BUFS
    buf_shape = (rows * ROW_CHUNKS, LANES)
    return pl.pallas_call(
        _experts_kernel,
        grid_spec=pltpu.PrefetchScalarGridSpec(
            num_scalar_prefetch=3,
            grid=(n_blocks,),
            in_specs=[pl.BlockSpec(memory_space=pl.ANY),
                      pl.BlockSpec((None, d, f2), lambda b, e, n, s: (e[b], 0, 0)),
                      pl.BlockSpec((None, 1, f2), lambda b, e, n, s: (e[b], 0, 0)),
                      pl.BlockSpec((None, f, d), lambda b, e, n, s: (e[b], 0, 0)),
                      pl.BlockSpec((None, 1, d), lambda b, e, n, s: (e[b], 0, 0))],
            out_specs=pl.BlockSpec(buf_shape, lambda b, e, n, s: (b, 0)),
            scratch_shapes=[pltpu.VMEM((GATHER_BUFS,) + buf_shape, F32),
                            pltpu.VMEM((rows, d), BF16),
                            pltpu.VMEM((d, f2), BF16),
                            pltpu.VMEM((f, d), BF16),
                            pltpu.SemaphoreType.DMA((GATHER_BUFS,))]),
        out_shape=jax.ShapeDtypeStruct((slots.shape[0] * ROW_CHUNKS, LANES), F32),
        compiler_params=_params(("arbitrary",), 56 * 1024 * 1024),
        name="experts",
    )(blk_exp, n_used, slots, h2rows, w_gate_up, b_gate_up.reshape(n_exp, 1, f2), w_down,
      b_down.reshape(n_exp, 1, d))


def _combine_kernel(src_ref, cnt_ref, dst_ref, pos_ref, gate_ref, y_ref, x1_ref, gf_ref, g_ref, o_ref,
                    stage_ref, tok_ref, sem):
    i = pl.program_id(0)
    nt = pl.num_programs(0)
    tm = x1_ref.shape[0]
    n_exp = cnt_ref.shape[0] // nt

    def fetch(tile, buf):
        def per_expert(e, c):
            seg = tile * n_exp + e
            cnt, src, dst = cnt_ref[seg], src_ref[seg], dst_ref[seg]
            done = jnp.int32(0)
            size = tm
            while size >= 1:
                take = (cnt & size) != 0

                @pl.when(take)
                def _(size=size, done=done):
                    pltpu.make_async_copy(
                        y_ref.at[pl.ds(pl.multiple_of((src + done) * ROW_CHUNKS, ROW_CHUNKS),
                                       size * ROW_CHUNKS), :],
                        stage_ref.at[buf, pl.ds(pl.multiple_of((dst + done) * ROW_CHUNKS, ROW_CHUNKS),
                                                size * ROW_CHUNKS), :],
                        sem.at[buf]).start()

                done = done + (cnt & size)
                size //= 2
            return c
        lax.fori_loop(0, n_exp, per_expert, 0)

    @pl.when(i == 0)
    def _():
        fetch(0, 0)

    @pl.when(i + 1 < nt)
    def _():
        fetch(i + 1, (i + 1) % 2)

    buf = i % 2
    pltpu.make_async_copy(stage_ref.at[buf], stage_ref.at[buf], sem.at[buf]).wait()

    for t in range(tm):
        acc = None
        for kk in range(TOP_K):
            row = stage_ref[buf, pl.ds(pl.multiple_of(pos_ref[kk, t], ROW_CHUNKS), ROW_CHUNKS), :]
            term = row * gate_ref[kk, t]
            acc = term if acc is None else acc + term
        tok_ref[t * ROW_CHUNKS:(t + 1) * ROW_CHUNKS, :] = acc

    moe = jnp.concatenate([tok_ref[pl.ds(c, tm, stride=ROW_CHUNKS), :] for c in range(ROW_CHUNKS)],
                          axis=-1)
    x2 = x1_ref[...] + gf_ref[...] * moe
    o_ref[...] = (x2 * lax.rsqrt(jnp.mean(x2 * x2, axis=-1, keepdims=True) + NORM_EPS)) * g_ref[...]


def _combine(y, plan, x1, gate_f, g_final):
    bsz, s, d = x1.shape
    tm = COMBINE_ROWS
    nt = bsz * s // tm
    per_b = s // tm
    seg_src, seg_cnt, seg_dst, pos3, gates3 = plan
    per_tile = pl.BlockSpec((None, TOP_K, tm), lambda i, *_: (i, 0, 0), memory_space=pltpu.SMEM)
    return pl.pallas_call(
        _combine_kernel,
        grid_spec=pltpu.PrefetchScalarGridSpec(
            num_scalar_prefetch=3,
            grid=(nt,),
            in_specs=[per_tile, per_tile,
                      pl.BlockSpec(memory_space=pl.ANY),
                      pl.BlockSpec((None, tm, d), lambda i, *_: (i // per_b, i % per_b, 0)),
                      pl.BlockSpec((None, 1, d), lambda i, *_: (i // per_b, 0, 0)),
                      pl.BlockSpec((1, d), lambda i, *_: (0, 0))],
            out_specs=pl.BlockSpec((None, tm, d), lambda i, *_: (i // per_b, i % per_b, 0)),
            scratch_shapes=[pltpu.VMEM((2, TOP_K * tm * ROW_CHUNKS, LANES), F32),
                            pltpu.VMEM((tm * ROW_CHUNKS, LANES), F32),
                            pltpu.SemaphoreType.DMA((2,))]),
        out_shape=jax.ShapeDtypeStruct((bsz, s, d), F32),
        compiler_params=_params(("arbitrary",)),
        name="combine",
    )(seg_src, seg_cnt, seg_dst, pos3, gates3, y, x1, gate_f, g_final.reshape(1, d))


def _combine_plan(idx, rank, gates, pstart):
    tm = COMBINE_ROWS
    t = idx.shape[1]
    nt = t // tm
    experts = jnp.arange(N_EXPERTS, dtype=I32)
    hot = idx.reshape(TOP_K, nt, tm)[..., None] == experts
    tile_cnt = jnp.sum(hot.astype(I32), axis=(0, 2))
    tile_base = jnp.cumsum(tile_cnt, axis=0) - tile_cnt
    stage_at = jnp.cumsum(tile_cnt, axis=1) - tile_cnt
    pos = rank.reshape(TOP_K, nt, tm) + jnp.sum(
        jnp.where(hot, (stage_at - tile_base)[None, :, None, :], 0), axis=-1)
    flat = lambda a: a.reshape(-1).astype(I32)
    return (flat(pstart[None, :] + tile_base), flat(tile_cnt), flat(stage_at),
            (pos * ROW_CHUNKS).transpose(1, 0, 2).astype(I32),
            gates.reshape(TOP_K, nt, tm).transpose(1, 0, 2))


def _routing_plan(idx, rank, counts, n_rows):
    rows = EXPERT_ROWS
    experts = jnp.arange(N_EXPERTS, dtype=I32)

    def lookup(table, e):
        return jnp.sum(jnp.where(e[..., None] == experts, table, 0), axis=-1)

    def segment_of(ends, pos):
        return jnp.minimum(jnp.sum((pos[..., None] >= ends).astype(I32), axis=-1), N_EXPERTS - 1)

    padded = ((counts + rows - 1) // rows) * rows
    pend = jnp.cumsum(padded)
    pstart = pend - padded
    dest = lookup(pstart, idx) + rank
    n_blocks = n_rows // rows
    blk_exp = segment_of(pend, jnp.arange(n_blocks, dtype=I32) * rows)
    n_used = (pend[-1] // rows).astype(I32).reshape(1)
    pad = padded - counts
    pad_end = jnp.cumsum(pad)
    j = jnp.arange(n_rows - idx.size, dtype=I32)
    pe = segment_of(pad_end, j)
    in_pad = lookup(pstart + counts - (pad_end - pad), pe) + j
    fill = jnp.where(j < pad_end[-1], in_pad, pend[-1] + (j - pad_end[-1])).astype(I32)
    return dest.astype(I32), blk_exp.astype(I32), n_used, fill, pstart


def kernel(x, c, w_ada, b_ada, g_mix, w_in, conv_w, conv_b, w_rg_a, b_rg_a, w_rg_x, b_rg_x, lam, w_out,
           g_ffn, w_router, b_router, w_gate_up, b_gate_up, w_down, b_down, g_final):
    bsz, s, d = x.shape
    t = bsz * s
    depth = w_ada.shape[0]
    assert depth == 1, "the combine kernel applies the final norm; one layer only"
    for l in range(depth):
        mod = _ada(c, w_ada[l], b_ada[l]).reshape(bsz, 6, 1, d)
        shift_m, scale_m, gate_m, shift_f, scale_f, gate_f = (mod[:, j] for j in range(6))
        q, k, v, xr, gr = _inproj(x, g_mix[l], shift_m, scale_m, w_in[l].astype(BF16))
        attn = _attention(q, k, v)
        rec = _rglru(xr, gr, conv_w[l], conv_b[l], w_rg_a[l], b_rg_a[l], w_rg_x[l], b_rg_x[l], lam[l])
        x1, h2rows, idx, gates, rank, cnt = _outproj(
            attn, rec, w_out[l].astype(BF16), x, gate_m, g_ffn[l], shift_f, scale_f, w_router[l], b_router[l])
        n_rows = t * TOP_K + N_EXPERTS * EXPERT_ROWS
        dest, blk_exp, n_used, fill, pstart = _routing_plan(idx, rank, cnt[:, 0], n_rows)
        slots = _slots(dest.reshape(-1), fill)
        y = _experts(h2rows, slots, blk_exp, n_used, w_gate_up[l], b_gate_up[l], w_down[l], b_down[l])
        x = _combine(y, _combine_plan(idx, rank, gates, pstart), x1, gate_f, g_final)
    return x
```

```python
import functools

import jax
import jax.numpy as jnp
from jax import lax
from jax.experimental import pallas as pl
from jax.experimental.pallas import tpu as pltpu

F32 = jnp.float32
BF16 = jnp.bfloat16
I32 = jnp.int32

HEAD_DIM = 64
DILATED_PATTERNS = ((128, 1), (512, 4), (2048, 16))
REC_BLOCKS = 8
CONV_WIDTH = 4
RG_C = 8.0
N_EXPERTS = 32
TOP_K = 4
SWIGLU_LIMIT = 7.0
SWIGLU_ALPHA = 1.702
NORM_EPS = 1e-6
NEG_INF = -1e30
LOG2_E = 1.4426950408889634

LANES = 128
SUBLANES = 8
ROW_CHUNKS = 8
VMEM_LIMIT = 48 * 1024 * 1024

INPROJ_ROWS = 512
RGLRU_ROWS = 512
OUTPROJ_ROWS = 512
ATTN_BLOCK = 128
ATTN_UNROLL = 8
EXPERT_ROWS = 512
COMBINE_ROWS = 256


def _params(sem, vmem=VMEM_LIMIT):
    return pltpu.CompilerParams(dimension_semantics=sem, vmem_limit_bytes=vmem)


def _ada_kernel(c_ref, w_ref, b_ref, o_ref):
    o_ref[...] = jnp.dot(c_ref[...], w_ref[...], precision=lax.Precision.HIGHEST,
                         preferred_element_type=F32) + b_ref[...]


def _ada(c, w, b):
    bsz, d = c.shape
    n = w.shape[1]
    return pl.pallas_call(
        _ada_kernel,
        grid=(n // d,),
        in_specs=[pl.BlockSpec((bsz, d), lambda j: (0, 0)),
                  pl.BlockSpec((d, d), lambda j: (0, j)),
                  pl.BlockSpec((1, d), lambda j: (0, j))],
        out_specs=pl.BlockSpec((bsz, d), lambda j: (0, j)),
        out_shape=jax.ShapeDtypeStruct((bsz, n), F32),
        compiler_params=_params(("arbitrary",)),
        name="ada",
    )(c, w, b.reshape(1, n))


def _rms_modulate(x, g, shift, scale):
    y = x * lax.rsqrt(jnp.mean(x * x, axis=-1, keepdims=True) + NORM_EPS)
    return (y * g) * (1.0 + scale) + shift


def _inproj_kernel(x_ref, g_ref, sh_ref, sc_ref, w_ref, q_ref, k_ref, v_ref, xr_ref, gr_ref):
    h = _rms_modulate(x_ref[...], g_ref[...], sh_ref[...], sc_ref[...]).astype(BF16)
    outs = (q_ref, k_ref, v_ref, xr_ref, gr_ref)
    width = q_ref.shape[-1]
    for j, o_ref in enumerate(outs):
        z = jnp.dot(h, w_ref[:, j * width:(j + 1) * width], preferred_element_type=F32)
        o_ref[...] = z.astype(o_ref.dtype)


def _inproj(x, g, shift, scale, w_bf):
    bsz, s, d = x.shape
    n = w_bf.shape[1]
    width = n // 5
    tm = INPROJ_ROWS
    row = pl.BlockSpec((None, 1, d), lambda b, i: (b, 0, 0))
    out_blk = pl.BlockSpec((None, tm, width), lambda b, i: (b, i, 0))
    shp = lambda dt: jax.ShapeDtypeStruct((bsz, s, width), dt)
    return pl.pallas_call(
        _inproj_kernel,
        grid=(bsz, s // tm),
        in_specs=[pl.BlockSpec((None, tm, d), lambda b, i: (b, i, 0)),
                  pl.BlockSpec((1, d), lambda b, i: (0, 0)),
                  row, row,
                  pl.BlockSpec((d, n), lambda b, i: (0, 0))],
        out_specs=[out_blk] * 5,
        out_shape=[shp(BF16), shp(BF16), shp(BF16), shp(F32), shp(F32)],
        compiler_params=_params(("arbitrary", "arbitrary")),
        name="inproj",
    )(x, g.reshape(1, d), shift, scale, w_bf)


def _sigmoid(x):
    return 0.5 * jnp.tanh(0.5 * x) + 0.5


def _gelu_tanh(x):
    return 0.5 * x * (1.0 + jnp.tanh(0.7978845608028654 * (x + 0.044715 * (x * x * x))))


def _rglru_kernel(xr_ref, gr_ref, cw_ref, cb_ref, wa_ref, ba_ref, wx_ref, bx_ref, lam_ref,
                  o_ref, xe_ref, a_ref, u_ref, tail_ref, h_ref):
    ts, ch = xr_ref.shape
    pad = SUBLANES

    @pl.when(pl.program_id(1) == 0)
    def _():
        tail_ref[...] = jnp.zeros_like(tail_ref)
        h_ref[...] = jnp.zeros_like(h_ref)

    x = xr_ref[...]
    xe_ref[0:pad, :] = tail_ref[...]
    xe_ref[pad:pad + ts, :] = x
    tail_ref[...] = x[ts - pad:ts, :]
    xc = cb_ref[...] + jnp.zeros((ts, ch), F32)
    for j in range(CONV_WIDTH):
        off = pad - (CONV_WIDTH - 1) + j
        xc = xc + cw_ref[j:j + 1, :] * xe_ref[off:off + ts, :]

    xb = xc.astype(BF16)
    r = _sigmoid(jnp.dot(xb, wa_ref[...], preferred_element_type=F32) + ba_ref[...])
    i = _sigmoid(jnp.dot(xb, wx_ref[...], preferred_element_type=F32) + bx_ref[...])
    nl = -lam_ref[...]
    softplus = jnp.maximum(nl, 0.0) + jnp.log(1.0 + jnp.exp(-jnp.abs(nl)))
    log_a = (-RG_C) * r * softplus
    a = jnp.exp(log_a)
    mult = jnp.sqrt(1.0 - a * a)
    u = mult * (i * xc)

    ng = ts // SUBLANES
    a3 = a.reshape(ng, SUBLANES, ch)
    u3 = u.reshape(ng, SUBLANES, ch)
    rid = lax.broadcasted_iota(I32, (ng, SUBLANES, ch), 1)
    for sft in (1, 2, 4):
        a_s = pltpu.roll(a3, sft, 1)
        u_s = pltpu.roll(u3, sft, 1)
        keep = rid >= sft
        u3 = jnp.where(keep, a3 * u_s + u3, u3)
        a3 = jnp.where(keep, a3 * a_s, a3)
    a_ref[...] = a3.reshape(ts, ch)
    u_ref[...] = u3.reshape(ts, ch)

    inner = 8

    def body(gi, hprev):
        for jj in range(inner):
            base = pl.multiple_of((gi * inner + jj) * SUBLANES, SUBLANES)
            hcur = u_ref[pl.ds(base, SUBLANES), :] + a_ref[pl.ds(base, SUBLANES), :] * hprev
            u_ref[pl.ds(base, SUBLANES), :] = hcur
            hprev = hcur[SUBLANES - 1:SUBLANES, :]
        return hprev

    h_ref[...] = lax.fori_loop(0, ng // inner, body, h_ref[...])
    o_ref[...] = (u_ref[...] * _gelu_tanh(gr_ref[...])).astype(o_ref.dtype)


def _block_diag(w):
    nb, hin, hout = w.shape
    eye = jnp.eye(nb, dtype=w.dtype)
    return (eye[:, None, :, None] * w[:, :, None, :]).reshape(nb * hin, nb * hout)


def _rglru(xr, gr, conv_w, conv_b, w_a, b_a, w_x, b_x, lam):
    bsz, s, ch = xr.shape
    ts = RGLRU_ROWS
    blk = pl.BlockSpec((None, ts, ch), lambda b, t: (b, t, 0))
    full = lambda r, c: pl.BlockSpec((r, c), lambda b, t: (0, 0))
    return pl.pallas_call(
        _rglru_kernel,
        grid=(bsz, s // ts),
        in_specs=[blk, blk, full(CONV_WIDTH, ch), full(1, ch), full(ch, ch), full(1, ch),
                  full(ch, ch), full(1, ch), full(1, ch)],
        out_specs=blk,
        out_shape=jax.ShapeDtypeStruct((bsz, s, ch), BF16),
        scratch_shapes=[pltpu.VMEM((ts + SUBLANES, ch), F32),
                        pltpu.VMEM((ts, ch), F32),
                        pltpu.VMEM((ts, ch), F32),
                        pltpu.VMEM((SUBLANES, ch), F32),
                        pltpu.VMEM((1, ch), F32)],
        compiler_params=_params(("arbitrary", "arbitrary")),
        name="rglru",
    )(xr, gr, conv_w, conv_b.reshape(1, ch), _block_diag(w_a).astype(BF16), b_a.reshape(1, ch),
      _block_diag(w_x).astype(BF16), b_x.reshape(1, ch), lam.reshape(1, ch))


def _attn_kernel(q_ref, k_ref, v_ref, o_ref, f_ref, f4_ref, qs_ref, kp_ref, vp_ref, op_ref, lp_ref,
                 pc_ref, ml_ref):
    s = q_ref.shape[0]
    blk = ATTN_BLOCK
    n_blocks = s // blk
    unroll = ATTN_UNROLL

    lane = lax.broadcasted_iota(I32, (blk, LANES), 1)
    head0 = lane < HEAD_DIM
    head0_wide = lax.broadcasted_iota(I32, (blk, 2 * LANES), 1) % LANES < HEAD_DIM
    qi = lax.broadcasted_iota(I32, (2 * blk, blk), 0) % blk
    kj = lax.broadcasted_iota(I32, (2 * blk, blk), 1)
    prev_ok = kj >= qi
    cur_ok = kj <= qi
    dims = (((1,), (1,)), ((), ()))

    def block_rows(j, dil):
        per_res = (s // dil) // blk
        start = j // per_res + (j % per_res) * (dil * blk)
        if dil == 1:
            return pl.ds(pl.multiple_of(start, blk), blk)
        return pl.ds(start, blk, stride=dil)

    dils = tuple(d for _, d in DILATED_PATTERNS)
    assert dils == (1, 4, 16)

    def permute(src_ref, store):
        f_ref[...] = src_ref[...].astype(F32)

        def body0(j, c):
            store(0, j, f_ref[block_rows(j, 1), :])
            return c

        def body1(j, c):
            x = f_ref[block_rows(j, 4), :]
            f4_ref[pl.ds(pl.multiple_of(j * blk, blk), blk), :] = x
            store(1, j, x)
            return c

        def body2(j, c):
            per_res = (s // 16) // blk
            res, nblk = j // per_res, j % per_res
            start = (res % 4) * (s // 4) + res // 4 + nblk * (4 * blk)
            store(2, j, f4_ref[pl.ds(start, blk, stride=4), :])
            return c

        for body in (body0, body1, body2):
            lax.fori_loop(0, n_blocks, body, 0)

    def store_q(p, j, x):
        x = x * (HEAD_DIM ** -0.5 * LOG2_E)
        base = pl.multiple_of(j * (2 * blk), 2 * blk)
        qs_ref[p, pl.ds(base, blk), :] = jnp.where(head0, x, 0.0).astype(BF16)
        qs_ref[p, pl.ds(base + blk, blk), :] = jnp.where(head0, 0.0, x).astype(BF16)

    def store_kv(dst_ref):
        def store(p, j, x):
            dst_ref[p, pl.ds(pl.multiple_of((j + 1) * blk, blk), blk), 0:LANES] = x.astype(BF16)
        return store

    for p in range(len(DILATED_PATTERNS)):
        kp_ref[p, 0:blk, :] = jnp.zeros((blk, LANES), BF16)
        vp_ref[p, 0:blk, 0:LANES] = jnp.zeros((blk, LANES), BF16)
        vp_ref[p, :, LANES:] = jnp.ones((s + blk, LANES), BF16)
    permute(q_ref, store_q)
    permute(k_ref, store_kv(kp_ref))
    permute(v_ref, store_kv(vp_ref))

    def kv_rows(j):
        return pl.ds(pl.multiple_of(j * blk, blk), 2 * blk)

    def scores(p, g):
        per_res = (s // dils[p]) // blk
        for u in range(unroll):
            j = g * unroll + u
            no_prev = jnp.where(j % per_res > 0, 0.0, NEG_INF)
            qs = qs_ref[p, pl.ds(pl.multiple_of(j * (2 * blk), 2 * blk), 2 * blk), :]
            sc = lax.dot_general(qs, kp_ref[p, kv_rows(j), :], dims, preferred_element_type=F32)
            s_prev = jnp.where(prev_ok, sc[:, :blk], NEG_INF) + no_prev
            s_cur = jnp.where(cur_ok, sc[:, blk:], NEG_INF)
            m = jnp.max(jnp.maximum(s_prev, s_cur), axis=-1, keepdims=True)
            pc_ref[g % 2, u] = jnp.concatenate(
                [jnp.exp2(s_prev - m), jnp.exp2(s_cur - m)], axis=-1).astype(BF16)
            ml_ref[g % 2, u] = jnp.where(head0, m[:blk], m[blk:])

    def outputs(p, g):
        for u in range(unroll):
            j = g * unroll + u
            rl = jnp.dot(pc_ref[g % 2, u], vp_ref[p, kv_rows(j), :], preferred_element_type=F32)
            rl = jnp.where(head0_wide, rl[:blk], rl[blk:])
            l = rl[:, LANES:]
            rows = block_rows(j, dils[p])
            op_ref[p, rows, :] = rl[:, :LANES] / l
            lp_ref[p, rows, :] = ml_ref[g % 2, u] + jnp.log2(l)

    n_groups = n_blocks // unroll
    assert n_groups % 2 == 0
    n_pat = len(DILATED_PATTERNS)
    scores(0, 0)
    for p, (window, dil) in enumerate(DILATED_PATTERNS):
        assert window // dil == blk

        def body(g, carry, p=p):
            outputs(p, g - 1)
            scores(p, g)
            return carry

        lax.fori_loop(1, n_groups, body, 0)
        outputs(p, n_groups - 1)
        if p + 1 < n_pat:
            scores(p + 1, 0)

    chunk = 512
    for c0 in range(0, s, chunk):
        sl = pl.ds(c0, chunk)
        l0, l1, l2 = lp_ref[0, sl, :], lp_ref[1, sl, :], lp_ref[2, sl, :]
        m = jnp.maximum(jnp.maximum(l0, l1), l2)
        w0, w1, w2 = jnp.exp2(l0 - m), jnp.exp2(l1 - m), jnp.exp2(l2 - m)
        num = w0 * op_ref[0, sl, :] + w1 * op_ref[1, sl, :] + w2 * op_ref[2, sl, :]
        o_ref[sl, :] = (num / (w0 + w1 + w2)).astype(o_ref.dtype)


def _attention(q, k, v):
    bsz, s, da = q.shape
    blk = pl.BlockSpec((None, s, LANES), lambda b, h: (b, 0, h))
    n_pat = len(DILATED_PATTERNS)
    return pl.pallas_call(
        _attn_kernel,
        grid=(bsz, da // LANES),
        in_specs=[blk, blk, blk],
        out_specs=blk,
        out_shape=jax.ShapeDtypeStruct((bsz, s, da), BF16),
        scratch_shapes=[pltpu.VMEM((s, LANES), F32),
                        pltpu.VMEM((s, LANES), F32),
                        pltpu.VMEM((n_pat, 2 * s, LANES), BF16),
                        pltpu.VMEM((n_pat, s + ATTN_BLOCK, LANES), BF16),
                        pltpu.VMEM((n_pat, s + ATTN_BLOCK, 2 * LANES), BF16),
                        pltpu.VMEM((n_pat, s, LANES), F32),
                        pltpu.VMEM((n_pat, s, LANES), F32),
                        pltpu.VMEM((2, ATTN_UNROLL, 2 * ATTN_BLOCK, 2 * ATTN_BLOCK), BF16),
                        pltpu.VMEM((2, ATTN_UNROLL, ATTN_BLOCK, LANES), F32)],
        compiler_params=_params(("arbitrary", "arbitrary"), 56 * 1024 * 1024),
        name="attn",
    )(q, k, v)


def _outproj_kernel(attn_ref, rec_ref, w_ref, x_ref, gm_ref, g_ref, sh_ref, sc_ref, wr_ref, br_ref,
                    x1_ref, h2_ref, idx_ref, gate_ref, rank_ref, cnt_ref, base_ref):
    tm = x_ref.shape[0]
    da = attn_ref.shape[1]

    @pl.when(jnp.logical_and(pl.program_id(0) == 0, pl.program_id(1) == 0))
    def _():
        base_ref[...] = jnp.zeros_like(base_ref)

    y = (jnp.dot(attn_ref[...], w_ref[0:da, :], preferred_element_type=F32)
         + jnp.dot(rec_ref[...], w_ref[da:, :], preferred_element_type=F32))
    x1 = x_ref[...] + gm_ref[...] * y
    x1_ref[...] = x1
    h2 = _rms_modulate(x1, g_ref[...], sh_ref[...], sc_ref[...])
    for c in range(ROW_CHUNKS):
        h2_ref[pl.ds(c, tm, stride=ROW_CHUNKS), :] = h2[:, c * LANES:(c + 1) * LANES]

    work = lax.dot_general(wr_ref[...], h2, (((1,), (1,)), ((), ())), precision=lax.Precision.HIGHEST,
                           preferred_element_type=F32) + br_ref[...]
    n_exp = work.shape[0]
    eid = lax.broadcasted_iota(I32, (n_exp, tm), 0)
    vals, hots = [], []
    for kk in range(TOP_K):
        m = jnp.max(work, axis=0, keepdims=True)
        sel = jnp.min(jnp.where(work == m, eid, n_exp), axis=0, keepdims=True)
        hot = eid == sel
        idx_ref[kk:kk + 1, :] = sel
        vals.append(m)
        hots.append(hot)
        work = jnp.where(hot, -jnp.inf, work)
    exps = [jnp.exp(v - vals[0]) for v in vals]
    den = exps[0] + exps[1] + exps[2] + exps[3]
    for kk in range(TOP_K):
        gate_ref[kk:kk + 1, :] = exps[kk] / den

    cnt = jnp.zeros((n_exp, tm), F32)
    for hot in hots:
        cnt = cnt + hot.astype(F32)
    tr = lax.broadcasted_iota(I32, (tm, tm), 0)
    tc = lax.broadcasted_iota(I32, (tm, tm), 1)
    before = jnp.where(tr < tc, 1.0, 0.0).astype(BF16)
    prior = jnp.dot(cnt.astype(BF16), before, preferred_element_type=F32) + base_ref[:, 0:1]
    for kk in range(TOP_K):
        rank_ref[kk:kk + 1, :] = jnp.sum(jnp.where(hots[kk], prior, 0.0), axis=0,
                                         keepdims=True).astype(I32)
    base_ref[...] = base_ref[...] + jnp.sum(cnt, axis=1, keepdims=True)
    cnt_ref[...] = base_ref[...].astype(I32)


def _outproj(attn, rec, w_bf, x, gate_m, g_ffn, shift_f, scale_f, w_router, b_router):
    bsz, s, d = x.shape
    da = attn.shape[-1]
    n_exp = w_router.shape[1]
    tm = OUTPROJ_ROWS
    nt = s // tm
    t = bsz * s
    row = pl.BlockSpec((None, 1, d), lambda b, i: (b, 0, 0))
    small = pl.BlockSpec((TOP_K, tm), lambda b, i: (0, b * nt + i))
    return pl.pallas_call(
        _outproj_kernel,
        grid=(bsz, nt),
        in_specs=[pl.BlockSpec((None, tm, da), lambda b, i: (b, i, 0)),
                  pl.BlockSpec((None, tm, da), lambda b, i: (b, i, 0)),
                  pl.BlockSpec((d, d), lambda b, i: (0, 0)),
                  pl.BlockSpec((None, tm, d), lambda b, i: (b, i, 0)),
                  row,
                  pl.BlockSpec((1, d), lambda b, i: (0, 0)),
                  row, row,
                  pl.BlockSpec((n_exp, d), lambda b, i: (0, 0)),
                  pl.BlockSpec((n_exp, 1), lambda b, i: (0, 0))],
        out_specs=[pl.BlockSpec((None, tm, d), lambda b, i: (b, i, 0)),
                   pl.BlockSpec((tm * ROW_CHUNKS, LANES), lambda b, i: (b * nt + i, 0)),
                   small, small, small,
                   pl.BlockSpec((n_exp, LANES), lambda b, i: (0, 0))],
        out_shape=[jax.ShapeDtypeStruct((bsz, s, d), F32),
                   jax.ShapeDtypeStruct((t * ROW_CHUNKS, LANES), F32),
                   jax.ShapeDtypeStruct((TOP_K, t), I32),
                   jax.ShapeDtypeStruct((TOP_K, t), F32),
                   jax.ShapeDtypeStruct((TOP_K, t), I32),
                   jax.ShapeDtypeStruct((n_exp, LANES), I32)],
        scratch_shapes=[pltpu.VMEM((n_exp, LANES), F32)],
        compiler_params=_params(("arbitrary", "arbitrary")),
        name="outproj_router",
    )(attn, rec, w_bf, x, gate_m, g_ffn.reshape(1, d), shift_f, scale_f, w_router.T,
      b_router.reshape(n_exp, 1))


def _row_slice(ref, row):
    return ref.at[pl.ds(pl.multiple_of(row * ROW_CHUNKS, ROW_CHUNKS), ROW_CHUNKS), :]


def _slots_kernel(dest_ref, fill_ref, slot_ref):
    n_assign = dest_ref.shape[0]
    n_fill = fill_ref.shape[0]
    unroll = 32

    def assign_body(i, c):
        for u in range(unroll):
            slot_ref[dest_ref[i * unroll + u]] = i * unroll + u
        return c

    lax.fori_loop(0, n_assign // unroll, assign_body, 0)

    def fill_body(i, c):
        for u in range(unroll):
            slot_ref[fill_ref[i * unroll + u]] = n_assign + i * unroll + u
        return c

    lax.fori_loop(0, n_fill // unroll, fill_body, 0)


def _slots(dest_flat, fill):
    smem = pl.BlockSpec(memory_space=pltpu.SMEM)
    n = dest_flat.shape[0] + fill.shape[0]
    return pl.pallas_call(
        _slots_kernel,
        in_specs=[smem, smem],
        out_specs=smem,
        out_shape=jax.ShapeDtypeStruct((n,), I32),
        name="slots",
    )(dest_flat, fill)


GATHER_BUFS = 3


def _experts_kernel(bexp_ref, nused_ref, slot_ref, h2_ref, wgu_ref, bgu_ref, wd_ref, bd_ref, y_ref,
                    xg_ref, xs_ref, wgu_bf, wd_bf, gsem):
    b = pl.program_id(0)
    nb = pl.num_programs(0)
    rows = xs_ref.shape[0]
    f = wd_ref.shape[0]
    n_tok = h2_ref.shape[0] // ROW_CHUNKS
    assert n_tok & (n_tok - 1) == 0
    used = b < nused_ref[0]
    changed = jnp.logical_or(b == 0, bexp_ref[b] != bexp_ref[jnp.maximum(b - 1, 0)])

    def buf_rows(ref, buf, i):
        if isinstance(i, int):
            return ref.at[buf, pl.ds(i * ROW_CHUNKS, ROW_CHUNKS), :]
        return ref.at[buf, pl.ds(pl.multiple_of(i * ROW_CHUNKS, ROW_CHUNKS), ROW_CHUNKS), :]

    def fetch_row(base, buf, i):
        tok = slot_ref[base + i] & (n_tok - 1)
        return pltpu.make_async_copy(_row_slice(h2_ref, tok), buf_rows(xg_ref, buf, i), gsem.at[buf])

    def fetch_block(blk, buf, rolled=False):
        base = blk * rows
        if rolled:
            def body(i, c):
                fetch_row(base, buf, i).start()
                return c
            lax.fori_loop(0, rows, body, 0)
        else:
            for i in range(rows):
                fetch_row(base, buf, i).start(priority=i % 2)

    def wait_block(buf):
        pltpu.make_async_copy(xg_ref.at[buf], xg_ref.at[buf], gsem.at[buf]).wait()

    @pl.when(b == 0)
    def _():
        fetch_block(0, 0, rolled=True)
        fetch_block(1, 1, rolled=True)

    gbuf = b % GATHER_BUFS
    wait_block(gbuf)

    def fetch_ahead():
        fetch_block(jnp.minimum(b + 2, nb - 1), (b + 2) % GATHER_BUFS)

    @pl.when(jnp.logical_and(used, changed))
    def _():
        wgu_bf[...] = wgu_ref[...].astype(BF16)
        wd_bf[...] = wd_ref[...].astype(BF16)

    @pl.when(used)
    def _():
        for c in range(ROW_CHUNKS):
            xs_ref[:, c * LANES:(c + 1) * LANES] = (
                xg_ref[gbuf, pl.ds(c, rows, stride=ROW_CHUNKS), :].astype(BF16))
        fetch_ahead()
        gu = jnp.dot(xs_ref[...], wgu_bf[...], preferred_element_type=F32) + bgu_ref[...]
        gate = jnp.minimum(gu[:, :f], SWIGLU_LIMIT)
        up = jnp.clip(gu[:, f:], -SWIGLU_LIMIT, SWIGLU_LIMIT)
        glu = gate * jax.nn.sigmoid(gate * SWIGLU_ALPHA)
        act = ((up + 1.0) * glu).astype(BF16)
        y = jnp.dot(act, wd_bf[...], preferred_element_type=F32) + bd_ref[...]
        for c in range(ROW_CHUNKS):
            y_ref[pl.ds(c, rows, stride=ROW_CHUNKS), :] = y[:, c * LANES:(c + 1) * LANES]

    @pl.when(jnp.logical_not(used))
    def _():
        fetch_ahead()
        y_ref[...] = jnp.zeros_like(y_ref)

    @pl.when(b == nb - 1)
    def _():
        wait_block((b + 1) % GATHER_BUFS)
        wait_block((b + 2) % GATHER_BUFS)


def _experts(h2rows, slots, blk_exp, n_used, w_gate_up, b_gate_up, w_down, b_down):
    n_exp, d, f2 = w_gate_up.shape
    f = w_down.shape[1]
    rows = EXPERT_ROWS
    n_blocks = slots.shape[0] // rows
    assert n_blocks >= GATHER_BUFS
    buf_shape = (rows * ROW_CHUNKS, LANES)
    return pl.pallas_call(
        _experts_kernel,
        grid_spec=pltpu.PrefetchScalarGridSpec(
            num_scalar_prefetch=3,
            grid=(n_blocks,),
            in_specs=[pl.BlockSpec(memory_space=pl.ANY),
                      pl.BlockSpec((None, d, f2), lambda b, e, n, s: (e[b], 0, 0)),
                      pl.BlockSpec((None, 1, f2), lambda b, e, n, s: (e[b], 0, 0)),
                      pl.BlockSpec((None, f, d), lambda b, e, n, s: (e[b], 0, 0)),
                      pl.BlockSpec((None, 1, d), lambda b, e, n, s: (e[b], 0, 0))],
            out_specs=pl.BlockSpec(buf_shape, lambda b, e, n, s: (b, 0)),
            scratch_shapes=[pltpu.VMEM((GATHER_BUFS,) + buf_shape, F32),
                            pltpu.VMEM((rows, d), BF16),
                            pltpu.VMEM((d, f2), BF16),
                            pltpu.VMEM((f, d), BF16),
                            pltpu.SemaphoreType.DMA((GATHER_BUFS,))]),
        out_shape=jax.ShapeDtypeStruct((slots.shape[0] * ROW_CHUNKS, LANES), F32),
        compiler_params=_params(("arbitrary",), 56 * 1024 * 1024),
        name="experts",
    )(blk_exp, n_used, slots, h2rows, w_gate_up, b_gate_up.reshape(n_exp, 1, f2), w_down,
      b_down.reshape(n_exp, 1, d))


def _combine_kernel(src_ref, cnt_ref, dst_ref, pos_ref, gate_ref, y_ref, x1_ref, gf_ref, g_ref, o_ref,
                    stage_ref, tok_ref, sem):
    i = pl.program_id(0)
    nt = pl.num_programs(0)
    tm = x1_ref.shape[0]
    n_exp = cnt_ref.shape[0] // nt

    def fetch(tile, buf):
        def per_expert(e, c):
            seg = tile * n_exp + e
            cnt, src, dst = cnt_ref[seg], src_ref[seg], dst_ref[seg]
            done = jnp.int32(0)
            size = tm
            while size >= 1:
                take = (cnt & size) != 0

                @pl.when(take)
                def _(size=size, done=done):
                    pltpu.make_async_copy(
                        y_ref.at[pl.ds(pl.multiple_of((src + done) * ROW_CHUNKS, ROW_CHUNKS),
                                       size * ROW_CHUNKS), :],
                        stage_ref.at[buf, pl.ds(pl.multiple_of((dst + done) * ROW_CHUNKS, ROW_CHUNKS),
                                                size * ROW_CHUNKS), :],
                        sem.at[buf]).start()

                done = done + (cnt & size)
                size //= 2
            return c
        lax.fori_loop(0, n_exp, per_expert, 0)

    @pl.when(i == 0)
    def _():
        fetch(0, 0)

    @pl.when(i + 1 < nt)
    def _():
        fetch(i + 1, (i + 1) % 2)

    buf = i % 2
    pltpu.make_async_copy(stage_ref.at[buf], stage_ref.at[buf], sem.at[buf]).wait()

    for t in range(tm):
        acc = None
        for kk in range(TOP_K):
            row = stage_ref[buf, pl.ds(pl.multiple_of(pos_ref[kk, t], ROW_CHUNKS), ROW_CHUNKS), :]
            term = row * gate_ref[kk, t]
            acc = term if acc is None else acc + term
        tok_ref[t * ROW_CHUNKS:(t + 1) * ROW_CHUNKS, :] = acc

    moe = jnp.concatenate([tok_ref[pl.ds(c, tm, stride=ROW_CHUNKS), :] for c in range(ROW_CHUNKS)],
                          axis=-1)
    x2 = x1_ref[...] + gf_ref[...] * moe
    o_ref[...] = (x2 * lax.rsqrt(jnp.mean(x2 * x2, axis=-1, keepdims=True) + NORM_EPS)) * g_ref[...]


def _combine(y, plan, x1, gate_f, g_final):
    bsz, s, d = x1.shape
    tm = COMBINE_ROWS
    nt = bsz * s // tm
    per_b = s // tm
    seg_src, seg_cnt, seg_dst, pos3, gates3 = plan
    per_tile = pl.BlockSpec((None, TOP_K, tm), lambda i, *_: (i, 0, 0), memory_space=pltpu.SMEM)
    return pl.pallas_call(
        _combine_kernel,
        grid_spec=pltpu.PrefetchScalarGridSpec(
            num_scalar_prefetch=3,
            grid=(nt,),
            in_specs=[per_tile, per_tile,
                      pl.BlockSpec(memory_space=pl.ANY),
                      pl.BlockSpec((None, tm, d), lambda i, *_: (i // per_b, i % per_b, 0)),
                      pl.BlockSpec((None, 1, d), lambda i, *_: (i // per_b, 0, 0)),
                      pl.BlockSpec((1, d), lambda i, *_: (0, 0))],
            out_specs=pl.BlockSpec((None, tm, d), lambda i, *_: (i // per_b, i % per_b, 0)),
            scratch_shapes=[pltpu.VMEM((2, TOP_K * tm * ROW_CHUNKS, LANES), F32),
                            pltpu.VMEM((tm * ROW_CHUNKS, LANES), F32),
                            pltpu.SemaphoreType.DMA((2,))]),
        out_shape=jax.ShapeDtypeStruct((bsz, s, d), F32),
        compiler_params=_params(("arbitrary",)),
        name="combine",
    )(seg_src, seg_cnt, seg_dst, pos3, gates3, y, x1, gate_f, g_final.reshape(1, d))


def _combine_plan(idx, rank, gates, pstart):
    tm = COMBINE_ROWS
    t = idx.shape[1]
    nt = t // tm
    experts = jnp.arange(N_EXPERTS, dtype=I32)
    hot = idx.reshape(TOP_K, nt, tm)[..., None] == experts
    tile_cnt = jnp.sum(hot.astype(I32), axis=(0, 2))
    tile_base = jnp.cumsum(tile_cnt, axis=0) - tile_cnt
    stage_at = jnp.cumsum(tile_cnt, axis=1) - tile_cnt
    pos = rank.reshape(TOP_K, nt, tm) + jnp.sum(
        jnp.where(hot, (stage_at - tile_base)[None, :, None, :], 0), axis=-1)
    flat = lambda a: a.reshape(-1).astype(I32)
    return (flat(pstart[None, :] + tile_base), flat(tile_cnt), flat(stage_at),
            (pos * ROW_CHUNKS).transpose(1, 0, 2).astype(I32),
            gates.reshape(TOP_K, nt, tm).transpose(1, 0, 2))


def _routing_plan(idx, rank, counts, n_rows):
    rows = EXPERT_ROWS
    experts = jnp.arange(N_EXPERTS, dtype=I32)

    def lookup(table, e):
        return jnp.sum(jnp.where(e[..., None] == experts, table, 0), axis=-1)

    def segment_of(ends, pos):
        return jnp.minimum(jnp.sum((pos[..., None] >= ends).astype(I32), axis=-1), N_EXPERTS - 1)

    padded = ((counts + rows - 1) // rows) * rows
    pend = jnp.cumsum(padded)
    pstart = pend - padded
    dest = lookup(pstart, idx) + rank
    n_blocks = n_rows // rows
    blk_exp = segment_of(pend, jnp.arange(n_blocks, dtype=I32) * rows)
    n_used = (pend[-1] // rows).astype(I32).reshape(1)
    pad = padded - counts
    pad_end = jnp.cumsum(pad)
    j = jnp.arange(n_rows - idx.size, dtype=I32)
    pe = segment_of(pad_end, j)
    in_pad = lookup(pstart + counts - (pad_end - pad), pe) + j
    fill = jnp.where(j < pad_end[-1], in_pad, pend[-1] + (j - pad_end[-1])).astype(I32)
    return dest.astype(I32), blk_exp.astype(I32), n_used, fill, pstart


def kernel(x, c, w_ada, b_ada, g_mix, w_in, conv_w, conv_b, w_rg_a, b_rg_a, w_rg_x, b_rg_x, lam, w_out,
           g_ffn, w_router, b_router, w_gate_up, b_gate_up, w_down, b_down, g_final):
    bsz, s, d = x.shape
    t = bsz * s
    depth = w_ada.shape[0]
    assert depth == 1, "the combine kernel applies the final norm; one layer only"
    for l in range(depth):
        mod = _ada(c, w_ada[l], b_ada[l]).reshape(bsz, 6, 1, d)
        shift_m, scale_m, gate_m, shift_f, scale_f, gate_f = (mod[:, j] for j in range(6))
        q, k, v, xr, gr = _inproj(x, g_mix[l], shift_m, scale_m, w_in[l].astype(BF16))
        attn = _attention(q, k, v)
        rec = _rglru(xr, gr, conv_w[l], conv_b[l], w_rg_a[l], b_rg_a[l], w_rg_x[l], b_rg_x[l], lam[l])
        x1, h2rows, idx, gates, rank, cnt = _outproj(
            attn, rec, w_out[l].astype(BF16), x, gate_m, g_ffn[l], shift_f, scale_f, w_router[l], b_router[l])
        n_rows = t * TOP_K + N_EXPERTS * EXPERT_ROWS
        dest, blk_exp, n_used, fill, pstart = _routing_plan(idx, rank, cnt[:, 0], n_rows)
        slots = _slots(dest.reshape(-1), fill)
        y = _experts(h2rows, slots, blk_exp, n_used, w_gate_up[l], b_gate_up[l], w_down[l], b_down[l])
        x = _combine(y, _combine_plan(idx, rank, gates, pstart), x1, gate_f, g_final)
    return x
```

```python
import functools

import jax
import jax.numpy as jnp
from jax import lax
from jax.experimental import pallas as pl
from jax.experimental.pallas import tpu as pltpu

F32 = jnp.float32
BF16 = jnp.bfloat16
I32 = jnp.int32

HEAD_DIM = 64
DILATED_PATTERNS = ((128, 1), (512, 4), (2048, 16))
REC_BLOCKS = 8
CONV_WIDTH = 4
RG_C = 8.0
N_EXPERTS = 32
TOP_K = 4
SWIGLU_LIMIT = 7.0
SWIGLU_ALPHA = 1.702
NORM_EPS = 1e-6
NEG_INF = -1e30
LOG2_E = 1.4426950408889634

LANES = 128
SUBLANES = 8
ROW_CHUNKS = 8
VMEM_LIMIT = 48 * 1024 * 1024

INPROJ_ROWS = 512
RGLRU_ROWS = 512
OUTPROJ_ROWS = 512
ATTN_BLOCK = 128
ATTN_UNROLL = 8
EXPERT_ROWS = 512
COMBINE_ROWS = 256


def _params(sem, vmem=VMEM_LIMIT):
    return pltpu.CompilerParams(dimension_semantics=sem, vmem_limit_bytes=vmem)


def _ada_kernel(c_ref, w_ref, b_ref, o_ref):
    o_ref[...] = jnp.dot(c_ref[...], w_ref[...], precision=lax.Precision.HIGHEST,
                         preferred_element_type=F32) + b_ref[...]


def _ada(c, w, b):
    bsz, d = c.shape
    n = w.shape[1]
    return pl.pallas_call(
        _ada_kernel,
        grid=(n // d,),
        in_specs=[pl.BlockSpec((bsz, d), lambda j: (0, 0)),
                  pl.BlockSpec((d, d), lambda j: (0, j)),
                  pl.BlockSpec((1, d), lambda j: (0, j))],
        out_specs=pl.BlockSpec((bsz, d), lambda j: (0, j)),
        out_shape=jax.ShapeDtypeStruct((bsz, n), F32),
        compiler_params=_params(("arbitrary",)),
        name="ada",
    )(c, w, b.reshape(1, n))


def _rms_modulate(x, g, shift, scale):
    y = x * lax.rsqrt(jnp.mean(x * x, axis=-1, keepdims=True) + NORM_EPS)
    return (y * g) * (1.0 + scale) + shift


def _inproj_kernel(x_ref, g_ref, sh_ref, sc_ref, w_ref, q_ref, k_ref, v_ref, xr_ref, gr_ref):
    h = _rms_modulate(x_ref[...], g_ref[...], sh_ref[...], sc_ref[...]).astype(BF16)
    outs = (q_ref, k_ref, v_ref, xr_ref, gr_ref)
    width = q_ref.shape[-1]
    for j, o_ref in enumerate(outs):
        z = jnp.dot(h, w_ref[:, j * width:(j + 1) * width], preferred_element_type=F32)
        o_ref[...] = z.astype(o_ref.dtype)


def _inproj(x, g, shift, scale, w_bf):
    bsz, s, d = x.shape
    n = w_bf.shape[1]
    width = n // 5
    tm = INPROJ_ROWS
    row = pl.BlockSpec((None, 1, d), lambda b, i: (b, 0, 0))
    out_blk = pl.BlockSpec((None, tm, width), lambda b, i: (b, i, 0))
    shp = lambda dt: jax.ShapeDtypeStruct((bsz, s, width), dt)
    return pl.pallas_call(
        _inproj_kernel,
        grid=(bsz, s // tm),
        in_specs=[pl.BlockSpec((None, tm, d), lambda b, i: (b, i, 0)),
                  pl.BlockSpec((1, d), lambda b, i: (0, 0)),
                  row, row,
                  pl.BlockSpec((d, n), lambda b, i: (0, 0))],
        out_specs=[out_blk] * 5,
        out_shape=[shp(BF16), shp(BF16), shp(BF16), shp(F32), shp(F32)],
        compiler_params=_params(("arbitrary", "arbitrary")),
        name="inproj",
    )(x, g.reshape(1, d), shift, scale, w_bf)


def _sigmoid(x):
    return 0.5 * jnp.tanh(0.5 * x) + 0.5


def _gelu_tanh(x):
    return 0.5 * x * (1.0 + jnp.tanh(0.7978845608028654 * (x + 0.044715 * (x * x * x))))


def _rglru_kernel(xr_ref, gr_ref, cw_ref, cb_ref, wa_ref, ba_ref, wx_ref, bx_ref, lam_ref,
                  o_ref, xe_ref, a_ref, u_ref, tail_ref, h_ref):
    ts, ch = xr_ref.shape
    pad = SUBLANES

    @pl.when(pl.program_id(1) == 0)
    def _():
        tail_ref[...] = jnp.zeros_like(tail_ref)
        h_ref[...] = jnp.zeros_like(h_ref)

    x = xr_ref[...]
    xe_ref[0:pad, :] = tail_ref[...]
    xe_ref[pad:pad + ts, :] = x
    tail_ref[...] = x[ts - pad:ts, :]
    xc = cb_ref[...] + jnp.zeros((ts, ch), F32)
    for j in range(CONV_WIDTH):
        off = pad - (CONV_WIDTH - 1) + j
        xc = xc + cw_ref[j:j + 1, :] * xe_ref[off:off + ts, :]

    xb = xc.astype(BF16)
    r = _sigmoid(jnp.dot(xb, wa_ref[...], preferred_element_type=F32) + ba_ref[...])
    i = _sigmoid(jnp.dot(xb, wx_ref[...], preferred_element_type=F32) + bx_ref[...])
    nl = -lam_ref[...]
    softplus = jnp.maximum(nl, 0.0) + jnp.log(1.0 + jnp.exp(-jnp.abs(nl)))
    log_a = (-RG_C) * r * softplus
    a = jnp.exp(log_a)
    mult = jnp.sqrt(1.0 - a * a)
    u = mult * (i * xc)

    ng = ts // SUBLANES
    a3 = a.reshape(ng, SUBLANES, ch)
    u3 = u.reshape(ng, SUBLANES, ch)
    rid = lax.broadcasted_iota(I32, (ng, SUBLANES, ch), 1)
    for sft in (1, 2, 4):
        a_s = pltpu.roll(a3, sft, 1)
        u_s = pltpu.roll(u3, sft, 1)
        keep = rid >= sft
        u3 = jnp.where(keep, a3 * u_s + u3, u3)
        a3 = jnp.where(keep, a3 * a_s, a3)
    a_ref[...] = a3.reshape(ts, ch)
    u_ref[...] = u3.reshape(ts, ch)

    inner = 8

    def body(gi, hprev):
        for jj in range(inner):
            base = pl.multiple_of((gi * inner + jj) * SUBLANES, SUBLANES)
            hcur = u_ref[pl.ds(base, SUBLANES), :] + a_ref[pl.ds(base, SUBLANES), :] * hprev
            u_ref[pl.ds(base, SUBLANES), :] = hcur
            hprev = hcur[SUBLANES - 1:SUBLANES, :]
        return hprev

    h_ref[...] = lax.fori_loop(0, ng // inner, body, h_ref[...])
    o_ref[...] = (u_ref[...] * _gelu_tanh(gr_ref[...])).astype(o_ref.dtype)


def _block_diag(w):
    nb, hin, hout = w.shape
    eye = jnp.eye(nb, dtype=w.dtype)
    return (eye[:, None, :, None] * w[:, :, None, :]).reshape(nb * hin, nb * hout)


def _rglru(xr, gr, conv_w, conv_b, w_a, b_a, w_x, b_x, lam):
    bsz, s, ch = xr.shape
    ts = RGLRU_ROWS
    blk = pl.BlockSpec((None, ts, ch), lambda b, t: (b, t, 0))
    full = lambda r, c: pl.BlockSpec((r, c), lambda b, t: (0, 0))
    return pl.pallas_call(
        _rglru_kernel,
        grid=(bsz, s // ts),
        in_specs=[blk, blk, full(CONV_WIDTH, ch), full(1, ch), full(ch, ch), full(1, ch),
                  full(ch, ch), full(1, ch), full(1, ch)],
        out_specs=blk,
        out_shape=jax.ShapeDtypeStruct((bsz, s, ch), BF16),
        scratch_shapes=[pltpu.VMEM((ts + SUBLANES, ch), F32),
                        pltpu.VMEM((ts, ch), F32),
                        pltpu.VMEM((ts, ch), F32),
                        pltpu.VMEM((SUBLANES, ch), F32),
                        pltpu.VMEM((1, ch), F32)],
        compiler_params=_params(("arbitrary", "arbitrary")),
        name="rglru",
    )(xr, gr, conv_w, conv_b.reshape(1, ch), _block_diag(w_a).astype(BF16), b_a.reshape(1, ch),
      _block_diag(w_x).astype(BF16), b_x.reshape(1, ch), lam.reshape(1, ch))


def _attn_kernel(q_ref, k_ref, v_ref, o_ref, f_ref, f4_ref, qs_ref, kp_ref, vp_ref, op_ref, lp_ref,
                 pc_ref, ml_ref):
    s = q_ref.shape[0]
    blk = ATTN_BLOCK
    n_blocks = s // blk
    unroll = ATTN_UNROLL

    lane = lax.broadcasted_iota(I32, (blk, LANES), 1)
    head0 = lane < HEAD_DIM
    head0_wide = lax.broadcasted_iota(I32, (blk, 2 * LANES), 1) % LANES < HEAD_DIM
    qi = lax.broadcasted_iota(I32, (2 * blk, blk), 0) % blk
    kj = lax.broadcasted_iota(I32, (2 * blk, blk), 1)
    prev_ok = kj >= qi
    cur_ok = kj <= qi
    dims = (((1,), (1,)), ((), ()))

    def block_rows(j, dil):
        per_res = (s // dil) // blk
        start = j // per_res + (j % per_res) * (dil * blk)
        if dil == 1:
            return pl.ds(pl.multiple_of(start, blk), blk)
        return pl.ds(start, blk, stride=dil)

    dils = tuple(d for _, d in DILATED_PATTERNS)
    assert dils == (1, 4, 16)

    def permute(src_ref, store):
        f_ref[...] = src_ref[...].astype(F32)

        def body0(j, c):
            store(0, j, f_ref[block_rows(j, 1), :])
            return c

        def body1(j, c):
            x = f_ref[block_rows(j, 4), :]
            f4_ref[pl.ds(pl.multiple_of(j * blk, blk), blk), :] = x
            store(1, j, x)
            return c

        def body2(j, c):
            per_res = (s // 16) // blk
            res, nblk = j // per_res, j % per_res
            start = (res % 4) * (s // 4) + res // 4 + nblk * (4 * blk)
            store(2, j, f4_ref[pl.ds(start, blk, stride=4), :])
            return c

        inner = 4
        for body in (body0, body1, body2):
            def several(g, c, body=body):
                for u in range(inner):
                    body(g * inner + u, c)
                return c
            lax.fori_loop(0, n_blocks // inner, several, 0)

    def store_q(p, j, x):
        x = x * (HEAD_DIM ** -0.5 * LOG2_E)
        base = pl.multiple_of(j * (2 * blk), 2 * blk)
        qs_ref[p, pl.ds(base, blk), :] = jnp.where(head0, x, 0.0).astype(BF16)
        qs_ref[p, pl.ds(base + blk, blk), :] = jnp.where(head0, 0.0, x).astype(BF16)

    def store_kv(dst_ref):
        def store(p, j, x):
            dst_ref[p, pl.ds(pl.multiple_of((j + 1) * blk, blk), blk), 0:LANES] = x.astype(BF16)
        return store

    for p in range(len(DILATED_PATTERNS)):
        kp_ref[p, 0:blk, :] = jnp.zeros((blk, LANES), BF16)
        vp_ref[p, 0:blk, 0:LANES] = jnp.zeros((blk, LANES), BF16)
        vp_ref[p, :, LANES:] = jnp.ones((s + blk, LANES), BF16)
    permute(q_ref, store_q)
    permute(k_ref, store_kv(kp_ref))
    permute(v_ref, store_kv(vp_ref))

    def kv_rows(j):
        return pl.ds(pl.multiple_of(j * blk, blk), 2 * blk)

    def scores(p, g):
        per_res = (s // dils[p]) // blk
        for u in range(unroll):
            j = g * unroll + u
            no_prev = jnp.where(j % per_res > 0, 0.0, NEG_INF)
            qs = qs_ref[p, pl.ds(pl.multiple_of(j * (2 * blk), 2 * blk), 2 * blk), :]
            sc = lax.dot_general(qs, kp_ref[p, kv_rows(j), :], dims, preferred_element_type=F32)
            s_prev = jnp.where(prev_ok, sc[:, :blk], NEG_INF) + no_prev
            s_cur = jnp.where(cur_ok, sc[:, blk:], NEG_INF)
            m = jnp.max(jnp.maximum(s_prev, s_cur), axis=-1, keepdims=True)
            pc_ref[g % 2, u] = jnp.concatenate(
                [jnp.exp2(s_prev - m), jnp.exp2(s_cur - m)], axis=-1).astype(BF16)
            ml_ref[g % 2, u] = jnp.where(head0, m[:blk], m[blk:])

    def outputs(p, g):
        for u in range(unroll):
            j = g * unroll + u
            rl = jnp.dot(pc_ref[g % 2, u], vp_ref[p, kv_rows(j), :], preferred_element_type=F32)
            rl = jnp.where(head0_wide, rl[:blk], rl[blk:])
            l = rl[:, LANES:]
            rows = block_rows(j, dils[p])
            op_ref[p, rows, :] = rl[:, :LANES] / l
            lp_ref[p, rows, :] = ml_ref[g % 2, u] + jnp.log2(l)

    n_groups = n_blocks // unroll
    assert n_groups % 2 == 0
    n_pat = len(DILATED_PATTERNS)
    scores(0, 0)
    for p, (window, dil) in enumerate(DILATED_PATTERNS):
        assert window // dil == blk

        def body(g, carry, p=p):
            outputs(p, g - 1)
            scores(p, g)
            return carry

        lax.fori_loop(1, n_groups, body, 0)
        outputs(p, n_groups - 1)
        if p + 1 < n_pat:
            scores(p + 1, 0)

    chunk = 512
    for c0 in range(0, s, chunk):
        sl = pl.ds(c0, chunk)
        l0, l1, l2 = lp_ref[0, sl, :], lp_ref[1, sl, :], lp_ref[2, sl, :]
        m = jnp.maximum(jnp.maximum(l0, l1), l2)
        w0, w1, w2 = jnp.exp2(l0 - m), jnp.exp2(l1 - m), jnp.exp2(l2 - m)
        num = w0 * op_ref[0, sl, :] + w1 * op_ref[1, sl, :] + w2 * op_ref[2, sl, :]
        o_ref[sl, :] = (num / (w0 + w1 + w2)).astype(o_ref.dtype)


def _attention(q, k, v):
    bsz, s, da = q.shape
    blk = pl.BlockSpec((None, s, LANES), lambda b, h: (b, 0, h))
    n_pat = len(DILATED_PATTERNS)
    return pl.pallas_call(
        _attn_kernel,
        grid=(bsz, da // LANES),
        in_specs=[blk, blk, blk],
        out_specs=blk,
        out_shape=jax.ShapeDtypeStruct((bsz, s, da), BF16),
        scratch_shapes=[pltpu.VMEM((s, LANES), F32),
                        pltpu.VMEM((s, LANES), F32),
                        pltpu.VMEM((n_pat, 2 * s, LANES), BF16),
                        pltpu.VMEM((n_pat, s + ATTN_BLOCK, LANES), BF16),
                        pltpu.VMEM((n_pat, s + ATTN_BLOCK, 2 * LANES), BF16),
                        pltpu.VMEM((n_pat, s, LANES), F32),
                        pltpu.VMEM((n_pat, s, LANES), F32),
                        pltpu.VMEM((2, ATTN_UNROLL, 2 * ATTN_BLOCK, 2 * ATTN_BLOCK), BF16),
                        pltpu.VMEM((2, ATTN_UNROLL, ATTN_BLOCK, LANES), F32)],
        compiler_params=_params(("arbitrary", "arbitrary"), 56 * 1024 * 1024),
        name="attn",
    )(q, k, v)


def _outproj_kernel(attn_ref, rec_ref, w_ref, x_ref, gm_ref, g_ref, sh_ref, sc_ref, wr_ref, br_ref,
                    x1_ref, h2_ref, idx_ref, gate_ref, rank_ref, cnt_ref, base_ref):
    tm = x_ref.shape[0]
    da = attn_ref.shape[1]

    @pl.when(jnp.logical_and(pl.program_id(0) == 0, pl.program_id(1) == 0))
    def _():
        base_ref[...] = jnp.zeros_like(base_ref)

    n_exp = wr_ref.shape[0]
    parts = 1
    tp = tm // parts
    eid = lax.broadcasted_iota(I32, (n_exp, tp), 0)
    tr = lax.broadcasted_iota(I32, (tp, tp), 0)
    tc = lax.broadcasted_iota(I32, (tp, tp), 1)
    before = jnp.where(tr < tc, 1.0, 0.0).astype(BF16)
    base = base_ref[:, 0:1]

    for part in range(parts):
        r0 = part * tp
        y = (jnp.dot(attn_ref[r0:r0 + tp, :], w_ref[0:da, :], preferred_element_type=F32)
             + jnp.dot(rec_ref[r0:r0 + tp, :], w_ref[da:, :], preferred_element_type=F32))
        x1 = x_ref[r0:r0 + tp, :] + gm_ref[...] * y
        x1_ref[r0:r0 + tp, :] = x1
        h2 = _rms_modulate(x1, g_ref[...], sh_ref[...], sc_ref[...])
        for c in range(ROW_CHUNKS):
            h2_ref[pl.ds(r0 * ROW_CHUNKS + c, tp, stride=ROW_CHUNKS), :] = h2[:, c * LANES:(c + 1) * LANES]

        work = lax.dot_general(wr_ref[...], h2, (((1,), (1,)), ((), ())), precision=lax.Precision.HIGHEST,
                               preferred_element_type=F32) + br_ref[...]
        vals, hots = [], []
        for kk in range(TOP_K):
            m = jnp.max(work, axis=0, keepdims=True)
            sel = jnp.min(jnp.where(work == m, eid, n_exp), axis=0, keepdims=True)
            hot = eid == sel
            idx_ref[kk:kk + 1, r0:r0 + tp] = sel
            vals.append(m)
            hots.append(hot)
            work = jnp.where(hot, -jnp.inf, work)
        exps = [jnp.exp(v - vals[0]) for v in vals]
        den = exps[0] + exps[1] + exps[2] + exps[3]
        for kk in range(TOP_K):
            gate_ref[kk:kk + 1, r0:r0 + tp] = exps[kk] / den

        cnt = jnp.zeros((n_exp, tp), F32)
        for hot in hots:
            cnt = cnt + hot.astype(F32)
        prior = jnp.dot(cnt.astype(BF16), before, preferred_element_type=F32) + base
        for kk in range(TOP_K):
            rank_ref[kk:kk + 1, r0:r0 + tp] = jnp.sum(jnp.where(hots[kk], prior, 0.0), axis=0,
                                                      keepdims=True).astype(I32)
        base = base + jnp.sum(cnt, axis=1, keepdims=True)

    base_ref[...] = jnp.broadcast_to(base, base_ref.shape)
    cnt_ref[...] = base_ref[...].astype(I32)


def _outproj(attn, rec, w_bf, x, gate_m, g_ffn, shift_f, scale_f, w_router, b_router):
    bsz, s, d = x.shape
    da = attn.shape[-1]
    n_exp = w_router.shape[1]
    tm = OUTPROJ_ROWS
    nt = s // tm
    t = bsz * s
    row = pl.BlockSpec((None, 1, d), lambda b, i: (b, 0, 0))
    small = pl.BlockSpec((TOP_K, tm), lambda b, i: (0, b * nt + i))
    return pl.pallas_call(
        _outproj_kernel,
        grid=(bsz, nt),
        in_specs=[pl.BlockSpec((None, tm, da), lambda b, i: (b, i, 0)),
                  pl.BlockSpec((None, tm, da), lambda b, i: (b, i, 0)),
                  pl.BlockSpec((d, d), lambda b, i: (0, 0)),
                  pl.BlockSpec((None, tm, d), lambda b, i: (b, i, 0)),
                  row,
                  pl.BlockSpec((1, d), lambda b, i: (0, 0)),
                  row, row,
                  pl.BlockSpec((n_exp, d), lambda b, i: (0, 0)),
                  pl.BlockSpec((n_exp, 1), lambda b, i: (0, 0))],
        out_specs=[pl.BlockSpec((None, tm, d), lambda b, i: (b, i, 0)),
                   pl.BlockSpec((tm * ROW_CHUNKS, LANES), lambda b, i: (b * nt + i, 0)),
                   small, small, small,
                   pl.BlockSpec((n_exp, LANES), lambda b, i: (0, 0))],
        out_shape=[jax.ShapeDtypeStruct((bsz, s, d), F32),
                   jax.ShapeDtypeStruct((t * ROW_CHUNKS, LANES), F32),
                   jax.ShapeDtypeStruct((TOP_K, t), I32),
                   jax.ShapeDtypeStruct((TOP_K, t), F32),
                   jax.ShapeDtypeStruct((TOP_K, t), I32),
                   jax.ShapeDtypeStruct((n_exp, LANES), I32)],
        scratch_shapes=[pltpu.VMEM((n_exp, LANES), F32)],
        compiler_params=_params(("arbitrary", "arbitrary")),
        name="outproj_router",
    )(attn, rec, w_bf, x, gate_m, g_ffn.reshape(1, d), shift_f, scale_f, w_router.T,
      b_router.reshape(n_exp, 1))


def _row_slice(ref, row):
    return ref.at[pl.ds(pl.multiple_of(row * ROW_CHUNKS, ROW_CHUNKS), ROW_CHUNKS), :]


def _slots_kernel(dest_ref, slot_ref, zero_ref, sem):
    n_assign = dest_ref.shape[0]
    unroll = 32

    zero_ref[...] = jnp.zeros_like(zero_ref)
    clear = pltpu.make_async_copy(zero_ref, slot_ref, sem)
    clear.start()
    clear.wait()

    def assign_body(i, c):
        for u in range(unroll):
            slot_ref[dest_ref[i * unroll + u]] = i * unroll + u
        return c

    lax.fori_loop(0, n_assign // unroll, assign_body, 0)


def _slots(dest_flat, n_slots):
    smem = pl.BlockSpec(memory_space=pltpu.SMEM)
    return pl.pallas_call(
        _slots_kernel,
        in_specs=[smem],
        out_specs=smem,
        out_shape=jax.ShapeDtypeStruct((n_slots,), I32),
        scratch_shapes=[pltpu.VMEM((n_slots,), I32), pltpu.SemaphoreType.DMA(())],
        name="slots",
    )(dest_flat)


GATHER_BUFS = 3


def _experts_kernel(bexp_ref, nused_ref, slot_ref, h2_ref, wgu_ref, bgu_ref, wd_ref, bd_ref, y_ref,
                    xg_ref, xs_ref, wgu_bf, wd_bf, gsem):
    b = pl.program_id(0)
    nb = pl.num_programs(0)
    rows = xs_ref.shape[0]
    f = wd_ref.shape[0]
    n_tok = h2_ref.shape[0] // ROW_CHUNKS
    assert n_tok & (n_tok - 1) == 0
    used = b < nused_ref[0]
    changed = jnp.logical_or(b == 0, bexp_ref[b] != bexp_ref[jnp.maximum(b - 1, 0)])

    def buf_rows(ref, buf, i):
        if isinstance(i, int):
            return ref.at[buf, pl.ds(i * ROW_CHUNKS, ROW_CHUNKS), :]
        return ref.at[buf, pl.ds(pl.multiple_of(i * ROW_CHUNKS, ROW_CHUNKS), ROW_CHUNKS), :]

    def fetch_row(base, buf, i):
        tok = slot_ref[base + i] & (n_tok - 1)
        return pltpu.make_async_copy(_row_slice(h2_ref, tok), buf_rows(xg_ref, buf, i), gsem.at[buf])

    def fetch_block(blk, buf, rolled=False):
        base = blk * rows
        if rolled:
            def body(i, c):
                fetch_row(base, buf, i).start()
                return c
            lax.fori_loop(0, rows, body, 0)
        else:
            for i in range(rows):
                fetch_row(base, buf, i).start(priority=i % 2)

    def wait_block(buf):
        pltpu.make_async_copy(xg_ref.at[buf], xg_ref.at[buf], gsem.at[buf]).wait()

    @pl.when(b == 0)
    def _():
        fetch_block(0, 0, rolled=True)
        fetch_block(1, 1, rolled=True)

    gbuf = b % GATHER_BUFS
    n_used = nused_ref[0]

    @pl.when(b < n_used + 2)
    def _():
        wait_block(gbuf)

    def fetch_ahead():
        fetch_block(jnp.minimum(b + 2, nb - 1), (b + 2) % GATHER_BUFS)

    @pl.when(jnp.logical_and(used, changed))
    def _():
        wgu_bf[...] = wgu_ref[...].astype(BF16)
        wd_bf[...] = wd_ref[...].astype(BF16)

    @pl.when(used)
    def _():
        for c in range(ROW_CHUNKS):
            xs_ref[:, c * LANES:(c + 1) * LANES] = (
                xg_ref[gbuf, pl.ds(c, rows, stride=ROW_CHUNKS), :].astype(BF16))
        fetch_ahead()
        gu = jnp.dot(xs_ref[...], wgu_bf[...], preferred_element_type=F32) + bgu_ref[...]
        gate = jnp.minimum(gu[:, :f], SWIGLU_LIMIT)
        up = jnp.clip(gu[:, f:], -SWIGLU_LIMIT, SWIGLU_LIMIT)
        glu = gate * jax.nn.sigmoid(gate * SWIGLU_ALPHA)
        act = ((up + 1.0) * glu).astype(BF16)
        y = jnp.dot(act, wd_bf[...], preferred_element_type=F32) + bd_ref[...]
        for c in range(ROW_CHUNKS):
            y_ref[pl.ds(c, rows, stride=ROW_CHUNKS), :] = y[:, c * LANES:(c + 1) * LANES]

    @pl.when(jnp.logical_not(used))
    def _():
        y_ref[...] = jnp.zeros_like(y_ref)

    for late in (1, 2):
        @pl.when(jnp.logical_and(b == nb - 1, n_used > nb - 3 + late))
        def _(late=late):
            wait_block((b + late) % GATHER_BUFS)


def _experts(h2rows, slots, blk_exp, n_used, w_gate_up, b_gate_up, w_down, b_down):
    n_exp, d, f2 = w_gate_up.shape
    f = w_down.shape[1]
    rows = EXPERT_ROWS
    n_blocks = slots.shape[0] // rows
    assert n_blocks >= GATHER_BUFS
    buf_shape = (rows * ROW_CHUNKS, LANES)
    return pl.pallas_call(
        _experts_kernel,
        grid_spec=pltpu.PrefetchScalarGridSpec(
            num_scalar_prefetch=3,
            grid=(n_blocks,),
            in_specs=[pl.BlockSpec(memory_space=pl.ANY),
                      pl.BlockSpec((None, d, f2), lambda b, e, n, s: (e[b], 0, 0)),
                      pl.BlockSpec((None, 1, f2), lambda b, e, n, s: (e[b], 0, 0)),
                      pl.BlockSpec((None, f, d), lambda b, e, n, s: (e[b], 0, 0)),
                      pl.BlockSpec((None, 1, d), lambda b, e, n, s: (e[b], 0, 0))],
            out_specs=pl.BlockSpec(buf_shape, lambda b, e, n, s: (b, 0)),
            scratch_shapes=[pltpu.VMEM((GATHER_BUFS,) + buf_shape, F32),
                            pltpu.VMEM((rows, d), BF16),
                            pltpu.VMEM((d, f2), BF16),
                            pltpu.VMEM((f, d), BF16),
                            pltpu.SemaphoreType.DMA((GATHER_BUFS,))]),
        out_shape=jax.ShapeDtypeStruct((slots.shape[0] * ROW_CHUNKS, LANES), F32),
        compiler_params=_params(("arbitrary",), 56 * 1024 * 1024),
        name="experts",
    )(blk_exp, n_used, slots, h2rows, w_gate_up, b_gate_up.reshape(n_exp, 1, f2), w_down,
      b_down.reshape(n_exp, 1, d))


def _combine_kernel(src_ref, cnt_ref, dst_ref, pos_ref, gate_ref, y_ref, x1_ref, gf_ref, g_ref, o_ref,
                    stage_ref, tok_ref, sem):
    i = pl.program_id(0)
    nt = pl.num_programs(0)
    tm = x1_ref.shape[0]
    n_exp = cnt_ref.shape[0] // nt

    def fetch(tile, buf):
        def per_expert(e, c):
            seg = tile * n_exp + e
            cnt, src, dst = cnt_ref[seg], src_ref[seg], dst_ref[seg]
            done = jnp.int32(0)
            size = tm
            while size >= 1:
                take = (cnt & size) != 0

                @pl.when(take)
                def _(size=size, done=done):
                    pltpu.make_async_copy(
                        y_ref.at[pl.ds(pl.multiple_of((src + done) * ROW_CHUNKS, ROW_CHUNKS),
                                       size * ROW_CHUNKS), :],
                        stage_ref.at[buf, pl.ds(pl.multiple_of((dst + done) * ROW_CHUNKS, ROW_CHUNKS),
                                                size * ROW_CHUNKS), :],
                        sem.at[buf]).start()

                done = done + (cnt & size)
                size //= 2
            return c
        lax.fori_loop(0, n_exp, per_expert, 0)

    @pl.when(i == 0)
    def _():
        fetch(0, 0)

    @pl.when(i + 1 < nt)
    def _():
        fetch(i + 1, (i + 1) % 2)

    buf = i % 2
    pltpu.make_async_copy(stage_ref.at[buf], stage_ref.at[buf], sem.at[buf]).wait()

    for t in range(tm):
        acc = None
        for kk in range(TOP_K):
            row = stage_ref[buf, pl.ds(pl.multiple_of(pos_ref[0, kk * tm + t], ROW_CHUNKS), ROW_CHUNKS), :]
            term = row * gate_ref[0, kk * tm + t]
            acc = term if acc is None else acc + term
        tok_ref[t * ROW_CHUNKS:(t + 1) * ROW_CHUNKS, :] = acc

    moe = jnp.concatenate([tok_ref[pl.ds(c, tm, stride=ROW_CHUNKS), :] for c in range(ROW_CHUNKS)],
                          axis=-1)
    x2 = x1_ref[...] + gf_ref[...] * moe
    o_ref[...] = (x2 * lax.rsqrt(jnp.mean(x2 * x2, axis=-1, keepdims=True) + NORM_EPS)) * g_ref[...]


def _combine(y, plan, x1, gate_f, g_final):
    bsz, s, d = x1.shape
    tm = COMBINE_ROWS
    nt = bsz * s // tm
    per_b = s // tm
    seg_src, seg_cnt, seg_dst, pos3, gates3 = plan
    per_tile = pl.BlockSpec((None, 1, TOP_K * tm), lambda i, *_: (i, 0, 0), memory_space=pltpu.SMEM)
    return pl.pallas_call(
        _combine_kernel,
        grid_spec=pltpu.PrefetchScalarGridSpec(
            num_scalar_prefetch=3,
            grid=(nt,),
            in_specs=[per_tile, per_tile,
                      pl.BlockSpec(memory_space=pl.ANY),
                      pl.BlockSpec((None, tm, d), lambda i, *_: (i // per_b, i % per_b, 0)),
                      pl.BlockSpec((None, 1, d), lambda i, *_: (i // per_b, 0, 0)),
                      pl.BlockSpec((1, d), lambda i, *_: (0, 0))],
            out_specs=pl.BlockSpec((None, tm, d), lambda i, *_: (i // per_b, i % per_b, 0)),
            scratch_shapes=[pltpu.VMEM((2, TOP_K * tm * ROW_CHUNKS, LANES), F32),
                            pltpu.VMEM((tm * ROW_CHUNKS, LANES), F32),
                            pltpu.SemaphoreType.DMA((2,))]),
        out_shape=jax.ShapeDtypeStruct((bsz, s, d), F32),
        compiler_params=_params(("arbitrary",)),
        name="combine",
    )(seg_src, seg_cnt, seg_dst, pos3, gates3, y, x1, gate_f, g_final.reshape(1, d))


def _combine_plan(idx, rank, gates, pstart):
    tm = COMBINE_ROWS
    t = idx.shape[1]
    nt = t // tm
    experts = jnp.arange(N_EXPERTS, dtype=I32)
    hot = idx.reshape(TOP_K, nt, tm)[..., None] == experts
    tile_cnt = jnp.sum(hot.astype(I32), axis=(0, 2))
    tile_base = jnp.cumsum(tile_cnt, axis=0) - tile_cnt
    stage_at = jnp.cumsum(tile_cnt, axis=1) - tile_cnt
    pos = rank.reshape(TOP_K, nt, tm) + jnp.sum(
        jnp.where(hot, (stage_at - tile_base)[None, :, None, :], 0), axis=-1)
    flat = lambda a: a.reshape(-1).astype(I32)
    return (flat(pstart[None, :] + tile_base), flat(tile_cnt), flat(stage_at),
            (pos * ROW_CHUNKS).transpose(1, 0, 2).reshape(nt, 1, TOP_K * tm).astype(I32),
            gates.reshape(TOP_K, nt, tm).transpose(1, 0, 2).reshape(nt, 1, TOP_K * tm))


def _routing_plan(idx, rank, counts, n_rows):
    rows = EXPERT_ROWS
    experts = jnp.arange(N_EXPERTS, dtype=I32)

    def lookup(table, e):
        return jnp.sum(jnp.where(e[..., None] == experts, table, 0), axis=-1)

    def segment_of(ends, pos):
        return jnp.minimum(jnp.sum((pos[..., None] >= ends).astype(I32), axis=-1), N_EXPERTS - 1)

    padded = ((counts + rows - 1) // rows) * rows
    pend = jnp.cumsum(padded)
    pstart = pend - padded
    dest = lookup(pstart, idx) + rank
    n_blocks = n_rows // rows
    blk_exp = segment_of(pend, jnp.arange(n_blocks, dtype=I32) * rows)
    n_used = (pend[-1] // rows).astype(I32).reshape(1)
    return dest.astype(I32), blk_exp.astype(I32), n_used, pstart


def kernel(x, c, w_ada, b_ada, g_mix, w_in, conv_w, conv_b, w_rg_a, b_rg_a, w_rg_x, b_rg_x, lam, w_out,
           g_ffn, w_router, b_router, w_gate_up, b_gate_up, w_down, b_down, g_final):
    bsz, s, d = x.shape
    t = bsz * s
    depth = w_ada.shape[0]
    assert depth == 1, "the combine kernel applies the final norm; one layer only"
    for l in range(depth):
        mod = _ada(c, w_ada[l], b_ada[l]).reshape(bsz, 6, 1, d)
        shift_m, scale_m, gate_m, shift_f, scale_f, gate_f = (mod[:, j] for j in range(6))
        q, k, v, xr, gr = _inproj(x, g_mix[l], shift_m, scale_m, w_in[l].astype(BF16))
        attn = _attention(q, k, v)
        rec = _rglru(xr, gr, conv_w[l], conv_b[l], w_rg_a[l], b_rg_a[l], w_rg_x[l], b_rg_x[l], lam[l])
        x1, h2rows, idx, gates, rank, cnt = _outproj(
            attn, rec, w_out[l].astype(BF16), x, gate_m, g_ffn[l], shift_f, scale_f, w_router[l], b_router[l])
        n_rows = t * TOP_K + N_EXPERTS * EXPERT_ROWS
        dest, blk_exp, n_used, pstart = _routing_plan(idx, rank, cnt[:, 0], n_rows)
        slots = _slots(dest.reshape(-1), n_rows)
        y = _experts(h2rows, slots, blk_exp, n_used, w_gate_up[l], b_gate_up[l], w_down[l], b_down[l])
        x = _combine(y, _combine_plan(idx, rank, gates, pstart), x1, gate_f, g_final)
    return x
```

```python
import functools

import jax
import jax.numpy as jnp
from jax import lax
from jax.experimental import pallas as pl
from jax.experimental.pallas import tpu as pltpu

F32 = jnp.float32
BF16 = jnp.bfloat16
I32 = jnp.int32

HEAD_DIM = 64
DILATED_PATTERNS = ((128, 1), (512, 4), (2048, 16))
REC_BLOCKS = 8
CONV_WIDTH = 4
RG_C = 8.0
N_EXPERTS = 32
TOP_K = 4
SWIGLU_LIMIT = 7.0
SWIGLU_ALPHA = 1.702
NORM_EPS = 1e-6
NEG_INF = -1e30
LOG2_E = 1.4426950408889634

LANES = 128
SUBLANES = 8
ROW_CHUNKS = 8
VMEM_LIMIT = 48 * 1024 * 1024

INPROJ_ROWS = 512
RGLRU_ROWS = 512
OUTPROJ_ROWS = 512
ATTN_BLOCK = 128
ATTN_UNROLL = 8
EXPERT_ROWS = 512
COMBINE_ROWS = 256


def _params(sem, vmem=VMEM_LIMIT):
    return pltpu.CompilerParams(dimension_semantics=sem, vmem_limit_bytes=vmem)


def _ada_kernel(c_ref, w_ref, b_ref, o_ref):
    o_ref[...] = jnp.dot(c_ref[...], w_ref[...], precision=lax.Precision.HIGHEST,
                         preferred_element_type=F32) + b_ref[...]


def _ada(c, w, b):
    bsz, d = c.shape
    n = w.shape[1]
    return pl.pallas_call(
        _ada_kernel,
        grid=(n // d,),
        in_specs=[pl.BlockSpec((bsz, d), lambda j: (0, 0)),
                  pl.BlockSpec((d, d), lambda j: (0, j)),
                  pl.BlockSpec((1, d), lambda j: (0, j))],
        out_specs=pl.BlockSpec((bsz, d), lambda j: (0, j)),
        out_shape=jax.ShapeDtypeStruct((bsz, n), F32),
        compiler_params=_params(("arbitrary",)),
        name="ada",
    )(c, w, b.reshape(1, n))


def _rms_modulate(x, g, shift, scale):
    y = x * lax.rsqrt(jnp.mean(x * x, axis=-1, keepdims=True) + NORM_EPS)
    return (y * g) * (1.0 + scale) + shift


def _inproj_kernel(x_ref, g_ref, sh_ref, sc_ref, w_ref, q_ref, k_ref, v_ref, xr_ref, gr_ref):
    h = _rms_modulate(x_ref[...], g_ref[...], sh_ref[...], sc_ref[...]).astype(BF16)
    outs = (q_ref, k_ref, v_ref, xr_ref, gr_ref)
    width = q_ref.shape[-1]
    for j, o_ref in enumerate(outs):
        z = jnp.dot(h, w_ref[:, j * width:(j + 1) * width], preferred_element_type=F32)
        o_ref[...] = z.astype(o_ref.dtype)


def _inproj(x, g, shift, scale, w_bf):
    bsz, s, d = x.shape
    n = w_bf.shape[1]
    width = n // 5
    tm = INPROJ_ROWS
    row = pl.BlockSpec((None, 1, d), lambda b, i: (b, 0, 0))
    out_blk = pl.BlockSpec((None, tm, width), lambda b, i: (b, i, 0))
    shp = lambda dt: jax.ShapeDtypeStruct((bsz, s, width), dt)
    return pl.pallas_call(
        _inproj_kernel,
        grid=(bsz, s // tm),
        in_specs=[pl.BlockSpec((None, tm, d), lambda b, i: (b, i, 0)),
                  pl.BlockSpec((1, d), lambda b, i: (0, 0)),
                  row, row,
                  pl.BlockSpec((d, n), lambda b, i: (0, 0))],
        out_specs=[out_blk] * 5,
        out_shape=[shp(BF16), shp(BF16), shp(BF16), shp(F32), shp(F32)],
        compiler_params=_params(("arbitrary", "arbitrary")),
        name="inproj",
    )(x, g.reshape(1, d), shift, scale, w_bf)


def _sigmoid(x):
    return 0.5 * jnp.tanh(0.5 * x) + 0.5


def _gelu_tanh(x):
    return 0.5 * x * (1.0 + jnp.tanh(0.7978845608028654 * (x + 0.044715 * (x * x * x))))


def _rglru_kernel(xr_ref, gr_ref, cw_ref, cb_ref, wa_ref, ba_ref, wx_ref, bx_ref, lam_ref,
                  o_ref, xe_ref, a_ref, u_ref, tail_ref, h_ref):
    ts, ch = xr_ref.shape
    pad = SUBLANES

    @pl.when(pl.program_id(1) == 0)
    def _():
        tail_ref[...] = jnp.zeros_like(tail_ref)
        h_ref[...] = jnp.zeros_like(h_ref)

    x = xr_ref[...]
    xe_ref[0:pad, :] = tail_ref[...]
    xe_ref[pad:pad + ts, :] = x
    tail_ref[...] = x[ts - pad:ts, :]
    xc = cb_ref[...] + jnp.zeros((ts, ch), F32)
    for j in range(CONV_WIDTH):
        off = pad - (CONV_WIDTH - 1) + j
        xc = xc + cw_ref[j:j + 1, :] * xe_ref[off:off + ts, :]

    xb = xc.astype(BF16)
    r = _sigmoid(jnp.dot(xb, wa_ref[...], preferred_element_type=F32) + ba_ref[...])
    i = _sigmoid(jnp.dot(xb, wx_ref[...], preferred_element_type=F32) + bx_ref[...])
    nl = -lam_ref[...]
    softplus = jnp.maximum(nl, 0.0) + jnp.log(1.0 + jnp.exp(-jnp.abs(nl)))
    log_a = (-RG_C) * r * softplus
    a = jnp.exp(log_a)
    mult = jnp.sqrt(1.0 - a * a)
    u = mult * (i * xc)

    ng = ts // SUBLANES
    a3 = a.reshape(ng, SUBLANES, ch)
    u3 = u.reshape(ng, SUBLANES, ch)
    rid = lax.broadcasted_iota(I32, (ng, SUBLANES, ch), 1)
    for sft in (1, 2, 4):
        a_s = pltpu.roll(a3, sft, 1)
        u_s = pltpu.roll(u3, sft, 1)
        keep = rid >= sft
        u3 = jnp.where(keep, a3 * u_s + u3, u3)
        a3 = jnp.where(keep, a3 * a_s, a3)
    a_ref[...] = a3.reshape(ts, ch)
    u_ref[...] = u3.reshape(ts, ch)

    inner = 8

    def body(gi, hprev):
        for jj in range(inner):
            base = pl.multiple_of((gi * inner + jj) * SUBLANES, SUBLANES)
            hcur = u_ref[pl.ds(base, SUBLANES), :] + a_ref[pl.ds(base, SUBLANES), :] * hprev
            u_ref[pl.ds(base, SUBLANES), :] = hcur
            hprev = hcur[SUBLANES - 1:SUBLANES, :]
        return hprev

    h_ref[...] = lax.fori_loop(0, ng // inner, body, h_ref[...])
    o_ref[...] = (u_ref[...] * _gelu_tanh(gr_ref[...])).astype(o_ref.dtype)


def _block_diag(w):
    nb, hin, hout = w.shape
    eye = jnp.eye(nb, dtype=w.dtype)
    return (eye[:, None, :, None] * w[:, :, None, :]).reshape(nb * hin, nb * hout)


def _rglru(xr, gr, conv_w, conv_b, w_a, b_a, w_x, b_x, lam):
    bsz, s, ch = xr.shape
    ts = RGLRU_ROWS
    blk = pl.BlockSpec((None, ts, ch), lambda b, t: (b, t, 0))
    full = lambda r, c: pl.BlockSpec((r, c), lambda b, t: (0, 0))
    return pl.pallas_call(
        _rglru_kernel,
        grid=(bsz, s // ts),
        in_specs=[blk, blk, full(CONV_WIDTH, ch), full(1, ch), full(ch, ch), full(1, ch),
                  full(ch, ch), full(1, ch), full(1, ch)],
        out_specs=blk,
        out_shape=jax.ShapeDtypeStruct((bsz, s, ch), BF16),
        scratch_shapes=[pltpu.VMEM((ts + SUBLANES, ch), F32),
                        pltpu.VMEM((ts, ch), F32),
                        pltpu.VMEM((ts, ch), F32),
                        pltpu.VMEM((SUBLANES, ch), F32),
                        pltpu.VMEM((1, ch), F32)],
        compiler_params=_params(("arbitrary", "arbitrary")),
        name="rglru",
    )(xr, gr, conv_w, conv_b.reshape(1, ch), _block_diag(w_a).astype(BF16), b_a.reshape(1, ch),
      _block_diag(w_x).astype(BF16), b_x.reshape(1, ch), lam.reshape(1, ch))


def _attn_kernel(q_ref, k_ref, v_ref, o_ref, f_ref, f4_ref, qs_ref, kp_ref, vp_ref, op_ref, lp_ref,
                 pc_ref, ml_ref):
    s = q_ref.shape[0]
    blk = ATTN_BLOCK
    n_blocks = s // blk
    unroll = ATTN_UNROLL

    lane = lax.broadcasted_iota(I32, (blk, LANES), 1)
    head0 = lane < HEAD_DIM
    head0_wide = lax.broadcasted_iota(I32, (blk, 2 * LANES), 1) % LANES < HEAD_DIM
    qi = lax.broadcasted_iota(I32, (2 * blk, blk), 0) % blk
    kj = lax.broadcasted_iota(I32, (2 * blk, blk), 1)
    prev_ok = kj >= qi
    cur_ok = kj <= qi
    dims = (((1,), (1,)), ((), ()))

    def block_rows(j, dil):
        per_res = (s // dil) // blk
        start = j // per_res + (j % per_res) * (dil * blk)
        if dil == 1:
            return pl.ds(pl.multiple_of(start, blk), blk)
        return pl.ds(start, blk, stride=dil)

    dils = tuple(d for _, d in DILATED_PATTERNS)
    assert dils == (1, 4, 16)

    def permute(src_ref, store):
        f_ref[...] = src_ref[...].astype(F32)

        def body0(j, c):
            store(0, j, f_ref[block_rows(j, 1), :])
            return c

        def body1(j, c):
            x = f_ref[block_rows(j, 4), :]
            f4_ref[pl.ds(pl.multiple_of(j * blk, blk), blk), :] = x
            store(1, j, x)
            return c

        def body2(j, c):
            per_res = (s // 16) // blk
            res, nblk = j // per_res, j % per_res
            start = (res % 4) * (s // 4) + res // 4 + nblk * (4 * blk)
            store(2, j, f4_ref[pl.ds(start, blk, stride=4), :])
            return c

        inner = 4
        for body in (body0, body1, body2):
            def several(g, c, body=body):
                for u in range(inner):
                    body(g * inner + u, c)
                return c
            lax.fori_loop(0, n_blocks // inner, several, 0)

    def store_q(p, j, x):
        x = x * (HEAD_DIM ** -0.5 * LOG2_E)
        base = pl.multiple_of(j * (2 * blk), 2 * blk)
        qs_ref[p, pl.ds(base, blk), :] = jnp.where(head0, x, 0.0).astype(BF16)
        qs_ref[p, pl.ds(base + blk, blk), :] = jnp.where(head0, 0.0, x).astype(BF16)

    def store_kv(dst_ref):
        def store(p, j, x):
            dst_ref[p, pl.ds(pl.multiple_of((j + 1) * blk, blk), blk), 0:LANES] = x.astype(BF16)
        return store

    for p in range(len(DILATED_PATTERNS)):
        kp_ref[p, 0:blk, :] = jnp.zeros((blk, LANES), BF16)
        vp_ref[p, 0:blk, 0:LANES] = jnp.zeros((blk, LANES), BF16)
        vp_ref[p, :, LANES:] = jnp.ones((s + blk, LANES), BF16)
    permute(q_ref, store_q)
    permute(k_ref, store_kv(kp_ref))
    permute(v_ref, store_kv(vp_ref))

    def kv_rows(j):
        return pl.ds(pl.multiple_of(j * blk, blk), 2 * blk)

    def scores(p, g):
        per_res = (s // dils[p]) // blk
        for u in range(unroll):
            j = g * unroll + u
            no_prev = jnp.where(j % per_res > 0, 0.0, NEG_INF)
            qs = qs_ref[p, pl.ds(pl.multiple_of(j * (2 * blk), 2 * blk), 2 * blk), :]
            sc = lax.dot_general(qs, kp_ref[p, kv_rows(j), :], dims, preferred_element_type=F32)
            s_prev = jnp.where(prev_ok, sc[:, :blk], NEG_INF) + no_prev
            s_cur = jnp.where(cur_ok, sc[:, blk:], NEG_INF)
            m = jnp.max(jnp.maximum(s_prev, s_cur), axis=-1, keepdims=True)
            pc_ref[g % 2, u] = jnp.concatenate(
                [jnp.exp2(s_prev - m), jnp.exp2(s_cur - m)], axis=-1).astype(BF16)
            ml_ref[g % 2, u] = jnp.where(head0, m[:blk], m[blk:])

    def outputs(p, g):
        for u in range(unroll):
            j = g * unroll + u
            rl = jnp.dot(pc_ref[g % 2, u], vp_ref[p, kv_rows(j), :], preferred_element_type=F32)
            rl = jnp.where(head0_wide, rl[:blk], rl[blk:])
            l = rl[:, LANES:]
            rows = block_rows(j, dils[p])
            op_ref[p, rows, :] = rl[:, :LANES] / l
            lp_ref[p, rows, :] = ml_ref[g % 2, u] + jnp.log2(l)

    n_groups = n_blocks // unroll
    assert n_groups % 2 == 0
    n_pat = len(DILATED_PATTERNS)
    scores(0, 0)
    for p, (window, dil) in enumerate(DILATED_PATTERNS):
        assert window // dil == blk

        def body(g, carry, p=p):
            outputs(p, g - 1)
            scores(p, g)
            return carry

        lax.fori_loop(1, n_groups, body, 0)
        outputs(p, n_groups - 1)
        if p + 1 < n_pat:
            scores(p + 1, 0)

    chunk = 512
    for c0 in range(0, s, chunk):
        sl = pl.ds(c0, chunk)
        l0, l1, l2 = lp_ref[0, sl, :], lp_ref[1, sl, :], lp_ref[2, sl, :]
        m = jnp.maximum(jnp.maximum(l0, l1), l2)
        w0, w1, w2 = jnp.exp2(l0 - m), jnp.exp2(l1 - m), jnp.exp2(l2 - m)
        num = w0 * op_ref[0, sl, :] + w1 * op_ref[1, sl, :] + w2 * op_ref[2, sl, :]
        o_ref[sl, :] = (num / (w0 + w1 + w2)).astype(o_ref.dtype)


def _attention(q, k, v):
    bsz, s, da = q.shape
    blk = pl.BlockSpec((None, s, LANES), lambda b, h: (b, 0, h))
    n_pat = len(DILATED_PATTERNS)
    return pl.pallas_call(
        _attn_kernel,
        grid=(bsz, da // LANES),
        in_specs=[blk, blk, blk],
        out_specs=blk,
        out_shape=jax.ShapeDtypeStruct((bsz, s, da), BF16),
        scratch_shapes=[pltpu.VMEM((s, LANES), F32),
                        pltpu.VMEM((s, LANES), F32),
                        pltpu.VMEM((n_pat, 2 * s, LANES), BF16),
                        pltpu.VMEM((n_pat, s + ATTN_BLOCK, LANES), BF16),
                        pltpu.VMEM((n_pat, s + ATTN_BLOCK, 2 * LANES), BF16),
                        pltpu.VMEM((n_pat, s, LANES), F32),
                        pltpu.VMEM((n_pat, s, LANES), F32),
                        pltpu.VMEM((2, ATTN_UNROLL, 2 * ATTN_BLOCK, 2 * ATTN_BLOCK), BF16),
                        pltpu.VMEM((2, ATTN_UNROLL, ATTN_BLOCK, LANES), F32)],
        compiler_params=_params(("arbitrary", "arbitrary"), 56 * 1024 * 1024),
        name="attn",
    )(q, k, v)


def _outproj_kernel(attn_ref, rec_ref, w_ref, x_ref, gm_ref, g_ref, sh_ref, sc_ref, wr_ref, br_ref,
                    x1_ref, h2_ref, idx_ref, gate_ref, rank_ref, cnt_ref, base_ref):
    tm = x_ref.shape[0]
    da = attn_ref.shape[1]

    @pl.when(jnp.logical_and(pl.program_id(0) == 0, pl.program_id(1) == 0))
    def _():
        base_ref[...] = jnp.zeros_like(base_ref)

    n_exp = wr_ref.shape[0]
    parts = 1
    tp = tm // parts
    eid = lax.broadcasted_iota(I32, (n_exp, tp), 0)
    tr = lax.broadcasted_iota(I32, (tp, tp), 0)
    tc = lax.broadcasted_iota(I32, (tp, tp), 1)
    before = jnp.where(tr < tc, 1.0, 0.0).astype(BF16)
    base = base_ref[:, 0:1]

    for part in range(parts):
        r0 = part * tp
        y = (jnp.dot(attn_ref[r0:r0 + tp, :], w_ref[0:da, :], preferred_element_type=F32)
             + jnp.dot(rec_ref[r0:r0 + tp, :], w_ref[da:, :], preferred_element_type=F32))
        x1 = x_ref[r0:r0 + tp, :] + gm_ref[...] * y
        x1_ref[r0:r0 + tp, :] = x1
        h2 = _rms_modulate(x1, g_ref[...], sh_ref[...], sc_ref[...])
        for c in range(ROW_CHUNKS):
            h2_ref[pl.ds(r0 * ROW_CHUNKS + c, tp, stride=ROW_CHUNKS), :] = h2[:, c * LANES:(c + 1) * LANES]

        work = lax.dot_general(wr_ref[...], h2, (((1,), (1,)), ((), ())), precision=lax.Precision.HIGHEST,
                               preferred_element_type=F32) + br_ref[...]
        vals, hots = [], []
        for kk in range(TOP_K):
            m = jnp.max(work, axis=0, keepdims=True)
            sel = jnp.min(jnp.where(work == m, eid, n_exp), axis=0, keepdims=True)
            hot = eid == sel
            idx_ref[kk:kk + 1, r0:r0 + tp] = sel
            vals.append(m)
            hots.append(hot)
            work = jnp.where(hot, -jnp.inf, work)
        exps = [jnp.exp(v - vals[0]) for v in vals]
        den = exps[0] + exps[1] + exps[2] + exps[3]
        for kk in range(TOP_K):
            gate_ref[kk:kk + 1, r0:r0 + tp] = exps[kk] / den

        cnt = jnp.zeros((n_exp, tp), F32)
        for hot in hots:
            cnt = cnt + hot.astype(F32)
        prior = jnp.dot(cnt.astype(BF16), before, preferred_element_type=F32) + base
        for kk in range(TOP_K):
            rank_ref[kk:kk + 1, r0:r0 + tp] = jnp.sum(jnp.where(hots[kk], prior, 0.0), axis=0,
                                                      keepdims=True).astype(I32)
        base = base + jnp.sum(cnt, axis=1, keepdims=True)

    base_ref[...] = jnp.broadcast_to(base, base_ref.shape)
    cnt_ref[...] = base_ref[...].astype(I32)


def _outproj(attn, rec, w_bf, x, gate_m, g_ffn, shift_f, scale_f, w_router, b_router):
    bsz, s, d = x.shape
    da = attn.shape[-1]
    n_exp = w_router.shape[1]
    tm = OUTPROJ_ROWS
    nt = s // tm
    t = bsz * s
    row = pl.BlockSpec((None, 1, d), lambda b, i: (b, 0, 0))
    small = pl.BlockSpec((TOP_K, tm), lambda b, i: (0, b * nt + i))
    return pl.pallas_call(
        _outproj_kernel,
        grid=(bsz, nt),
        in_specs=[pl.BlockSpec((None, tm, da), lambda b, i: (b, i, 0)),
                  pl.BlockSpec((None, tm, da), lambda b, i: (b, i, 0)),
                  pl.BlockSpec((d, d), lambda b, i: (0, 0)),
                  pl.BlockSpec((None, tm, d), lambda b, i: (b, i, 0)),
                  row,
                  pl.BlockSpec((1, d), lambda b, i: (0, 0)),
                  row, row,
                  pl.BlockSpec((n_exp, d), lambda b, i: (0, 0)),
                  pl.BlockSpec((n_exp, 1), lambda b, i: (0, 0))],
        out_specs=[pl.BlockSpec((None, tm, d), lambda b, i: (b, i, 0)),
                   pl.BlockSpec((tm * ROW_CHUNKS, LANES), lambda b, i: (b * nt + i, 0)),
                   small, small, small,
                   pl.BlockSpec((n_exp, LANES), lambda b, i: (0, 0))],
        out_shape=[jax.ShapeDtypeStruct((bsz, s, d), F32),
                   jax.ShapeDtypeStruct((t * ROW_CHUNKS, LANES), F32),
                   jax.ShapeDtypeStruct((TOP_K, t), I32),
                   jax.ShapeDtypeStruct((TOP_K, t), F32),
                   jax.ShapeDtypeStruct((TOP_K, t), I32),
                   jax.ShapeDtypeStruct((n_exp, LANES), I32)],
        scratch_shapes=[pltpu.VMEM((n_exp, LANES), F32)],
        compiler_params=_params(("arbitrary", "arbitrary")),
        name="outproj_router",
    )(attn, rec, w_bf, x, gate_m, g_ffn.reshape(1, d), shift_f, scale_f, w_router.T,
      b_router.reshape(n_exp, 1))


def _row_slice(ref, row):
    return ref.at[pl.ds(pl.multiple_of(row * ROW_CHUNKS, ROW_CHUNKS), ROW_CHUNKS), :]


def _slots_kernel(dest_ref, init_ref, slot_ref, sem):
    n_assign = dest_ref.shape[0]
    unroll = 32

    init = pltpu.make_async_copy(init_ref, slot_ref, sem)
    init.start()
    init.wait()

    def assign_body(i, c):
        for u in range(unroll):
            slot_ref[dest_ref[i * unroll + u]] = i * unroll + u
        return c

    lax.fori_loop(0, n_assign // unroll, assign_body, 0)


def _slots(dest_flat, n_slots):
    smem = pl.BlockSpec(memory_space=pltpu.SMEM)
    return pl.pallas_call(
        _slots_kernel,
        in_specs=[smem, pl.BlockSpec(memory_space=pltpu.VMEM)],
        out_specs=smem,
        out_shape=jax.ShapeDtypeStruct((n_slots,), I32),
        scratch_shapes=[pltpu.SemaphoreType.DMA(())],
        name="slots",
    )(dest_flat, jnp.arange(n_slots, dtype=I32))


GATHER_BUFS = 3


def _experts_kernel(bexp_ref, nused_ref, slot_ref, h2_ref, wgu_ref, bgu_ref, wd_ref, bd_ref, y_ref,
                    xg_ref, xs_ref, wgu_bf, wd_bf, gsem):
    b = pl.program_id(0)
    nb = pl.num_programs(0)
    rows = xs_ref.shape[0]
    f = wd_ref.shape[0]
    n_tok = h2_ref.shape[0] // ROW_CHUNKS
    assert n_tok & (n_tok - 1) == 0
    used = b < nused_ref[0]
    changed = jnp.logical_or(b == 0, bexp_ref[b] != bexp_ref[jnp.maximum(b - 1, 0)])

    def buf_rows(ref, buf, i):
        if isinstance(i, int):
            return ref.at[buf, pl.ds(i * ROW_CHUNKS, ROW_CHUNKS), :]
        return ref.at[buf, pl.ds(pl.multiple_of(i * ROW_CHUNKS, ROW_CHUNKS), ROW_CHUNKS), :]

    def fetch_row(base, buf, i):
        tok = slot_ref[base + i] & (n_tok - 1)
        return pltpu.make_async_copy(_row_slice(h2_ref, tok), buf_rows(xg_ref, buf, i), gsem.at[buf])

    def fetch_block(blk, buf, rolled=False):
        base = blk * rows
        if rolled:
            def body(i, c):
                fetch_row(base, buf, i).start()
                return c
            lax.fori_loop(0, rows, body, 0)
        else:
            for i in range(rows):
                fetch_row(base, buf, i).start(priority=i % 2)

    def wait_block(buf):
        pltpu.make_async_copy(xg_ref.at[buf], xg_ref.at[buf], gsem.at[buf]).wait()

    @pl.when(b == 0)
    def _():
        fetch_block(0, 0, rolled=True)
        fetch_block(1, 1, rolled=True)

    gbuf = b % GATHER_BUFS
    n_used = nused_ref[0]

    @pl.when(b < n_used + 2)
    def _():
        wait_block(gbuf)

    def fetch_ahead():
        fetch_block(jnp.minimum(b + 2, nb - 1), (b + 2) % GATHER_BUFS)

    @pl.when(jnp.logical_and(used, changed))
    def _():
        wgu_bf[...] = wgu_ref[...].astype(BF16)
        wd_bf[...] = wd_ref[...].astype(BF16)

    @pl.when(used)
    def _():
        for c in range(ROW_CHUNKS):
            xs_ref[:, c * LANES:(c + 1) * LANES] = (
                xg_ref[gbuf, pl.ds(c, rows, stride=ROW_CHUNKS), :].astype(BF16))
        fetch_ahead()
        gu = jnp.dot(xs_ref[...], wgu_bf[...], preferred_element_type=F32) + bgu_ref[...]
        gate = jnp.minimum(gu[:, :f], SWIGLU_LIMIT)
        up = jnp.clip(gu[:, f:], -SWIGLU_LIMIT, SWIGLU_LIMIT)
        glu = gate * jax.nn.sigmoid(gate * SWIGLU_ALPHA)
        act = ((up + 1.0) * glu).astype(BF16)
        y = jnp.dot(act, wd_bf[...], preferred_element_type=F32) + bd_ref[...]
        for c in range(ROW_CHUNKS):
            y_ref[pl.ds(c, rows, stride=ROW_CHUNKS), :] = y[:, c * LANES:(c + 1) * LANES]

    @pl.when(jnp.logical_not(used))
    def _():
        y_ref[...] = jnp.zeros_like(y_ref)

    for late in (1, 2):
        @pl.when(jnp.logical_and(b == nb - 1, n_used > nb - 3 + late))
        def _(late=late):
            wait_block((b + late) % GATHER_BUFS)


def _experts(h2rows, slots, blk_exp, n_used, w_gate_up, b_gate_up, w_down, b_down):
    n_exp, d, f2 = w_gate_up.shape
    f = w_down.shape[1]
    rows = EXPERT_ROWS
    n_blocks = slots.shape[0] // rows
    assert n_blocks >= GATHER_BUFS
    buf_shape = (rows * ROW_CHUNKS, LANES)
    return pl.pallas_call(
        _experts_kernel,
        grid_spec=pltpu.PrefetchScalarGridSpec(
            num_scalar_prefetch=3,
            grid=(n_blocks,),
            in_specs=[pl.BlockSpec(memory_space=pl.ANY),
                      pl.BlockSpec((None, d, f2), lambda b, e, n, s: (e[b], 0, 0)),
                      pl.BlockSpec((None, 1, f2), lambda b, e, n, s: (e[b], 0, 0)),
                      pl.BlockSpec((None, f, d), lambda b, e, n, s: (e[b], 0, 0)),
                      pl.BlockSpec((None, 1, d), lambda b, e, n, s: (e[b], 0, 0))],
            out_specs=pl.BlockSpec(buf_shape, lambda b, e, n, s: (b, 0)),
            scratch_shapes=[pltpu.VMEM((GATHER_BUFS,) + buf_shape, F32),
                            pltpu.VMEM((rows, d), BF16),
                            pltpu.VMEM((d, f2), BF16),
                            pltpu.VMEM((f, d), BF16),
                            pltpu.SemaphoreType.DMA((GATHER_BUFS,))]),
        out_shape=jax.ShapeDtypeStruct((slots.shape[0] * ROW_CHUNKS, LANES), F32),
        compiler_params=_params(("arbitrary",), 56 * 1024 * 1024),
        name="experts",
    )(blk_exp, n_used, slots, h2rows, w_gate_up, b_gate_up.reshape(n_exp, 1, f2), w_down,
      b_down.reshape(n_exp, 1, d))


def _combine_kernel(src_ref, cnt_ref, dst_ref, pos_ref, gate_ref, y_ref, x1_ref, gf_ref, g_ref, o_ref,
                    stage_ref, tok_ref, sem):
    i = pl.program_id(0)
    nt = pl.num_programs(0)
    tm = x1_ref.shape[0]
    n_exp = cnt_ref.shape[0] // nt

    def fetch(tile, buf):
        def per_expert(e, c):
            seg = tile * n_exp + e
            cnt, src, dst = cnt_ref[seg], src_ref[seg], dst_ref[seg]
            done = jnp.int32(0)
            size = tm
            while size >= 1:
                take = (cnt & size) != 0

                @pl.when(take)
                def _(size=size, done=done):
                    pltpu.make_async_copy(
                        y_ref.at[pl.ds(pl.multiple_of((src + done) * ROW_CHUNKS, ROW_CHUNKS),
                                       size * ROW_CHUNKS), :],
                        stage_ref.at[buf, pl.ds(pl.multiple_of((dst + done) * ROW_CHUNKS, ROW_CHUNKS),
                                                size * ROW_CHUNKS), :],
                        sem.at[buf]).start()

                done = done + (cnt & size)
                size //= 2
            return c
        lax.fori_loop(0, n_exp, per_expert, 0)

    @pl.when(i == 0)
    def _():
        fetch(0, 0)

    @pl.when(i + 1 < nt)
    def _():
        fetch(i + 1, (i + 1) % 2)

    buf = i % 2
    pltpu.make_async_copy(stage_ref.at[buf], stage_ref.at[buf], sem.at[buf]).wait()

    for t in range(tm):
        acc = None
        for kk in range(TOP_K):
            row = stage_ref[buf, pl.ds(pl.multiple_of(pos_ref[0, kk * tm + t], ROW_CHUNKS), ROW_CHUNKS), :]
            term = row * gate_ref[0, kk * tm + t]
            acc = term if acc is None else acc + term
        tok_ref[t * ROW_CHUNKS:(t + 1) * ROW_CHUNKS, :] = acc

    moe = jnp.concatenate([tok_ref[pl.ds(c, tm, stride=ROW_CHUNKS), :] for c in range(ROW_CHUNKS)],
                          axis=-1)
    x2 = x1_ref[...] + gf_ref[...] * moe
    o_ref[...] = (x2 * lax.rsqrt(jnp.mean(x2 * x2, axis=-1, keepdims=True) + NORM_EPS)) * g_ref[...]


def _combine(y, plan, x1, gate_f, g_final):
    bsz, s, d = x1.shape
    tm = COMBINE_ROWS
    nt = bsz * s // tm
    per_b = s // tm
    seg_src, seg_cnt, seg_dst, pos3, gates3 = plan
    per_tile = pl.BlockSpec((None, 1, TOP_K * tm), lambda i, *_: (i, 0, 0), memory_space=pltpu.SMEM)
    return pl.pallas_call(
        _combine_kernel,
        grid_spec=pltpu.PrefetchScalarGridSpec(
            num_scalar_prefetch=3,
            grid=(nt,),
            in_specs=[per_tile, per_tile,
                      pl.BlockSpec(memory_space=pl.ANY),
                      pl.BlockSpec((None, tm, d), lambda i, *_: (i // per_b, i % per_b, 0)),
                      pl.BlockSpec((None, 1, d), lambda i, *_: (i // per_b, 0, 0)),
                      pl.BlockSpec((1, d), lambda i, *_: (0, 0))],
            out_specs=pl.BlockSpec((None, tm, d), lambda i, *_: (i // per_b, i % per_b, 0)),
            scratch_shapes=[pltpu.VMEM((2, TOP_K * tm * ROW_CHUNKS, LANES), F32),
                            pltpu.VMEM((tm * ROW_CHUNKS, LANES), F32),
                            pltpu.SemaphoreType.DMA((2,))]),
        out_shape=jax.ShapeDtypeStruct((bsz, s, d), F32),
        compiler_params=_params(("arbitrary",)),
        name="combine",
    )(seg_src, seg_cnt, seg_dst, pos3, gates3, y, x1, gate_f, g_final.reshape(1, d))


def _combine_plan(idx, rank, gates, pstart):
    tm = COMBINE_ROWS
    t = idx.shape[1]
    nt = t // tm
    experts = jnp.arange(N_EXPERTS, dtype=I32)
    hot = idx.reshape(TOP_K, nt, tm)[..., None] == experts
    tile_cnt = jnp.sum(hot.astype(I32), axis=(0, 2))
    tile_base = jnp.cumsum(tile_cnt, axis=0) - tile_cnt
    stage_at = jnp.cumsum(tile_cnt, axis=1) - tile_cnt
    pos = rank.reshape(TOP_K, nt, tm) + jnp.sum(
        jnp.where(hot, (stage_at - tile_base)[None, :, None, :], 0), axis=-1)
    flat = lambda a: a.reshape(-1).astype(I32)
    return (flat(pstart[None, :] + tile_base), flat(tile_cnt), flat(stage_at),
            (pos * ROW_CHUNKS).transpose(1, 0, 2).reshape(nt, 1, TOP_K * tm).astype(I32),
            gates.reshape(TOP_K, nt, tm).transpose(1, 0, 2).reshape(nt, 1, TOP_K * tm))


def _routing_plan(idx, rank, counts, n_rows):
    rows = EXPERT_ROWS
    experts = jnp.arange(N_EXPERTS, dtype=I32)

    def lookup(table, e):
        return jnp.sum(jnp.where(e[..., None] == experts, table, 0), axis=-1)

    def segment_of(ends, pos):
        return jnp.minimum(jnp.sum((pos[..., None] >= ends).astype(I32), axis=-1), N_EXPERTS - 1)

    padded = ((counts + rows - 1) // rows) * rows
    pend = jnp.cumsum(padded)
    pstart = pend - padded
    dest = lookup(pstart, idx) + rank
    n_blocks = n_rows // rows
    blk_exp = segment_of(pend, jnp.arange(n_blocks, dtype=I32) * rows)
    n_used = (pend[-1] // rows).astype(I32).reshape(1)
    return dest.astype(I32), blk_exp.astype(I32), n_used, pstart


def kernel(x, c, w_ada, b_ada, g_mix, w_in, conv_w, conv_b, w_rg_a, b_rg_a, w_rg_x, b_rg_x, lam, w_out,
           g_ffn, w_router, b_router, w_gate_up, b_gate_up, w_down, b_down, g_final):
    bsz, s, d = x.shape
    t = bsz * s
    depth = w_ada.shape[0]
    assert depth == 1, "the combine kernel applies the final norm; one layer only"
    for l in range(depth):
        mod = _ada(c, w_ada[l], b_ada[l]).reshape(bsz, 6, 1, d)
        shift_m, scale_m, gate_m, shift_f, scale_f, gate_f = (mod[:, j] for j in range(6))
        q, k, v, xr, gr = _inproj(x, g_mix[l], shift_m, scale_m, w_in[l].astype(BF16))
        attn = _attention(q, k, v)
        rec = _rglru(xr, gr, conv_w[l], conv_b[l], w_rg_a[l], b_rg_a[l], w_rg_x[l], b_rg_x[l], lam[l])
        x1, h2rows, idx, gates, rank, cnt = _outproj(
            attn, rec, w_out[l].astype(BF16), x, gate_m, g_ffn[l], shift_f, scale_f, w_router[l], b_router[l])
        n_rows = t * TOP_K + N_EXPERTS * EXPERT_ROWS
        dest, blk_exp, n_used, pstart = _routing_plan(idx, rank, cnt[:, 0], n_rows)
        slots = _slots(dest.reshape(-1), n_rows)
        y = _experts(h2rows, slots, blk_exp, n_used, w_gate_up[l], b_gate_up[l], w_down[l], b_down[l])
        x = _combine(y, _combine_plan(idx, rank, gates, pstart), x1, gate_f, g_final)
    return x
```

```python
import jax
import jax.numpy as jnp
from jax import lax
from jax.experimental import pallas as pl
from jax.experimental.pallas import tpu as pltpu

F32 = jnp.float32
BF16 = jnp.bfloat16
I32 = jnp.int32

HEAD_DIM = 64
DILATED_PATTERNS = ((128, 1), (512, 4), (2048, 16))
REC_BLOCKS = 8
CONV_WIDTH = 4
RG_C = 8.0
N_EXPERTS = 32
TOP_K = 4
SWIGLU_LIMIT = 7.0
SWIGLU_ALPHA = 1.702
NORM_EPS = 1e-6
NEG_INF = -1e30
LOG2_E = 1.4426950408889634

LANES = 128
SUBLANES = 8
ROW_CHUNKS = 8
VMEM_LIMIT = 48 * 1024 * 1024

INPROJ_ROWS = 512
RGLRU_ROWS = 512
OUTPROJ_ROWS = 512
ATTN_BLOCK = 128
ATTN_UNROLL = 8
EXPERT_ROWS = 512
COMBINE_ROWS = 512
COMBINE_PIECE = 64


def _params(sem, vmem=VMEM_LIMIT):
    return pltpu.CompilerParams(dimension_semantics=sem, vmem_limit_bytes=vmem)


def _ada_kernel(c_ref, w_ref, b_ref, o_ref):
    o_ref[...] = jnp.dot(c_ref[...], w_ref[...], precision=lax.Precision.HIGHEST,
                         preferred_element_type=F32) + b_ref[...]


def _ada(c, w, b):
    bsz, d = c.shape
    n = w.shape[1]
    return pl.pallas_call(
        _ada_kernel,
        grid=(n // d,),
        in_specs=[pl.BlockSpec((bsz, d), lambda j: (0, 0)),
                  pl.BlockSpec((d, d), lambda j: (0, j)),
                  pl.BlockSpec((1, d), lambda j: (0, j))],
        out_specs=pl.BlockSpec((bsz, d), lambda j: (0, j)),
        out_shape=jax.ShapeDtypeStruct((bsz, n), F32),
        compiler_params=_params(("arbitrary",)),
        name="ada",
    )(c, w, b.reshape(1, n))


def _rms_modulate(x, g, shift, scale):
    y = x * lax.rsqrt(jnp.mean(x * x, axis=-1, keepdims=True) + NORM_EPS)
    return (y * g) * (1.0 + scale) + shift


def _inproj_kernel(x_ref, g_ref, sh_ref, sc_ref, w_ref, q_ref, k_ref, v_ref, xr_ref, gr_ref):
    h = _rms_modulate(x_ref[...], g_ref[...], sh_ref[...], sc_ref[...]).astype(BF16)
    outs = (q_ref, k_ref, v_ref, xr_ref, gr_ref)
    width = q_ref.shape[-1]
    for j, o_ref in enumerate(outs):
        z = jnp.dot(h, w_ref[:, j * width:(j + 1) * width], preferred_element_type=F32)
        o_ref[...] = z.astype(o_ref.dtype)


def _inproj(x, g, shift, scale, w_bf):
    bsz, s, d = x.shape
    n = w_bf.shape[1]
    width = n // 5
    tm = INPROJ_ROWS
    row = pl.BlockSpec((None, 1, d), lambda b, i: (b, 0, 0))
    out_blk = pl.BlockSpec((None, tm, width), lambda b, i: (b, i, 0))
    shp = lambda dt: jax.ShapeDtypeStruct((bsz, s, width), dt)
    return pl.pallas_call(
        _inproj_kernel,
        grid=(bsz, s // tm),
        in_specs=[pl.BlockSpec((None, tm, d), lambda b, i: (b, i, 0)),
                  pl.BlockSpec((1, d), lambda b, i: (0, 0)),
                  row, row,
                  pl.BlockSpec((d, n), lambda b, i: (0, 0))],
        out_specs=[out_blk] * 5,
        out_shape=[shp(BF16), shp(BF16), shp(BF16), shp(F32), shp(F32)],
        compiler_params=_params(("arbitrary", "arbitrary")),
        name="inproj",
    )(x, g.reshape(1, d), shift, scale, w_bf)


def _sigmoid(x):
    return 0.5 * jnp.tanh(0.5 * x) + 0.5


def _gelu_tanh(x):
    return 0.5 * x * (1.0 + jnp.tanh(0.7978845608028654 * (x + 0.044715 * (x * x * x))))


def _rglru_kernel(xr_ref, gr_ref, cw_ref, cb_ref, wa_ref, ba_ref, wx_ref, bx_ref, lam_ref,
                  o_ref, xe_ref, a_ref, u_ref, tail_ref, h_ref):
    ts, ch = xr_ref.shape
    pad = SUBLANES

    @pl.when(pl.program_id(1) == 0)
    def _():
        tail_ref[...] = jnp.zeros_like(tail_ref)
        h_ref[...] = jnp.zeros_like(h_ref)

    x = xr_ref[...]
    xe_ref[0:pad, :] = tail_ref[...]
    xe_ref[pad:pad + ts, :] = x
    tail_ref[...] = x[ts - pad:ts, :]
    xc = cb_ref[...] + jnp.zeros((ts, ch), F32)
    for j in range(CONV_WIDTH):
        off = pad - (CONV_WIDTH - 1) + j
        xc = xc + cw_ref[j:j + 1, :] * xe_ref[off:off + ts, :]

    xb = xc.astype(BF16)
    r = _sigmoid(jnp.dot(xb, wa_ref[...], preferred_element_type=F32) + ba_ref[...])
    i = _sigmoid(jnp.dot(xb, wx_ref[...], preferred_element_type=F32) + bx_ref[...])
    nl = -lam_ref[...]
    softplus = jnp.maximum(nl, 0.0) + jnp.log(1.0 + jnp.exp(-jnp.abs(nl)))
    log_a = (-RG_C) * r * softplus
    a = jnp.exp(log_a)
    mult = jnp.sqrt(1.0 - a * a)
    u = mult * (i * xc)

    ng = ts // SUBLANES
    a3 = a.reshape(ng, SUBLANES, ch)
    u3 = u.reshape(ng, SUBLANES, ch)
    rid = lax.broadcasted_iota(I32, (ng, SUBLANES, ch), 1)
    for sft in (1, 2, 4):
        a_s = pltpu.roll(a3, sft, 1)
        u_s = pltpu.roll(u3, sft, 1)
        keep = rid >= sft
        u3 = jnp.where(keep, a3 * u_s + u3, u3)
        a3 = jnp.where(keep, a3 * a_s, a3)
    a_ref[...] = a3.reshape(ts, ch)
    u_ref[...] = u3.reshape(ts, ch)

    inner = 8

    def body(gi, hprev):
        for jj in range(inner):
            base = pl.multiple_of((gi * inner + jj) * SUBLANES, SUBLANES)
            hcur = u_ref[pl.ds(base, SUBLANES), :] + a_ref[pl.ds(base, SUBLANES), :] * hprev
            u_ref[pl.ds(base, SUBLANES), :] = hcur
            hprev = hcur[SUBLANES - 1:SUBLANES, :]
        return hprev

    h_ref[...] = lax.fori_loop(0, ng // inner, body, h_ref[...])
    o_ref[...] = (u_ref[...] * _gelu_tanh(gr_ref[...])).astype(o_ref.dtype)


def _block_diag(w):
    nb, hin, hout = w.shape
    eye = jnp.eye(nb, dtype=w.dtype)
    return (eye[:, None, :, None] * w[:, :, None, :]).reshape(nb * hin, nb * hout)


def _rglru(xr, gr, conv_w, conv_b, w_a, b_a, w_x, b_x, lam):
    bsz, s, ch = xr.shape
    ts = RGLRU_ROWS
    blk = pl.BlockSpec((None, ts, ch), lambda b, t: (b, t, 0))
    full = lambda r, c: pl.BlockSpec((r, c), lambda b, t: (0, 0))
    return pl.pallas_call(
        _rglru_kernel,
        grid=(bsz, s // ts),
        in_specs=[blk, blk, full(CONV_WIDTH, ch), full(1, ch), full(ch, ch), full(1, ch),
                  full(ch, ch), full(1, ch), full(1, ch)],
        out_specs=blk,
        out_shape=jax.ShapeDtypeStruct((bsz, s, ch), BF16),
        scratch_shapes=[pltpu.VMEM((ts + SUBLANES, ch), F32),
                        pltpu.VMEM((ts, ch), F32),
                        pltpu.VMEM((ts, ch), F32),
                        pltpu.VMEM((SUBLANES, ch), F32),
                        pltpu.VMEM((1, ch), F32)],
        compiler_params=_params(("arbitrary", "arbitrary")),
        name="rglru",
    )(xr, gr, conv_w, conv_b.reshape(1, ch), _block_diag(w_a).astype(BF16), b_a.reshape(1, ch),
      _block_diag(w_x).astype(BF16), b_x.reshape(1, ch), lam.reshape(1, ch))


def _attn_kernel(q_ref, k_ref, v_ref, o_ref, f_ref, f4_ref, qs_ref, kp_ref, vp_ref, op_ref, lp_ref,
                 pc_ref, ml_ref):
    s = q_ref.shape[0]
    blk = ATTN_BLOCK
    n_blocks = s // blk
    unroll = ATTN_UNROLL

    lane = lax.broadcasted_iota(I32, (blk, LANES), 1)
    head0 = lane < HEAD_DIM
    head0_wide = lax.broadcasted_iota(I32, (blk, 2 * LANES), 1) % LANES < HEAD_DIM
    qi = lax.broadcasted_iota(I32, (2 * blk, blk), 0) % blk
    kj = lax.broadcasted_iota(I32, (2 * blk, blk), 1)
    prev_ok = kj >= qi
    cur_ok = kj <= qi
    dims = (((1,), (1,)), ((), ()))

    def block_rows(j, dil):
        per_res = (s // dil) // blk
        start = j // per_res + (j % per_res) * (dil * blk)
        if dil == 1:
            return pl.ds(pl.multiple_of(start, blk), blk)
        return pl.ds(start, blk, stride=dil)

    dils = tuple(d for _, d in DILATED_PATTERNS)
    assert dils == (1, 4, 16)

    def permute(src_ref, store):
        f_ref[...] = src_ref[...].astype(F32)

        def body0(j, c):
            store(0, j, f_ref[block_rows(j, 1), :])
            return c

        def body1(j, c):
            x = f_ref[block_rows(j, 4), :]
            f4_ref[pl.ds(pl.multiple_of(j * blk, blk), blk), :] = x
            store(1, j, x)
            return c

        def body2(j, c):
            per_res = (s // 16) // blk
            res, nblk = j // per_res, j % per_res
            start = (res % 4) * (s // 4) + res // 4 + nblk * (4 * blk)
            store(2, j, f4_ref[pl.ds(start, blk, stride=4), :])
            return c

        inner = 4
        for body in (body0, body1, body2):
            def several(g, c, body=body):
                for u in range(inner):
                    body(g * inner + u, c)
                return c
            lax.fori_loop(0, n_blocks // inner, several, 0)

    def store_q(p, j, x):
        x = x * (HEAD_DIM ** -0.5 * LOG2_E)
        base = pl.multiple_of(j * (2 * blk), 2 * blk)
        qs_ref[p, pl.ds(base, blk), :] = jnp.where(head0, x, 0.0).astype(BF16)
        qs_ref[p, pl.ds(base + blk, blk), :] = jnp.where(head0, 0.0, x).astype(BF16)

    def store_kv(dst_ref):
        def store(p, j, x):
            dst_ref[p, pl.ds(pl.multiple_of((j + 1) * blk, blk), blk), 0:LANES] = x.astype(BF16)
        return store

    for p in range(len(DILATED_PATTERNS)):
        kp_ref[p, 0:blk, :] = jnp.zeros((blk, LANES), BF16)
        vp_ref[p, 0:blk, 0:LANES] = jnp.zeros((blk, LANES), BF16)
        vp_ref[p, :, LANES:] = jnp.ones((s + blk, LANES), BF16)
    permute(q_ref, store_q)
    permute(k_ref, store_kv(kp_ref))
    permute(v_ref, store_kv(vp_ref))

    def kv_rows(j):
        return pl.ds(pl.multiple_of(j * blk, blk), 2 * blk)

    def scores(p, g):
        per_res = (s // dils[p]) // blk
        for u in range(unroll):
            j = g * unroll + u
            no_prev = jnp.where(j % per_res > 0, 0.0, NEG_INF)
            qs = qs_ref[p, pl.ds(pl.multiple_of(j * (2 * blk), 2 * blk), 2 * blk), :]
            sc = lax.dot_general(qs, kp_ref[p, kv_rows(j), :], dims, preferred_element_type=F32)
            s_prev = jnp.where(prev_ok, sc[:, :blk], NEG_INF) + no_prev
            s_cur = jnp.where(cur_ok, sc[:, blk:], NEG_INF)
            m = jnp.max(jnp.maximum(s_prev, s_cur), axis=-1, keepdims=True)
            pc_ref[g % 2, u] = jnp.concatenate(
                [jnp.exp2(s_prev - m), jnp.exp2(s_cur - m)], axis=-1).astype(BF16)
            ml_ref[g % 2, u] = jnp.where(head0, m[:blk], m[blk:])

    def outputs(p, g):
        for u in range(unroll):
            j = g * unroll + u
            rl = jnp.dot(pc_ref[g % 2, u], vp_ref[p, kv_rows(j), :], preferred_element_type=F32)
            rl = jnp.where(head0_wide, rl[:blk], rl[blk:])
            l = rl[:, LANES:]
            rows = block_rows(j, dils[p])
            op_ref[p, rows, :] = rl[:, :LANES] / l
            lp_ref[p, rows, :] = ml_ref[g % 2, u] + jnp.log2(l)

    n_groups = n_blocks // unroll
    assert n_groups % 2 == 0
    n_pat = len(DILATED_PATTERNS)
    scores(0, 0)
    for p, (window, dil) in enumerate(DILATED_PATTERNS):
        assert window // dil == blk

        def body(g, carry, p=p):
            outputs(p, g - 1)
            scores(p, g)
            return carry

        lax.fori_loop(1, n_groups, body, 0)
        outputs(p, n_groups - 1)
        if p + 1 < n_pat:
            scores(p + 1, 0)

    chunk = 512
    for c0 in range(0, s, chunk):
        sl = pl.ds(c0, chunk)
        l0, l1, l2 = lp_ref[0, sl, :], lp_ref[1, sl, :], lp_ref[2, sl, :]
        m = jnp.maximum(jnp.maximum(l0, l1), l2)
        w0, w1, w2 = jnp.exp2(l0 - m), jnp.exp2(l1 - m), jnp.exp2(l2 - m)
        num = w0 * op_ref[0, sl, :] + w1 * op_ref[1, sl, :] + w2 * op_ref[2, sl, :]
        o_ref[sl, :] = (num / (w0 + w1 + w2)).astype(o_ref.dtype)


def _attention(q, k, v):
    bsz, s, da = q.shape
    blk = pl.BlockSpec((None, s, LANES), lambda b, h: (b, 0, h))
    n_pat = len(DILATED_PATTERNS)
    return pl.pallas_call(
        _attn_kernel,
        grid=(bsz, da // LANES),
        in_specs=[blk, blk, blk],
        out_specs=blk,
        out_shape=jax.ShapeDtypeStruct((bsz, s, da), BF16),
        scratch_shapes=[pltpu.VMEM((s, LANES), F32),
                        pltpu.VMEM((s, LANES), F32),
                        pltpu.VMEM((n_pat, 2 * s, LANES), BF16),
                        pltpu.VMEM((n_pat, s + ATTN_BLOCK, LANES), BF16),
                        pltpu.VMEM((n_pat, s + ATTN_BLOCK, 2 * LANES), BF16),
                        pltpu.VMEM((n_pat, s, LANES), F32),
                        pltpu.VMEM((n_pat, s, LANES), F32),
                        pltpu.VMEM((2, ATTN_UNROLL, 2 * ATTN_BLOCK, 2 * ATTN_BLOCK), BF16),
                        pltpu.VMEM((2, ATTN_UNROLL, ATTN_BLOCK, LANES), F32)],
        compiler_params=_params(("arbitrary", "arbitrary"), 56 * 1024 * 1024),
        name="attn",
    )(q, k, v)


def _outproj_kernel(attn_ref, rec_ref, w_ref, x_ref, gm_ref, g_ref, sh_ref, sc_ref, wr_ref, br_ref,
                    x1_ref, h2_ref, idx_ref, gate_ref, rank_ref, cnt_ref, base_ref):
    tm = x_ref.shape[0]
    da = attn_ref.shape[1]

    @pl.when(jnp.logical_and(pl.program_id(0) == 0, pl.program_id(1) == 0))
    def _():
        base_ref[...] = jnp.zeros_like(base_ref)

    y = (jnp.dot(attn_ref[...], w_ref[0:da, :], preferred_element_type=F32)
         + jnp.dot(rec_ref[...], w_ref[da:, :], preferred_element_type=F32))
    x1 = x_ref[...] + gm_ref[...] * y
    x1_ref[...] = x1
    h2 = _rms_modulate(x1, g_ref[...], sh_ref[...], sc_ref[...])
    for c in range(ROW_CHUNKS):
        h2_ref[pl.ds(c, tm, stride=ROW_CHUNKS), :] = h2[:, c * LANES:(c + 1) * LANES]

    work = lax.dot_general(wr_ref[...], h2, (((1,), (1,)), ((), ())), precision=lax.Precision.HIGHEST,
                           preferred_element_type=F32) + br_ref[...]
    n_exp = work.shape[0]
    eid = lax.broadcasted_iota(I32, (n_exp, tm), 0)
    vals, hots = [], []
    for kk in range(TOP_K):
        m = jnp.max(work, axis=0, keepdims=True)
        sel = jnp.min(jnp.where(work == m, eid, n_exp), axis=0, keepdims=True)
        hot = eid == sel
        idx_ref[kk:kk + 1, :] = sel
        vals.append(m)
        hots.append(hot)
        work = jnp.where(hot, -jnp.inf, work)
    exps = [jnp.exp(v - vals[0]) for v in vals]
    den = exps[0] + exps[1] + exps[2] + exps[3]
    for kk in range(TOP_K):
        gate_ref[kk:kk + 1, :] = exps[kk] / den

    cnt = jnp.zeros((n_exp, tm), F32)
    for hot in hots:
        cnt = cnt + hot.astype(F32)
    tr = lax.broadcasted_iota(I32, (tm, tm), 0)
    tc = lax.broadcasted_iota(I32, (tm, tm), 1)
    before = jnp.where(tr < tc, 1.0, 0.0).astype(BF16)
    prior = jnp.dot(cnt.astype(BF16), before, preferred_element_type=F32) + base_ref[:, 0:1]
    for kk in range(TOP_K):
        rank_ref[kk:kk + 1, :] = jnp.sum(jnp.where(hots[kk], prior, 0.0), axis=0,
                                         keepdims=True).astype(I32)
    base_ref[...] = base_ref[...] + jnp.sum(cnt, axis=1, keepdims=True)
    cnt_ref[...] = base_ref[...].astype(I32)


def _outproj(attn, rec, w_bf, x, gate_m, g_ffn, shift_f, scale_f, w_router, b_router):
    bsz, s, d = x.shape
    da = attn.shape[-1]
    n_exp = w_router.shape[1]
    tm = OUTPROJ_ROWS
    nt = s // tm
    t = bsz * s
    row = pl.BlockSpec((None, 1, d), lambda b, i: (b, 0, 0))
    small = pl.BlockSpec((TOP_K, tm), lambda b, i: (0, b * nt + i))
    return pl.pallas_call(
        _outproj_kernel,
        grid=(bsz, nt),
        in_specs=[pl.BlockSpec((None, tm, da), lambda b, i: (b, i, 0)),
                  pl.BlockSpec((None, tm, da), lambda b, i: (b, i, 0)),
                  pl.BlockSpec((d, d), lambda b, i: (0, 0)),
                  pl.BlockSpec((None, tm, d), lambda b, i: (b, i, 0)),
                  row,
                  pl.BlockSpec((1, d), lambda b, i: (0, 0)),
                  row, row,
                  pl.BlockSpec((n_exp, d), lambda b, i: (0, 0)),
                  pl.BlockSpec((n_exp, 1), lambda b, i: (0, 0))],
        out_specs=[pl.BlockSpec((None, tm, d), lambda b, i: (b, i, 0)),
                   pl.BlockSpec((tm * ROW_CHUNKS, LANES), lambda b, i: (b * nt + i, 0)),
                   small, small, small,
                   pl.BlockSpec((n_exp, LANES), lambda b, i: (0, 0))],
        out_shape=[jax.ShapeDtypeStruct((bsz, s, d), F32),
                   jax.ShapeDtypeStruct((t * ROW_CHUNKS, LANES), F32),
                   jax.ShapeDtypeStruct((TOP_K, t), I32),
                   jax.ShapeDtypeStruct((TOP_K, t), F32),
                   jax.ShapeDtypeStruct((TOP_K, t), I32),
                   jax.ShapeDtypeStruct((n_exp, LANES), I32)],
        scratch_shapes=[pltpu.VMEM((n_exp, LANES), F32)],
        compiler_params=_params(("arbitrary", "arbitrary")),
        name="outproj_router",
    )(attn, rec, w_bf, x, gate_m, g_ffn.reshape(1, d), shift_f, scale_f, w_router.T,
      b_router.reshape(n_exp, 1))


def _row_slice(ref, row):
    return ref.at[pl.ds(pl.multiple_of(row * ROW_CHUNKS, ROW_CHUNKS), ROW_CHUNKS), :]


def _slots_kernel(dest_ref, init_ref, slot_ref, sem):
    n_assign = dest_ref.shape[0]
    unroll = 32

    init = pltpu.make_async_copy(init_ref, slot_ref, sem)
    init.start()
    init.wait()

    def assign_body(i, c):
        for u in range(unroll):
            slot_ref[dest_ref[i * unroll + u]] = i * unroll + u
        return c

    lax.fori_loop(0, n_assign // unroll, assign_body, 0)


def _slots(dest_flat, n_slots):
    smem = pl.BlockSpec(memory_space=pltpu.SMEM)
    return pl.pallas_call(
        _slots_kernel,
        in_specs=[smem, pl.BlockSpec(memory_space=pltpu.VMEM)],
        out_specs=smem,
        out_shape=jax.ShapeDtypeStruct((n_slots,), I32),
        scratch_shapes=[pltpu.SemaphoreType.DMA(())],
        name="slots",
    )(dest_flat, jnp.arange(n_slots, dtype=I32))


GATHER_BUFS = 3


def _experts_kernel(bexp_ref, nused_ref, slot_ref, h2_ref, wgu_ref, bgu_ref, wd_ref, bd_ref, y_ref,
                    xg_ref, xs_ref, wgu_bf, wd_bf, gsem):
    b = pl.program_id(0)
    nb = pl.num_programs(0)
    rows = xs_ref.shape[0]
    f = wd_ref.shape[0]
    n_tok = h2_ref.shape[0] // ROW_CHUNKS
    assert n_tok & (n_tok - 1) == 0
    used = b < nused_ref[0]
    changed = jnp.logical_or(b == 0, bexp_ref[b] != bexp_ref[jnp.maximum(b - 1, 0)])

    def buf_rows(ref, buf, i):
        if isinstance(i, int):
            return ref.at[buf, pl.ds(i * ROW_CHUNKS, ROW_CHUNKS), :]
        return ref.at[buf, pl.ds(pl.multiple_of(i * ROW_CHUNKS, ROW_CHUNKS), ROW_CHUNKS), :]

    def fetch_row(base, buf, i):
        tok = slot_ref[base + i] & (n_tok - 1)
        return pltpu.make_async_copy(_row_slice(h2_ref, tok), buf_rows(xg_ref, buf, i), gsem.at[buf])

    def fetch_block(blk, buf, rolled=False):
        base = blk * rows
        if rolled:
            def body(i, c):
                fetch_row(base, buf, i).start()
                return c
            lax.fori_loop(0, rows, body, 0)
        else:
            for i in range(rows):
                fetch_row(base, buf, i).start(priority=i % 2)

    def wait_block(buf):
        pltpu.make_async_copy(xg_ref.at[buf], xg_ref.at[buf], gsem.at[buf]).wait()

    @pl.when(b == 0)
    def _():
        fetch_block(0, 0, rolled=True)
        fetch_block(1, 1, rolled=True)

    gbuf = b % GATHER_BUFS
    n_used = nused_ref[0]

    @pl.when(b < n_used + 2)
    def _():
        wait_block(gbuf)

    def fetch_ahead():
        fetch_block(jnp.minimum(b + 2, nb - 1), (b + 2) % GATHER_BUFS)

    @pl.when(jnp.logical_and(used, changed))
    def _():
        wgu_bf[...] = wgu_ref[...].astype(BF16)
        wd_bf[...] = wd_ref[...].astype(BF16)

    @pl.when(used)
    def _():
        for c in range(ROW_CHUNKS):
            xs_ref[:, c * LANES:(c + 1) * LANES] = (
                xg_ref[gbuf, pl.ds(c, rows, stride=ROW_CHUNKS), :].astype(BF16))
        fetch_ahead()
        gu = jnp.dot(xs_ref[...], wgu_bf[...], preferred_element_type=F32) + bgu_ref[...]
        gate = jnp.minimum(gu[:, :f], SWIGLU_LIMIT)
        up = jnp.clip(gu[:, f:], -SWIGLU_LIMIT, SWIGLU_LIMIT)
        glu = gate * jax.nn.sigmoid(gate * SWIGLU_ALPHA)
        act = ((up + 1.0) * glu).astype(BF16)
        y = jnp.dot(act, wd_bf[...], preferred_element_type=F32) + bd_ref[...]
        for c in range(ROW_CHUNKS):
            y_ref[pl.ds(c, rows, stride=ROW_CHUNKS), :] = y[:, c * LANES:(c + 1) * LANES]

    @pl.when(jnp.logical_not(used))
    def _():
        y_ref[...] = jnp.zeros_like(y_ref)

    for late in (1, 2):
        @pl.when(jnp.logical_and(b == nb - 1, n_used > nb - 3 + late))
        def _(late=late):
            wait_block((b + late) % GATHER_BUFS)


def _experts(h2rows, slots, blk_exp, n_used, w_gate_up, b_gate_up, w_down, b_down):
    n_exp, d, f2 = w_gate_up.shape
    f = w_down.shape[1]
    rows = EXPERT_ROWS
    n_blocks = slots.shape[0] // rows
    assert n_blocks >= GATHER_BUFS
    buf_shape = (rows * ROW_CHUNKS, LANES)
    return pl.pallas_call(
        _experts_kernel,
        grid_spec=pltpu.PrefetchScalarGridSpec(
            num_scalar_prefetch=3,
            grid=(n_blocks,),
            in_specs=[pl.BlockSpec(memory_space=pl.ANY),
                      pl.BlockSpec((None, d, f2), lambda b, e, n, s: (e[b], 0, 0)),
                      pl.BlockSpec((None, 1, f2), lambda b, e, n, s: (e[b], 0, 0)),
                      pl.BlockSpec((None, f, d), lambda b, e, n, s: (e[b], 0, 0)),
                      pl.BlockSpec((None, 1, d), lambda b, e, n, s: (e[b], 0, 0))],
            out_specs=pl.BlockSpec(buf_shape, lambda b, e, n, s: (b, 0)),
            scratch_shapes=[pltpu.VMEM((GATHER_BUFS,) + buf_shape, F32),
                            pltpu.VMEM((rows, d), BF16),
                            pltpu.VMEM((d, f2), BF16),
                            pltpu.VMEM((f, d), BF16),
                            pltpu.SemaphoreType.DMA((GATHER_BUFS,))]),
        out_shape=jax.ShapeDtypeStruct((slots.shape[0] * ROW_CHUNKS, LANES), F32),
        compiler_params=_params(("arbitrary",), 56 * 1024 * 1024),
        name="experts",
    )(blk_exp, n_used, slots, h2rows, w_gate_up, b_gate_up.reshape(n_exp, 1, f2), w_down,
      b_down.reshape(n_exp, 1, d))


def _combine_kernel(src_ref, cnt_ref, dst_ref, pos_ref, gate_ref, y_ref, x1_ref, gf_ref, g_ref, o_ref,
                    stage_ref, tok_ref, sem):
    i = pl.program_id(0)
    nt = pl.num_programs(0)
    tm = x1_ref.shape[0]
    n_exp = cnt_ref.shape[0] // nt

    def fetch(tile, buf):
        def per_expert(e, c):
            seg = tile * n_exp + e
            cnt, src, dst = cnt_ref[seg], src_ref[seg], dst_ref[seg]

            def piece(done, size):
                return pltpu.make_async_copy(
                    y_ref.at[pl.ds(pl.multiple_of((src + done) * ROW_CHUNKS, ROW_CHUNKS),
                                   size * ROW_CHUNKS), :],
                    stage_ref.at[buf, pl.ds(pl.multiple_of((dst + done) * ROW_CHUNKS, ROW_CHUNKS),
                                            size * ROW_CHUNKS), :],
                    sem.at[buf])

            def whole(j, c2):
                piece(j * COMBINE_PIECE, COMBINE_PIECE).start()
                return c2

            n_whole = cnt // COMBINE_PIECE
            lax.fori_loop(0, n_whole, whole, 0)
            done = n_whole * COMBINE_PIECE
            size = COMBINE_PIECE // 2
            while size >= 1:
                @pl.when((cnt & size) != 0)
                def _(size=size, done=done):
                    piece(done, size).start()

                done = done + (cnt & size)
                size //= 2
            return c
        lax.fori_loop(0, n_exp, per_expert, 0)

    @pl.when(i == 0)
    def _():
        fetch(0, 0)

    @pl.when(i + 1 < nt)
    def _():
        fetch(i + 1, (i + 1) % 2)

    buf = i % 2
    pltpu.make_async_copy(stage_ref.at[buf], stage_ref.at[buf], sem.at[buf]).wait()

    for t in range(tm):
        acc = None
        for kk in range(TOP_K):
            row = stage_ref[buf, pl.ds(pl.multiple_of(pos_ref[0, kk * tm + t], ROW_CHUNKS), ROW_CHUNKS), :]
            term = row * gate_ref[0, kk * tm + t]
            acc = term if acc is None else acc + term
        tok_ref[t * ROW_CHUNKS:(t + 1) * ROW_CHUNKS, :] = acc

    moe = jnp.concatenate([tok_ref[pl.ds(c, tm, stride=ROW_CHUNKS), :] for c in range(ROW_CHUNKS)],
                          axis=-1)
    x2 = x1_ref[...] + gf_ref[...] * moe
    o_ref[...] = (x2 * lax.rsqrt(jnp.mean(x2 * x2, axis=-1, keepdims=True) + NORM_EPS)) * g_ref[...]


def _combine(y, plan, x1, gate_f, g_final):
    bsz, s, d = x1.shape
    tm = COMBINE_ROWS
    nt = bsz * s // tm
    per_b = s // tm
    seg_src, seg_cnt, seg_dst, pos3, gates3 = plan
    per_tile = pl.BlockSpec((None, 1, TOP_K * tm), lambda i, *_: (i, 0, 0), memory_space=pltpu.SMEM)
    return pl.pallas_call(
        _combine_kernel,
        grid_spec=pltpu.PrefetchScalarGridSpec(
            num_scalar_prefetch=3,
            grid=(nt,),
            in_specs=[per_tile, per_tile,
                      pl.BlockSpec(memory_space=pl.ANY),
                      pl.BlockSpec((None, tm, d), lambda i, *_: (i // per_b, i % per_b, 0)),
                      pl.BlockSpec((None, 1, d), lambda i, *_: (i // per_b, 0, 0)),
                      pl.BlockSpec((1, d), lambda i, *_: (0, 0))],
            out_specs=pl.BlockSpec((None, tm, d), lambda i, *_: (i // per_b, i % per_b, 0)),
            scratch_shapes=[pltpu.VMEM((2, TOP_K * tm * ROW_CHUNKS, LANES), F32),
                            pltpu.VMEM((tm * ROW_CHUNKS, LANES), F32),
                            pltpu.SemaphoreType.DMA((2,))]),
        out_shape=jax.ShapeDtypeStruct((bsz, s, d), F32),
        compiler_params=_params(("arbitrary",)),
        name="combine",
    )(seg_src, seg_cnt, seg_dst, pos3, gates3, y, x1, gate_f, g_final.reshape(1, d))


def _combine_plan(idx, rank, gates, pstart):
    tm = COMBINE_ROWS
    t = idx.shape[1]
    nt = t // tm
    experts = jnp.arange(N_EXPERTS, dtype=I32)
    hot = idx.reshape(TOP_K, nt, tm)[..., None] == experts
    tile_cnt = jnp.sum(hot.astype(I32), axis=(0, 2))
    tile_base = jnp.cumsum(tile_cnt, axis=0) - tile_cnt
    stage_at = jnp.cumsum(tile_cnt, axis=1) - tile_cnt
    pos = rank.reshape(TOP_K, nt, tm) + jnp.sum(
        jnp.where(hot, (stage_at - tile_base)[None, :, None, :], 0), axis=-1)
    flat = lambda a: a.reshape(-1).astype(I32)
    return (flat(pstart[None, :] + tile_base), flat(tile_cnt), flat(stage_at),
            (pos * ROW_CHUNKS).transpose(1, 0, 2).reshape(nt, 1, TOP_K * tm).astype(I32),
            gates.reshape(TOP_K, nt, tm).transpose(1, 0, 2).reshape(nt, 1, TOP_K * tm))


def _routing_plan(idx, rank, counts, n_rows):
    rows = EXPERT_ROWS
    experts = jnp.arange(N_EXPERTS, dtype=I32)

    def lookup(table, e):
        return jnp.sum(jnp.where(e[..., None] == experts, table, 0), axis=-1)

    def segment_of(ends, pos):
        return jnp.minimum(jnp.sum((pos[..., None] >= ends).astype(I32), axis=-1), N_EXPERTS - 1)

    padded = ((counts + rows - 1) // rows) * rows
    pend = jnp.cumsum(padded)
    pstart = pend - padded
    dest = lookup(pstart, idx) + rank
    n_blocks = n_rows // rows
    blk_exp = segment_of(pend, jnp.arange(n_blocks, dtype=I32) * rows)
    n_used = (pend[-1] // rows).astype(I32).reshape(1)
    return dest.astype(I32), blk_exp.astype(I32), n_used, pstart


def kernel(x, c, w_ada, b_ada, g_mix, w_in, conv_w, conv_b, w_rg_a, b_rg_a, w_rg_x, b_rg_x, lam, w_out,
           g_ffn, w_router, b_router, w_gate_up, b_gate_up, w_down, b_down, g_final):
    bsz, s, d = x.shape
    t = bsz * s
    depth = w_ada.shape[0]
    assert depth == 1, "the combine kernel applies the final norm; one layer only"
    for l in range(depth):
        mod = _ada(c, w_ada[l], b_ada[l]).reshape(bsz, 6, 1, d)
        shift_m, scale_m, gate_m, shift_f, scale_f, gate_f = (mod[:, j] for j in range(6))
        q, k, v, xr, gr = _inproj(x, g_mix[l], shift_m, scale_m, w_in[l].astype(BF16))
        attn = _attention(q, k, v)
        rec = _rglru(xr, gr, conv_w[l], conv_b[l], w_rg_a[l], b_rg_a[l], w_rg_x[l], b_rg_x[l], lam[l])
        x1, h2rows, idx, gates, rank, cnt = _outproj(
            attn, rec, w_out[l].astype(BF16), x, gate_m, g_ffn[l], shift_f, scale_f, w_router[l], b_router[l])
        n_rows = t * TOP_K + N_EXPERTS * EXPERT_ROWS
        dest, blk_exp, n_used, pstart = _routing_plan(idx, rank, cnt[:, 0], n_rows)
        slots = _slots(dest.reshape(-1), n_rows)
        y = _experts(h2rows, slots, blk_exp, n_used, w_gate_up[l], b_gate_up[l], w_down[l], b_down[l])
        x = _combine(y, _combine_plan(idx, rank, gates, pstart), x1, gate_f, g_final)
    return x
```

```python
import jax
import jax.numpy as jnp
from jax import lax
from jax.experimental import pallas as pl
from jax.experimental.pallas import tpu as pltpu

F32 = jnp.float32
BF16 = jnp.bfloat16
I32 = jnp.int32

HEAD_DIM = 64
DILATED_PATTERNS = ((128, 1), (512, 4), (2048, 16))
REC_BLOCKS = 8
CONV_WIDTH = 4
RG_C = 8.0
N_EXPERTS = 32
TOP_K = 4
SWIGLU_LIMIT = 7.0
SWIGLU_ALPHA = 1.702
NORM_EPS = 1e-6
NEG_INF = -1e30
LOG2_E = 1.4426950408889634

LANES = 128
SUBLANES = 8
ROW_CHUNKS = 8
VMEM_LIMIT = 48 * 1024 * 1024

INPROJ_ROWS = 512
RGLRU_ROWS = 512
OUTPROJ_ROWS = 512
ATTN_BLOCK = 128
ATTN_UNROLL = 8
EXPERT_ROWS = 512
COMBINE_ROWS = 512
COMBINE_PIECE = 64


def _params(sem, vmem=VMEM_LIMIT):
    return pltpu.CompilerParams(dimension_semantics=sem, vmem_limit_bytes=vmem)


def _ada_kernel(c_ref, w_ref, b_ref, o_ref):
    o_ref[...] = jnp.dot(c_ref[...], w_ref[...], precision=lax.Precision.HIGHEST,
                         preferred_element_type=F32) + b_ref[...]


def _ada(c, w, b):
    bsz, d = c.shape
    n = w.shape[1]
    return pl.pallas_call(
        _ada_kernel,
        grid=(n // d,),
        in_specs=[pl.BlockSpec((bsz, d), lambda j: (0, 0)),
                  pl.BlockSpec((d, d), lambda j: (0, j)),
                  pl.BlockSpec((1, d), lambda j: (0, j))],
        out_specs=pl.BlockSpec((bsz, d), lambda j: (0, j)),
        out_shape=jax.ShapeDtypeStruct((bsz, n), F32),
        compiler_params=_params(("arbitrary",)),
        name="ada",
    )(c, w, b.reshape(1, n))


def _rms_modulate(x, g, shift, scale):
    y = x * lax.rsqrt(jnp.mean(x * x, axis=-1, keepdims=True) + NORM_EPS)
    return (y * g) * (1.0 + scale) + shift


def _inproj_kernel(x_ref, g_ref, sh_ref, sc_ref, w_ref, q_ref, k_ref, v_ref, xr_ref, gr_ref):
    h = _rms_modulate(x_ref[...], g_ref[...], sh_ref[...], sc_ref[...]).astype(BF16)
    outs = (q_ref, k_ref, v_ref, xr_ref, gr_ref)
    width = q_ref.shape[-1]
    for j, o_ref in enumerate(outs):
        z = jnp.dot(h, w_ref[:, j * width:(j + 1) * width], preferred_element_type=F32)
        o_ref[...] = z.astype(o_ref.dtype)


def _inproj(x, g, shift, scale, w_bf):
    bsz, s, d = x.shape
    n = w_bf.shape[1]
    width = n // 5
    tm = INPROJ_ROWS
    row = pl.BlockSpec((None, 1, d), lambda b, i: (b, 0, 0))
    out_blk = pl.BlockSpec((None, tm, width), lambda b, i: (b, i, 0))
    shp = lambda dt: jax.ShapeDtypeStruct((bsz, s, width), dt)
    return pl.pallas_call(
        _inproj_kernel,
        grid=(bsz, s // tm),
        in_specs=[pl.BlockSpec((None, tm, d), lambda b, i: (b, i, 0)),
                  pl.BlockSpec((1, d), lambda b, i: (0, 0)),
                  row, row,
                  pl.BlockSpec((d, n), lambda b, i: (0, 0))],
        out_specs=[out_blk] * 5,
        out_shape=[shp(BF16), shp(BF16), shp(BF16), shp(F32), shp(F32)],
        compiler_params=_params(("arbitrary", "arbitrary")),
        name="inproj",
    )(x, g.reshape(1, d), shift, scale, w_bf)


def _sigmoid(x):
    return 0.5 * jnp.tanh(0.5 * x) + 0.5


def _gelu_tanh(x):
    return 0.5 * x * (1.0 + jnp.tanh(0.7978845608028654 * (x + 0.044715 * (x * x * x))))


def _rglru_kernel(xr_ref, gr_ref, cw_ref, cb_ref, wa_ref, ba_ref, wx_ref, bx_ref, lam_ref,
                  o_ref, xe_ref, a_ref, u_ref, tail_ref, h_ref):
    ts, ch = xr_ref.shape
    pad = SUBLANES

    @pl.when(pl.program_id(1) == 0)
    def _():
        tail_ref[...] = jnp.zeros_like(tail_ref)
        h_ref[...] = jnp.zeros_like(h_ref)

    x = xr_ref[...]
    xe_ref[0:pad, :] = tail_ref[...]
    xe_ref[pad:pad + ts, :] = x
    tail_ref[...] = x[ts - pad:ts, :]
    xc = cb_ref[...] + jnp.zeros((ts, ch), F32)
    for j in range(CONV_WIDTH):
        off = pad - (CONV_WIDTH - 1) + j
        xc = xc + cw_ref[j:j + 1, :] * xe_ref[off:off + ts, :]

    xb = xc.astype(BF16)
    r = _sigmoid(jnp.dot(xb, wa_ref[...], preferred_element_type=F32) + ba_ref[...])
    i = _sigmoid(jnp.dot(xb, wx_ref[...], preferred_element_type=F32) + bx_ref[...])
    nl = -lam_ref[...]
    softplus = jnp.maximum(nl, 0.0) + jnp.log(1.0 + jnp.exp(-jnp.abs(nl)))
    log_a = (-RG_C) * r * softplus
    a = jnp.exp(log_a)
    mult = jnp.sqrt(1.0 - a * a)
    u = mult * (i * xc)

    ng = ts // SUBLANES
    a3 = a.reshape(ng, SUBLANES, ch)
    u3 = u.reshape(ng, SUBLANES, ch)
    rid = lax.broadcasted_iota(I32, (ng, SUBLANES, ch), 1)
    for sft in (1, 2, 4):
        a_s = pltpu.roll(a3, sft, 1)
        u_s = pltpu.roll(u3, sft, 1)
        keep = rid >= sft
        u3 = jnp.where(keep, a3 * u_s + u3, u3)
        a3 = jnp.where(keep, a3 * a_s, a3)
    a_ref[...] = a3.reshape(ts, ch)
    u_ref[...] = u3.reshape(ts, ch)

    inner = 8

    def body(gi, hprev):
        for jj in range(inner):
            base = pl.multiple_of((gi * inner + jj) * SUBLANES, SUBLANES)
            hcur = u_ref[pl.ds(base, SUBLANES), :] + a_ref[pl.ds(base, SUBLANES), :] * hprev
            u_ref[pl.ds(base, SUBLANES), :] = hcur
            hprev = hcur[SUBLANES - 1:SUBLANES, :]
        return hprev

    h_ref[...] = lax.fori_loop(0, ng // inner, body, h_ref[...])
    o_ref[...] = (u_ref[...] * _gelu_tanh(gr_ref[...])).astype(o_ref.dtype)


def _block_diag(w):
    nb, hin, hout = w.shape
    eye = jnp.eye(nb, dtype=w.dtype)
    return (eye[:, None, :, None] * w[:, :, None, :]).reshape(nb * hin, nb * hout)


def _rglru(xr, gr, conv_w, conv_b, w_a, b_a, w_x, b_x, lam):
    bsz, s, ch = xr.shape
    ts = RGLRU_ROWS
    blk = pl.BlockSpec((None, ts, ch), lambda b, t: (b, t, 0))
    full = lambda r, c: pl.BlockSpec((r, c), lambda b, t: (0, 0))
    return pl.pallas_call(
        _rglru_kernel,
        grid=(bsz, s // ts),
        in_specs=[blk, blk, full(CONV_WIDTH, ch), full(1, ch), full(ch, ch), full(1, ch),
                  full(ch, ch), full(1, ch), full(1, ch)],
        out_specs=blk,
        out_shape=jax.ShapeDtypeStruct((bsz, s, ch), BF16),
        scratch_shapes=[pltpu.VMEM((ts + SUBLANES, ch), F32),
                        pltpu.VMEM((ts, ch), F32),
                        pltpu.VMEM((ts, ch), F32),
                        pltpu.VMEM((SUBLANES, ch), F32),
                        pltpu.VMEM((1, ch), F32)],
        compiler_params=_params(("arbitrary", "arbitrary")),
        name="rglru",
    )(xr, gr, conv_w, conv_b.reshape(1, ch), _block_diag(w_a).astype(BF16), b_a.reshape(1, ch),
      _block_diag(w_x).astype(BF16), b_x.reshape(1, ch), lam.reshape(1, ch))


def _attn_kernel(q_ref, k_ref, v_ref, o_ref, f_ref, f4_ref, qs_ref, kp_ref, vp_ref, op_ref, lp_ref,
                 pc_ref, ml_ref):
    s = q_ref.shape[0]
    blk = ATTN_BLOCK
    n_blocks = s // blk
    unroll = ATTN_UNROLL

    lane = lax.broadcasted_iota(I32, (blk, LANES), 1)
    head0 = lane < HEAD_DIM
    head0_wide = lax.broadcasted_iota(I32, (blk, 2 * LANES), 1) % LANES < HEAD_DIM
    qi = lax.broadcasted_iota(I32, (2 * blk, blk), 0) % blk
    kj = lax.broadcasted_iota(I32, (2 * blk, blk), 1)
    prev_ok = kj >= qi
    cur_ok = kj <= qi
    dims = (((1,), (1,)), ((), ()))

    def block_rows(j, dil):
        per_res = (s // dil) // blk
        start = j // per_res + (j % per_res) * (dil * blk)
        if dil == 1:
            return pl.ds(pl.multiple_of(start, blk), blk)
        return pl.ds(start, blk, stride=dil)

    dils = tuple(d for _, d in DILATED_PATTERNS)
    assert dils == (1, 4, 16)

    def permute(src_ref, store):
        f_ref[...] = src_ref[...].astype(F32)

        def body0(j, c):
            store(0, j, f_ref[block_rows(j, 1), :])
            return c

        def body1(j, c):
            x = f_ref[block_rows(j, 4), :]
            f4_ref[pl.ds(pl.multiple_of(j * blk, blk), blk), :] = x
            store(1, j, x)
            return c

        def body2(j, c):
            per_res = (s // 16) // blk
            res, nblk = j // per_res, j % per_res
            start = (res % 4) * (s // 4) + res // 4 + nblk * (4 * blk)
            store(2, j, f4_ref[pl.ds(start, blk, stride=4), :])
            return c

        for body in (body0, body1, body2):
            for j in range(n_blocks):
                body(j, 0)

    def store_q(p, j, x):
        x = x * (HEAD_DIM ** -0.5 * LOG2_E)
        base = pl.multiple_of(j * (2 * blk), 2 * blk)
        qs_ref[p, pl.ds(base, blk), :] = jnp.where(head0, x, 0.0).astype(BF16)
        qs_ref[p, pl.ds(base + blk, blk), :] = jnp.where(head0, 0.0, x).astype(BF16)

    def store_kv(dst_ref):
        def store(p, j, x):
            dst_ref[p, pl.ds(pl.multiple_of((j + 1) * blk, blk), blk), 0:LANES] = x.astype(BF16)
        return store

    for p in range(len(DILATED_PATTERNS)):
        kp_ref[p, 0:blk, :] = jnp.zeros((blk, LANES), BF16)
        vp_ref[p, 0:blk, 0:LANES] = jnp.zeros((blk, LANES), BF16)
        vp_ref[p, :, LANES:] = jnp.ones((s + blk, LANES), BF16)
    permute(q_ref, store_q)
    permute(k_ref, store_kv(kp_ref))
    permute(v_ref, store_kv(vp_ref))

    def kv_rows(j):
        return pl.ds(pl.multiple_of(j * blk, blk), 2 * blk)

    def scores(p, g):
        per_res = (s // dils[p]) // blk
        for u in range(unroll):
            j = g * unroll + u
            no_prev = jnp.where(j % per_res > 0, 0.0, NEG_INF)
            qs = qs_ref[p, pl.ds(pl.multiple_of(j * (2 * blk), 2 * blk), 2 * blk), :]
            sc = lax.dot_general(qs, kp_ref[p, kv_rows(j), :], dims, preferred_element_type=F32)
            s_prev = jnp.where(prev_ok, sc[:, :blk], NEG_INF) + no_prev
            s_cur = jnp.where(cur_ok, sc[:, blk:], NEG_INF)
            m = jnp.max(jnp.maximum(s_prev, s_cur), axis=-1, keepdims=True)
            pc_ref[g % 2, u] = jnp.concatenate(
                [jnp.exp2(s_prev - m), jnp.exp2(s_cur - m)], axis=-1).astype(BF16)
            ml_ref[g % 2, u] = jnp.where(head0, m[:blk], m[blk:])

    def outputs(p, g):
        for u in range(unroll):
            j = g * unroll + u
            rl = jnp.dot(pc_ref[g % 2, u], vp_ref[p, kv_rows(j), :], preferred_element_type=F32)
            rl = jnp.where(head0_wide, rl[:blk], rl[blk:])
            l = rl[:, LANES:]
            rows = block_rows(j, dils[p])
            op_ref[p, rows, :] = rl[:, :LANES] / l
            lp_ref[p, rows, :] = ml_ref[g % 2, u] + jnp.log2(l)

    n_groups = n_blocks // unroll
    assert n_groups % 2 == 0
    n_pat = len(DILATED_PATTERNS)
    scores(0, 0)
    for p, (window, dil) in enumerate(DILATED_PATTERNS):
        assert window // dil == blk

        def body(g, carry, p=p):
            outputs(p, g - 1)
            scores(p, g)
            return carry

        lax.fori_loop(1, n_groups, body, 0)
        outputs(p, n_groups - 1)
        if p + 1 < n_pat:
            scores(p + 1, 0)

    chunk = 512
    for c0 in range(0, s, chunk):
        sl = pl.ds(c0, chunk)
        l0, l1, l2 = lp_ref[0, sl, :], lp_ref[1, sl, :], lp_ref[2, sl, :]
        m = jnp.maximum(jnp.maximum(l0, l1), l2)
        w0, w1, w2 = jnp.exp2(l0 - m), jnp.exp2(l1 - m), jnp.exp2(l2 - m)
        num = w0 * op_ref[0, sl, :] + w1 * op_ref[1, sl, :] + w2 * op_ref[2, sl, :]
        o_ref[sl, :] = (num / (w0 + w1 + w2)).astype(o_ref.dtype)


def _attention(q, k, v):
    bsz, s, da = q.shape
    blk = pl.BlockSpec((None, s, LANES), lambda b, h: (b, 0, h))
    n_pat = len(DILATED_PATTERNS)
    return pl.pallas_call(
        _attn_kernel,
        grid=(bsz, da // LANES),
        in_specs=[blk, blk, blk],
        out_specs=blk,
        out_shape=jax.ShapeDtypeStruct((bsz, s, da), BF16),
        scratch_shapes=[pltpu.VMEM((s, LANES), F32),
                        pltpu.VMEM((s, LANES), F32),
                        pltpu.VMEM((n_pat, 2 * s, LANES), BF16),
                        pltpu.VMEM((n_pat, s + ATTN_BLOCK, LANES), BF16),
                        pltpu.VMEM((n_pat, s + ATTN_BLOCK, 2 * LANES), BF16),
                        pltpu.VMEM((n_pat, s, LANES), F32),
                        pltpu.VMEM((n_pat, s, LANES), F32),
                        pltpu.VMEM((2, ATTN_UNROLL, 2 * ATTN_BLOCK, 2 * ATTN_BLOCK), BF16),
                        pltpu.VMEM((2, ATTN_UNROLL, ATTN_BLOCK, LANES), F32)],
        compiler_params=_params(("arbitrary", "arbitrary"), 56 * 1024 * 1024),
        name="attn",
    )(q, k, v)


def _outproj_kernel(attn_ref, rec_ref, w_ref, x_ref, gm_ref, g_ref, sh_ref, sc_ref, wr_ref, br_ref,
                    x1_ref, h2_ref, idx_ref, gate_ref, rank_ref, cnt_ref, base_ref):
    tm = x_ref.shape[0]
    da = attn_ref.shape[1]

    @pl.when(jnp.logical_and(pl.program_id(0) == 0, pl.program_id(1) == 0))
    def _():
        base_ref[...] = jnp.zeros_like(base_ref)

    y = (jnp.dot(attn_ref[...], w_ref[0:da, :], preferred_element_type=F32)
         + jnp.dot(rec_ref[...], w_ref[da:, :], preferred_element_type=F32))
    x1 = x_ref[...] + gm_ref[...] * y
    x1_ref[...] = x1
    h2 = _rms_modulate(x1, g_ref[...], sh_ref[...], sc_ref[...])
    for c in range(ROW_CHUNKS):
        h2_ref[pl.ds(c, tm, stride=ROW_CHUNKS), :] = h2[:, c * LANES:(c + 1) * LANES]

    work = lax.dot_general(wr_ref[...], h2, (((1,), (1,)), ((), ())), precision=lax.Precision.HIGHEST,
                           preferred_element_type=F32) + br_ref[...]
    n_exp = work.shape[0]
    eid = lax.broadcasted_iota(I32, (n_exp, tm), 0)
    vals, hots = [], []
    for kk in range(TOP_K):
        m = jnp.max(work, axis=0, keepdims=True)
        sel = jnp.min(jnp.where(work == m, eid, n_exp), axis=0, keepdims=True)
        hot = eid == sel
        idx_ref[kk:kk + 1, :] = sel
        vals.append(m)
        hots.append(hot)
        work = jnp.where(hot, -jnp.inf, work)
    exps = [jnp.exp(v - vals[0]) for v in vals]
    den = exps[0] + exps[1] + exps[2] + exps[3]
    for kk in range(TOP_K):
        gate_ref[kk:kk + 1, :] = exps[kk] / den

    cnt = jnp.zeros((n_exp, tm), F32)
    for hot in hots:
        cnt = cnt + hot.astype(F32)
    tr = lax.broadcasted_iota(I32, (tm, tm), 0)
    tc = lax.broadcasted_iota(I32, (tm, tm), 1)
    before = jnp.where(tr < tc, 1.0, 0.0).astype(BF16)
    prior = jnp.dot(cnt.astype(BF16), before, preferred_element_type=F32) + base_ref[:, 0:1]
    for kk in range(TOP_K):
        rank_ref[kk:kk + 1, :] = jnp.sum(jnp.where(hots[kk], prior, 0.0), axis=0,
                                         keepdims=True).astype(I32)
    base_ref[...] = base_ref[...] + jnp.sum(cnt, axis=1, keepdims=True)
    cnt_ref[...] = base_ref[...].astype(I32)


def _outproj(attn, rec, w_bf, x, gate_m, g_ffn, shift_f, scale_f, w_router, b_router):
    bsz, s, d = x.shape
    da = attn.shape[-1]
    n_exp = w_router.shape[1]
    tm = OUTPROJ_ROWS
    nt = s // tm
    t = bsz * s
    row = pl.BlockSpec((None, 1, d), lambda b, i: (b, 0, 0))
    small = pl.BlockSpec((TOP_K, tm), lambda b, i: (0, b * nt + i))
    return pl.pallas_call(
        _outproj_kernel,
        grid=(bsz, nt),
        in_specs=[pl.BlockSpec((None, tm, da), lambda b, i: (b, i, 0)),
                  pl.BlockSpec((None, tm, da), lambda b, i: (b, i, 0)),
                  pl.BlockSpec((d, d), lambda b, i: (0, 0)),
                  pl.BlockSpec((None, tm, d), lambda b, i: (b, i, 0)),
                  row,
                  pl.BlockSpec((1, d), lambda b, i: (0, 0)),
                  row, row,
                  pl.BlockSpec((n_exp, d), lambda b, i: (0, 0)),
                  pl.BlockSpec((n_exp, 1), lambda b, i: (0, 0))],
        out_specs=[pl.BlockSpec((None, tm, d), lambda b, i: (b, i, 0)),
                   pl.BlockSpec((tm * ROW_CHUNKS, LANES), lambda b, i: (b * nt + i, 0)),
                   small, small, small,
                   pl.BlockSpec((n_exp, LANES), lambda b, i: (0, 0))],
        out_shape=[jax.ShapeDtypeStruct((bsz, s, d), F32),
                   jax.ShapeDtypeStruct((t * ROW_CHUNKS, LANES), F32),
                   jax.ShapeDtypeStruct((TOP_K, t), I32),
                   jax.ShapeDtypeStruct((TOP_K, t), F32),
                   jax.ShapeDtypeStruct((TOP_K, t), I32),
                   jax.ShapeDtypeStruct((n_exp, LANES), I32)],
        scratch_shapes=[pltpu.VMEM((n_exp, LANES), F32)],
        compiler_params=_params(("arbitrary", "arbitrary")),
        name="outproj_router",
    )(attn, rec, w_bf, x, gate_m, g_ffn.reshape(1, d), shift_f, scale_f, w_router.T,
      b_router.reshape(n_exp, 1))


def _row_slice(ref, row):
    return ref.at[pl.ds(pl.multiple_of(row * ROW_CHUNKS, ROW_CHUNKS), ROW_CHUNKS), :]


def _slots_kernel(dest_ref, init_ref, slot_ref, sem):
    n_assign = dest_ref.shape[0]
    unroll = 32

    init = pltpu.make_async_copy(init_ref, slot_ref, sem)
    init.start()
    init.wait()

    def assign_body(i, c):
        for u in range(unroll):
            slot_ref[dest_ref[i * unroll + u]] = i * unroll + u
        return c

    lax.fori_loop(0, n_assign // unroll, assign_body, 0)


def _slots(dest_flat, n_slots):
    smem = pl.BlockSpec(memory_space=pltpu.SMEM)
    return pl.pallas_call(
        _slots_kernel,
        in_specs=[smem, pl.BlockSpec(memory_space=pltpu.VMEM)],
        out_specs=smem,
        out_shape=jax.ShapeDtypeStruct((n_slots,), I32),
        scratch_shapes=[pltpu.SemaphoreType.DMA(())],
        name="slots",
    )(dest_flat, jnp.arange(n_slots, dtype=I32))


GATHER_BUFS = 3


def _experts_kernel(bexp_ref, bvalid_ref, nused_ref, slot_ref, h2_ref, wgu_ref, bgu_ref, wd_ref, bd_ref,
                    y_ref, xg_ref, xs_ref, wgu_bf, wd_bf, gsem):
    b = pl.program_id(0)
    nb = pl.num_programs(0)
    rows = xs_ref.shape[0]
    f = wd_ref.shape[0]
    n_tok = h2_ref.shape[0] // ROW_CHUNKS
    assert n_tok & (n_tok - 1) == 0
    used = b < nused_ref[0]
    changed = jnp.logical_or(b == 0, bexp_ref[b] != bexp_ref[jnp.maximum(b - 1, 0)])

    def buf_rows(ref, buf, i):
        if isinstance(i, int):
            return ref.at[buf, pl.ds(i * ROW_CHUNKS, ROW_CHUNKS), :]
        return ref.at[buf, pl.ds(pl.multiple_of(i * ROW_CHUNKS, ROW_CHUNKS), ROW_CHUNKS), :]

    def fetch_row(base, buf, i):
        tok = slot_ref[base + i] & (n_tok - 1)
        return pltpu.make_async_copy(_row_slice(h2_ref, tok), buf_rows(xg_ref, buf, i), gsem.at[buf])

    def fetch_block(blk, buf, rolled=False):
        base = blk * rows
        if rolled:
            def body(i, c):
                fetch_row(base, buf, i).start()
                return c
            lax.fori_loop(0, rows, body, 0)
        else:
            for i in range(rows):
                fetch_row(base, buf, i).start(priority=i % 2)

    def wait_block(buf):
        pltpu.make_async_copy(xg_ref.at[buf], xg_ref.at[buf], gsem.at[buf]).wait()

    @pl.when(b == 0)
    def _():
        fetch_block(0, 0, rolled=True)
        fetch_block(1, 1, rolled=True)

    gbuf = b % GATHER_BUFS
    n_used = nused_ref[0]

    @pl.when(b < n_used + 2)
    def _():
        wait_block(gbuf)

    def fetch_ahead():
        fetch_block(jnp.minimum(b + 2, nb - 1), (b + 2) % GATHER_BUFS)

    @pl.when(jnp.logical_and(used, changed))
    def _():
        wgu_bf[...] = wgu_ref[...].astype(BF16)
        wd_bf[...] = wd_ref[...].astype(BF16)

    def run_expert(m):
        for c in range(ROW_CHUNKS):
            xs_ref[0:m, c * LANES:(c + 1) * LANES] = (
                xg_ref[gbuf, pl.ds(c, m, stride=ROW_CHUNKS), :].astype(BF16))
        fetch_ahead()
        gu = jnp.dot(xs_ref[0:m, :], wgu_bf[...], preferred_element_type=F32) + bgu_ref[...]
        gate = jnp.minimum(gu[:, :f], SWIGLU_LIMIT)
        up = jnp.clip(gu[:, f:], -SWIGLU_LIMIT, SWIGLU_LIMIT)
        glu = gate * jax.nn.sigmoid(gate * SWIGLU_ALPHA)
        act = ((up + 1.0) * glu).astype(BF16)
        y = jnp.dot(act, wd_bf[...], preferred_element_type=F32) + bd_ref[...]
        for c in range(ROW_CHUNKS):
            y_ref[pl.ds(c, m, stride=ROW_CHUNKS), :] = y[:, c * LANES:(c + 1) * LANES]
        if m < rows:
            y_ref[m * ROW_CHUNKS:, :] = jnp.zeros(((rows - m) * ROW_CHUNKS, LANES), F32)

    few = bvalid_ref[b] <= rows // 2

    @pl.when(jnp.logical_and(used, jnp.logical_not(few)))
    def _():
        run_expert(rows)

    @pl.when(jnp.logical_and(used, few))
    def _():
        run_expert(rows // 2)

    @pl.when(jnp.logical_not(used))
    def _():
        y_ref[...] = jnp.zeros_like(y_ref)

    for late in (1, 2):
        @pl.when(jnp.logical_and(b == nb - 1, n_used > nb - 3 + late))
        def _(late=late):
            wait_block((b + late) % GATHER_BUFS)


def _experts(h2rows, slots, blk_exp, blk_valid, n_used, w_gate_up, b_gate_up, w_down, b_down):
    n_exp, d, f2 = w_gate_up.shape
    f = w_down.shape[1]
    rows = EXPERT_ROWS
    n_blocks = slots.shape[0] // rows
    assert n_blocks >= GATHER_BUFS
    buf_shape = (rows * ROW_CHUNKS, LANES)
    return pl.pallas_call(
        _experts_kernel,
        grid_spec=pltpu.PrefetchScalarGridSpec(
            num_scalar_prefetch=4,
            grid=(n_blocks,),
            in_specs=[pl.BlockSpec(memory_space=pl.ANY),
                      pl.BlockSpec((None, d, f2), lambda b, e, *_: (e[b], 0, 0)),
                      pl.BlockSpec((None, 1, f2), lambda b, e, *_: (e[b], 0, 0)),
                      pl.BlockSpec((None, f, d), lambda b, e, *_: (e[b], 0, 0)),
                      pl.BlockSpec((None, 1, d), lambda b, e, *_: (e[b], 0, 0))],
            out_specs=pl.BlockSpec(buf_shape, lambda b, e, *_: (b, 0)),
            scratch_shapes=[pltpu.VMEM((GATHER_BUFS,) + buf_shape, F32),
                            pltpu.VMEM((rows, d), BF16),
                            pltpu.VMEM((d, f2), BF16),
                            pltpu.VMEM((f, d), BF16),
                            pltpu.SemaphoreType.DMA((GATHER_BUFS,))]),
        out_shape=jax.ShapeDtypeStruct((slots.shape[0] * ROW_CHUNKS, LANES), F32),
        compiler_params=_params(("arbitrary",), 56 * 1024 * 1024),
        name="experts",
    )(blk_exp, blk_valid, n_used, slots, h2rows, w_gate_up, b_gate_up.reshape(n_exp, 1, f2), w_down,
      b_down.reshape(n_exp, 1, d))


def _combine_kernel(src_ref, cnt_ref, dst_ref, pos_ref, gate_ref, y_ref, x1_ref, gf_ref, g_ref, o_ref,
                    stage_ref, tok_ref, sem):
    i = pl.program_id(0)
    nt = pl.num_programs(0)
    tm = x1_ref.shape[0]
    n_exp = cnt_ref.shape[0] // nt

    def fetch(tile, buf):
        def per_expert(e, c):
            seg = tile * n_exp + e
            cnt, src, dst = cnt_ref[seg], src_ref[seg], dst_ref[seg]

            def piece(done, size):
                return pltpu.make_async_copy(
                    y_ref.at[pl.ds(pl.multiple_of((src + done) * ROW_CHUNKS, ROW_CHUNKS),
                                   size * ROW_CHUNKS), :],
                    stage_ref.at[buf, pl.ds(pl.multiple_of((dst + done) * ROW_CHUNKS, ROW_CHUNKS),
                                            size * ROW_CHUNKS), :],
                    sem.at[buf])

            def whole(j, c2):
                piece(j * COMBINE_PIECE, COMBINE_PIECE).start()
                return c2

            n_whole = cnt // COMBINE_PIECE
            lax.fori_loop(0, n_whole, whole, 0)
            done = n_whole * COMBINE_PIECE
            size = COMBINE_PIECE // 2
            while size >= 1:
                @pl.when((cnt & size) != 0)
                def _(size=size, done=done):
                    piece(done, size).start()

                done = done + (cnt & size)
                size //= 2
            return c
        lax.fori_loop(0, n_exp, per_expert, 0)

    @pl.when(i == 0)
    def _():
        fetch(0, 0)

    @pl.when(i + 1 < nt)
    def _():
        fetch(i + 1, (i + 1) % 2)

    buf = i % 2
    pltpu.make_async_copy(stage_ref.at[buf], stage_ref.at[buf], sem.at[buf]).wait()

    for t in range(tm):
        acc = None
        for kk in range(TOP_K):
            row = stage_ref[buf, pl.ds(pl.multiple_of(pos_ref[0, kk * tm + t], ROW_CHUNKS), ROW_CHUNKS), :]
            term = row * gate_ref[0, kk * tm + t]
            acc = term if acc is None else acc + term
        tok_ref[t * ROW_CHUNKS:(t + 1) * ROW_CHUNKS, :] = acc

    moe = jnp.concatenate([tok_ref[pl.ds(c, tm, stride=ROW_CHUNKS), :] for c in range(ROW_CHUNKS)],
                          axis=-1)
    x2 = x1_ref[...] + gf_ref[...] * moe
    o_ref[...] = (x2 * lax.rsqrt(jnp.mean(x2 * x2, axis=-1, keepdims=True) + NORM_EPS)) * g_ref[...]


def _combine(y, plan, x1, gate_f, g_final):
    bsz, s, d = x1.shape
    tm = COMBINE_ROWS
    nt = bsz * s // tm
    per_b = s // tm
    seg_src, seg_cnt, seg_dst, pos3, gates3 = plan
    per_tile = pl.BlockSpec((None, 1, TOP_K * tm), lambda i, *_: (i, 0, 0), memory_space=pltpu.SMEM)
    return pl.pallas_call(
        _combine_kernel,
        grid_spec=pltpu.PrefetchScalarGridSpec(
            num_scalar_prefetch=3,
            grid=(nt,),
            in_specs=[per_tile, per_tile,
                      pl.BlockSpec(memory_space=pl.ANY),
                      pl.BlockSpec((None, tm, d), lambda i, *_: (i // per_b, i % per_b, 0)),
                      pl.BlockSpec((None, 1, d), lambda i, *_: (i // per_b, 0, 0)),
                      pl.BlockSpec((1, d), lambda i, *_: (0, 0))],
            out_specs=pl.BlockSpec((None, tm, d), lambda i, *_: (i // per_b, i % per_b, 0)),
            scratch_shapes=[pltpu.VMEM((2, TOP_K * tm * ROW_CHUNKS, LANES), F32),
                            pltpu.VMEM((tm * ROW_CHUNKS, LANES), F32),
                            pltpu.SemaphoreType.DMA((2,))]),
        out_shape=jax.ShapeDtypeStruct((bsz, s, d), F32),
        compiler_params=_params(("arbitrary",)),
        name="combine",
    )(seg_src, seg_cnt, seg_dst, pos3, gates3, y, x1, gate_f, g_final.reshape(1, d))


def _combine_plan(idx, rank, gates, pstart):
    tm = COMBINE_ROWS
    t = idx.shape[1]
    nt = t // tm
    experts = jnp.arange(N_EXPERTS, dtype=I32)
    hot = idx.reshape(TOP_K, nt, tm)[..., None] == experts
    tile_cnt = jnp.sum(hot.astype(I32), axis=(0, 2))
    tile_base = jnp.cumsum(tile_cnt, axis=0) - tile_cnt
    stage_at = jnp.cumsum(tile_cnt, axis=1) - tile_cnt
    pos = rank.reshape(TOP_K, nt, tm) + jnp.sum(
        jnp.where(hot, (stage_at - tile_base)[None, :, None, :], 0), axis=-1)
    flat = lambda a: a.reshape(-1).astype(I32)
    return (flat(pstart[None, :] + tile_base), flat(tile_cnt), flat(stage_at),
            (pos * ROW_CHUNKS).transpose(1, 0, 2).reshape(nt, 1, TOP_K * tm).astype(I32),
            gates.reshape(TOP_K, nt, tm).transpose(1, 0, 2).reshape(nt, 1, TOP_K * tm))


def _routing_plan(idx, rank, counts, n_rows):
    rows = EXPERT_ROWS
    experts = jnp.arange(N_EXPERTS, dtype=I32)

    def lookup(table, e):
        return jnp.sum(jnp.where(e[..., None] == experts, table, 0), axis=-1)

    def segment_of(ends, pos):
        return jnp.minimum(jnp.sum((pos[..., None] >= ends).astype(I32), axis=-1), N_EXPERTS - 1)

    padded = ((counts + rows - 1) // rows) * rows
    pend = jnp.cumsum(padded)
    pstart = pend - padded
    dest = lookup(pstart, idx) + rank
    n_blocks = n_rows // rows
    blk_exp = segment_of(pend, jnp.arange(n_blocks, dtype=I32) * rows)
    n_used = (pend[-1] // rows).astype(I32).reshape(1)
    blk_first = jnp.arange(n_blocks, dtype=I32) * rows - lookup(pstart, blk_exp)
    blk_valid = jnp.clip(lookup(counts, blk_exp) - blk_first, 0, rows)
    return dest.astype(I32), blk_exp.astype(I32), blk_valid.astype(I32), n_used, pstart


def kernel(x, c, w_ada, b_ada, g_mix, w_in, conv_w, conv_b, w_rg_a, b_rg_a, w_rg_x, b_rg_x, lam, w_out,
           g_ffn, w_router, b_router, w_gate_up, b_gate_up, w_down, b_down, g_final):
    bsz, s, d = x.shape
    t = bsz * s
    depth = w_ada.shape[0]
    assert depth == 1, "the combine kernel applies the final norm; one layer only"
    for l in range(depth):
        mod = _ada(c, w_ada[l], b_ada[l]).reshape(bsz, 6, 1, d)
        shift_m, scale_m, gate_m, shift_f, scale_f, gate_f = (mod[:, j] for j in range(6))
        q, k, v, xr, gr = _inproj(x, g_mix[l], shift_m, scale_m, w_in[l].astype(BF16))
        attn = _attention(q, k, v)
        rec = _rglru(xr, gr, conv_w[l], conv_b[l], w_rg_a[l], b_rg_a[l], w_rg_x[l], b_rg_x[l], lam[l])
        x1, h2rows, idx, gates, rank, cnt = _outproj(
            attn, rec, w_out[l].astype(BF16), x, gate_m, g_ffn[l], shift_f, scale_f, w_router[l], b_router[l])
        n_rows = t * TOP_K + N_EXPERTS * EXPERT_ROWS
        dest, blk_exp, blk_valid, n_used, pstart = _routing_plan(idx, rank, cnt[:, 0], n_rows)
        slots = _slots(dest.reshape(-1), n_rows)
        y = _experts(h2rows, slots, blk_exp, blk_valid, n_used, w_gate_up[l], b_gate_up[l], w_down[l],
                     b_down[l])
        x = _combine(y, _combine_plan(idx, rank, gates, pstart), x1, gate_f, g_final)
    return x
```

```python
import jax
import jax.numpy as jnp
from jax import lax
from jax.experimental import pallas as pl
from jax.experimental.pallas import tpu as pltpu

F32 = jnp.float32
BF16 = jnp.bfloat16
I32 = jnp.int32

HEAD_DIM = 64
DILATED_PATTERNS = ((128, 1), (512, 4), (2048, 16))
REC_BLOCKS = 8
CONV_WIDTH = 4
RG_C = 8.0
N_EXPERTS = 32
TOP_K = 4
SWIGLU_LIMIT = 7.0
SWIGLU_ALPHA = 1.702
NORM_EPS = 1e-6
NEG_INF = -1e30
LOG2_E = 1.4426950408889634

LANES = 128
SUBLANES = 8
ROW_CHUNKS = 8
VMEM_LIMIT = 48 * 1024 * 1024

INPROJ_ROWS = 512
RGLRU_ROWS = 512
OUTPROJ_ROWS = 512
ATTN_BLOCK = 128
ATTN_UNROLL = 8
EXPERT_ROWS = 512
COMBINE_ROWS = 512
COMBINE_PIECE = 64


def _params(sem, vmem=VMEM_LIMIT):
    return pltpu.CompilerParams(dimension_semantics=sem, vmem_limit_bytes=vmem)


def _halves(a):
    hi = a.astype(BF16)
    return hi, (a - hi.astype(F32)).astype(BF16)


def _ada_kernel(c_ref, w_ref, b_ref, o_ref):
    c_hi, c_lo = _halves(c_ref[...])
    w_hi, w_lo = _halves(w_ref[...])
    dot = lambda a, b: jnp.dot(a, b, preferred_element_type=F32)
    o_ref[...] = dot(c_hi, w_hi) + (dot(c_lo, w_hi) + dot(c_hi, w_lo)) + b_ref[...]


def _ada(c, w, b):
    bsz, d = c.shape
    n = w.shape[1]
    return pl.pallas_call(
        _ada_kernel,
        grid=(n // d,),
        in_specs=[pl.BlockSpec((bsz, d), lambda j: (0, 0)),
                  pl.BlockSpec((d, d), lambda j: (0, j)),
                  pl.BlockSpec((1, d), lambda j: (0, j))],
        out_specs=pl.BlockSpec((bsz, d), lambda j: (0, j)),
        out_shape=jax.ShapeDtypeStruct((bsz, n), F32),
        compiler_params=_params(("arbitrary",)),
        name="ada",
    )(c, w, b.reshape(1, n))


def _rms_modulate(x, g, shift, scale):
    y = x * lax.rsqrt(jnp.mean(x * x, axis=-1, keepdims=True) + NORM_EPS)
    return (y * g) * (1.0 + scale) + shift


def _inproj_kernel(x_ref, g_ref, sh_ref, sc_ref, w_ref, q_ref, k_ref, v_ref, xr_ref, gr_ref):
    h = _rms_modulate(x_ref[...], g_ref[...], sh_ref[...], sc_ref[...]).astype(BF16)
    outs = (q_ref, k_ref, v_ref, xr_ref, gr_ref)
    width = q_ref.shape[-1]
    for j, o_ref in enumerate(outs):
        z = jnp.dot(h, w_ref[:, j * width:(j + 1) * width], preferred_element_type=F32)
        o_ref[...] = z.astype(o_ref.dtype)


def _inproj(x, g, shift, scale, w_bf):
    bsz, s, d = x.shape
    n = w_bf.shape[1]
    width = n // 5
    tm = INPROJ_ROWS
    row = pl.BlockSpec((None, 1, d), lambda b, i: (b, 0, 0))
    out_blk = pl.BlockSpec((None, tm, width), lambda b, i: (b, i, 0))
    shp = lambda dt: jax.ShapeDtypeStruct((bsz, s, width), dt)
    return pl.pallas_call(
        _inproj_kernel,
        grid=(bsz, s // tm),
        in_specs=[pl.BlockSpec((None, tm, d), lambda b, i: (b, i, 0)),
                  pl.BlockSpec((1, d), lambda b, i: (0, 0)),
                  row, row,
                  pl.BlockSpec((d, n), lambda b, i: (0, 0))],
        out_specs=[out_blk] * 5,
        out_shape=[shp(BF16), shp(BF16), shp(BF16), shp(F32), shp(F32)],
        compiler_params=_params(("arbitrary", "arbitrary")),
        name="inproj",
    )(x, g.reshape(1, d), shift, scale, w_bf)


def _sigmoid(x):
    return 0.5 * jnp.tanh(0.5 * x) + 0.5


def _gelu_tanh(x):
    return 0.5 * x * (1.0 + jnp.tanh(0.7978845608028654 * (x + 0.044715 * (x * x * x))))


def _rglru_kernel(xr_ref, gr_ref, cw_ref, cb_ref, wa_ref, ba_ref, wx_ref, bx_ref, lam_ref,
                  o_ref, xe_ref, a_ref, u_ref, tail_ref, h_ref):
    ts, ch = xr_ref.shape
    pad = SUBLANES

    @pl.when(pl.program_id(1) == 0)
    def _():
        tail_ref[...] = jnp.zeros_like(tail_ref)
        h_ref[...] = jnp.zeros_like(h_ref)

    x = xr_ref[...]
    xe_ref[0:pad, :] = tail_ref[...]
    xe_ref[pad:pad + ts, :] = x
    tail_ref[...] = x[ts - pad:ts, :]
    xc = cb_ref[...] + jnp.zeros((ts, ch), F32)
    for j in range(CONV_WIDTH):
        off = pad - (CONV_WIDTH - 1) + j
        xc = xc + cw_ref[j:j + 1, :] * xe_ref[off:off + ts, :]

    xb = xc.astype(BF16)
    r = _sigmoid(jnp.dot(xb, wa_ref[...], preferred_element_type=F32) + ba_ref[...])
    i = _sigmoid(jnp.dot(xb, wx_ref[...], preferred_element_type=F32) + bx_ref[...])
    nl = -lam_ref[...]
    softplus = jnp.maximum(nl, 0.0) + jnp.log(1.0 + jnp.exp(-jnp.abs(nl)))
    log_a = (-RG_C) * r * softplus
    a = jnp.exp(log_a)
    mult = jnp.sqrt(1.0 - a * a)
    u = mult * (i * xc)

    ng = ts // SUBLANES
    a3 = a.reshape(ng, SUBLANES, ch)
    u3 = u.reshape(ng, SUBLANES, ch)
    rid = lax.broadcasted_iota(I32, (ng, SUBLANES, ch), 1)
    for sft in (1, 2, 4):
        a_s = pltpu.roll(a3, sft, 1)
        u_s = pltpu.roll(u3, sft, 1)
        keep = rid >= sft
        u3 = jnp.where(keep, a3 * u_s + u3, u3)
        a3 = jnp.where(keep, a3 * a_s, a3)
    a_ref[...] = a3.reshape(ts, ch)
    u_ref[...] = u3.reshape(ts, ch)

    inner = 8

    def body(gi, hprev):
        for jj in range(inner):
            base = pl.multiple_of((gi * inner + jj) * SUBLANES, SUBLANES)
            hcur = u_ref[pl.ds(base, SUBLANES), :] + a_ref[pl.ds(base, SUBLANES), :] * hprev
            u_ref[pl.ds(base, SUBLANES), :] = hcur
            hprev = hcur[SUBLANES - 1:SUBLANES, :]
        return hprev

    h_ref[...] = lax.fori_loop(0, ng // inner, body, h_ref[...])
    o_ref[...] = (u_ref[...] * _gelu_tanh(gr_ref[...])).astype(o_ref.dtype)


def _block_diag(w):
    nb, hin, hout = w.shape
    eye = jnp.eye(nb, dtype=w.dtype)
    return (eye[:, None, :, None] * w[:, :, None, :]).reshape(nb * hin, nb * hout)


def _rglru(xr, gr, conv_w, conv_b, w_a, b_a, w_x, b_x, lam):
    bsz, s, ch = xr.shape
    ts = RGLRU_ROWS
    blk = pl.BlockSpec((None, ts, ch), lambda b, t: (b, t, 0))
    full = lambda r, c: pl.BlockSpec((r, c), lambda b, t: (0, 0))
    return pl.pallas_call(
        _rglru_kernel,
        grid=(bsz, s // ts),
        in_specs=[blk, blk, full(CONV_WIDTH, ch), full(1, ch), full(ch, ch), full(1, ch),
                  full(ch, ch), full(1, ch), full(1, ch)],
        out_specs=blk,
        out_shape=jax.ShapeDtypeStruct((bsz, s, ch), BF16),
        scratch_shapes=[pltpu.VMEM((ts + SUBLANES, ch), F32),
                        pltpu.VMEM((ts, ch), F32),
                        pltpu.VMEM((ts, ch), F32),
                        pltpu.VMEM((SUBLANES, ch), F32),
                        pltpu.VMEM((1, ch), F32)],
        compiler_params=_params(("arbitrary", "arbitrary")),
        name="rglru",
    )(xr, gr, conv_w, conv_b.reshape(1, ch), _block_diag(w_a).astype(BF16), b_a.reshape(1, ch),
      _block_diag(w_x).astype(BF16), b_x.reshape(1, ch), lam.reshape(1, ch))


def _attn_kernel(q_ref, k_ref, v_ref, o_ref, f_ref, f4_ref, qs_ref, kp_ref, vp_ref, op_ref, lp_ref,
                 pc_ref, ml_ref):
    s = q_ref.shape[0]
    blk = ATTN_BLOCK
    n_blocks = s // blk
    unroll = ATTN_UNROLL

    lane = lax.broadcasted_iota(I32, (blk, LANES), 1)
    head0 = lane < HEAD_DIM
    head0_wide = lax.broadcasted_iota(I32, (blk, 2 * LANES), 1) % LANES < HEAD_DIM
    qi = lax.broadcasted_iota(I32, (2 * blk, blk), 0) % blk
    kj = lax.broadcasted_iota(I32, (2 * blk, blk), 1)
    prev_ok = kj >= qi
    cur_ok = kj <= qi
    dims = (((1,), (1,)), ((), ()))

    def block_rows(j, dil):
        per_res = (s // dil) // blk
        start = j // per_res + (j % per_res) * (dil * blk)
        if dil == 1:
            return pl.ds(pl.multiple_of(start, blk), blk)
        return pl.ds(start, blk, stride=dil)

    dils = tuple(d for _, d in DILATED_PATTERNS)
    assert dils == (1, 4, 16)

    def permute(src_ref, store):
        f_ref[...] = src_ref[...].astype(F32)

        def body0(j, c):
            store(0, j, f_ref[block_rows(j, 1), :])
            return c

        def body1(j, c):
            x = f_ref[block_rows(j, 4), :]
            f4_ref[pl.ds(pl.multiple_of(j * blk, blk), blk), :] = x
            store(1, j, x)
            return c

        def body2(j, c):
            per_res = (s // 16) // blk
            res, nblk = j // per_res, j % per_res
            start = (res % 4) * (s // 4) + res // 4 + nblk * (4 * blk)
            store(2, j, f4_ref[pl.ds(start, blk, stride=4), :])
            return c

        for body in (body0, body1, body2):
            for j in range(n_blocks):
                body(j, 0)

    def store_q(p, j, x):
        x = x * (HEAD_DIM ** -0.5 * LOG2_E)
        base = pl.multiple_of(j * (2 * blk), 2 * blk)
        qs_ref[p, pl.ds(base, blk), :] = jnp.where(head0, x, 0.0).astype(BF16)
        qs_ref[p, pl.ds(base + blk, blk), :] = jnp.where(head0, 0.0, x).astype(BF16)

    def store_kv(dst_ref):
        def store(p, j, x):
            dst_ref[p, pl.ds(pl.multiple_of((j + 1) * blk, blk), blk), 0:LANES] = x.astype(BF16)
        return store

    for p in range(len(DILATED_PATTERNS)):
        kp_ref[p, 0:blk, :] = jnp.zeros((blk, LANES), BF16)
        vp_ref[p, 0:blk, 0:LANES] = jnp.zeros((blk, LANES), BF16)
        vp_ref[p, :, LANES:] = jnp.ones((s + blk, LANES), BF16)
    permute(q_ref, store_q)
    permute(k_ref, store_kv(kp_ref))
    permute(v_ref, store_kv(vp_ref))

    def kv_rows(j):
        return pl.ds(pl.multiple_of(j * blk, blk), 2 * blk)

    def scores(p, g):
        per_res = (s // dils[p]) // blk
        for u in range(unroll):
            j = g * unroll + u
            no_prev = jnp.where(j % per_res > 0, 0.0, NEG_INF)
            qs = qs_ref[p, pl.ds(pl.multiple_of(j * (2 * blk), 2 * blk), 2 * blk), :]
            sc = lax.dot_general(qs, kp_ref[p, kv_rows(j), :], dims, preferred_element_type=F32)
            s_prev = jnp.where(prev_ok, sc[:, :blk], NEG_INF) + no_prev
            s_cur = jnp.where(cur_ok, sc[:, blk:], NEG_INF)
            m = jnp.max(jnp.maximum(s_prev, s_cur), axis=-1, keepdims=True)
            pc_ref[g % 2, u] = jnp.concatenate(
                [jnp.exp2(s_prev - m), jnp.exp2(s_cur - m)], axis=-1).astype(BF16)
            ml_ref[g % 2, u] = jnp.where(head0, m[:blk], m[blk:])

    def outputs(p, g):
        for u in range(unroll):
            j = g * unroll + u
            rl = jnp.dot(pc_ref[g % 2, u], vp_ref[p, kv_rows(j), :], preferred_element_type=F32)
            rl = jnp.where(head0_wide, rl[:blk], rl[blk:])
            l = rl[:, LANES:]
            rows = block_rows(j, dils[p])
            op_ref[p, rows, :] = rl[:, :LANES] / l
            lp_ref[p, rows, :] = ml_ref[g % 2, u] + jnp.log2(l)

    n_groups = n_blocks // unroll
    assert n_groups % 2 == 0
    n_pat = len(DILATED_PATTERNS)
    scores(0, 0)
    for p, (window, dil) in enumerate(DILATED_PATTERNS):
        assert window // dil == blk

        def body(g, carry, p=p):
            outputs(p, g - 1)
            scores(p, g)
            return carry

        lax.fori_loop(1, n_groups, body, 0)
        outputs(p, n_groups - 1)
        if p + 1 < n_pat:
            scores(p + 1, 0)

    chunk = 512
    for c0 in range(0, s, chunk):
        sl = pl.ds(c0, chunk)
        l0, l1, l2 = lp_ref[0, sl, :], lp_ref[1, sl, :], lp_ref[2, sl, :]
        m = jnp.maximum(jnp.maximum(l0, l1), l2)
        w0, w1, w2 = jnp.exp2(l0 - m), jnp.exp2(l1 - m), jnp.exp2(l2 - m)
        num = w0 * op_ref[0, sl, :] + w1 * op_ref[1, sl, :] + w2 * op_ref[2, sl, :]
        o_ref[sl, :] = (num / (w0 + w1 + w2)).astype(o_ref.dtype)


def _attention(q, k, v):
    bsz, s, da = q.shape
    blk = pl.BlockSpec((None, s, LANES), lambda b, h: (b, 0, h))
    n_pat = len(DILATED_PATTERNS)
    return pl.pallas_call(
        _attn_kernel,
        grid=(bsz, da // LANES),
        in_specs=[blk, blk, blk],
        out_specs=blk,
        out_shape=jax.ShapeDtypeStruct((bsz, s, da), BF16),
        scratch_shapes=[pltpu.VMEM((s, LANES), F32),
                        pltpu.VMEM((s, LANES), F32),
                        pltpu.VMEM((n_pat, 2 * s, LANES), BF16),
                        pltpu.VMEM((n_pat, s + ATTN_BLOCK, LANES), BF16),
                        pltpu.VMEM((n_pat, s + ATTN_BLOCK, 2 * LANES), BF16),
                        pltpu.VMEM((n_pat, s, LANES), F32),
                        pltpu.VMEM((n_pat, s, LANES), F32),
                        pltpu.VMEM((2, ATTN_UNROLL, 2 * ATTN_BLOCK, 2 * ATTN_BLOCK), BF16),
                        pltpu.VMEM((2, ATTN_UNROLL, ATTN_BLOCK, LANES), F32)],
        compiler_params=_params(("arbitrary", "arbitrary"), 56 * 1024 * 1024),
        name="attn",
    )(q, k, v)


def _outproj_kernel(attn_ref, rec_ref, w_ref, x_ref, gm_ref, g_ref, sh_ref, sc_ref, wr_ref, br_ref,
                    x1_ref, h2_ref, idx_ref, gate_ref, rank_ref, cnt_ref, base_ref):
    tm = x_ref.shape[0]
    da = attn_ref.shape[1]

    @pl.when(jnp.logical_and(pl.program_id(0) == 0, pl.program_id(1) == 0))
    def _():
        base_ref[...] = jnp.zeros_like(base_ref)

    y = (jnp.dot(attn_ref[...], w_ref[0:da, :], preferred_element_type=F32)
         + jnp.dot(rec_ref[...], w_ref[da:, :], preferred_element_type=F32))
    x1 = x_ref[...] + gm_ref[...] * y
    x1_ref[...] = x1
    h2 = _rms_modulate(x1, g_ref[...], sh_ref[...], sc_ref[...])
    for c in range(ROW_CHUNKS):
        h2_ref[pl.ds(c, tm, stride=ROW_CHUNKS), :] = h2[:, c * LANES:(c + 1) * LANES]

    def logits(w, h):
        return lax.dot_general(w, h, (((1,), (1,)), ((), ())), preferred_element_type=F32)

    w_hi, w_lo = _halves(wr_ref[...])
    h_hi, h_lo = _halves(h2)
    work = logits(w_hi, h_hi) + (logits(w_hi, h_lo) + logits(w_lo, h_hi)) + br_ref[...]
    n_exp = work.shape[0]
    eid = lax.broadcasted_iota(I32, (n_exp, tm), 0)
    vals, hots = [], []
    for kk in range(TOP_K):
        m = jnp.max(work, axis=0, keepdims=True)
        sel = jnp.min(jnp.where(work == m, eid, n_exp), axis=0, keepdims=True)
        hot = eid == sel
        idx_ref[kk:kk + 1, :] = sel
        vals.append(m)
        hots.append(hot)
        work = jnp.where(hot, -jnp.inf, work)
    exps = [jnp.exp(v - vals[0]) for v in vals]
    den = exps[0] + exps[1] + exps[2] + exps[3]
    for kk in range(TOP_K):
        gate_ref[kk:kk + 1, :] = exps[kk] / den

    cnt = jnp.zeros((n_exp, tm), F32)
    for hot in hots:
        cnt = cnt + hot.astype(F32)
    tr = lax.broadcasted_iota(I32, (tm, tm), 0)
    tc = lax.broadcasted_iota(I32, (tm, tm), 1)
    before = jnp.where(tr < tc, 1.0, 0.0).astype(BF16)
    prior = jnp.dot(cnt.astype(BF16), before, preferred_element_type=F32) + base_ref[:, 0:1]
    for kk in range(TOP_K):
        rank_ref[kk:kk + 1, :] = jnp.sum(jnp.where(hots[kk], prior, 0.0), axis=0,
                                         keepdims=True).astype(I32)
    base_ref[...] = base_ref[...] + jnp.sum(cnt, axis=1, keepdims=True)
    cnt_ref[...] = base_ref[...].astype(I32)


def _outproj(attn, rec, w_bf, x, gate_m, g_ffn, shift_f, scale_f, w_router, b_router):
    bsz, s, d = x.shape
    da = attn.shape[-1]
    n_exp = w_router.shape[1]
    tm = OUTPROJ_ROWS
    nt = s // tm
    t = bsz * s
    row = pl.BlockSpec((None, 1, d), lambda b, i: (b, 0, 0))
    small = pl.BlockSpec((TOP_K, tm), lambda b, i: (0, b * nt + i))
    return pl.pallas_call(
        _outproj_kernel,
        grid=(bsz, nt),
        in_specs=[pl.BlockSpec((None, tm, da), lambda b, i: (b, i, 0)),
                  pl.BlockSpec((None, tm, da), lambda b, i: (b, i, 0)),
                  pl.BlockSpec((d, d), lambda b, i: (0, 0)),
                  pl.BlockSpec((None, tm, d), lambda b, i: (b, i, 0)),
                  row,
                  pl.BlockSpec((1, d), lambda b, i: (0, 0)),
                  row, row,
                  pl.BlockSpec((n_exp, d), lambda b, i: (0, 0)),
                  pl.BlockSpec((n_exp, 1), lambda b, i: (0, 0))],
        out_specs=[pl.BlockSpec((None, tm, d), lambda b, i: (b, i, 0)),
                   pl.BlockSpec((tm * ROW_CHUNKS, LANES), lambda b, i: (b * nt + i, 0)),
                   small, small, small,
                   pl.BlockSpec((n_exp, LANES), lambda b, i: (0, 0))],
        out_shape=[jax.ShapeDtypeStruct((bsz, s, d), F32),
                   jax.ShapeDtypeStruct((t * ROW_CHUNKS, LANES), F32),
                   jax.ShapeDtypeStruct((TOP_K, t), I32),
                   jax.ShapeDtypeStruct((TOP_K, t), F32),
                   jax.ShapeDtypeStruct((TOP_K, t), I32),
                   jax.ShapeDtypeStruct((n_exp, LANES), I32)],
        scratch_shapes=[pltpu.VMEM((n_exp, LANES), F32)],
        compiler_params=_params(("arbitrary", "arbitrary")),
        name="outproj_router",
    )(attn, rec, w_bf, x, gate_m, g_ffn.reshape(1, d), shift_f, scale_f, w_router.T,
      b_router.reshape(n_exp, 1))


def _row_slice(ref, row):
    return ref.at[pl.ds(pl.multiple_of(row * ROW_CHUNKS, ROW_CHUNKS), ROW_CHUNKS), :]


def _slots_kernel(dest_ref, init_ref, slot_ref, sem):
    n_assign = dest_ref.shape[0]
    unroll = 32

    init = pltpu.make_async_copy(init_ref, slot_ref, sem)
    init.start()
    init.wait()

    def assign_body(i, c):
        for u in range(unroll):
            slot_ref[dest_ref[i * unroll + u]] = i * unroll + u
        return c

    lax.fori_loop(0, n_assign // unroll, assign_body, 0)


def _slots(dest_flat, n_slots):
    smem = pl.BlockSpec(memory_space=pltpu.SMEM)
    return pl.pallas_call(
        _slots_kernel,
        in_specs=[smem, pl.BlockSpec(memory_space=pltpu.VMEM)],
        out_specs=smem,
        out_shape=jax.ShapeDtypeStruct((n_slots,), I32),
        scratch_shapes=[pltpu.SemaphoreType.DMA(())],
        name="slots",
    )(dest_flat, jnp.arange(n_slots, dtype=I32))


GATHER_BUFS = 3


def _experts_kernel(bexp_ref, bvalid_ref, nused_ref, slot_ref, h2_ref, wgu_ref, bgu_ref, wd_ref, bd_ref,
                    y_ref, xg_ref, xs_ref, wgu_bf, wd_bf, gsem):
    b = pl.program_id(0)
    nb = pl.num_programs(0)
    rows = xs_ref.shape[0]
    f = wd_ref.shape[0]
    n_tok = h2_ref.shape[0] // ROW_CHUNKS
    assert n_tok & (n_tok - 1) == 0
    used = b < nused_ref[0]
    changed = jnp.logical_or(b == 0, bexp_ref[b] != bexp_ref[jnp.maximum(b - 1, 0)])

    def buf_rows(ref, buf, i):
        if isinstance(i, int):
            return ref.at[buf, pl.ds(i * ROW_CHUNKS, ROW_CHUNKS), :]
        return ref.at[buf, pl.ds(pl.multiple_of(i * ROW_CHUNKS, ROW_CHUNKS), ROW_CHUNKS), :]

    def fetch_row(base, buf, i):
        tok = slot_ref[base + i] & (n_tok - 1)
        return pltpu.make_async_copy(_row_slice(h2_ref, tok), buf_rows(xg_ref, buf, i), gsem.at[buf])

    def fetch_block(blk, buf, rolled=False):
        base = blk * rows
        if rolled:
            def body(i, c):
                fetch_row(base, buf, i).start()
                return c
            lax.fori_loop(0, rows, body, 0)
        else:
            for i in range(rows):
                fetch_row(base, buf, i).start(priority=i % 2)

    def wait_block(buf):
        pltpu.make_async_copy(xg_ref.at[buf], xg_ref.at[buf], gsem.at[buf]).wait()

    @pl.when(b == 0)
    def _():
        fetch_block(0, 0, rolled=True)
        fetch_block(1, 1, rolled=True)

    gbuf = b % GATHER_BUFS
    n_used = nused_ref[0]

    @pl.when(b < n_used + 2)
    def _():
        wait_block(gbuf)

    def fetch_ahead():
        fetch_block(jnp.minimum(b + 2, nb - 1), (b + 2) % GATHER_BUFS)

    @pl.when(jnp.logical_and(used, changed))
    def _():
        wgu_bf[...] = wgu_ref[...].astype(BF16)
        wd_bf[...] = wd_ref[...].astype(BF16)

    def run_expert(m):
        for c in range(ROW_CHUNKS):
            xs_ref[0:m, c * LANES:(c + 1) * LANES] = (
                xg_ref[gbuf, pl.ds(c, m, stride=ROW_CHUNKS), :].astype(BF16))
        fetch_ahead()
        gu = jnp.dot(xs_ref[0:m, :], wgu_bf[...], preferred_element_type=F32) + bgu_ref[...]
        gate = jnp.minimum(gu[:, :f], SWIGLU_LIMIT)
        up = jnp.clip(gu[:, f:], -SWIGLU_LIMIT, SWIGLU_LIMIT)
        glu = gate * jax.nn.sigmoid(gate * SWIGLU_ALPHA)
        act = ((up + 1.0) * glu).astype(BF16)
        y = jnp.dot(act, wd_bf[...], preferred_element_type=F32) + bd_ref[...]
        for c in range(ROW_CHUNKS):
            y_ref[pl.ds(c, m, stride=ROW_CHUNKS), :] = y[:, c * LANES:(c + 1) * LANES]
        if m < rows:
            y_ref[m * ROW_CHUNKS:, :] = jnp.zeros(((rows - m) * ROW_CHUNKS, LANES), F32)

    few = bvalid_ref[b] <= rows // 2

    @pl.when(jnp.logical_and(used, jnp.logical_not(few)))
    def _():
        run_expert(rows)

    @pl.when(jnp.logical_and(used, few))
    def _():
        run_expert(rows // 2)

    @pl.when(jnp.logical_not(used))
    def _():
        y_ref[...] = jnp.zeros_like(y_ref)

    for late in (1, 2):
        @pl.when(jnp.logical_and(b == nb - 1, n_used > nb - 3 + late))
        def _(late=late):
            wait_block((b + late) % GATHER_BUFS)


def _experts(h2rows, slots, blk_exp, blk_valid, n_used, w_gate_up, b_gate_up, w_down, b_down):
    n_exp, d, f2 = w_gate_up.shape
    f = w_down.shape[1]
    rows = EXPERT_ROWS
    n_blocks = slots.shape[0] // rows
    assert n_blocks >= GATHER_BUFS
    buf_shape = (rows * ROW_CHUNKS, LANES)
    return pl.pallas_call(
        _experts_kernel,
        grid_spec=pltpu.PrefetchScalarGridSpec(
            num_scalar_prefetch=4,
            grid=(n_blocks,),
            in_specs=[pl.BlockSpec(memory_space=pl.ANY),
                      pl.BlockSpec((None, d, f2), lambda b, e, *_: (e[b], 0, 0)),
                      pl.BlockSpec((None, 1, f2), lambda b, e, *_: (e[b], 0, 0)),
                      pl.BlockSpec((None, f, d), lambda b, e, *_: (e[b], 0, 0)),
                      pl.BlockSpec((None, 1, d), lambda b, e, *_: (e[b], 0, 0))],
            out_specs=pl.BlockSpec(buf_shape, lambda b, e, *_: (b, 0)),
            scratch_shapes=[pltpu.VMEM((GATHER_BUFS,) + buf_shape, F32),
                            pltpu.VMEM((rows, d), BF16),
                            pltpu.VMEM((d, f2), BF16),
                            pltpu.VMEM((f, d), BF16),
                            pltpu.SemaphoreType.DMA((GATHER_BUFS,))]),
        out_shape=jax.ShapeDtypeStruct((slots.shape[0] * ROW_CHUNKS, LANES), F32),
        compiler_params=_params(("arbitrary",), 56 * 1024 * 1024),
        name="experts",
    )(blk_exp, blk_valid, n_used, slots, h2rows, w_gate_up, b_gate_up.reshape(n_exp, 1, f2), w_down,
      b_down.reshape(n_exp, 1, d))


def _combine_kernel(src_ref, cnt_ref, dst_ref, pos_ref, gate_ref, y_ref, x1_ref, gf_ref, g_ref, o_ref,
                    stage_ref, tok_ref, sem):
    i = pl.program_id(0)
    nt = pl.num_programs(0)
    tm = x1_ref.shape[0]
    n_exp = cnt_ref.shape[0] // nt

    def fetch(tile, buf):
        def per_expert(e, c):
            seg = tile * n_exp + e
            cnt, src, dst = cnt_ref[seg], src_ref[seg], dst_ref[seg]

            def piece(done, size):
                return pltpu.make_async_copy(
                    y_ref.at[pl.ds(pl.multiple_of((src + done) * ROW_CHUNKS, ROW_CHUNKS),
                                   size * ROW_CHUNKS), :],
                    stage_ref.at[buf, pl.ds(pl.multiple_of((dst + done) * ROW_CHUNKS, ROW_CHUNKS),
                                            size * ROW_CHUNKS), :],
                    sem.at[buf])

            def whole(j, c2):
                piece(j * COMBINE_PIECE, COMBINE_PIECE).start()
                return c2

            n_whole = cnt // COMBINE_PIECE
            lax.fori_loop(0, n_whole, whole, 0)
            done = n_whole * COMBINE_PIECE
            size = COMBINE_PIECE // 2
            while size >= 1:
                @pl.when((cnt & size) != 0)
                def _(size=size, done=done):
                    piece(done, size).start()

                done = done + (cnt & size)
                size //= 2
            return c
        lax.fori_loop(0, n_exp, per_expert, 0)

    @pl.when(i == 0)
    def _():
        fetch(0, 0)

    @pl.when(i + 1 < nt)
    def _():
        fetch(i + 1, (i + 1) % 2)

    buf = i % 2
    pltpu.make_async_copy(stage_ref.at[buf], stage_ref.at[buf], sem.at[buf]).wait()

    for t in range(tm):
        acc = None
        for kk in range(TOP_K):
            row = stage_ref[buf, pl.ds(pl.multiple_of(pos_ref[0, kk * tm + t], ROW_CHUNKS), ROW_CHUNKS), :]
            term = row * gate_ref[0, kk * tm + t]
            acc = term if acc is None else acc + term
        tok_ref[t * ROW_CHUNKS:(t + 1) * ROW_CHUNKS, :] = acc

    moe = jnp.concatenate([tok_ref[pl.ds(c, tm, stride=ROW_CHUNKS), :] for c in range(ROW_CHUNKS)],
                          axis=-1)
    x2 = x1_ref[...] + gf_ref[...] * moe
    o_ref[...] = (x2 * lax.rsqrt(jnp.mean(x2 * x2, axis=-1, keepdims=True) + NORM_EPS)) * g_ref[...]


def _combine(y, plan, x1, gate_f, g_final):
    bsz, s, d = x1.shape
    tm = COMBINE_ROWS
    nt = bsz * s // tm
    per_b = s // tm
    seg_src, seg_cnt, seg_dst, pos3, gates3 = plan
    per_tile = pl.BlockSpec((None, 1, TOP_K * tm), lambda i, *_: (i, 0, 0), memory_space=pltpu.SMEM)
    return pl.pallas_call(
        _combine_kernel,
        grid_spec=pltpu.PrefetchScalarGridSpec(
            num_scalar_prefetch=3,
            grid=(nt,),
            in_specs=[per_tile, per_tile,
                      pl.BlockSpec(memory_space=pl.ANY),
                      pl.BlockSpec((None, tm, d), lambda i, *_: (i // per_b, i % per_b, 0)),
                      pl.BlockSpec((None, 1, d), lambda i, *_: (i // per_b, 0, 0)),
                      pl.BlockSpec((1, d), lambda i, *_: (0, 0))],
            out_specs=pl.BlockSpec((None, tm, d), lambda i, *_: (i // per_b, i % per_b, 0)),
            scratch_shapes=[pltpu.VMEM((2, TOP_K * tm * ROW_CHUNKS, LANES), F32),
                            pltpu.VMEM((tm * ROW_CHUNKS, LANES), F32),
                            pltpu.SemaphoreType.DMA((2,))]),
        out_shape=jax.ShapeDtypeStruct((bsz, s, d), F32),
        compiler_params=_params(("arbitrary",)),
        name="combine",
    )(seg_src, seg_cnt, seg_dst, pos3, gates3, y, x1, gate_f, g_final.reshape(1, d))


def _combine_plan(idx, rank, gates, pstart):
    tm = COMBINE_ROWS
    t = idx.shape[1]
    nt = t // tm
    experts = jnp.arange(N_EXPERTS, dtype=I32)
    hot = idx.reshape(TOP_K, nt, tm)[..., None] == experts
    tile_cnt = jnp.sum(hot.astype(I32), axis=(0, 2))
    tile_base = jnp.cumsum(tile_cnt, axis=0) - tile_cnt
    stage_at = jnp.cumsum(tile_cnt, axis=1) - tile_cnt
    pos = rank.reshape(TOP_K, nt, tm) + jnp.sum(
        jnp.where(hot, (stage_at - tile_base)[None, :, None, :], 0), axis=-1)
    flat = lambda a: a.reshape(-1).astype(I32)
    return (flat(pstart[None, :] + tile_base), flat(tile_cnt), flat(stage_at),
            (pos * ROW_CHUNKS).transpose(1, 0, 2).reshape(nt, 1, TOP_K * tm).astype(I32),
            gates.reshape(TOP_K, nt, tm).transpose(1, 0, 2).reshape(nt, 1, TOP_K * tm))


def _routing_plan(idx, rank, counts, n_rows):
    rows = EXPERT_ROWS
    experts = jnp.arange(N_EXPERTS, dtype=I32)

    def lookup(table, e):
        return jnp.sum(jnp.where(e[..., None] == experts, table, 0), axis=-1)

    def segment_of(ends, pos):
        return jnp.minimum(jnp.sum((pos[..., None] >= ends).astype(I32), axis=-1), N_EXPERTS - 1)

    padded = ((counts + rows - 1) // rows) * rows
    pend = jnp.cumsum(padded)
    pstart = pend - padded
    dest = lookup(pstart, idx) + rank
    n_blocks = n_rows // rows
    blk_exp = segment_of(pend, jnp.arange(n_blocks, dtype=I32) * rows)
    n_used = (pend[-1] // rows).astype(I32).reshape(1)
    blk_first = jnp.arange(n_blocks, dtype=I32) * rows - lookup(pstart, blk_exp)
    blk_valid = jnp.clip(lookup(counts, blk_exp) - blk_first, 0, rows)
    return dest.astype(I32), blk_exp.astype(I32), blk_valid.astype(I32), n_used, pstart


def kernel(x, c, w_ada, b_ada, g_mix, w_in, conv_w, conv_b, w_rg_a, b_rg_a, w_rg_x, b_rg_x, lam, w_out,
           g_ffn, w_router, b_router, w_gate_up, b_gate_up, w_down, b_down, g_final):
    bsz, s, d = x.shape
    t = bsz * s
    depth = w_ada.shape[0]
    assert depth == 1, "the combine kernel applies the final norm; one layer only"
    for l in range(depth):
        mod = _ada(c, w_ada[l], b_ada[l]).reshape(bsz, 6, 1, d)
        shift_m, scale_m, gate_m, shift_f, scale_f, gate_f = (mod[:, j] for j in range(6))
        q, k, v, xr, gr = _inproj(x, g_mix[l], shift_m, scale_m, w_in[l].astype(BF16))
        attn = _attention(q, k, v)
        rec = _rglru(xr, gr, conv_w[l], conv_b[l], w_rg_a[l], b_rg_a[l], w_rg_x[l], b_rg_x[l], lam[l])
        x1, h2rows, idx, gates, rank, cnt = _outproj(
            attn, rec, w_out[l].astype(BF16), x, gate_m, g_ffn[l], shift_f, scale_f, w_router[l], b_router[l])
        n_rows = t * TOP_K + N_EXPERTS * EXPERT_ROWS
        dest, blk_exp, blk_valid, n_used, pstart = _routing_plan(idx, rank, cnt[:, 0], n_rows)
        slots = _slots(dest.reshape(-1), n_rows)
        y = _experts(h2rows, slots, blk_exp, blk_valid, n_used, w_gate_up[l], b_gate_up[l], w_down[l],
                     b_down[l])
        x = _combine(y, _combine_plan(idx, rank, gates, pstart), x1, gate_f, g_final)
    return x
```

```python
import jax
import jax.numpy as jnp
from jax import lax
from jax.experimental import pallas as pl
from jax.experimental.pallas import tpu as pltpu

F32 = jnp.float32
BF16 = jnp.bfloat16
I32 = jnp.int32

HEAD_DIM = 64
DILATED_PATTERNS = ((128, 1), (512, 4), (2048, 16))
REC_BLOCKS = 8
CONV_WIDTH = 4
RG_C = 8.0
N_EXPERTS = 32
TOP_K = 4
SWIGLU_LIMIT = 7.0
SWIGLU_ALPHA = 1.702
NORM_EPS = 1e-6
NEG_INF = -1e30
LOG2_E = 1.4426950408889634

LANES = 128
SUBLANES = 8
ROW_CHUNKS = 8
VMEM_LIMIT = 48 * 1024 * 1024

INPROJ_ROWS = 1024
RGLRU_ROWS = 1024
OUTPROJ_ROWS = 1024
ATTN_BLOCK = 128
ATTN_UNROLL = 8
EXPERT_ROWS = 512
COMBINE_ROWS = 512
COMBINE_PIECE = 64


def _params(sem, vmem=VMEM_LIMIT):
    return pltpu.CompilerParams(dimension_semantics=sem, vmem_limit_bytes=vmem)


def _halves(a):
    hi = a.astype(BF16)
    return hi, (a - hi.astype(F32)).astype(BF16)


def _ada_kernel(c_ref, w_ref, b_ref, o_ref):
    c_hi, c_lo = _halves(c_ref[...])
    w_hi, w_lo = _halves(w_ref[...])
    dot = lambda a, b: jnp.dot(a, b, preferred_element_type=F32)
    o_ref[...] = dot(c_hi, w_hi) + (dot(c_lo, w_hi) + dot(c_hi, w_lo)) + b_ref[...]


def _ada(c, w, b):
    bsz, d = c.shape
    n = w.shape[1]
    return pl.pallas_call(
        _ada_kernel,
        grid=(n // d,),
        in_specs=[pl.BlockSpec((bsz, d), lambda j: (0, 0)),
                  pl.BlockSpec((d, d), lambda j: (0, j)),
                  pl.BlockSpec((1, d), lambda j: (0, j))],
        out_specs=pl.BlockSpec((bsz, d), lambda j: (0, j)),
        out_shape=jax.ShapeDtypeStruct((bsz, n), F32),
        compiler_params=_params(("arbitrary",)),
        name="ada",
    )(c, w, b.reshape(1, n))


def _rms_modulate(x, g, shift, scale):
    y = x * lax.rsqrt(jnp.mean(x * x, axis=-1, keepdims=True) + NORM_EPS)
    return (y * g) * (1.0 + scale) + shift


def _inproj_kernel(x_ref, g_ref, sh_ref, sc_ref, w_ref, q_ref, k_ref, v_ref, xr_ref, gr_ref):
    h = _rms_modulate(x_ref[...], g_ref[...], sh_ref[...], sc_ref[...]).astype(BF16)
    outs = (q_ref, k_ref, v_ref, xr_ref, gr_ref)
    width = q_ref.shape[-1]
    for j, o_ref in enumerate(outs):
        z = jnp.dot(h, w_ref[:, j * width:(j + 1) * width], preferred_element_type=F32)
        o_ref[...] = z.astype(o_ref.dtype)


def _inproj(x, g, shift, scale, w_bf):
    bsz, s, d = x.shape
    n = w_bf.shape[1]
    width = n // 5
    tm = INPROJ_ROWS
    row = pl.BlockSpec((None, 1, d), lambda b, i: (b, 0, 0))
    out_blk = pl.BlockSpec((None, tm, width), lambda b, i: (b, i, 0))
    shp = lambda dt: jax.ShapeDtypeStruct((bsz, s, width), dt)
    return pl.pallas_call(
        _inproj_kernel,
        grid=(bsz, s // tm),
        in_specs=[pl.BlockSpec((None, tm, d), lambda b, i: (b, i, 0)),
                  pl.BlockSpec((1, d), lambda b, i: (0, 0)),
                  row, row,
                  pl.BlockSpec((d, n), lambda b, i: (0, 0))],
        out_specs=[out_blk] * 5,
        out_shape=[shp(BF16), shp(BF16), shp(BF16), shp(F32), shp(F32)],
        compiler_params=_params(("arbitrary", "arbitrary")),
        name="inproj",
    )(x, g.reshape(1, d), shift, scale, w_bf)


def _sigmoid(x):
    return 0.5 * jnp.tanh(0.5 * x) + 0.5


def _gelu_tanh(x):
    return 0.5 * x * (1.0 + jnp.tanh(0.7978845608028654 * (x + 0.044715 * (x * x * x))))


def _rglru_kernel(xr_ref, gr_ref, cw_ref, cb_ref, wa_ref, ba_ref, wx_ref, bx_ref, lam_ref,
                  o_ref, xe_ref, a_ref, u_ref, tail_ref, h_ref):
    ts, ch = xr_ref.shape
    pad = SUBLANES

    @pl.when(pl.program_id(1) == 0)
    def _():
        tail_ref[...] = jnp.zeros_like(tail_ref)
        h_ref[...] = jnp.zeros_like(h_ref)

    x = xr_ref[...]
    xe_ref[0:pad, :] = tail_ref[...]
    xe_ref[pad:pad + ts, :] = x
    tail_ref[...] = x[ts - pad:ts, :]
    xc = cb_ref[...] + jnp.zeros((ts, ch), F32)
    for j in range(CONV_WIDTH):
        off = pad - (CONV_WIDTH - 1) + j
        xc = xc + cw_ref[j:j + 1, :] * xe_ref[off:off + ts, :]

    xb = xc.astype(BF16)
    r = _sigmoid(jnp.dot(xb, wa_ref[...], preferred_element_type=F32) + ba_ref[...])
    i = _sigmoid(jnp.dot(xb, wx_ref[...], preferred_element_type=F32) + bx_ref[...])
    nl = -lam_ref[...]
    softplus = jnp.maximum(nl, 0.0) + jnp.log(1.0 + jnp.exp(-jnp.abs(nl)))
    log_a = (-RG_C) * r * softplus
    a = jnp.exp(log_a)
    mult = jnp.sqrt(1.0 - a * a)
    u = mult * (i * xc)

    ng = ts // SUBLANES
    a3 = a.reshape(ng, SUBLANES, ch)
    u3 = u.reshape(ng, SUBLANES, ch)
    rid = lax.broadcasted_iota(I32, (ng, SUBLANES, ch), 1)
    for sft in (1, 2, 4):
        a_s = pltpu.roll(a3, sft, 1)
        u_s = pltpu.roll(u3, sft, 1)
        keep = rid >= sft
        u3 = jnp.where(keep, a3 * u_s + u3, u3)
        a3 = jnp.where(keep, a3 * a_s, a3)
    a_ref[...] = a3.reshape(ts, ch)
    u_ref[...] = u3.reshape(ts, ch)

    inner = 8

    def body(gi, hprev):
        for jj in range(inner):
            base = pl.multiple_of((gi * inner + jj) * SUBLANES, SUBLANES)
            hcur = u_ref[pl.ds(base, SUBLANES), :] + a_ref[pl.ds(base, SUBLANES), :] * hprev
            u_ref[pl.ds(base, SUBLANES), :] = hcur
            hprev = hcur[SUBLANES - 1:SUBLANES, :]
        return hprev

    h_ref[...] = lax.fori_loop(0, ng // inner, body, h_ref[...])
    o_ref[...] = (u_ref[...] * _gelu_tanh(gr_ref[...])).astype(o_ref.dtype)


def _block_diag(w):
    nb, hin, hout = w.shape
    eye = jnp.eye(nb, dtype=w.dtype)
    return (eye[:, None, :, None] * w[:, :, None, :]).reshape(nb * hin, nb * hout)


def _rglru(xr, gr, conv_w, conv_b, w_a, b_a, w_x, b_x, lam):
    bsz, s, ch = xr.shape
    ts = RGLRU_ROWS
    blk = pl.BlockSpec((None, ts, ch), lambda b, t: (b, t, 0))
    full = lambda r, c: pl.BlockSpec((r, c), lambda b, t: (0, 0))
    return pl.pallas_call(
        _rglru_kernel,
        grid=(bsz, s // ts),
        in_specs=[blk, blk, full(CONV_WIDTH, ch), full(1, ch), full(ch, ch), full(1, ch),
                  full(ch, ch), full(1, ch), full(1, ch)],
        out_specs=blk,
        out_shape=jax.ShapeDtypeStruct((bsz, s, ch), BF16),
        scratch_shapes=[pltpu.VMEM((ts + SUBLANES, ch), F32),
                        pltpu.VMEM((ts, ch), F32),
                        pltpu.VMEM((ts, ch), F32),
                        pltpu.VMEM((SUBLANES, ch), F32),
                        pltpu.VMEM((1, ch), F32)],
        compiler_params=_params(("arbitrary", "arbitrary")),
        name="rglru",
    )(xr, gr, conv_w, conv_b.reshape(1, ch), _block_diag(w_a).astype(BF16), b_a.reshape(1, ch),
      _block_diag(w_x).astype(BF16), b_x.reshape(1, ch), lam.reshape(1, ch))


def _attn_kernel(q_ref, k_ref, v_ref, o_ref, f_ref, f4_ref, qs_ref, kp_ref, vp_ref, op_ref, lp_ref,
                 pc_ref, ml_ref):
    s = q_ref.shape[0]
    blk = ATTN_BLOCK
    n_blocks = s // blk
    unroll = ATTN_UNROLL

    lane = lax.broadcasted_iota(I32, (blk, LANES), 1)
    head0 = lane < HEAD_DIM
    head0_wide = lax.broadcasted_iota(I32, (blk, 2 * LANES), 1) % LANES < HEAD_DIM
    qi = lax.broadcasted_iota(I32, (2 * blk, blk), 0) % blk
    kj = lax.broadcasted_iota(I32, (2 * blk, blk), 1)
    prev_ok = kj >= qi
    cur_ok = kj <= qi
    dims = (((1,), (1,)), ((), ()))

    def block_rows(j, dil):
        per_res = (s // dil) // blk
        start = j // per_res + (j % per_res) * (dil * blk)
        if dil == 1:
            return pl.ds(pl.multiple_of(start, blk), blk)
        return pl.ds(start, blk, stride=dil)

    dils = tuple(d for _, d in DILATED_PATTERNS)
    assert dils == (1, 4, 16)

    def permute(src_ref, store):
        f_ref[...] = src_ref[...].astype(F32)

        def body0(j, c):
            store(0, j, f_ref[block_rows(j, 1), :])
            return c

        def body1(j, c):
            x = f_ref[block_rows(j, 4), :]
            f4_ref[pl.ds(pl.multiple_of(j * blk, blk), blk), :] = x
            store(1, j, x)
            return c

        def body2(j, c):
            per_res = (s // 16) // blk
            res, nblk = j // per_res, j % per_res
            start = (res % 4) * (s // 4) + res // 4 + nblk * (4 * blk)
            store(2, j, f4_ref[pl.ds(start, blk, stride=4), :])
            return c

        for body in (body0, body1, body2):
            for j in range(n_blocks):
                body(j, 0)

    def store_q(p, j, x):
        x = x * (HEAD_DIM ** -0.5 * LOG2_E)
        base = pl.multiple_of(j * (2 * blk), 2 * blk)
        qs_ref[p, pl.ds(base, blk), :] = jnp.where(head0, x, 0.0).astype(BF16)
        qs_ref[p, pl.ds(base + blk, blk), :] = jnp.where(head0, 0.0, x).astype(BF16)

    def store_kv(dst_ref):
        def store(p, j, x):
            dst_ref[p, pl.ds(pl.multiple_of((j + 1) * blk, blk), blk), 0:LANES] = x.astype(BF16)
        return store

    for p in range(len(DILATED_PATTERNS)):
        kp_ref[p, 0:blk, :] = jnp.zeros((blk, LANES), BF16)
        vp_ref[p, 0:blk, 0:LANES] = jnp.zeros((blk, LANES), BF16)
        vp_ref[p, :, LANES:] = jnp.ones((s + blk, LANES), BF16)
    permute(q_ref, store_q)
    permute(k_ref, store_kv(kp_ref))
    permute(v_ref, store_kv(vp_ref))

    def kv_rows(j):
        return pl.ds(pl.multiple_of(j * blk, blk), 2 * blk)

    def scores(p, g):
        per_res = (s // dils[p]) // blk
        for u in range(unroll):
            j = g * unroll + u
            no_prev = jnp.where(j % per_res > 0, 0.0, NEG_INF)
            qs = qs_ref[p, pl.ds(pl.multiple_of(j * (2 * blk), 2 * blk), 2 * blk), :]
            sc = lax.dot_general(qs, kp_ref[p, kv_rows(j), :], dims, preferred_element_type=F32)
            s_prev = jnp.where(prev_ok, sc[:, :blk], NEG_INF) + no_prev
            s_cur = jnp.where(cur_ok, sc[:, blk:], NEG_INF)
            m = jnp.max(jnp.maximum(s_prev, s_cur), axis=-1, keepdims=True)
            pc_ref[g % 2, u] = jnp.concatenate(
                [jnp.exp2(s_prev - m), jnp.exp2(s_cur - m)], axis=-1).astype(BF16)
            ml_ref[g % 2, u] = jnp.where(head0, m[:blk], m[blk:])

    def outputs(p, g):
        for u in range(unroll):
            j = g * unroll + u
            rl = jnp.dot(pc_ref[g % 2, u], vp_ref[p, kv_rows(j), :], preferred_element_type=F32)
            rl = jnp.where(head0_wide, rl[:blk], rl[blk:])
            l = rl[:, LANES:]
            rows = block_rows(j, dils[p])
            op_ref[p, rows, :] = rl[:, :LANES] / l
            lp_ref[p, rows, :] = ml_ref[g % 2, u] + jnp.log2(l)

    n_groups = n_blocks // unroll
    assert n_groups % 2 == 0
    n_pat = len(DILATED_PATTERNS)
    scores(0, 0)
    for p, (window, dil) in enumerate(DILATED_PATTERNS):
        assert window // dil == blk

        def body(g, carry, p=p):
            outputs(p, g - 1)
            scores(p, g)
            return carry

        lax.fori_loop(1, n_groups, body, 0)
        outputs(p, n_groups - 1)
        if p + 1 < n_pat:
            scores(p + 1, 0)

    chunk = 512
    for c0 in range(0, s, chunk):
        sl = pl.ds(c0, chunk)
        l0, l1, l2 = lp_ref[0, sl, :], lp_ref[1, sl, :], lp_ref[2, sl, :]
        m = jnp.maximum(jnp.maximum(l0, l1), l2)
        w0, w1, w2 = jnp.exp2(l0 - m), jnp.exp2(l1 - m), jnp.exp2(l2 - m)
        num = w0 * op_ref[0, sl, :] + w1 * op_ref[1, sl, :] + w2 * op_ref[2, sl, :]
        o_ref[sl, :] = (num / (w0 + w1 + w2)).astype(o_ref.dtype)


def _attention(q, k, v):
    bsz, s, da = q.shape
    blk = pl.BlockSpec((None, s, LANES), lambda b, h: (b, 0, h))
    n_pat = len(DILATED_PATTERNS)
    return pl.pallas_call(
        _attn_kernel,
        grid=(bsz, da // LANES),
        in_specs=[blk, blk, blk],
        out_specs=blk,
        out_shape=jax.ShapeDtypeStruct((bsz, s, da), BF16),
        scratch_shapes=[pltpu.VMEM((s, LANES), F32),
                        pltpu.VMEM((s, LANES), F32),
                        pltpu.VMEM((n_pat, 2 * s, LANES), BF16),
                        pltpu.VMEM((n_pat, s + ATTN_BLOCK, LANES), BF16),
                        pltpu.VMEM((n_pat, s + ATTN_BLOCK, 2 * LANES), BF16),
                        pltpu.VMEM((n_pat, s, LANES), F32),
                        pltpu.VMEM((n_pat, s, LANES), F32),
                        pltpu.VMEM((2, ATTN_UNROLL, 2 * ATTN_BLOCK, 2 * ATTN_BLOCK), BF16),
                        pltpu.VMEM((2, ATTN_UNROLL, ATTN_BLOCK, LANES), F32)],
        compiler_params=_params(("arbitrary", "arbitrary"), 56 * 1024 * 1024),
        name="attn",
    )(q, k, v)


def _outproj_kernel(attn_ref, rec_ref, w_ref, x_ref, gm_ref, g_ref, sh_ref, sc_ref, wr_ref, br_ref,
                    x1_ref, h2_ref, idx_ref, gate_ref, rank_ref, cnt_ref, base_ref):
    tm = x_ref.shape[0]
    da = attn_ref.shape[1]

    @pl.when(jnp.logical_and(pl.program_id(0) == 0, pl.program_id(1) == 0))
    def _():
        base_ref[...] = jnp.zeros_like(base_ref)

    y = (jnp.dot(attn_ref[...], w_ref[0:da, :], preferred_element_type=F32)
         + jnp.dot(rec_ref[...], w_ref[da:, :], preferred_element_type=F32))
    x1 = x_ref[...] + gm_ref[...] * y
    x1_ref[...] = x1
    h2 = _rms_modulate(x1, g_ref[...], sh_ref[...], sc_ref[...])
    for c in range(ROW_CHUNKS):
        h2_ref[pl.ds(c, tm, stride=ROW_CHUNKS), :] = h2[:, c * LANES:(c + 1) * LANES]

    def logits(w, h):
        return lax.dot_general(w, h, (((1,), (1,)), ((), ())), preferred_element_type=F32)

    w_hi, w_lo = _halves(wr_ref[...])
    h_hi, h_lo = _halves(h2)
    work = logits(w_hi, h_hi) + (logits(w_hi, h_lo) + logits(w_lo, h_hi)) + br_ref[...]
    n_exp = work.shape[0]
    eid = lax.broadcasted_iota(I32, (n_exp, tm), 0)
    vals, hots = [], []
    for kk in range(TOP_K):
        m = jnp.max(work, axis=0, keepdims=True)
        sel = jnp.min(jnp.where(work == m, eid, n_exp), axis=0, keepdims=True)
        hot = eid == sel
        idx_ref[kk:kk + 1, :] = sel
        vals.append(m)
        hots.append(hot)
        work = jnp.where(hot, -jnp.inf, work)
    exps = [jnp.exp(v - vals[0]) for v in vals]
    den = exps[0] + exps[1] + exps[2] + exps[3]
    for kk in range(TOP_K):
        gate_ref[kk:kk + 1, :] = exps[kk] / den

    cnt = jnp.zeros((n_exp, tm), F32)
    for hot in hots:
        cnt = cnt + hot.astype(F32)
    tr = lax.broadcasted_iota(I32, (tm, tm), 0)
    tc = lax.broadcasted_iota(I32, (tm, tm), 1)
    before = jnp.where(tr < tc, 1.0, 0.0).astype(BF16)
    prior = jnp.dot(cnt.astype(BF16), before, preferred_element_type=F32) + base_ref[:, 0:1]
    for kk in range(TOP_K):
        rank_ref[kk:kk + 1, :] = jnp.sum(jnp.where(hots[kk], prior, 0.0), axis=0,
                                         keepdims=True).astype(I32)
    base_ref[...] = base_ref[...] + jnp.sum(cnt, axis=1, keepdims=True)
    cnt_ref[...] = base_ref[...].astype(I32)


def _outproj(attn, rec, w_bf, x, gate_m, g_ffn, shift_f, scale_f, w_router, b_router):
    bsz, s, d = x.shape
    da = attn.shape[-1]
    n_exp = w_router.shape[1]
    tm = OUTPROJ_ROWS
    nt = s // tm
    t = bsz * s
    row = pl.BlockSpec((None, 1, d), lambda b, i: (b, 0, 0))
    small = pl.BlockSpec((TOP_K, tm), lambda b, i: (0, b * nt + i))
    return pl.pallas_call(
        _outproj_kernel,
        grid=(bsz, nt),
        in_specs=[pl.BlockSpec((None, tm, da), lambda b, i: (b, i, 0)),
                  pl.BlockSpec((None, tm, da), lambda b, i: (b, i, 0)),
                  pl.BlockSpec((d, d), lambda b, i: (0, 0)),
                  pl.BlockSpec((None, tm, d), lambda b, i: (b, i, 0)),
                  row,
                  pl.BlockSpec((1, d), lambda b, i: (0, 0)),
                  row, row,
                  pl.BlockSpec((n_exp, d), lambda b, i: (0, 0)),
                  pl.BlockSpec((n_exp, 1), lambda b, i: (0, 0))],
        out_specs=[pl.BlockSpec((None, tm, d), lambda b, i: (b, i, 0)),
                   pl.BlockSpec((tm * ROW_CHUNKS, LANES), lambda b, i: (b * nt + i, 0)),
                   small, small, small,
                   pl.BlockSpec((n_exp, LANES), lambda b, i: (0, 0))],
        out_shape=[jax.ShapeDtypeStruct((bsz, s, d), F32),
                   jax.ShapeDtypeStruct((t * ROW_CHUNKS, LANES), F32),
                   jax.ShapeDtypeStruct((TOP_K, t), I32),
                   jax.ShapeDtypeStruct((TOP_K, t), F32),
                   jax.ShapeDtypeStruct((TOP_K, t), I32),
                   jax.ShapeDtypeStruct((n_exp, LANES), I32)],
        scratch_shapes=[pltpu.VMEM((n_exp, LANES), F32)],
        compiler_params=_params(("arbitrary", "arbitrary")),
        name="outproj_router",
    )(attn, rec, w_bf, x, gate_m, g_ffn.reshape(1, d), shift_f, scale_f, w_router.T,
      b_router.reshape(n_exp, 1))


def _row_slice(ref, row):
    return ref.at[pl.ds(pl.multiple_of(row * ROW_CHUNKS, ROW_CHUNKS), ROW_CHUNKS), :]


def _slots_kernel(dest_ref, init_ref, slot_ref, sem):
    n_assign = dest_ref.shape[0]
    unroll = 32

    init = pltpu.make_async_copy(init_ref, slot_ref, sem)
    init.start()
    init.wait()

    def assign_body(i, c):
        for u in range(unroll):
            slot_ref[dest_ref[i * unroll + u]] = i * unroll + u
        return c

    lax.fori_loop(0, n_assign // unroll, assign_body, 0)


def _slots(dest_flat, n_slots):
    smem = pl.BlockSpec(memory_space=pltpu.SMEM)
    return pl.pallas_call(
        _slots_kernel,
        in_specs=[smem, pl.BlockSpec(memory_space=pltpu.VMEM)],
        out_specs=smem,
        out_shape=jax.ShapeDtypeStruct((n_slots,), I32),
        scratch_shapes=[pltpu.SemaphoreType.DMA(())],
        name="slots",
    )(dest_flat, jnp.arange(n_slots, dtype=I32))


GATHER_BUFS = 3


def _experts_kernel(bexp_ref, bvalid_ref, nused_ref, slot_ref, h2_ref, wgu_ref, bgu_ref, wd_ref, bd_ref,
                    y_ref, xg_ref, xs_ref, wgu_bf, wd_bf, gsem):
    b = pl.program_id(0)
    nb = pl.num_programs(0)
    rows = xs_ref.shape[0]
    f = wd_ref.shape[0]
    n_tok = h2_ref.shape[0] // ROW_CHUNKS
    assert n_tok & (n_tok - 1) == 0
    used = b < nused_ref[0]
    changed = jnp.logical_or(b == 0, bexp_ref[b] != bexp_ref[jnp.maximum(b - 1, 0)])

    def buf_rows(ref, buf, i):
        if isinstance(i, int):
            return ref.at[buf, pl.ds(i * ROW_CHUNKS, ROW_CHUNKS), :]
        return ref.at[buf, pl.ds(pl.multiple_of(i * ROW_CHUNKS, ROW_CHUNKS), ROW_CHUNKS), :]

    def fetch_row(base, buf, i):
        tok = slot_ref[base + i] & (n_tok - 1)
        return pltpu.make_async_copy(_row_slice(h2_ref, tok), buf_rows(xg_ref, buf, i), gsem.at[buf])

    def fetch_block(blk, buf, rolled=False):
        base = blk * rows
        if rolled:
            def body(i, c):
                fetch_row(base, buf, i).start()
                return c
            lax.fori_loop(0, rows, body, 0)
        else:
            for i in range(rows):
                fetch_row(base, buf, i).start(priority=i % 2)

    def wait_block(buf):
        pltpu.make_async_copy(xg_ref.at[buf], xg_ref.at[buf], gsem.at[buf]).wait()

    @pl.when(b == 0)
    def _():
        fetch_block(0, 0, rolled=True)
        fetch_block(1, 1, rolled=True)

    gbuf = b % GATHER_BUFS
    n_used = nused_ref[0]

    @pl.when(b < n_used + 2)
    def _():
        wait_block(gbuf)

    def fetch_ahead():
        fetch_block(jnp.minimum(b + 2, nb - 1), (b + 2) % GATHER_BUFS)

    @pl.when(jnp.logical_and(used, changed))
    def _():
        wgu_bf[...] = wgu_ref[...].astype(BF16)
        wd_bf[...] = wd_ref[...].astype(BF16)

    def run_expert(m):
        for c in range(ROW_CHUNKS):
            xs_ref[0:m, c * LANES:(c + 1) * LANES] = (
                xg_ref[gbuf, pl.ds(c, m, stride=ROW_CHUNKS), :].astype(BF16))
        fetch_ahead()
        gu = jnp.dot(xs_ref[0:m, :], wgu_bf[...], preferred_element_type=F32) + bgu_ref[...]
        gate = jnp.minimum(gu[:, :f], SWIGLU_LIMIT)
        up = jnp.clip(gu[:, f:], -SWIGLU_LIMIT, SWIGLU_LIMIT)
        glu = gate * jax.nn.sigmoid(gate * SWIGLU_ALPHA)
        act = ((up + 1.0) * glu).astype(BF16)
        y = jnp.dot(act, wd_bf[...], preferred_element_type=F32) + bd_ref[...]
        for c in range(ROW_CHUNKS):
            y_ref[pl.ds(c, m, stride=ROW_CHUNKS), :] = y[:, c * LANES:(c + 1) * LANES]
        if m < rows:
            y_ref[m * ROW_CHUNKS:, :] = jnp.zeros(((rows - m) * ROW_CHUNKS, LANES), F32)

    few = bvalid_ref[b] <= rows // 2

    @pl.when(jnp.logical_and(used, jnp.logical_not(few)))
    def _():
        run_expert(rows)

    @pl.when(jnp.logical_and(used, few))
    def _():
        run_expert(rows // 2)

    @pl.when(jnp.logical_not(used))
    def _():
        y_ref[...] = jnp.zeros_like(y_ref)

    for late in (1, 2):
        @pl.when(jnp.logical_and(b == nb - 1, n_used > nb - 3 + late))
        def _(late=late):
            wait_block((b + late) % GATHER_BUFS)


def _experts(h2rows, slots, blk_exp, blk_valid, n_used, w_gate_up, b_gate_up, w_down, b_down):
    n_exp, d, f2 = w_gate_up.shape
    f = w_down.shape[1]
    rows = EXPERT_ROWS
    n_blocks = slots.shape[0] // rows
    assert n_blocks >= GATHER_BUFS
    buf_shape = (rows * ROW_CHUNKS, LANES)
    return pl.pallas_call(
        _experts_kernel,
        grid_spec=pltpu.PrefetchScalarGridSpec(
            num_scalar_prefetch=4,
            grid=(n_blocks,),
            in_specs=[pl.BlockSpec(memory_space=pl.ANY),
                      pl.BlockSpec((None, d, f2), lambda b, e, *_: (e[b], 0, 0)),
                      pl.BlockSpec((None, 1, f2), lambda b, e, *_: (e[b], 0, 0)),
                      pl.BlockSpec((None, f, d), lambda b, e, *_: (e[b], 0, 0)),
                      pl.BlockSpec((None, 1, d), lambda b, e, *_: (e[b], 0, 0))],
            out_specs=pl.BlockSpec(buf_shape, lambda b, e, *_: (b, 0)),
            scratch_shapes=[pltpu.VMEM((GATHER_BUFS,) + buf_shape, F32),
                            pltpu.VMEM((rows, d), BF16),
                            pltpu.VMEM((d, f2), BF16),
                            pltpu.VMEM((f, d), BF16),
                            pltpu.SemaphoreType.DMA((GATHER_BUFS,))]),
        out_shape=jax.ShapeDtypeStruct((slots.shape[0] * ROW_CHUNKS, LANES), F32),
        compiler_params=_params(("arbitrary",), 56 * 1024 * 1024),
        name="experts",
    )(blk_exp, blk_valid, n_used, slots, h2rows, w_gate_up, b_gate_up.reshape(n_exp, 1, f2), w_down,
      b_down.reshape(n_exp, 1, d))


def _combine_kernel(src_ref, cnt_ref, dst_ref, pos_ref, gate_ref, y_ref, x1_ref, gf_ref, g_ref, o_ref,
                    stage_ref, tok_ref, sem):
    i = pl.program_id(0)
    nt = pl.num_programs(0)
    tm = x1_ref.shape[0]
    n_exp = cnt_ref.shape[0] // nt

    def fetch(tile, buf):
        def per_expert(e, c):
            seg = tile * n_exp + e
            cnt, src, dst = cnt_ref[seg], src_ref[seg], dst_ref[seg]

            def piece(done, size):
                return pltpu.make_async_copy(
                    y_ref.at[pl.ds(pl.multiple_of((src + done) * ROW_CHUNKS, ROW_CHUNKS),
                                   size * ROW_CHUNKS), :],
                    stage_ref.at[buf, pl.ds(pl.multiple_of((dst + done) * ROW_CHUNKS, ROW_CHUNKS),
                                            size * ROW_CHUNKS), :],
                    sem.at[buf])

            def whole(j, c2):
                piece(j * COMBINE_PIECE, COMBINE_PIECE).start()
                return c2

            n_whole = cnt // COMBINE_PIECE
            lax.fori_loop(0, n_whole, whole, 0)
            done = n_whole * COMBINE_PIECE
            size = COMBINE_PIECE // 2
            while size >= 1:
                @pl.when((cnt & size) != 0)
                def _(size=size, done=done):
                    piece(done, size).start()

                done = done + (cnt & size)
                size //= 2
            return c
        lax.fori_loop(0, n_exp, per_expert, 0)

    @pl.when(i == 0)
    def _():
        fetch(0, 0)

    @pl.when(i + 1 < nt)
    def _():
        fetch(i + 1, (i + 1) % 2)

    buf = i % 2
    pltpu.make_async_copy(stage_ref.at[buf], stage_ref.at[buf], sem.at[buf]).wait()

    for t in range(tm):
        acc = None
        for kk in range(TOP_K):
            row = stage_ref[buf, pl.ds(pl.multiple_of(pos_ref[0, kk * tm + t], ROW_CHUNKS), ROW_CHUNKS), :]
            term = row * gate_ref[0, kk * tm + t]
            acc = term if acc is None else acc + term
        tok_ref[t * ROW_CHUNKS:(t + 1) * ROW_CHUNKS, :] = acc

    moe = jnp.concatenate([tok_ref[pl.ds(c, tm, stride=ROW_CHUNKS), :] for c in range(ROW_CHUNKS)],
                          axis=-1)
    x2 = x1_ref[...] + gf_ref[...] * moe
    o_ref[...] = (x2 * lax.rsqrt(jnp.mean(x2 * x2, axis=-1, keepdims=True) + NORM_EPS)) * g_ref[...]


def _combine(y, plan, x1, gate_f, g_final):
    bsz, s, d = x1.shape
    tm = COMBINE_ROWS
    nt = bsz * s // tm
    per_b = s // tm
    seg_src, seg_cnt, seg_dst, pos3, gates3 = plan
    per_tile = pl.BlockSpec((None, 1, TOP_K * tm), lambda i, *_: (i, 0, 0), memory_space=pltpu.SMEM)
    return pl.pallas_call(
        _combine_kernel,
        grid_spec=pltpu.PrefetchScalarGridSpec(
            num_scalar_prefetch=3,
            grid=(nt,),
            in_specs=[per_tile, per_tile,
                      pl.BlockSpec(memory_space=pl.ANY),
                      pl.BlockSpec((None, tm, d), lambda i, *_: (i // per_b, i % per_b, 0)),
                      pl.BlockSpec((None, 1, d), lambda i, *_: (i // per_b, 0, 0)),
                      pl.BlockSpec((1, d), lambda i, *_: (0, 0))],
            out_specs=pl.BlockSpec((None, tm, d), lambda i, *_: (i // per_b, i % per_b, 0)),
            scratch_shapes=[pltpu.VMEM((2, TOP_K * tm * ROW_CHUNKS, LANES), F32),
                            pltpu.VMEM((tm * ROW_CHUNKS, LANES), F32),
                            pltpu.SemaphoreType.DMA((2,))]),
        out_shape=jax.ShapeDtypeStruct((bsz, s, d), F32),
        compiler_params=_params(("arbitrary",)),
        name="combine",
    )(seg_src, seg_cnt, seg_dst, pos3, gates3, y, x1, gate_f, g_final.reshape(1, d))


def _combine_plan(idx, rank, gates, pstart):
    tm = COMBINE_ROWS
    t = idx.shape[1]
    nt = t // tm
    experts = jnp.arange(N_EXPERTS, dtype=I32)
    hot = idx.reshape(TOP_K, nt, tm)[..., None] == experts
    tile_cnt = jnp.sum(hot.astype(I32), axis=(0, 2))
    tile_base = jnp.cumsum(tile_cnt, axis=0) - tile_cnt
    stage_at = jnp.cumsum(tile_cnt, axis=1) - tile_cnt
    pos = rank.reshape(TOP_K, nt, tm) + jnp.sum(
        jnp.where(hot, (stage_at - tile_base)[None, :, None, :], 0), axis=-1)
    flat = lambda a: a.reshape(-1).astype(I32)
    return (flat(pstart[None, :] + tile_base), flat(tile_cnt), flat(stage_at),
            (pos * ROW_CHUNKS).transpose(1, 0, 2).reshape(nt, 1, TOP_K * tm).astype(I32),
            gates.reshape(TOP_K, nt, tm).transpose(1, 0, 2).reshape(nt, 1, TOP_K * tm))


def _routing_plan(idx, rank, counts, n_rows):
    rows = EXPERT_ROWS
    experts = jnp.arange(N_EXPERTS, dtype=I32)

    def lookup(table, e):
        return jnp.sum(jnp.where(e[..., None] == experts, table, 0), axis=-1)

    def segment_of(ends, pos):
        return jnp.minimum(jnp.sum((pos[..., None] >= ends).astype(I32), axis=-1), N_EXPERTS - 1)

    padded = ((counts + rows - 1) // rows) * rows
    pend = jnp.cumsum(padded)
    pstart = pend - padded
    dest = lookup(pstart, idx) + rank
    n_blocks = n_rows // rows
    blk_exp = segment_of(pend, jnp.arange(n_blocks, dtype=I32) * rows)
    n_used = (pend[-1] // rows).astype(I32).reshape(1)
    blk_first = jnp.arange(n_blocks, dtype=I32) * rows - lookup(pstart, blk_exp)
    blk_valid = jnp.clip(lookup(counts, blk_exp) - blk_first, 0, rows)
    return dest.astype(I32), blk_exp.astype(I32), blk_valid.astype(I32), n_used, pstart


def kernel(x, c, w_ada, b_ada, g_mix, w_in, conv_w, conv_b, w_rg_a, b_rg_a, w_rg_x, b_rg_x, lam, w_out,
           g_ffn, w_router, b_router, w_gate_up, b_gate_up, w_down, b_down, g_final):
    bsz, s, d = x.shape
    t = bsz * s
    depth = w_ada.shape[0]
    assert depth == 1, "the combine kernel applies the final norm; one layer only"
    for l in range(depth):
        mod = _ada(c, w_ada[l], b_ada[l]).reshape(bsz, 6, 1, d)
        shift_m, scale_m, gate_m, shift_f, scale_f, gate_f = (mod[:, j] for j in range(6))
        q, k, v, xr, gr = _inproj(x, g_mix[l], shift_m, scale_m, w_in[l].astype(BF16))
        attn = _attention(q, k, v)
        rec = _rglru(xr, gr, conv_w[l], conv_b[l], w_rg_a[l], b_rg_a[l], w_rg_x[l], b_rg_x[l], lam[l])
        x1, h2rows, idx, gates, rank, cnt = _outproj(
            attn, rec, w_out[l].astype(BF16), x, gate_m, g_ffn[l], shift_f, scale_f, w_router[l], b_router[l])
        n_rows = t * TOP_K + N_EXPERTS * EXPERT_ROWS
        dest, blk_exp, blk_valid, n_used, pstart = _routing_plan(idx, rank, cnt[:, 0], n_rows)
        slots = _slots(dest.reshape(-1), n_rows)
        y = _experts(h2rows, slots, blk_exp, blk_valid, n_used, w_gate_up[l], b_gate_up[l], w_down[l],
                     b_down[l])
        x = _combine(y, _combine_plan(idx, rank, gates, pstart), x1, gate_f, g_final)
    return x
```

```python
import jax
import jax.numpy as jnp
from jax import lax
from jax.experimental import pallas as pl
from jax.experimental.pallas import tpu as pltpu

F32 = jnp.float32
BF16 = jnp.bfloat16
I32 = jnp.int32

HEAD_DIM = 64
DILATED_PATTERNS = ((128, 1), (512, 4), (2048, 16))
REC_BLOCKS = 8
CONV_WIDTH = 4
RG_C = 8.0
N_EXPERTS = 32
TOP_K = 4
SWIGLU_LIMIT = 7.0
SWIGLU_ALPHA = 1.702
NORM_EPS = 1e-6
NEG_INF = -1e30
LOG2_E = 1.4426950408889634

LANES = 128
SUBLANES = 8
ROW_CHUNKS = 8
VMEM_LIMIT = 48 * 1024 * 1024

INPROJ_ROWS = 1024
RGLRU_ROWS = 1024
OUTPROJ_ROWS = 1024
ATTN_BLOCK = 128
ATTN_UNROLL = 8
EXPERT_ROWS = 512
COMBINE_ROWS = 512
COMBINE_PIECE = 64


def _params(sem, vmem=VMEM_LIMIT):
    return pltpu.CompilerParams(dimension_semantics=sem, vmem_limit_bytes=vmem)


def _halves(a):
    hi = a.astype(BF16)
    return hi, (a - hi.astype(F32)).astype(BF16)


def _ada_kernel(c_ref, w_ref, b_ref, o_ref):
    c_hi, c_lo = _halves(c_ref[...])
    w_hi, w_lo = _halves(w_ref[...])
    dot = lambda a, b: jnp.dot(a, b, preferred_element_type=F32)
    o_ref[...] = dot(c_hi, w_hi) + (dot(c_lo, w_hi) + dot(c_hi, w_lo)) + b_ref[...]


def _ada(c, w, b):
    bsz, d = c.shape
    n = w.shape[1]
    return pl.pallas_call(
        _ada_kernel,
        grid=(n // d,),
        in_specs=[pl.BlockSpec((bsz, d), lambda j: (0, 0)),
                  pl.BlockSpec((d, d), lambda j: (0, j)),
                  pl.BlockSpec((1, d), lambda j: (0, j))],
        out_specs=pl.BlockSpec((bsz, d), lambda j: (0, j)),
        out_shape=jax.ShapeDtypeStruct((bsz, n), F32),
        compiler_params=_params(("arbitrary",)),
        name="ada",
    )(c, w, b.reshape(1, n))


def _rms_modulate(x, g, shift, scale):
    y = x * lax.rsqrt(jnp.mean(x * x, axis=-1, keepdims=True) + NORM_EPS)
    return (y * g) * (1.0 + scale) + shift


def _inproj_kernel(x_ref, g_ref, sh_ref, sc_ref, w_ref, q_ref, k_ref, v_ref, xr_ref, gr_ref):
    h = _rms_modulate(x_ref[...], g_ref[...], sh_ref[...], sc_ref[...]).astype(BF16)
    outs = (q_ref, k_ref, v_ref, xr_ref, gr_ref)
    width = q_ref.shape[-1]
    for j, o_ref in enumerate(outs):
        z = jnp.dot(h, w_ref[:, j * width:(j + 1) * width], preferred_element_type=F32)
        o_ref[...] = z.astype(o_ref.dtype)


def _inproj(x, g, shift, scale, w_bf):
    bsz, s, d = x.shape
    n = w_bf.shape[1]
    width = n // 5
    tm = INPROJ_ROWS
    row = pl.BlockSpec((None, 1, d), lambda b, i: (b, 0, 0))
    out_blk = pl.BlockSpec((None, tm, width), lambda b, i: (b, i, 0))
    shp = lambda dt: jax.ShapeDtypeStruct((bsz, s, width), dt)
    return pl.pallas_call(
        _inproj_kernel,
        grid=(bsz, s // tm),
        in_specs=[pl.BlockSpec((None, tm, d), lambda b, i: (b, i, 0)),
                  pl.BlockSpec((1, d), lambda b, i: (0, 0)),
                  row, row,
                  pl.BlockSpec((d, n), lambda b, i: (0, 0))],
        out_specs=[out_blk] * 5,
        out_shape=[shp(BF16), shp(BF16), shp(BF16), shp(F32), shp(F32)],
        compiler_params=_params(("arbitrary", "arbitrary")),
        name="inproj",
    )(x, g.reshape(1, d), shift, scale, w_bf)


def _sigmoid(x):
    return 0.5 * jnp.tanh(0.5 * x) + 0.5


def _gelu_tanh(x):
    return 0.5 * x * (1.0 + jnp.tanh(0.7978845608028654 * (x + 0.044715 * (x * x * x))))


def _rglru_kernel(xr_ref, gr_ref, cw_ref, cb_ref, wa_ref, ba_ref, wx_ref, bx_ref, lam_ref,
                  o_ref, xe_ref, a_ref, u_ref, tail_ref, h_ref):
    ts, ch = xr_ref.shape
    pad = SUBLANES

    @pl.when(pl.program_id(1) == 0)
    def _():
        tail_ref[...] = jnp.zeros_like(tail_ref)
        h_ref[...] = jnp.zeros_like(h_ref)

    x = xr_ref[...]
    xe_ref[0:pad, :] = tail_ref[...]
    xe_ref[pad:pad + ts, :] = x
    tail_ref[...] = x[ts - pad:ts, :]
    xc = cb_ref[...] + jnp.zeros((ts, ch), F32)
    for j in range(CONV_WIDTH):
        off = pad - (CONV_WIDTH - 1) + j
        xc = xc + cw_ref[j:j + 1, :] * xe_ref[off:off + ts, :]

    xb = xc.astype(BF16)
    r = _sigmoid(jnp.dot(xb, wa_ref[...], preferred_element_type=F32) + ba_ref[...])
    i = _sigmoid(jnp.dot(xb, wx_ref[...], preferred_element_type=F32) + bx_ref[...])
    nl = -lam_ref[...]
    softplus = jnp.maximum(nl, 0.0) + jnp.log(1.0 + jnp.exp(-jnp.abs(nl)))
    log_a = (-RG_C) * r * softplus
    a = jnp.exp(log_a)
    mult = jnp.sqrt(1.0 - a * a)
    u = mult * (i * xc)

    ng = ts // SUBLANES
    a3 = a.reshape(ng, SUBLANES, ch)
    u3 = u.reshape(ng, SUBLANES, ch)
    rid = lax.broadcasted_iota(I32, (ng, SUBLANES, ch), 1)
    for sft in (1, 2, 4):
        a_s = pltpu.roll(a3, sft, 1)
        u_s = pltpu.roll(u3, sft, 1)
        keep = rid >= sft
        u3 = jnp.where(keep, a3 * u_s + u3, u3)
        a3 = jnp.where(keep, a3 * a_s, a3)
    a_ref[...] = a3.reshape(ts, ch)
    u_ref[...] = u3.reshape(ts, ch)

    inner = 8

    def body(gi, hprev):
        for jj in range(inner):
            base = pl.multiple_of((gi * inner + jj) * SUBLANES, SUBLANES)
            hcur = u_ref[pl.ds(base, SUBLANES), :] + a_ref[pl.ds(base, SUBLANES), :] * hprev
            u_ref[pl.ds(base, SUBLANES), :] = hcur
            hprev = hcur[SUBLANES - 1:SUBLANES, :]
        return hprev

    h_ref[...] = lax.fori_loop(0, ng // inner, body, h_ref[...])
    o_ref[...] = (u_ref[...] * _gelu_tanh(gr_ref[...])).astype(o_ref.dtype)


def _block_diag(w):
    nb, hin, hout = w.shape
    eye = jnp.eye(nb, dtype=w.dtype)
    return (eye[:, None, :, None] * w[:, :, None, :]).reshape(nb * hin, nb * hout)


def _rglru(xr, gr, conv_w, conv_b, w_a, b_a, w_x, b_x, lam):
    bsz, s, ch = xr.shape
    ts = RGLRU_ROWS
    blk = pl.BlockSpec((None, ts, ch), lambda b, t: (b, t, 0))
    full = lambda r, c: pl.BlockSpec((r, c), lambda b, t: (0, 0))
    return pl.pallas_call(
        _rglru_kernel,
        grid=(bsz, s // ts),
        in_specs=[blk, blk, full(CONV_WIDTH, ch), full(1, ch), full(ch, ch), full(1, ch),
                  full(ch, ch), full(1, ch), full(1, ch)],
        out_specs=blk,
        out_shape=jax.ShapeDtypeStruct((bsz, s, ch), BF16),
        scratch_shapes=[pltpu.VMEM((ts + SUBLANES, ch), F32),
                        pltpu.VMEM((ts, ch), F32),
                        pltpu.VMEM((ts, ch), F32),
                        pltpu.VMEM((SUBLANES, ch), F32),
                        pltpu.VMEM((1, ch), F32)],
        compiler_params=_params(("arbitrary", "arbitrary")),
        name="rglru",
    )(xr, gr, conv_w, conv_b.reshape(1, ch), _block_diag(w_a).astype(BF16), b_a.reshape(1, ch),
      _block_diag(w_x).astype(BF16), b_x.reshape(1, ch), lam.reshape(1, ch))


def _attn_kernel(q_ref, k_ref, v_ref, o_ref, f_ref, f4_ref, qs_ref, kp_ref, vp_ref, op_ref, lp_ref,
                 pc_ref, ml_ref):
    s = q_ref.shape[0]
    blk = ATTN_BLOCK
    n_blocks = s // blk
    unroll = ATTN_UNROLL

    lane = lax.broadcasted_iota(I32, (blk, LANES), 1)
    head0 = lane < HEAD_DIM
    head0_wide = lax.broadcasted_iota(I32, (blk, 2 * LANES), 1) % LANES < HEAD_DIM
    qi = lax.broadcasted_iota(I32, (2 * blk, blk), 0) % blk
    kj = lax.broadcasted_iota(I32, (2 * blk, blk), 1)
    prev_ok = kj >= qi
    cur_ok = kj <= qi
    dims = (((1,), (1,)), ((), ()))

    def block_rows(j, dil):
        per_res = (s // dil) // blk
        start = j // per_res + (j % per_res) * (dil * blk)
        if dil == 1:
            return pl.ds(pl.multiple_of(start, blk), blk)
        return pl.ds(start, blk, stride=dil)

    dils = tuple(d for _, d in DILATED_PATTERNS)
    assert dils == (1, 4, 16)

    def permute(src_ref, store):
        f_ref[...] = src_ref[...].astype(F32)

        def body0(j, c):
            store(0, j, f_ref[block_rows(j, 1), :])
            return c

        def body1(j, c):
            x = f_ref[block_rows(j, 4), :]
            f4_ref[pl.ds(pl.multiple_of(j * blk, blk), blk), :] = x
            store(1, j, x)
            return c

        def body2(j, c):
            per_res = (s // 16) // blk
            res, nblk = j // per_res, j % per_res
            start = (res % 4) * (s // 4) + res // 4 + nblk * (4 * blk)
            store(2, j, f4_ref[pl.ds(start, blk, stride=4), :])
            return c

        for body in (body0, body1, body2):
            for j in range(n_blocks):
                body(j, 0)

    def store_q(p, j, x):
        x = x * (HEAD_DIM ** -0.5 * LOG2_E)
        base = pl.multiple_of(j * (2 * blk), 2 * blk)
        qs_ref[p, pl.ds(base, blk), :] = jnp.where(head0, x, 0.0).astype(BF16)
        qs_ref[p, pl.ds(base + blk, blk), :] = jnp.where(head0, 0.0, x).astype(BF16)

    def store_kv(dst_ref):
        def store(p, j, x):
            dst_ref[p, pl.ds(pl.multiple_of((j + 1) * blk, blk), blk), 0:LANES] = x.astype(BF16)
        return store

    for p in range(len(DILATED_PATTERNS)):
        kp_ref[p, 0:blk, :] = jnp.zeros((blk, LANES), BF16)
        vp_ref[p, 0:blk, 0:LANES] = jnp.zeros((blk, LANES), BF16)
        vp_ref[p, :, LANES:] = jnp.ones((s + blk, LANES), BF16)
    permute(q_ref, store_q)
    permute(k_ref, store_kv(kp_ref))
    permute(v_ref, store_kv(vp_ref))

    def kv_rows(j):
        return pl.ds(pl.multiple_of(j * blk, blk), 2 * blk)

    def scores(p, g):
        per_res = (s // dils[p]) // blk
        for u in range(unroll):
            j = g * unroll + u
            no_prev = jnp.where(j % per_res > 0, 0.0, NEG_INF)
            qs = qs_ref[p, pl.ds(pl.multiple_of(j * (2 * blk), 2 * blk), 2 * blk), :]
            sc = lax.dot_general(qs, kp_ref[p, kv_rows(j), :], dims, preferred_element_type=F32)
            s_prev = jnp.where(prev_ok, sc[:, :blk], NEG_INF) + no_prev
            s_cur = jnp.where(cur_ok, sc[:, blk:], NEG_INF)
            m = jnp.max(jnp.maximum(s_prev, s_cur), axis=-1, keepdims=True)
            pc_ref[g % 2, u] = jnp.concatenate(
                [jnp.exp2(s_prev - m), jnp.exp2(s_cur - m)], axis=-1).astype(BF16)
            ml_ref[g % 2, u] = jnp.where(head0, m[:blk], m[blk:])

    def outputs(p, g):
        for u in range(unroll):
            j = g * unroll + u
            rl = jnp.dot(pc_ref[g % 2, u], vp_ref[p, kv_rows(j), :], preferred_element_type=F32)
            rl = jnp.where(head0_wide, rl[:blk], rl[blk:])
            l = rl[:, LANES:]
            if dils[p] == 16:
                per_res = (s // 16) // blk
                res, nblk = j // per_res, j % per_res
                rows = pl.ds((res % 4) * (s // 4) + res // 4 + nblk * (4 * blk), blk, stride=4)
            else:
                rows = block_rows(j, dils[p])
            op_ref[p, rows, :] = rl[:, :LANES] / l
            lp_ref[p, rows, :] = ml_ref[g % 2, u] + jnp.log2(l)

    n_groups = n_blocks // unroll
    assert n_groups % 2 == 0
    n_pat = len(DILATED_PATTERNS)
    scores(0, 0)
    for p, (window, dil) in enumerate(DILATED_PATTERNS):
        assert window // dil == blk

        def body(g, carry, p=p):
            outputs(p, g - 1)
            scores(p, g)
            return carry

        lax.fori_loop(1, n_groups, body, 0)
        outputs(p, n_groups - 1)
        if p + 1 < n_pat:
            scores(p + 1, 0)

    quarter, chunk = s // 4, 256
    for r4 in range(4):
        for l0 in range(0, quarter, chunk):
            tok = pl.ds(r4 + 4 * l0, chunk, stride=4)
            lay = pl.ds(r4 * quarter + l0, chunk)
            l0_, l1_, l2_ = lp_ref[0, tok, :], lp_ref[1, tok, :], lp_ref[2, lay, :]
            m = jnp.maximum(jnp.maximum(l0_, l1_), l2_)
            w0, w1, w2 = jnp.exp2(l0_ - m), jnp.exp2(l1_ - m), jnp.exp2(l2_ - m)
            num = w0 * op_ref[0, tok, :] + w1 * op_ref[1, tok, :] + w2 * op_ref[2, lay, :]
            f_ref[tok, :] = num / (w0 + w1 + w2)
    for c0 in range(0, s, 512):
        o_ref[c0:c0 + 512, :] = f_ref[c0:c0 + 512, :].astype(o_ref.dtype)


def _attention(q, k, v):
    bsz, s, da = q.shape
    blk = pl.BlockSpec((None, s, LANES), lambda b, h: (b, 0, h))
    n_pat = len(DILATED_PATTERNS)
    return pl.pallas_call(
        _attn_kernel,
        grid=(bsz, da // LANES),
        in_specs=[blk, blk, blk],
        out_specs=blk,
        out_shape=jax.ShapeDtypeStruct((bsz, s, da), BF16),
        scratch_shapes=[pltpu.VMEM((s, LANES), F32),
                        pltpu.VMEM((s, LANES), F32),
                        pltpu.VMEM((n_pat, 2 * s, LANES), BF16),
                        pltpu.VMEM((n_pat, s + ATTN_BLOCK, LANES), BF16),
                        pltpu.VMEM((n_pat, s + ATTN_BLOCK, 2 * LANES), BF16),
                        pltpu.VMEM((n_pat, s, LANES), F32),
                        pltpu.VMEM((n_pat, s, LANES), F32),
                        pltpu.VMEM((2, ATTN_UNROLL, 2 * ATTN_BLOCK, 2 * ATTN_BLOCK), BF16),
                        pltpu.VMEM((2, ATTN_UNROLL, ATTN_BLOCK, LANES), F32)],
        compiler_params=_params(("arbitrary", "arbitrary"), 56 * 1024 * 1024),
        name="attn",
    )(q, k, v)


def _outproj_kernel(attn_ref, rec_ref, w_ref, x_ref, gm_ref, g_ref, sh_ref, sc_ref, wr_ref, br_ref,
                    x1_ref, h2_ref, idx_ref, gate_ref, rank_ref, cnt_ref, base_ref):
    tm = x_ref.shape[0]
    da = attn_ref.shape[1]

    @pl.when(jnp.logical_and(pl.program_id(0) == 0, pl.program_id(1) == 0))
    def _():
        base_ref[...] = jnp.zeros_like(base_ref)

    y = (jnp.dot(attn_ref[...], w_ref[0:da, :], preferred_element_type=F32)
         + jnp.dot(rec_ref[...], w_ref[da:, :], preferred_element_type=F32))
    x1 = x_ref[...] + gm_ref[...] * y
    x1_ref[...] = x1
    h2 = _rms_modulate(x1, g_ref[...], sh_ref[...], sc_ref[...])
    for c in range(ROW_CHUNKS):
        h2_ref[pl.ds(c, tm, stride=ROW_CHUNKS), :] = h2[:, c * LANES:(c + 1) * LANES]

    def logits(w, h):
        return lax.dot_general(w, h, (((1,), (1,)), ((), ())), preferred_element_type=F32)

    w_hi, w_lo = _halves(wr_ref[...])
    h_hi, h_lo = _halves(h2)
    work = logits(w_hi, h_hi) + (logits(w_hi, h_lo) + logits(w_lo, h_hi)) + br_ref[...]
    n_exp = work.shape[0]
    eid = lax.broadcasted_iota(I32, (n_exp, tm), 0)
    vals, hots = [], []
    for kk in range(TOP_K):
        m = jnp.max(work, axis=0, keepdims=True)
        sel = jnp.min(jnp.where(work == m, eid, n_exp), axis=0, keepdims=True)
        hot = eid == sel
        idx_ref[kk:kk + 1, :] = sel
        vals.append(m)
        hots.append(hot)
        work = jnp.where(hot, -jnp.inf, work)
    exps = [jnp.exp(v - vals[0]) for v in vals]
    den = exps[0] + exps[1] + exps[2] + exps[3]
    for kk in range(TOP_K):
        gate_ref[kk:kk + 1, :] = exps[kk] / den

    cnt = jnp.zeros((n_exp, tm), F32)
    for hot in hots:
        cnt = cnt + hot.astype(F32)
    tr = lax.broadcasted_iota(I32, (tm, tm), 0)
    tc = lax.broadcasted_iota(I32, (tm, tm), 1)
    before = jnp.where(tr < tc, 1.0, 0.0).astype(BF16)
    prior = jnp.dot(cnt.astype(BF16), before, preferred_element_type=F32) + base_ref[:, 0:1]
    for kk in range(TOP_K):
        rank_ref[kk:kk + 1, :] = jnp.sum(jnp.where(hots[kk], prior, 0.0), axis=0,
                                         keepdims=True).astype(I32)
    base_ref[...] = base_ref[...] + jnp.sum(cnt, axis=1, keepdims=True)
    cnt_ref[...] = base_ref[...].astype(I32)


def _outproj(attn, rec, w_bf, x, gate_m, g_ffn, shift_f, scale_f, w_router, b_router):
    bsz, s, d = x.shape
    da = attn.shape[-1]
    n_exp = w_router.shape[1]
    tm = OUTPROJ_ROWS
    nt = s // tm
    t = bsz * s
    row = pl.BlockSpec((None, 1, d), lambda b, i: (b, 0, 0))
    small = pl.BlockSpec((TOP_K, tm), lambda b, i: (0, b * nt + i))
    return pl.pallas_call(
        _outproj_kernel,
        grid=(bsz, nt),
        in_specs=[pl.BlockSpec((None, tm, da), lambda b, i: (b, i, 0)),
                  pl.BlockSpec((None, tm, da), lambda b, i: (b, i, 0)),
                  pl.BlockSpec((d, d), lambda b, i: (0, 0)),
                  pl.BlockSpec((None, tm, d), lambda b, i: (b, i, 0)),
                  row,
                  pl.BlockSpec((1, d), lambda b, i: (0, 0)),
                  row, row,
                  pl.BlockSpec((n_exp, d), lambda b, i: (0, 0)),
                  pl.BlockSpec((n_exp, 1), lambda b, i: (0, 0))],
        out_specs=[pl.BlockSpec((None, tm, d), lambda b, i: (b, i, 0)),
                   pl.BlockSpec((tm * ROW_CHUNKS, LANES), lambda b, i: (b * nt + i, 0)),
                   small, small, small,
                   pl.BlockSpec((n_exp, LANES), lambda b, i: (0, 0))],
        out_shape=[jax.ShapeDtypeStruct((bsz, s, d), F32),
                   jax.ShapeDtypeStruct((t * ROW_CHUNKS, LANES), F32),
                   jax.ShapeDtypeStruct((TOP_K, t), I32),
                   jax.ShapeDtypeStruct((TOP_K, t), F32),
                   jax.ShapeDtypeStruct((TOP_K, t), I32),
                   jax.ShapeDtypeStruct((n_exp, LANES), I32)],
        scratch_shapes=[pltpu.VMEM((n_exp, LANES), F32)],
        compiler_params=_params(("arbitrary", "arbitrary")),
        name="outproj_router",
    )(attn, rec, w_bf, x, gate_m, g_ffn.reshape(1, d), shift_f, scale_f, w_router.T,
      b_router.reshape(n_exp, 1))


def _row_slice(ref, row):
    return ref.at[pl.ds(pl.multiple_of(row * ROW_CHUNKS, ROW_CHUNKS), ROW_CHUNKS), :]


def _slots_kernel(dest_ref, init_ref, slot_ref, sem):
    n_assign = dest_ref.shape[0]
    unroll = 32

    init = pltpu.make_async_copy(init_ref, slot_ref, sem)
    init.start()
    init.wait()

    def assign_body(i, c):
        for u in range(unroll):
            slot_ref[dest_ref[i * unroll + u]] = i * unroll + u
        return c

    lax.fori_loop(0, n_assign // unroll, assign_body, 0)


def _slots(dest_flat, n_slots):
    smem = pl.BlockSpec(memory_space=pltpu.SMEM)
    return pl.pallas_call(
        _slots_kernel,
        in_specs=[smem, pl.BlockSpec(memory_space=pltpu.VMEM)],
        out_specs=smem,
        out_shape=jax.ShapeDtypeStruct((n_slots,), I32),
        scratch_shapes=[pltpu.SemaphoreType.DMA(())],
        name="slots",
    )(dest_flat, jnp.arange(n_slots, dtype=I32))


GATHER_BUFS = 3


def _experts_kernel(bexp_ref, bvalid_ref, nused_ref, slot_ref, h2_ref, wgu_ref, bgu_ref, wd_ref, bd_ref,
                    y_ref, xg_ref, xs_ref, wgu_bf, wd_bf, gsem):
    b = pl.program_id(0)
    nb = pl.num_programs(0)
    rows = xs_ref.shape[0]
    f = wd_ref.shape[0]
    n_tok = h2_ref.shape[0] // ROW_CHUNKS
    assert n_tok & (n_tok - 1) == 0
    used = b < nused_ref[0]
    changed = jnp.logical_or(b == 0, bexp_ref[b] != bexp_ref[jnp.maximum(b - 1, 0)])

    def buf_rows(ref, buf, i):
        if isinstance(i, int):
            return ref.at[buf, pl.ds(i * ROW_CHUNKS, ROW_CHUNKS), :]
        return ref.at[buf, pl.ds(pl.multiple_of(i * ROW_CHUNKS, ROW_CHUNKS), ROW_CHUNKS), :]

    def fetch_row(base, buf, i):
        tok = slot_ref[base + i] & (n_tok - 1)
        return pltpu.make_async_copy(_row_slice(h2_ref, tok), buf_rows(xg_ref, buf, i), gsem.at[buf])

    def fetch_block(blk, buf, rolled=False):
        base = blk * rows
        if rolled:
            def body(i, c):
                fetch_row(base, buf, i).start()
                return c
            lax.fori_loop(0, rows, body, 0)
        else:
            for i in range(rows):
                fetch_row(base, buf, i).start(priority=i % 2)

    def wait_block(buf):
        pltpu.make_async_copy(xg_ref.at[buf], xg_ref.at[buf], gsem.at[buf]).wait()

    @pl.when(b == 0)
    def _():
        fetch_block(0, 0, rolled=True)
        fetch_block(1, 1, rolled=True)

    gbuf = b % GATHER_BUFS
    n_used = nused_ref[0]

    @pl.when(b < n_used + 2)
    def _():
        wait_block(gbuf)

    def fetch_ahead():
        fetch_block(jnp.minimum(b + 2, nb - 1), (b + 2) % GATHER_BUFS)

    @pl.when(jnp.logical_and(used, changed))
    def _():
        wgu_bf[...] = wgu_ref[...].astype(BF16)
        wd_bf[...] = wd_ref[...].astype(BF16)

    def run_expert(m):
        for c in range(ROW_CHUNKS):
            xs_ref[0:m, c * LANES:(c + 1) * LANES] = (
                xg_ref[gbuf, pl.ds(c, m, stride=ROW_CHUNKS), :].astype(BF16))
        fetch_ahead()
        gu = jnp.dot(xs_ref[0:m, :], wgu_bf[...], preferred_element_type=F32) + bgu_ref[...]
        gate = jnp.minimum(gu[:, :f], SWIGLU_LIMIT)
        up = jnp.clip(gu[:, f:], -SWIGLU_LIMIT, SWIGLU_LIMIT)
        glu = gate * jax.nn.sigmoid(gate * SWIGLU_ALPHA)
        act = ((up + 1.0) * glu).astype(BF16)
        y = jnp.dot(act, wd_bf[...], preferred_element_type=F32) + bd_ref[...]
        for c in range(ROW_CHUNKS):
            y_ref[pl.ds(c, m, stride=ROW_CHUNKS), :] = y[:, c * LANES:(c + 1) * LANES]
        if m < rows:
            y_ref[m * ROW_CHUNKS:, :] = jnp.zeros(((rows - m) * ROW_CHUNKS, LANES), F32)

    few = bvalid_ref[b] <= rows // 2

    @pl.when(jnp.logical_and(used, jnp.logical_not(few)))
    def _():
        run_expert(rows)

    @pl.when(jnp.logical_and(used, few))
    def _():
        run_expert(rows // 2)

    @pl.when(jnp.logical_not(used))
    def _():
        y_ref[...] = jnp.zeros_like(y_ref)

    for late in (1, 2):
        @pl.when(jnp.logical_and(b == nb - 1, n_used > nb - 3 + late))
        def _(late=late):
            wait_block((b + late) % GATHER_BUFS)


def _experts(h2rows, slots, blk_exp, blk_valid, n_used, w_gate_up, b_gate_up, w_down, b_down):
    n_exp, d, f2 = w_gate_up.shape
    f = w_down.shape[1]
    rows = EXPERT_ROWS
    n_blocks = slots.shape[0] // rows
    assert n_blocks >= GATHER_BUFS
    buf_shape = (rows * ROW_CHUNKS, LANES)
    return pl.pallas_call(
        _experts_kernel,
        grid_spec=pltpu.PrefetchScalarGridSpec(
            num_scalar_prefetch=4,
            grid=(n_blocks,),
            in_specs=[pl.BlockSpec(memory_space=pl.ANY),
                      pl.BlockSpec((None, d, f2), lambda b, e, *_: (e[b], 0, 0)),
                      pl.BlockSpec((None, 1, f2), lambda b, e, *_: (e[b], 0, 0)),
                      pl.BlockSpec((None, f, d), lambda b, e, *_: (e[b], 0, 0)),
                      pl.BlockSpec((None, 1, d), lambda b, e, *_: (e[b], 0, 0))],
            out_specs=pl.BlockSpec(buf_shape, lambda b, e, *_: (b, 0)),
            scratch_shapes=[pltpu.VMEM((GATHER_BUFS,) + buf_shape, F32),
                            pltpu.VMEM((rows, d), BF16),
                            pltpu.VMEM((d, f2), BF16),
                            pltpu.VMEM((f, d), BF16),
                            pltpu.SemaphoreType.DMA((GATHER_BUFS,))]),
        out_shape=jax.ShapeDtypeStruct((slots.shape[0] * ROW_CHUNKS, LANES), F32),
        compiler_params=_params(("arbitrary",), 56 * 1024 * 1024),
        name="experts",
    )(blk_exp, blk_valid, n_used, slots, h2rows, w_gate_up, b_gate_up.reshape(n_exp, 1, f2), w_down,
      b_down.reshape(n_exp, 1, d))


def _combine_kernel(src_ref, cnt_ref, dst_ref, pos_ref, gate_ref, y_ref, x1_ref, gf_ref, g_ref, o_ref,
                    stage_ref, tok_ref, sem):
    i = pl.program_id(0)
    nt = pl.num_programs(0)
    tm = x1_ref.shape[0]
    n_exp = cnt_ref.shape[0] // nt

    def fetch(tile, buf):
        def per_expert(e, c):
            seg = tile * n_exp + e
            cnt, src, dst = cnt_ref[seg], src_ref[seg], dst_ref[seg]

            def piece(done, size):
                return pltpu.make_async_copy(
                    y_ref.at[pl.ds(pl.multiple_of((src + done) * ROW_CHUNKS, ROW_CHUNKS),
                                   size * ROW_CHUNKS), :],
                    stage_ref.at[buf, pl.ds(pl.multiple_of((dst + done) * ROW_CHUNKS, ROW_CHUNKS),
                                            size * ROW_CHUNKS), :],
                    sem.at[buf])

            def whole(j, c2):
                piece(j * COMBINE_PIECE, COMBINE_PIECE).start()
                return c2

            n_whole = cnt // COMBINE_PIECE
            lax.fori_loop(0, n_whole, whole, 0)
            done = n_whole * COMBINE_PIECE
            size = COMBINE_PIECE // 2
            while size >= 1:
                @pl.when((cnt & size) != 0)
                def _(size=size, done=done):
                    piece(done, size).start()

                done = done + (cnt & size)
                size //= 2
            return c
        lax.fori_loop(0, n_exp, per_expert, 0)

    @pl.when(i == 0)
    def _():
        fetch(0, 0)

    @pl.when(i + 1 < nt)
    def _():
        fetch(i + 1, (i + 1) % 2)

    buf = i % 2
    pltpu.make_async_copy(stage_ref.at[buf], stage_ref.at[buf], sem.at[buf]).wait()

    for t in range(tm):
        acc = None
        for kk in range(TOP_K):
            row = stage_ref[buf, pl.ds(pl.multiple_of(pos_ref[0, kk * tm + t], ROW_CHUNKS), ROW_CHUNKS), :]
            term = row * gate_ref[0, kk * tm + t]
            acc = term if acc is None else acc + term
        tok_ref[t * ROW_CHUNKS:(t + 1) * ROW_CHUNKS, :] = acc

    moe = jnp.concatenate([tok_ref[pl.ds(c, tm, stride=ROW_CHUNKS), :] for c in range(ROW_CHUNKS)],
                          axis=-1)
    x2 = x1_ref[...] + gf_ref[...] * moe
    o_ref[...] = (x2 * lax.rsqrt(jnp.mean(x2 * x2, axis=-1, keepdims=True) + NORM_EPS)) * g_ref[...]


def _combine(y, plan, x1, gate_f, g_final):
    bsz, s, d = x1.shape
    tm = COMBINE_ROWS
    nt = bsz * s // tm
    per_b = s // tm
    seg_src, seg_cnt, seg_dst, pos3, gates3 = plan
    per_tile = pl.BlockSpec((None, 1, TOP_K * tm), lambda i, *_: (i, 0, 0), memory_space=pltpu.SMEM)
    return pl.pallas_call(
        _combine_kernel,
        grid_spec=pltpu.PrefetchScalarGridSpec(
            num_scalar_prefetch=3,
            grid=(nt,),
            in_specs=[per_tile, per_tile,
                      pl.BlockSpec(memory_space=pl.ANY),
                      pl.BlockSpec((None, tm, d), lambda i, *_: (i // per_b, i % per_b, 0)),
                      pl.BlockSpec((None, 1, d), lambda i, *_: (i // per_b, 0, 0)),
                      pl.BlockSpec((1, d), lambda i, *_: (0, 0))],
            out_specs=pl.BlockSpec((None, tm, d), lambda i, *_: (i // per_b, i % per_b, 0)),
            scratch_shapes=[pltpu.VMEM((2, TOP_K * tm * ROW_CHUNKS, LANES), F32),
                            pltpu.VMEM((tm * ROW_CHUNKS, LANES), F32),
                            pltpu.SemaphoreType.DMA((2,))]),
        out_shape=jax.ShapeDtypeStruct((bsz, s, d), F32),
        compiler_params=_params(("arbitrary",)),
        name="combine",
    )(seg_src, seg_cnt, seg_dst, pos3, gates3, y, x1, gate_f, g_final.reshape(1, d))


def _combine_plan(idx, rank, gates, pstart):
    tm = COMBINE_ROWS
    t = idx.shape[1]
    nt = t // tm
    experts = jnp.arange(N_EXPERTS, dtype=I32)
    hot = idx.reshape(TOP_K, nt, tm)[..., None] == experts
    tile_cnt = jnp.sum(hot.astype(I32), axis=(0, 2))
    tile_base = jnp.cumsum(tile_cnt, axis=0) - tile_cnt
    stage_at = jnp.cumsum(tile_cnt, axis=1) - tile_cnt
    pos = rank.reshape(TOP_K, nt, tm) + jnp.sum(
        jnp.where(hot, (stage_at - tile_base)[None, :, None, :], 0), axis=-1)
    flat = lambda a: a.reshape(-1).astype(I32)
    return (flat(pstart[None, :] + tile_base), flat(tile_cnt), flat(stage_at),
            (pos * ROW_CHUNKS).transpose(1, 0, 2).reshape(nt, 1, TOP_K * tm).astype(I32),
            gates.reshape(TOP_K, nt, tm).transpose(1, 0, 2).reshape(nt, 1, TOP_K * tm))


def _routing_plan(idx, rank, counts, n_rows):
    rows = EXPERT_ROWS
    experts = jnp.arange(N_EXPERTS, dtype=I32)

    def lookup(table, e):
        return jnp.sum(jnp.where(e[..., None] == experts, table, 0), axis=-1)

    def segment_of(ends, pos):
        return jnp.minimum(jnp.sum((pos[..., None] >= ends).astype(I32), axis=-1), N_EXPERTS - 1)

    padded = ((counts + rows - 1) // rows) * rows
    pend = jnp.cumsum(padded)
    pstart = pend - padded
    dest = lookup(pstart, idx) + rank
    n_blocks = n_rows // rows
    blk_exp = segment_of(pend, jnp.arange(n_blocks, dtype=I32) * rows)
    n_used = (pend[-1] // rows).astype(I32).reshape(1)
    blk_first = jnp.arange(n_blocks, dtype=I32) * rows - lookup(pstart, blk_exp)
    blk_valid = jnp.clip(lookup(counts, blk_exp) - blk_first, 0, rows)
    return dest.astype(I32), blk_exp.astype(I32), blk_valid.astype(I32), n_used, pstart


def kernel(x, c, w_ada, b_ada, g_mix, w_in, conv_w, conv_b, w_rg_a, b_rg_a, w_rg_x, b_rg_x, lam, w_out,
           g_ffn, w_router, b_router, w_gate_up, b_gate_up, w_down, b_down, g_final):
    bsz, s, d = x.shape
    t = bsz * s
    depth = w_ada.shape[0]
    assert depth == 1, "the combine kernel applies the final norm; one layer only"
    for l in range(depth):
        mod = _ada(c, w_ada[l], b_ada[l]).reshape(bsz, 6, 1, d)
        shift_m, scale_m, gate_m, shift_f, scale_f, gate_f = (mod[:, j] for j in range(6))
        q, k, v, xr, gr = _inproj(x, g_mix[l], shift_m, scale_m, w_in[l].astype(BF16))
        attn = _attention(q, k, v)
        rec = _rglru(xr, gr, conv_w[l], conv_b[l], w_rg_a[l], b_rg_a[l], w_rg_x[l], b_rg_x[l], lam[l])
        x1, h2rows, idx, gates, rank, cnt = _outproj(
            attn, rec, w_out[l].astype(BF16), x, gate_m, g_ffn[l], shift_f, scale_f, w_router[l], b_router[l])
        n_rows = t * TOP_K + N_EXPERTS * EXPERT_ROWS
        dest, blk_exp, blk_valid, n_used, pstart = _routing_plan(idx, rank, cnt[:, 0], n_rows)
        slots = _slots(dest.reshape(-1), n_rows)
        y = _experts(h2rows, slots, blk_exp, blk_valid, n_used, w_gate_up[l], b_gate_up[l], w_down[l],
                     b_down[l])
        x = _combine(y, _combine_plan(idx, rank, gates, pstart), x1, gate_f, g_final)
    return x
```

```python
import jax
import jax.numpy as jnp
from jax import lax
from jax.experimental import pallas as pl
from jax.experimental.pallas import tpu as pltpu

F32 = jnp.float32
BF16 = jnp.bfloat16
I32 = jnp.int32

HEAD_DIM = 64
DILATED_PATTERNS = ((128, 1), (512, 4), (2048, 16))
REC_BLOCKS = 8
CONV_WIDTH = 4
RG_C = 8.0
N_EXPERTS = 32
TOP_K = 4
SWIGLU_LIMIT = 7.0
SWIGLU_ALPHA = 1.702
NORM_EPS = 1e-6
NEG_INF = -1e30
LOG2_E = 1.4426950408889634

LANES = 128
SUBLANES = 8
ROW_CHUNKS = 8
VMEM_LIMIT = 48 * 1024 * 1024

INPROJ_ROWS = 1024
RGLRU_ROWS = 1024
OUTPROJ_ROWS = 1024
ATTN_BLOCK = 128
ATTN_UNROLL = 16
EXPERT_ROWS = 512
COMBINE_ROWS = 512
COMBINE_PIECE = 64


def _params(sem, vmem=VMEM_LIMIT):
    return pltpu.CompilerParams(dimension_semantics=sem, vmem_limit_bytes=vmem)


def _halves(a):
    hi = a.astype(BF16)
    return hi, (a - hi.astype(F32)).astype(BF16)


def _ada_kernel(c_ref, w_ref, b_ref, o_ref):
    c_hi, c_lo = _halves(c_ref[...])
    w_hi, w_lo = _halves(w_ref[...])
    dot = lambda a, b: jnp.dot(a, b, preferred_element_type=F32)
    o_ref[...] = dot(c_hi, w_hi) + (dot(c_lo, w_hi) + dot(c_hi, w_lo)) + b_ref[...]


def _ada(c, w, b):
    bsz, d = c.shape
    n = w.shape[1]
    return pl.pallas_call(
        _ada_kernel,
        grid=(n // d,),
        in_specs=[pl.BlockSpec((bsz, d), lambda j: (0, 0)),
                  pl.BlockSpec((d, d), lambda j: (0, j)),
                  pl.BlockSpec((1, d), lambda j: (0, j))],
        out_specs=pl.BlockSpec((bsz, d), lambda j: (0, j)),
        out_shape=jax.ShapeDtypeStruct((bsz, n), F32),
        compiler_params=_params(("arbitrary",)),
        name="ada",
    )(c, w, b.reshape(1, n))


def _rms_modulate(x, g, shift, scale):
    y = x * lax.rsqrt(jnp.mean(x * x, axis=-1, keepdims=True) + NORM_EPS)
    return (y * g) * (1.0 + scale) + shift


def _inproj_kernel(x_ref, g_ref, sh_ref, sc_ref, w_ref, q_ref, k_ref, v_ref, xr_ref, gr_ref):
    h = _rms_modulate(x_ref[...], g_ref[...], sh_ref[...], sc_ref[...]).astype(BF16)
    outs = (q_ref, k_ref, v_ref, xr_ref, gr_ref)
    width = q_ref.shape[-1]
    for j, o_ref in enumerate(outs):
        z = jnp.dot(h, w_ref[:, j * width:(j + 1) * width], preferred_element_type=F32)
        o_ref[...] = z.astype(o_ref.dtype)


def _inproj(x, g, shift, scale, w_bf):
    bsz, s, d = x.shape
    n = w_bf.shape[1]
    width = n // 5
    tm = INPROJ_ROWS
    row = pl.BlockSpec((None, 1, d), lambda b, i: (b, 0, 0))
    out_blk = pl.BlockSpec((None, tm, width), lambda b, i: (b, i, 0))
    shp = lambda dt: jax.ShapeDtypeStruct((bsz, s, width), dt)
    return pl.pallas_call(
        _inproj_kernel,
        grid=(bsz, s // tm),
        in_specs=[pl.BlockSpec((None, tm, d), lambda b, i: (b, i, 0)),
                  pl.BlockSpec((1, d), lambda b, i: (0, 0)),
                  row, row,
                  pl.BlockSpec((d, n), lambda b, i: (0, 0))],
        out_specs=[out_blk] * 5,
        out_shape=[shp(BF16), shp(BF16), shp(BF16), shp(F32), shp(F32)],
        compiler_params=_params(("arbitrary", "arbitrary")),
        name="inproj",
    )(x, g.reshape(1, d), shift, scale, w_bf)


def _sigmoid(x):
    return 0.5 * jnp.tanh(0.5 * x) + 0.5


def _gelu_tanh(x):
    return 0.5 * x * (1.0 + jnp.tanh(0.7978845608028654 * (x + 0.044715 * (x * x * x))))


def _rglru_kernel(xr_ref, gr_ref, cw_ref, cb_ref, wa_ref, ba_ref, wx_ref, bx_ref, lam_ref,
                  o_ref, xe_ref, a_ref, u_ref, tail_ref, h_ref):
    ts, ch = xr_ref.shape
    pad = SUBLANES

    @pl.when(pl.program_id(1) == 0)
    def _():
        tail_ref[...] = jnp.zeros_like(tail_ref)
        h_ref[...] = jnp.zeros_like(h_ref)

    x = xr_ref[...]
    xe_ref[0:pad, :] = tail_ref[...]
    xe_ref[pad:pad + ts, :] = x
    tail_ref[...] = x[ts - pad:ts, :]
    xc = cb_ref[...] + jnp.zeros((ts, ch), F32)
    for j in range(CONV_WIDTH):
        off = pad - (CONV_WIDTH - 1) + j
        xc = xc + cw_ref[j:j + 1, :] * xe_ref[off:off + ts, :]

    xb = xc.astype(BF16)
    r = _sigmoid(jnp.dot(xb, wa_ref[...], preferred_element_type=F32) + ba_ref[...])
    i = _sigmoid(jnp.dot(xb, wx_ref[...], preferred_element_type=F32) + bx_ref[...])
    nl = -lam_ref[...]
    softplus = jnp.maximum(nl, 0.0) + jnp.log(1.0 + jnp.exp(-jnp.abs(nl)))
    log_a = (-RG_C) * r * softplus
    a = jnp.exp(log_a)
    mult = jnp.sqrt(1.0 - a * a)
    u = mult * (i * xc)

    ng = ts // SUBLANES
    a3 = a.reshape(ng, SUBLANES, ch)
    u3 = u.reshape(ng, SUBLANES, ch)
    rid = lax.broadcasted_iota(I32, (ng, SUBLANES, ch), 1)
    for sft in (1, 2, 4):
        a_s = pltpu.roll(a3, sft, 1)
        u_s = pltpu.roll(u3, sft, 1)
        keep = rid >= sft
        u3 = jnp.where(keep, a3 * u_s + u3, u3)
        a3 = jnp.where(keep, a3 * a_s, a3)
    a_ref[...] = a3.reshape(ts, ch)
    u_ref[...] = u3.reshape(ts, ch)

    inner = 8

    def body(gi, hprev):
        for jj in range(inner):
            base = pl.multiple_of((gi * inner + jj) * SUBLANES, SUBLANES)
            hcur = u_ref[pl.ds(base, SUBLANES), :] + a_ref[pl.ds(base, SUBLANES), :] * hprev
            u_ref[pl.ds(base, SUBLANES), :] = hcur
            hprev = hcur[SUBLANES - 1:SUBLANES, :]
        return hprev

    h_ref[...] = lax.fori_loop(0, ng // inner, body, h_ref[...])
    o_ref[...] = (u_ref[...] * _gelu_tanh(gr_ref[...])).astype(o_ref.dtype)


def _block_diag(w):
    nb, hin, hout = w.shape
    eye = jnp.eye(nb, dtype=w.dtype)
    return (eye[:, None, :, None] * w[:, :, None, :]).reshape(nb * hin, nb * hout)


def _rglru(xr, gr, conv_w, conv_b, w_a, b_a, w_x, b_x, lam):
    bsz, s, ch = xr.shape
    ts = RGLRU_ROWS
    blk = pl.BlockSpec((None, ts, ch), lambda b, t: (b, t, 0))
    full = lambda r, c: pl.BlockSpec((r, c), lambda b, t: (0, 0))
    return pl.pallas_call(
        _rglru_kernel,
        grid=(bsz, s // ts),
        in_specs=[blk, blk, full(CONV_WIDTH, ch), full(1, ch), full(ch, ch), full(1, ch),
                  full(ch, ch), full(1, ch), full(1, ch)],
        out_specs=blk,
        out_shape=jax.ShapeDtypeStruct((bsz, s, ch), BF16),
        scratch_shapes=[pltpu.VMEM((ts + SUBLANES, ch), F32),
                        pltpu.VMEM((ts, ch), F32),
                        pltpu.VMEM((ts, ch), F32),
                        pltpu.VMEM((SUBLANES, ch), F32),
                        pltpu.VMEM((1, ch), F32)],
        compiler_params=_params(("arbitrary", "arbitrary")),
        name="rglru",
    )(xr, gr, conv_w, conv_b.reshape(1, ch), _block_diag(w_a).astype(BF16), b_a.reshape(1, ch),
      _block_diag(w_x).astype(BF16), b_x.reshape(1, ch), lam.reshape(1, ch))


def _attn_kernel(q_ref, k_ref, v_ref, o_ref, f_ref, f4_ref, qs_ref, kp_ref, vp_ref, op_ref, lp_ref,
                 pc_ref, ml_ref):
    s = q_ref.shape[0]
    blk = ATTN_BLOCK
    n_blocks = s // blk
    unroll = ATTN_UNROLL

    lane = lax.broadcasted_iota(I32, (blk, LANES), 1)
    head0 = lane < HEAD_DIM
    head0_wide = lax.broadcasted_iota(I32, (blk, 2 * LANES), 1) % LANES < HEAD_DIM
    qi = lax.broadcasted_iota(I32, (2 * blk, blk), 0) % blk
    kj = lax.broadcasted_iota(I32, (2 * blk, blk), 1)
    prev_ok = kj >= qi
    cur_ok = kj <= qi
    dims = (((1,), (1,)), ((), ()))

    def block_rows(j, dil):
        per_res = (s // dil) // blk
        start = j // per_res + (j % per_res) * (dil * blk)
        if dil == 1:
            return pl.ds(pl.multiple_of(start, blk), blk)
        return pl.ds(start, blk, stride=dil)

    dils = tuple(d for _, d in DILATED_PATTERNS)
    assert dils == (1, 4, 16)

    def permute(src_ref, store):
        f_ref[...] = src_ref[...].astype(F32)

        def body0(j, c):
            store(0, j, f_ref[block_rows(j, 1), :])
            return c

        def body1(j, c):
            x = f_ref[block_rows(j, 4), :]
            f4_ref[pl.ds(pl.multiple_of(j * blk, blk), blk), :] = x
            store(1, j, x)
            return c

        def body2(j, c):
            per_res = (s // 16) // blk
            res, nblk = j // per_res, j % per_res
            start = (res % 4) * (s // 4) + res // 4 + nblk * (4 * blk)
            store(2, j, f4_ref[pl.ds(start, blk, stride=4), :])
            return c

        for body in (body0, body1, body2):
            for j in range(n_blocks):
                body(j, 0)

    def store_q(p, j, x):
        x = x * (HEAD_DIM ** -0.5 * LOG2_E)
        base = pl.multiple_of(j * (2 * blk), 2 * blk)
        qs_ref[p, pl.ds(base, blk), :] = jnp.where(head0, x, 0.0).astype(BF16)
        qs_ref[p, pl.ds(base + blk, blk), :] = jnp.where(head0, 0.0, x).astype(BF16)

    def store_kv(dst_ref):
        def store(p, j, x):
            dst_ref[p, pl.ds(pl.multiple_of((j + 1) * blk, blk), blk), 0:LANES] = x.astype(BF16)
        return store

    for p in range(len(DILATED_PATTERNS)):
        kp_ref[p, 0:blk, :] = jnp.zeros((blk, LANES), BF16)
        vp_ref[p, 0:blk, 0:LANES] = jnp.zeros((blk, LANES), BF16)
        vp_ref[p, :, LANES:] = jnp.ones((s + blk, LANES), BF16)
    permute(q_ref, store_q)
    permute(k_ref, store_kv(kp_ref))
    permute(v_ref, store_kv(vp_ref))

    def kv_rows(j):
        return pl.ds(pl.multiple_of(j * blk, blk), 2 * blk)

    def scores(p, g):
        per_res = (s // dils[p]) // blk
        for u in range(unroll):
            j = g * unroll + u
            no_prev = jnp.where(j % per_res > 0, 0.0, NEG_INF)
            qs = qs_ref[p, pl.ds(pl.multiple_of(j * (2 * blk), 2 * blk), 2 * blk), :]
            sc = lax.dot_general(qs, kp_ref[p, kv_rows(j), :], dims, preferred_element_type=F32)
            s_prev = jnp.where(prev_ok, sc[:, :blk], NEG_INF) + no_prev
            s_cur = jnp.where(cur_ok, sc[:, blk:], NEG_INF)
            m = jnp.max(jnp.maximum(s_prev, s_cur), axis=-1, keepdims=True)
            pc_ref[g % 2, u] = jnp.concatenate(
                [jnp.exp2(s_prev - m), jnp.exp2(s_cur - m)], axis=-1).astype(BF16)
            ml_ref[g % 2, u] = jnp.where(head0, m[:blk], m[blk:])

    def outputs(p, g):
        for u in range(unroll):
            j = g * unroll + u
            rl = jnp.dot(pc_ref[g % 2, u], vp_ref[p, kv_rows(j), :], preferred_element_type=F32)
            rl = jnp.where(head0_wide, rl[:blk], rl[blk:])
            l = rl[:, LANES:]
            if dils[p] == 16:
                per_res = (s // 16) // blk
                res, nblk = j // per_res, j % per_res
                rows = pl.ds((res % 4) * (s // 4) + res // 4 + nblk * (4 * blk), blk, stride=4)
            else:
                rows = block_rows(j, dils[p])
            op_ref[p, rows, :] = rl[:, :LANES] / l
            lp_ref[p, rows, :] = ml_ref[g % 2, u] + jnp.log2(l)

    n_groups = n_blocks // unroll
    assert n_groups % 2 == 0
    n_pat = len(DILATED_PATTERNS)
    scores(0, 0)
    for p, (window, dil) in enumerate(DILATED_PATTERNS):
        assert window // dil == blk

        def body(g, carry, p=p):
            outputs(p, g - 1)
            scores(p, g)
            return carry

        lax.fori_loop(1, n_groups, body, 0)
        outputs(p, n_groups - 1)
        if p + 1 < n_pat:
            scores(p + 1, 0)

    quarter, chunk = s // 4, 256
    for r4 in range(4):
        for l0 in range(0, quarter, chunk):
            tok = pl.ds(r4 + 4 * l0, chunk, stride=4)
            lay = pl.ds(r4 * quarter + l0, chunk)
            l0_, l1_, l2_ = lp_ref[0, tok, :], lp_ref[1, tok, :], lp_ref[2, lay, :]
            m = jnp.maximum(jnp.maximum(l0_, l1_), l2_)
            w0, w1, w2 = jnp.exp2(l0_ - m), jnp.exp2(l1_ - m), jnp.exp2(l2_ - m)
            num = w0 * op_ref[0, tok, :] + w1 * op_ref[1, tok, :] + w2 * op_ref[2, lay, :]
            f_ref[tok, :] = num / (w0 + w1 + w2)
    for c0 in range(0, s, 512):
        o_ref[c0:c0 + 512, :] = f_ref[c0:c0 + 512, :].astype(o_ref.dtype)


def _attention(q, k, v):
    bsz, s, da = q.shape
    blk = pl.BlockSpec((None, s, LANES), lambda b, h: (b, 0, h))
    n_pat = len(DILATED_PATTERNS)
    return pl.pallas_call(
        _attn_kernel,
        grid=(bsz, da // LANES),
        in_specs=[blk, blk, blk],
        out_specs=blk,
        out_shape=jax.ShapeDtypeStruct((bsz, s, da), BF16),
        scratch_shapes=[pltpu.VMEM((s, LANES), F32),
                        pltpu.VMEM((s, LANES), F32),
                        pltpu.VMEM((n_pat, 2 * s, LANES), BF16),
                        pltpu.VMEM((n_pat, s + ATTN_BLOCK, LANES), BF16),
                        pltpu.VMEM((n_pat, s + ATTN_BLOCK, 2 * LANES), BF16),
                        pltpu.VMEM((n_pat, s, LANES), F32),
                        pltpu.VMEM((n_pat, s, LANES), F32),
                        pltpu.VMEM((2, ATTN_UNROLL, 2 * ATTN_BLOCK, 2 * ATTN_BLOCK), BF16),
                        pltpu.VMEM((2, ATTN_UNROLL, ATTN_BLOCK, LANES), F32)],
        compiler_params=_params(("arbitrary", "arbitrary"), 56 * 1024 * 1024),
        name="attn",
    )(q, k, v)


def _outproj_kernel(attn_ref, rec_ref, w_ref, x_ref, gm_ref, g_ref, sh_ref, sc_ref, wr_ref, br_ref,
                    x1_ref, h2_ref, idx_ref, gate_ref, rank_ref, cnt_ref, base_ref):
    tm = x_ref.shape[0]
    da = attn_ref.shape[1]

    @pl.when(jnp.logical_and(pl.program_id(0) == 0, pl.program_id(1) == 0))
    def _():
        base_ref[...] = jnp.zeros_like(base_ref)

    y = (jnp.dot(attn_ref[...], w_ref[0:da, :], preferred_element_type=F32)
         + jnp.dot(rec_ref[...], w_ref[da:, :], preferred_element_type=F32))
    x1 = x_ref[...] + gm_ref[...] * y
    x1_ref[...] = x1
    h2 = _rms_modulate(x1, g_ref[...], sh_ref[...], sc_ref[...])
    for c in range(ROW_CHUNKS):
        h2_ref[pl.ds(c, tm, stride=ROW_CHUNKS), :] = h2[:, c * LANES:(c + 1) * LANES]

    def logits(w, h):
        return lax.dot_general(w, h, (((1,), (1,)), ((), ())), preferred_element_type=F32)

    w_hi, w_lo = _halves(wr_ref[...])
    h_hi, h_lo = _halves(h2)
    work = logits(w_hi, h_hi) + (logits(w_hi, h_lo) + logits(w_lo, h_hi)) + br_ref[...]
    n_exp = work.shape[0]
    eid = lax.broadcasted_iota(I32, (n_exp, tm), 0)
    vals, hots = [], []
    for kk in range(TOP_K):
        m = jnp.max(work, axis=0, keepdims=True)
        sel = jnp.min(jnp.where(work == m, eid, n_exp), axis=0, keepdims=True)
        hot = eid == sel
        idx_ref[kk:kk + 1, :] = sel
        vals.append(m)
        hots.append(hot)
        work = jnp.where(hot, -jnp.inf, work)
    exps = [jnp.exp(v - vals[0]) for v in vals]
    den = exps[0] + exps[1] + exps[2] + exps[3]
    for kk in range(TOP_K):
        gate_ref[kk:kk + 1, :] = exps[kk] / den

    cnt = jnp.zeros((n_exp, tm), F32)
    for hot in hots:
        cnt = cnt + hot.astype(F32)
    tr = lax.broadcasted_iota(I32, (tm, tm), 0)
    tc = lax.broadcasted_iota(I32, (tm, tm), 1)
    before = jnp.where(tr < tc, 1.0, 0.0).astype(BF16)
    prior = jnp.dot(cnt.astype(BF16), before, preferred_element_type=F32) + base_ref[:, 0:1]
    for kk in range(TOP_K):
        rank_ref[kk:kk + 1, :] = jnp.sum(jnp.where(hots[kk], prior, 0.0), axis=0,
                                         keepdims=True).astype(I32)
    base_ref[...] = base_ref[...] + jnp.sum(cnt, axis=1, keepdims=True)
    cnt_ref[...] = base_ref[...].astype(I32)


def _outproj(attn, rec, w_bf, x, gate_m, g_ffn, shift_f, scale_f, w_router, b_router):
    bsz, s, d = x.shape
    da = attn.shape[-1]
    n_exp = w_router.shape[1]
    tm = OUTPROJ_ROWS
    nt = s // tm
    t = bsz * s
    row = pl.BlockSpec((None, 1, d), lambda b, i: (b, 0, 0))
    small = pl.BlockSpec((TOP_K, tm), lambda b, i: (0, b * nt + i))
    return pl.pallas_call(
        _outproj_kernel,
        grid=(bsz, nt),
        in_specs=[pl.BlockSpec((None, tm, da), lambda b, i: (b, i, 0)),
                  pl.BlockSpec((None, tm, da), lambda b, i: (b, i, 0)),
                  pl.BlockSpec((d, d), lambda b, i: (0, 0)),
                  pl.BlockSpec((None, tm, d), lambda b, i: (b, i, 0)),
                  row,
                  pl.BlockSpec((1, d), lambda b, i: (0, 0)),
                  row, row,
                  pl.BlockSpec((n_exp, d), lambda b, i: (0, 0)),
                  pl.BlockSpec((n_exp, 1), lambda b, i: (0, 0))],
        out_specs=[pl.BlockSpec((None, tm, d), lambda b, i: (b, i, 0)),
                   pl.BlockSpec((tm * ROW_CHUNKS, LANES), lambda b, i: (b * nt + i, 0)),
                   small, small, small,
                   pl.BlockSpec((n_exp, LANES), lambda b, i: (0, 0))],
        out_shape=[jax.ShapeDtypeStruct((bsz, s, d), F32),
                   jax.ShapeDtypeStruct((t * ROW_CHUNKS, LANES), F32),
                   jax.ShapeDtypeStruct((TOP_K, t), I32),
                   jax.ShapeDtypeStruct((TOP_K, t), F32),
                   jax.ShapeDtypeStruct((TOP_K, t), I32),
                   jax.ShapeDtypeStruct((n_exp, LANES), I32)],
        scratch_shapes=[pltpu.VMEM((n_exp, LANES), F32)],
        compiler_params=_params(("arbitrary", "arbitrary")),
        name="outproj_router",
    )(attn, rec, w_bf, x, gate_m, g_ffn.reshape(1, d), shift_f, scale_f, w_router.T,
      b_router.reshape(n_exp, 1))


def _row_slice(ref, row):
    return ref.at[pl.ds(pl.multiple_of(row * ROW_CHUNKS, ROW_CHUNKS), ROW_CHUNKS), :]


def _slots_kernel(dest_ref, init_ref, slot_ref, sem):
    n_assign = dest_ref.shape[0]
    unroll = 32

    init = pltpu.make_async_copy(init_ref, slot_ref, sem)
    init.start()
    init.wait()

    def assign_body(i, c):
        for u in range(unroll):
            slot_ref[dest_ref[i * unroll + u]] = i * unroll + u
        return c

    lax.fori_loop(0, n_assign // unroll, assign_body, 0)


def _slots(dest_flat, n_slots):
    smem = pl.BlockSpec(memory_space=pltpu.SMEM)
    return pl.pallas_call(
        _slots_kernel,
        in_specs=[smem, pl.BlockSpec(memory_space=pltpu.VMEM)],
        out_specs=smem,
        out_shape=jax.ShapeDtypeStruct((n_slots,), I32),
        scratch_shapes=[pltpu.SemaphoreType.DMA(())],
        name="slots",
    )(dest_flat, jnp.arange(n_slots, dtype=I32))


GATHER_BUFS = 3


def _experts_kernel(bexp_ref, bvalid_ref, nused_ref, slot_ref, h2_ref, wgu_ref, bgu_ref, wd_ref, bd_ref,
                    y_ref, xg_ref, xs_ref, wgu_bf, wd_bf, gsem):
    b = pl.program_id(0)
    nb = pl.num_programs(0)
    rows = xs_ref.shape[0]
    f = wd_ref.shape[0]
    n_tok = h2_ref.shape[0] // ROW_CHUNKS
    assert n_tok & (n_tok - 1) == 0
    used = b < nused_ref[0]
    changed = jnp.logical_or(b == 0, bexp_ref[b] != bexp_ref[jnp.maximum(b - 1, 0)])

    def buf_rows(ref, buf, i):
        if isinstance(i, int):
            return ref.at[buf, pl.ds(i * ROW_CHUNKS, ROW_CHUNKS), :]
        return ref.at[buf, pl.ds(pl.multiple_of(i * ROW_CHUNKS, ROW_CHUNKS), ROW_CHUNKS), :]

    def fetch_row(base, buf, i):
        tok = slot_ref[base + i] & (n_tok - 1)
        return pltpu.make_async_copy(_row_slice(h2_ref, tok), buf_rows(xg_ref, buf, i), gsem.at[buf])

    def fetch_block(blk, buf, rolled=False):
        base = blk * rows
        if rolled:
            def body(i, c):
                fetch_row(base, buf, i).start()
                return c
            lax.fori_loop(0, rows, body, 0)
        else:
            for i in range(rows):
                fetch_row(base, buf, i).start(priority=i % 2)

    def wait_block(buf):
        pltpu.make_async_copy(xg_ref.at[buf], xg_ref.at[buf], gsem.at[buf]).wait()

    @pl.when(b == 0)
    def _():
        fetch_block(0, 0, rolled=True)
        fetch_block(1, 1, rolled=True)

    gbuf = b % GATHER_BUFS
    n_used = nused_ref[0]

    @pl.when(b < n_used + 2)
    def _():
        wait_block(gbuf)

    def fetch_ahead():
        fetch_block(jnp.minimum(b + 2, nb - 1), (b + 2) % GATHER_BUFS)

    @pl.when(jnp.logical_and(used, changed))
    def _():
        wgu_bf[...] = wgu_ref[...].astype(BF16)
        wd_bf[...] = wd_ref[...].astype(BF16)

    def run_expert(m):
        for c in range(ROW_CHUNKS):
            xs_ref[0:m, c * LANES:(c + 1) * LANES] = (
                xg_ref[gbuf, pl.ds(c, m, stride=ROW_CHUNKS), :].astype(BF16))
        fetch_ahead()
        gu = jnp.dot(xs_ref[0:m, :], wgu_bf[...], preferred_element_type=F32) + bgu_ref[...]
        gate = jnp.minimum(gu[:, :f], SWIGLU_LIMIT)
        up = jnp.clip(gu[:, f:], -SWIGLU_LIMIT, SWIGLU_LIMIT)
        glu = gate * _sigmoid(gate * SWIGLU_ALPHA)
        act = ((up + 1.0) * glu).astype(BF16)
        y = jnp.dot(act, wd_bf[...], preferred_element_type=F32) + bd_ref[...]
        for c in range(ROW_CHUNKS):
            y_ref[pl.ds(c, m, stride=ROW_CHUNKS), :] = y[:, c * LANES:(c + 1) * LANES]
        if m < rows:
            y_ref[m * ROW_CHUNKS:, :] = jnp.zeros(((rows - m) * ROW_CHUNKS, LANES), F32)

    few = bvalid_ref[b] <= rows // 2

    @pl.when(jnp.logical_and(used, jnp.logical_not(few)))
    def _():
        run_expert(rows)

    @pl.when(jnp.logical_and(used, few))
    def _():
        run_expert(rows // 2)

    @pl.when(jnp.logical_not(used))
    def _():
        y_ref[...] = jnp.zeros_like(y_ref)

    for late in (1, 2):
        @pl.when(jnp.logical_and(b == nb - 1, n_used > nb - 3 + late))
        def _(late=late):
            wait_block((b + late) % GATHER_BUFS)


def _experts(h2rows, slots, blk_exp, blk_valid, n_used, w_gate_up, b_gate_up, w_down, b_down):
    n_exp, d, f2 = w_gate_up.shape
    f = w_down.shape[1]
    rows = EXPERT_ROWS
    n_blocks = slots.shape[0] // rows
    assert n_blocks >= GATHER_BUFS
    buf_shape = (rows * ROW_CHUNKS, LANES)
    return pl.pallas_call(
        _experts_kernel,
        grid_spec=pltpu.PrefetchScalarGridSpec(
            num_scalar_prefetch=4,
            grid=(n_blocks,),
            in_specs=[pl.BlockSpec(memory_space=pl.ANY),
                      pl.BlockSpec((None, d, f2), lambda b, e, *_: (e[b], 0, 0)),
                      pl.BlockSpec((None, 1, f2), lambda b, e, *_: (e[b], 0, 0)),
                      pl.BlockSpec((None, f, d), lambda b, e, *_: (e[b], 0, 0)),
                      pl.BlockSpec((None, 1, d), lambda b, e, *_: (e[b], 0, 0))],
            out_specs=pl.BlockSpec(buf_shape, lambda b, e, *_: (b, 0)),
            scratch_shapes=[pltpu.VMEM((GATHER_BUFS,) + buf_shape, F32),
                            pltpu.VMEM((rows, d), BF16),
                            pltpu.VMEM((d, f2), BF16),
                            pltpu.VMEM((f, d), BF16),
                            pltpu.SemaphoreType.DMA((GATHER_BUFS,))]),
        out_shape=jax.ShapeDtypeStruct((slots.shape[0] * ROW_CHUNKS, LANES), F32),
        compiler_params=_params(("arbitrary",), 56 * 1024 * 1024),
        name="experts",
    )(blk_exp, blk_valid, n_used, slots, h2rows, w_gate_up, b_gate_up.reshape(n_exp, 1, f2), w_down,
      b_down.reshape(n_exp, 1, d))


def _combine_kernel(src_ref, cnt_ref, dst_ref, pos_ref, gate_ref, y_ref, x1_ref, gf_ref, g_ref, o_ref,
                    stage_ref, tok_ref, sem):
    i = pl.program_id(0)
    nt = pl.num_programs(0)
    tm = x1_ref.shape[0]
    n_exp = cnt_ref.shape[0] // nt

    def fetch(tile, buf):
        def per_expert(e, c):
            seg = tile * n_exp + e
            cnt, src, dst = cnt_ref[seg], src_ref[seg], dst_ref[seg]

            def piece(done, size):
                return pltpu.make_async_copy(
                    y_ref.at[pl.ds(pl.multiple_of((src + done) * ROW_CHUNKS, ROW_CHUNKS),
                                   size * ROW_CHUNKS), :],
                    stage_ref.at[buf, pl.ds(pl.multiple_of((dst + done) * ROW_CHUNKS, ROW_CHUNKS),
                                            size * ROW_CHUNKS), :],
                    sem.at[buf])

            def whole(j, c2):
                piece(j * COMBINE_PIECE, COMBINE_PIECE).start()
                return c2

            n_whole = cnt // COMBINE_PIECE
            lax.fori_loop(0, n_whole, whole, 0)
            done = n_whole * COMBINE_PIECE
            size = COMBINE_PIECE // 2
            while size >= 1:
                @pl.when((cnt & size) != 0)
                def _(size=size, done=done):
                    piece(done, size).start()

                done = done + (cnt & size)
                size //= 2
            return c
        lax.fori_loop(0, n_exp, per_expert, 0)

    @pl.when(i == 0)
    def _():
        fetch(0, 0)

    @pl.when(i + 1 < nt)
    def _():
        fetch(i + 1, (i + 1) % 2)

    buf = i % 2
    pltpu.make_async_copy(stage_ref.at[buf], stage_ref.at[buf], sem.at[buf]).wait()

    for t in range(tm):
        acc = None
        for kk in range(TOP_K):
            row = stage_ref[buf, pl.ds(pl.multiple_of(pos_ref[0, kk * tm + t], ROW_CHUNKS), ROW_CHUNKS), :]
            term = row * gate_ref[0, kk * tm + t]
            acc = term if acc is None else acc + term
        tok_ref[t * ROW_CHUNKS:(t + 1) * ROW_CHUNKS, :] = acc

    moe = jnp.concatenate([tok_ref[pl.ds(c, tm, stride=ROW_CHUNKS), :] for c in range(ROW_CHUNKS)],
                          axis=-1)
    x2 = x1_ref[...] + gf_ref[...] * moe
    o_ref[...] = (x2 * lax.rsqrt(jnp.mean(x2 * x2, axis=-1, keepdims=True) + NORM_EPS)) * g_ref[...]


def _combine(y, plan, x1, gate_f, g_final):
    bsz, s, d = x1.shape
    tm = COMBINE_ROWS
    nt = bsz * s // tm
    per_b = s // tm
    seg_src, seg_cnt, seg_dst, pos3, gates3 = plan
    per_tile = pl.BlockSpec((None, 1, TOP_K * tm), lambda i, *_: (i, 0, 0), memory_space=pltpu.SMEM)
    return pl.pallas_call(
        _combine_kernel,
        grid_spec=pltpu.PrefetchScalarGridSpec(
            num_scalar_prefetch=3,
            grid=(nt,),
            in_specs=[per_tile, per_tile,
                      pl.BlockSpec(memory_space=pl.ANY),
                      pl.BlockSpec((None, tm, d), lambda i, *_: (i // per_b, i % per_b, 0)),
                      pl.BlockSpec((None, 1, d), lambda i, *_: (i // per_b, 0, 0)),
                      pl.BlockSpec((1, d), lambda i, *_: (0, 0))],
            out_specs=pl.BlockSpec((None, tm, d), lambda i, *_: (i // per_b, i % per_b, 0)),
            scratch_shapes=[pltpu.VMEM((2, TOP_K * tm * ROW_CHUNKS, LANES), F32),
                            pltpu.VMEM((tm * ROW_CHUNKS, LANES), F32),
                            pltpu.SemaphoreType.DMA((2,))]),
        out_shape=jax.ShapeDtypeStruct((bsz, s, d), F32),
        compiler_params=_params(("arbitrary",)),
        name="combine",
    )(seg_src, seg_cnt, seg_dst, pos3, gates3, y, x1, gate_f, g_final.reshape(1, d))


def _combine_plan(idx, rank, gates, pstart):
    tm = COMBINE_ROWS
    t = idx.shape[1]
    nt = t // tm
    experts = jnp.arange(N_EXPERTS, dtype=I32)
    hot = idx.reshape(TOP_K, nt, tm)[..., None] == experts
    tile_cnt = jnp.sum(hot.astype(I32), axis=(0, 2))
    tile_base = jnp.cumsum(tile_cnt, axis=0) - tile_cnt
    stage_at = jnp.cumsum(tile_cnt, axis=1) - tile_cnt
    pos = rank.reshape(TOP_K, nt, tm) + jnp.sum(
        jnp.where(hot, (stage_at - tile_base)[None, :, None, :], 0), axis=-1)
    flat = lambda a: a.reshape(-1).astype(I32)
    return (flat(pstart[None, :] + tile_base), flat(tile_cnt), flat(stage_at),
            (pos * ROW_CHUNKS).transpose(1, 0, 2).reshape(nt, 1, TOP_K * tm).astype(I32),
            gates.reshape(TOP_K, nt, tm).transpose(1, 0, 2).reshape(nt, 1, TOP_K * tm))


def _routing_plan(idx, rank, counts, n_rows):
    rows = EXPERT_ROWS
    experts = jnp.arange(N_EXPERTS, dtype=I32)

    def lookup(table, e):
        return jnp.sum(jnp.where(e[..., None] == experts, table, 0), axis=-1)

    def segment_of(ends, pos):
        return jnp.minimum(jnp.sum((pos[..., None] >= ends).astype(I32), axis=-1), N_EXPERTS - 1)

    padded = ((counts + rows - 1) // rows) * rows
    pend = jnp.cumsum(padded)
    pstart = pend - padded
    dest = lookup(pstart, idx) + rank
    n_blocks = n_rows // rows
    blk_exp = segment_of(pend, jnp.arange(n_blocks, dtype=I32) * rows)
    n_used = (pend[-1] // rows).astype(I32).reshape(1)
    blk_first = jnp.arange(n_blocks, dtype=I32) * rows - lookup(pstart, blk_exp)
    blk_valid = jnp.clip(lookup(counts, blk_exp) - blk_first, 0, rows)
    return dest.astype(I32), blk_exp.astype(I32), blk_valid.astype(I32), n_used, pstart


def kernel(x, c, w_ada, b_ada, g_mix, w_in, conv_w, conv_b, w_rg_a, b_rg_a, w_rg_x, b_rg_x, lam, w_out,
           g_ffn, w_router, b_router, w_gate_up, b_gate_up, w_down, b_down, g_final):
    bsz, s, d = x.shape
    t = bsz * s
    depth = w_ada.shape[0]
    assert depth == 1, "the combine kernel applies the final norm; one layer only"
    for l in range(depth):
        mod = _ada(c, w_ada[l], b_ada[l]).reshape(bsz, 6, 1, d)
        shift_m, scale_m, gate_m, shift_f, scale_f, gate_f = (mod[:, j] for j in range(6))
        q, k, v, xr, gr = _inproj(x, g_mix[l], shift_m, scale_m, w_in[l].astype(BF16))
        attn = _attention(q, k, v)
        rec = _rglru(xr, gr, conv_w[l], conv_b[l], w_rg_a[l], b_rg_a[l], w_rg_x[l], b_rg_x[l], lam[l])
        x1, h2rows, idx, gates, rank, cnt = _outproj(
            attn, rec, w_out[l].astype(BF16), x, gate_m, g_ffn[l], shift_f, scale_f, w_router[l], b_router[l])
        n_rows = t * TOP_K + N_EXPERTS * EXPERT_ROWS
        dest, blk_exp, blk_valid, n_used, pstart = _routing_plan(idx, rank, cnt[:, 0], n_rows)
        slots = _slots(dest.reshape(-1), n_rows)
        y = _experts(h2rows, slots, blk_exp, blk_valid, n_used, w_gate_up[l], b_gate_up[l], w_down[l],
                     b_down[l])
        x = _combine(y, _combine_plan(idx, rank, gates, pstart), x1, gate_f, g_final)
    return x
```

```python
import jax
import jax.numpy as jnp
from jax import lax
from jax.experimental import pallas as pl
from jax.experimental.pallas import tpu as pltpu

F32 = jnp.float32
BF16 = jnp.bfloat16
I32 = jnp.int32

HEAD_DIM = 64
DILATED_PATTERNS = ((128, 1), (512, 4), (2048, 16))
REC_BLOCKS = 8
CONV_WIDTH = 4
RG_C = 8.0
N_EXPERTS = 32
TOP_K = 4
SWIGLU_LIMIT = 7.0
SWIGLU_ALPHA = 1.702
NORM_EPS = 1e-6
NEG_INF = -1e30
LOG2_E = 1.4426950408889634

LANES = 128
SUBLANES = 8
ROW_CHUNKS = 8
VMEM_LIMIT = 48 * 1024 * 1024

INPROJ_ROWS = 1024
RGLRU_ROWS = 1024
OUTPROJ_ROWS = 1024
ATTN_BLOCK = 128
ATTN_UNROLL = 16
EXPERT_ROWS = 512
COMBINE_ROWS = 512
COMBINE_PIECE = 64


def _params(sem, vmem=VMEM_LIMIT):
    return pltpu.CompilerParams(dimension_semantics=sem, vmem_limit_bytes=vmem)


def _halves(a):
    hi = a.astype(BF16)
    return hi, (a - hi.astype(F32)).astype(BF16)


def _ada_kernel(c_ref, w_ref, b_ref, o_ref):
    c_hi, c_lo = _halves(c_ref[...])
    w_hi, w_lo = _halves(w_ref[...])
    dot = lambda a, b: jnp.dot(a, b, preferred_element_type=F32)
    o_ref[...] = dot(c_hi, w_hi) + (dot(c_lo, w_hi) + dot(c_hi, w_lo)) + b_ref[...]


def _ada(c, w, b):
    bsz, d = c.shape
    n = w.shape[1]
    return pl.pallas_call(
        _ada_kernel,
        grid=(n // d,),
        in_specs=[pl.BlockSpec((bsz, d), lambda j: (0, 0)),
                  pl.BlockSpec((d, d), lambda j: (0, j)),
                  pl.BlockSpec((1, d), lambda j: (0, j))],
        out_specs=pl.BlockSpec((bsz, d), lambda j: (0, j)),
        out_shape=jax.ShapeDtypeStruct((bsz, n), F32),
        compiler_params=_params(("arbitrary",)),
        name="ada",
    )(c, w, b.reshape(1, n))


def _rms_modulate(x, g, shift, scale):
    y = x * lax.rsqrt(jnp.mean(x * x, axis=-1, keepdims=True) + NORM_EPS)
    return (y * g) * (1.0 + scale) + shift


def _inproj_kernel(x_ref, g_ref, sh_ref, sc_ref, w_ref, q_ref, k_ref, v_ref, xr_ref, gr_ref):
    h = _rms_modulate(x_ref[...], g_ref[...], sh_ref[...], sc_ref[...]).astype(BF16)
    outs = (q_ref, k_ref, v_ref, xr_ref, gr_ref)
    width = q_ref.shape[-1]
    for j, o_ref in enumerate(outs):
        z = jnp.dot(h, w_ref[:, j * width:(j + 1) * width], preferred_element_type=F32)
        o_ref[...] = z.astype(o_ref.dtype)


def _inproj(x, g, shift, scale, w_bf):
    bsz, s, d = x.shape
    n = w_bf.shape[1]
    width = n // 5
    tm = INPROJ_ROWS
    row = pl.BlockSpec((None, 1, d), lambda b, i: (b, 0, 0))
    out_blk = pl.BlockSpec((None, tm, width), lambda b, i: (b, i, 0))
    shp = lambda dt: jax.ShapeDtypeStruct((bsz, s, width), dt)
    return pl.pallas_call(
        _inproj_kernel,
        grid=(bsz, s // tm),
        in_specs=[pl.BlockSpec((None, tm, d), lambda b, i: (b, i, 0)),
                  pl.BlockSpec((1, d), lambda b, i: (0, 0)),
                  row, row,
                  pl.BlockSpec((d, n), lambda b, i: (0, 0))],
        out_specs=[out_blk] * 5,
        out_shape=[shp(BF16), shp(BF16), shp(BF16), shp(F32), shp(F32)],
        compiler_params=_params(("arbitrary", "arbitrary")),
        name="inproj",
    )(x, g.reshape(1, d), shift, scale, w_bf)


def _sigmoid(x):
    return 0.5 * jnp.tanh(0.5 * x) + 0.5


def _gelu_tanh(x):
    return 0.5 * x * (1.0 + jnp.tanh(0.7978845608028654 * (x + 0.044715 * (x * x * x))))


def _rglru_kernel(xr_ref, gr_ref, cw_ref, cb_ref, wa_ref, ba_ref, wx_ref, bx_ref, lam_ref,
                  o_ref, xe_ref, a_ref, u_ref, tail_ref, h_ref):
    ts, ch = xr_ref.shape
    pad = SUBLANES

    @pl.when(pl.program_id(1) == 0)
    def _():
        tail_ref[...] = jnp.zeros_like(tail_ref)
        h_ref[...] = jnp.zeros_like(h_ref)

    x = xr_ref[...]
    xe_ref[0:pad, :] = tail_ref[...]
    xe_ref[pad:pad + ts, :] = x
    tail_ref[...] = x[ts - pad:ts, :]
    xc = cb_ref[...] + jnp.zeros((ts, ch), F32)
    for j in range(CONV_WIDTH):
        off = pad - (CONV_WIDTH - 1) + j
        xc = xc + cw_ref[j:j + 1, :] * xe_ref[off:off + ts, :]

    xb = xc.astype(BF16)
    r = _sigmoid(jnp.dot(xb, wa_ref[...], preferred_element_type=F32) + ba_ref[...])
    i = _sigmoid(jnp.dot(xb, wx_ref[...], preferred_element_type=F32) + bx_ref[...])
    nl = -lam_ref[...]
    softplus = jnp.maximum(nl, 0.0) + jnp.log(1.0 + jnp.exp(-jnp.abs(nl)))
    log_a = (-RG_C) * r * softplus
    a = jnp.exp(log_a)
    mult = jnp.sqrt(1.0 - a * a)
    u = mult * (i * xc)

    ng = ts // SUBLANES
    a3 = a.reshape(ng, SUBLANES, ch)
    u3 = u.reshape(ng, SUBLANES, ch)
    rid = lax.broadcasted_iota(I32, (ng, SUBLANES, ch), 1)
    for sft in (1, 2, 4):
        a_s = pltpu.roll(a3, sft, 1)
        u_s = pltpu.roll(u3, sft, 1)
        keep = rid >= sft
        u3 = jnp.where(keep, a3 * u_s + u3, u3)
        a3 = jnp.where(keep, a3 * a_s, a3)
    a_ref[...] = a3.reshape(ts, ch)
    u_ref[...] = u3.reshape(ts, ch)

    inner = 8

    def body(gi, hprev):
        for jj in range(inner):
            base = pl.multiple_of((gi * inner + jj) * SUBLANES, SUBLANES)
            hcur = u_ref[pl.ds(base, SUBLANES), :] + a_ref[pl.ds(base, SUBLANES), :] * hprev
            u_ref[pl.ds(base, SUBLANES), :] = hcur
            hprev = hcur[SUBLANES - 1:SUBLANES, :]
        return hprev

    h_ref[...] = lax.fori_loop(0, ng // inner, body, h_ref[...])
    o_ref[...] = (u_ref[...] * _gelu_tanh(gr_ref[...])).astype(o_ref.dtype)


def _block_diag(w):
    nb, hin, hout = w.shape
    eye = jnp.eye(nb, dtype=w.dtype)
    return (eye[:, None, :, None] * w[:, :, None, :]).reshape(nb * hin, nb * hout)


def _rglru(xr, gr, conv_w, conv_b, w_a, b_a, w_x, b_x, lam):
    bsz, s, ch = xr.shape
    ts = RGLRU_ROWS
    blk = pl.BlockSpec((None, ts, ch), lambda b, t: (b, t, 0))
    full = lambda r, c: pl.BlockSpec((r, c), lambda b, t: (0, 0))
    return pl.pallas_call(
        _rglru_kernel,
        grid=(bsz, s // ts),
        in_specs=[blk, blk, full(CONV_WIDTH, ch), full(1, ch), full(ch, ch), full(1, ch),
                  full(ch, ch), full(1, ch), full(1, ch)],
        out_specs=blk,
        out_shape=jax.ShapeDtypeStruct((bsz, s, ch), BF16),
        scratch_shapes=[pltpu.VMEM((ts + SUBLANES, ch), F32),
                        pltpu.VMEM((ts, ch), F32),
                        pltpu.VMEM((ts, ch), F32),
                        pltpu.VMEM((SUBLANES, ch), F32),
                        pltpu.VMEM((1, ch), F32)],
        compiler_params=_params(("arbitrary", "arbitrary")),
        name="rglru",
    )(xr, gr, conv_w, conv_b.reshape(1, ch), _block_diag(w_a).astype(BF16), b_a.reshape(1, ch),
      _block_diag(w_x).astype(BF16), b_x.reshape(1, ch), lam.reshape(1, ch))


def _attn_kernel(q_ref, k_ref, v_ref, o_ref, f_ref, f4_ref, qs_ref, kp_ref, vp_ref, op_ref, lp_ref,
                 pc_ref, ml_ref):
    s = q_ref.shape[0]
    blk = ATTN_BLOCK
    n_blocks = s // blk
    unroll = ATTN_UNROLL

    lane = lax.broadcasted_iota(I32, (blk, LANES), 1)
    head0 = lane < HEAD_DIM
    head0_wide = lax.broadcasted_iota(I32, (blk, 2 * LANES), 1) % LANES < HEAD_DIM
    qi = lax.broadcasted_iota(I32, (2 * blk, blk), 0) % blk
    kj = lax.broadcasted_iota(I32, (2 * blk, blk), 1)
    prev_ok = kj >= qi
    cur_ok = kj <= qi
    dims = (((1,), (1,)), ((), ()))

    def block_rows(j, dil):
        per_res = (s // dil) // blk
        start = j // per_res + (j % per_res) * (dil * blk)
        if dil == 1:
            return pl.ds(pl.multiple_of(start, blk), blk)
        return pl.ds(start, blk, stride=dil)

    dils = tuple(d for _, d in DILATED_PATTERNS)
    assert dils == (1, 4, 16)

    def permute(src_ref, store):
        f_ref[...] = src_ref[...].astype(F32)

        def body0(j, c):
            store(0, j, f_ref[block_rows(j, 1), :])
            return c

        def body1(j, c):
            x = f_ref[block_rows(j, 4), :]
            f4_ref[pl.ds(pl.multiple_of(j * blk, blk), blk), :] = x
            store(1, j, x)
            return c

        def body2(j, c):
            per_res = (s // 16) // blk
            res, nblk = j // per_res, j % per_res
            start = (res % 4) * (s // 4) + res // 4 + nblk * (4 * blk)
            store(2, j, f4_ref[pl.ds(start, blk, stride=4), :])
            return c

        for body in (body0, body1, body2):
            for j in range(n_blocks):
                body(j, 0)

    def store_q(p, j, x):
        x = x * (HEAD_DIM ** -0.5 * LOG2_E)
        base = pl.multiple_of(j * (2 * blk), 2 * blk)
        qs_ref[p, pl.ds(base, blk), :] = jnp.where(head0, x, 0.0).astype(BF16)
        qs_ref[p, pl.ds(base + blk, blk), :] = jnp.where(head0, 0.0, x).astype(BF16)

    def store_kv(dst_ref):
        def store(p, j, x):
            dst_ref[p, pl.ds(pl.multiple_of((j + 1) * blk, blk), blk), 0:LANES] = x.astype(BF16)
        return store

    for p in range(len(DILATED_PATTERNS)):
        kp_ref[p, 0:blk, :] = jnp.zeros((blk, LANES), BF16)
        vp_ref[p, 0:blk, 0:LANES] = jnp.zeros((blk, LANES), BF16)
        vp_ref[p, :, LANES:] = jnp.ones((s + blk, LANES), BF16)
    permute(q_ref, store_q)
    permute(k_ref, store_kv(kp_ref))
    permute(v_ref, store_kv(vp_ref))

    def kv_rows(j):
        return pl.ds(pl.multiple_of(j * blk, blk), 2 * blk)

    def scores(p, g):
        per_res = (s // dils[p]) // blk
        for u in range(unroll):
            j = g * unroll + u
            no_prev = jnp.where(j % per_res > 0, 0.0, NEG_INF)
            qs = qs_ref[p, pl.ds(pl.multiple_of(j * (2 * blk), 2 * blk), 2 * blk), :]
            sc = lax.dot_general(qs, kp_ref[p, kv_rows(j), :], dims, preferred_element_type=F32)
            s_prev = jnp.where(prev_ok, sc[:, :blk], NEG_INF) + no_prev
            s_cur = jnp.where(cur_ok, sc[:, blk:], NEG_INF)
            m = jnp.max(jnp.maximum(s_prev, s_cur), axis=-1, keepdims=True)
            pc_ref[g % 2, u] = jnp.concatenate(
                [jnp.exp2(s_prev - m), jnp.exp2(s_cur - m)], axis=-1).astype(BF16)
            ml_ref[g % 2, u] = jnp.where(head0, m[:blk], m[blk:])

    def outputs(p, g):
        for u in range(unroll):
            j = g * unroll + u
            rl = jnp.dot(pc_ref[g % 2, u], vp_ref[p, kv_rows(j), :], preferred_element_type=F32)
            rl = jnp.where(head0_wide, rl[:blk], rl[blk:])
            l = rl[:, LANES:]
            if dils[p] == 16:
                per_res = (s // 16) // blk
                res, nblk = j // per_res, j % per_res
                rows = pl.ds((res % 4) * (s // 4) + res // 4 + nblk * (4 * blk), blk, stride=4)
            else:
                rows = block_rows(j, dils[p])
            op_ref[p, rows, :] = rl[:, :LANES] / l
            lp_ref[p, rows, :] = ml_ref[g % 2, u] + jnp.log2(l)

    n_groups = n_blocks // unroll
    assert n_groups % 2 == 0
    n_pat = len(DILATED_PATTERNS)
    scores(0, 0)
    for p, (window, dil) in enumerate(DILATED_PATTERNS):
        assert window // dil == blk

        def body(g, carry, p=p):
            outputs(p, g - 1)
            scores(p, g)
            return carry

        lax.fori_loop(1, n_groups, body, 0)
        outputs(p, n_groups - 1)
        if p + 1 < n_pat:
            scores(p + 1, 0)

    quarter, chunk = s // 4, 256
    for r4 in range(4):
        for l0 in range(0, quarter, chunk):
            tok = pl.ds(r4 + 4 * l0, chunk, stride=4)
            lay = pl.ds(r4 * quarter + l0, chunk)
            l0_, l1_, l2_ = lp_ref[0, tok, :], lp_ref[1, tok, :], lp_ref[2, lay, :]
            m = jnp.maximum(jnp.maximum(l0_, l1_), l2_)
            w0, w1, w2 = jnp.exp2(l0_ - m), jnp.exp2(l1_ - m), jnp.exp2(l2_ - m)
            num = w0 * op_ref[0, tok, :] + w1 * op_ref[1, tok, :] + w2 * op_ref[2, lay, :]
            f_ref[tok, :] = num / (w0 + w1 + w2)
    for c0 in range(0, s, 512):
        o_ref[c0:c0 + 512, :] = f_ref[c0:c0 + 512, :].astype(o_ref.dtype)


def _attention(q, k, v):
    bsz, s, da = q.shape
    blk = pl.BlockSpec((None, s, LANES), lambda b, h: (b, 0, h))
    n_pat = len(DILATED_PATTERNS)
    return pl.pallas_call(
        _attn_kernel,
        grid=(bsz, da // LANES),
        in_specs=[blk, blk, blk],
        out_specs=blk,
        out_shape=jax.ShapeDtypeStruct((bsz, s, da), BF16),
        scratch_shapes=[pltpu.VMEM((s, LANES), F32),
                        pltpu.VMEM((s, LANES), F32),
                        pltpu.VMEM((n_pat, 2 * s, LANES), BF16),
                        pltpu.VMEM((n_pat, s + ATTN_BLOCK, LANES), BF16),
                        pltpu.VMEM((n_pat, s + ATTN_BLOCK, 2 * LANES), BF16),
                        pltpu.VMEM((n_pat, s, LANES), F32),
                        pltpu.VMEM((n_pat, s, LANES), F32),
                        pltpu.VMEM((2, ATTN_UNROLL, 2 * ATTN_BLOCK, 2 * ATTN_BLOCK), BF16),
                        pltpu.VMEM((2, ATTN_UNROLL, ATTN_BLOCK, LANES), F32)],
        compiler_params=_params(("arbitrary", "arbitrary"), 56 * 1024 * 1024),
        name="attn",
    )(q, k, v)


def _outproj_kernel(attn_ref, rec_ref, w_ref, x_ref, gm_ref, g_ref, sh_ref, sc_ref, wr_ref, br_ref,
                    x1_ref, h2_ref, idx_ref, gate_ref, rank_ref, cnt_ref, base_ref, before_ref):
    tm = x_ref.shape[0]
    da = attn_ref.shape[1]

    @pl.when(jnp.logical_and(pl.program_id(0) == 0, pl.program_id(1) == 0))
    def _():
        base_ref[...] = jnp.zeros_like(base_ref)
        tr = lax.broadcasted_iota(I32, (tm, tm), 0)
        tc = lax.broadcasted_iota(I32, (tm, tm), 1)
        before_ref[...] = jnp.where(tr < tc, 1.0, 0.0).astype(BF16)

    y = (jnp.dot(attn_ref[...], w_ref[0:da, :], preferred_element_type=F32)
         + jnp.dot(rec_ref[...], w_ref[da:, :], preferred_element_type=F32))
    x1 = x_ref[...] + gm_ref[...] * y
    x1_ref[...] = x1
    h2 = _rms_modulate(x1, g_ref[...], sh_ref[...], sc_ref[...])
    for c in range(ROW_CHUNKS):
        h2_ref[pl.ds(c, tm, stride=ROW_CHUNKS), :] = h2[:, c * LANES:(c + 1) * LANES]

    def logits(w, h):
        return lax.dot_general(w, h, (((1,), (1,)), ((), ())), preferred_element_type=F32)

    w_hi, w_lo = _halves(wr_ref[...])
    h_hi, h_lo = _halves(h2)
    work = logits(w_hi, h_hi) + (logits(w_hi, h_lo) + logits(w_lo, h_hi)) + br_ref[...]
    n_exp = work.shape[0]
    eid = lax.broadcasted_iota(I32, (n_exp, tm), 0)
    vals, hots = [], []
    for kk in range(TOP_K):
        m = jnp.max(work, axis=0, keepdims=True)
        sel = jnp.min(jnp.where(work == m, eid, n_exp), axis=0, keepdims=True)
        hot = eid == sel
        idx_ref[kk:kk + 1, :] = sel
        vals.append(m)
        hots.append(hot)
        work = jnp.where(hot, -jnp.inf, work)
    exps = [jnp.exp(v - vals[0]) for v in vals]
    den = exps[0] + exps[1] + exps[2] + exps[3]
    for kk in range(TOP_K):
        gate_ref[kk:kk + 1, :] = exps[kk] / den

    cnt = jnp.zeros((n_exp, tm), F32)
    for hot in hots:
        cnt = cnt + hot.astype(F32)
    prior = jnp.dot(cnt.astype(BF16), before_ref[...], preferred_element_type=F32) + base_ref[:, 0:1]
    for kk in range(TOP_K):
        rank_ref[kk:kk + 1, :] = jnp.sum(jnp.where(hots[kk], prior, 0.0), axis=0,
                                         keepdims=True).astype(I32)
    base_ref[...] = base_ref[...] + jnp.sum(cnt, axis=1, keepdims=True)
    cnt_ref[...] = base_ref[...].astype(I32)


def _outproj(attn, rec, w_bf, x, gate_m, g_ffn, shift_f, scale_f, w_router, b_router):
    bsz, s, d = x.shape
    da = attn.shape[-1]
    n_exp = w_router.shape[1]
    tm = OUTPROJ_ROWS
    nt = s // tm
    t = bsz * s
    row = pl.BlockSpec((None, 1, d), lambda b, i: (b, 0, 0))
    small = pl.BlockSpec((TOP_K, tm), lambda b, i: (0, b * nt + i))
    return pl.pallas_call(
        _outproj_kernel,
        grid=(bsz, nt),
        in_specs=[pl.BlockSpec((None, tm, da), lambda b, i: (b, i, 0)),
                  pl.BlockSpec((None, tm, da), lambda b, i: (b, i, 0)),
                  pl.BlockSpec((d, d), lambda b, i: (0, 0)),
                  pl.BlockSpec((None, tm, d), lambda b, i: (b, i, 0)),
                  row,
                  pl.BlockSpec((1, d), lambda b, i: (0, 0)),
                  row, row,
                  pl.BlockSpec((n_exp, d), lambda b, i: (0, 0)),
                  pl.BlockSpec((n_exp, 1), lambda b, i: (0, 0))],
        out_specs=[pl.BlockSpec((None, tm, d), lambda b, i: (b, i, 0)),
                   pl.BlockSpec((tm * ROW_CHUNKS, LANES), lambda b, i: (b * nt + i, 0)),
                   small, small, small,
                   pl.BlockSpec((n_exp, LANES), lambda b, i: (0, 0))],
        out_shape=[jax.ShapeDtypeStruct((bsz, s, d), F32),
                   jax.ShapeDtypeStruct((t * ROW_CHUNKS, LANES), F32),
                   jax.ShapeDtypeStruct((TOP_K, t), I32),
                   jax.ShapeDtypeStruct((TOP_K, t), F32),
                   jax.ShapeDtypeStruct((TOP_K, t), I32),
                   jax.ShapeDtypeStruct((n_exp, LANES), I32)],
        scratch_shapes=[pltpu.VMEM((n_exp, LANES), F32), pltpu.VMEM((tm, tm), BF16)],
        compiler_params=_params(("arbitrary", "arbitrary")),
        name="outproj_router",
    )(attn, rec, w_bf, x, gate_m, g_ffn.reshape(1, d), shift_f, scale_f, w_router.T,
      b_router.reshape(n_exp, 1))


def _row_slice(ref, row):
    return ref.at[pl.ds(pl.multiple_of(row * ROW_CHUNKS, ROW_CHUNKS), ROW_CHUNKS), :]


def _slots_kernel(dest_ref, init_ref, slot_ref, sem):
    n_assign = dest_ref.shape[0]
    unroll = 32

    init = pltpu.make_async_copy(init_ref, slot_ref, sem)
    init.start()
    init.wait()

    def assign_body(i, c):
        for u in range(unroll):
            slot_ref[dest_ref[i * unroll + u]] = i * unroll + u
        return c

    lax.fori_loop(0, n_assign // unroll, assign_body, 0)


def _slots(dest_flat, n_slots):
    smem = pl.BlockSpec(memory_space=pltpu.SMEM)
    return pl.pallas_call(
        _slots_kernel,
        in_specs=[smem, pl.BlockSpec(memory_space=pltpu.VMEM)],
        out_specs=smem,
        out_shape=jax.ShapeDtypeStruct((n_slots,), I32),
        scratch_shapes=[pltpu.SemaphoreType.DMA(())],
        name="slots",
    )(dest_flat, jnp.arange(n_slots, dtype=I32))


GATHER_BUFS = 3


def _experts_kernel(bexp_ref, bvalid_ref, nused_ref, slot_ref, h2_ref, wgu_ref, bgu_ref, wd_ref, bd_ref,
                    y_ref, xg_ref, xs_ref, wgu_bf, wd_bf, gsem):
    b = pl.program_id(0)
    nb = pl.num_programs(0)
    rows = xs_ref.shape[0]
    f = wd_ref.shape[0]
    n_tok = h2_ref.shape[0] // ROW_CHUNKS
    assert n_tok & (n_tok - 1) == 0
    used = b < nused_ref[0]
    changed = jnp.logical_or(b == 0, bexp_ref[b] != bexp_ref[jnp.maximum(b - 1, 0)])

    def buf_rows(ref, buf, i):
        if isinstance(i, int):
            return ref.at[buf, pl.ds(i * ROW_CHUNKS, ROW_CHUNKS), :]
        return ref.at[buf, pl.ds(pl.multiple_of(i * ROW_CHUNKS, ROW_CHUNKS), ROW_CHUNKS), :]

    def fetch_row(base, buf, i):
        tok = slot_ref[base + i] & (n_tok - 1)
        return pltpu.make_async_copy(_row_slice(h2_ref, tok), buf_rows(xg_ref, buf, i), gsem.at[buf])

    def fetch_block(blk, buf, rolled=False):
        base = blk * rows
        if rolled:
            def body(i, c):
                fetch_row(base, buf, i).start()
                return c
            lax.fori_loop(0, rows, body, 0)
        else:
            for i in range(rows):
                fetch_row(base, buf, i).start(priority=i % 2)

    def wait_block(buf):
        pltpu.make_async_copy(xg_ref.at[buf], xg_ref.at[buf], gsem.at[buf]).wait()

    @pl.when(b == 0)
    def _():
        fetch_block(0, 0, rolled=True)
        fetch_block(1, 1, rolled=True)

    gbuf = b % GATHER_BUFS
    n_used = nused_ref[0]

    @pl.when(b < n_used + 2)
    def _():
        wait_block(gbuf)

    def fetch_ahead():
        fetch_block(jnp.minimum(b + 2, nb - 1), (b + 2) % GATHER_BUFS)

    @pl.when(jnp.logical_and(used, changed))
    def _():
        wgu_bf[...] = wgu_ref[...].astype(BF16)
        wd_bf[...] = wd_ref[...].astype(BF16)

    def run_expert(m):
        for c in range(ROW_CHUNKS):
            xs_ref[0:m, c * LANES:(c + 1) * LANES] = (
                xg_ref[gbuf, pl.ds(c, m, stride=ROW_CHUNKS), :].astype(BF16))
        fetch_ahead()
        gu = jnp.dot(xs_ref[0:m, :], wgu_bf[...], preferred_element_type=F32) + bgu_ref[...]
        gate = jnp.minimum(gu[:, :f], SWIGLU_LIMIT)
        up = jnp.clip(gu[:, f:], -SWIGLU_LIMIT, SWIGLU_LIMIT)
        glu = gate * _sigmoid(gate * SWIGLU_ALPHA)
        act = ((up + 1.0) * glu).astype(BF16)
        y = jnp.dot(act, wd_bf[...], preferred_element_type=F32) + bd_ref[...]
        for c in range(ROW_CHUNKS):
            y_ref[pl.ds(c, m, stride=ROW_CHUNKS), :] = y[:, c * LANES:(c + 1) * LANES]
        if m < rows:
            y_ref[m * ROW_CHUNKS:, :] = jnp.zeros(((rows - m) * ROW_CHUNKS, LANES), F32)

    few = bvalid_ref[b] <= rows // 2

    @pl.when(jnp.logical_and(used, jnp.logical_not(few)))
    def _():
        run_expert(rows)

    @pl.when(jnp.logical_and(used, few))
    def _():
        run_expert(rows // 2)

    @pl.when(jnp.logical_not(used))
    def _():
        y_ref[...] = jnp.zeros_like(y_ref)

    for late in (1, 2):
        @pl.when(jnp.logical_and(b == nb - 1, n_used > nb - 3 + late))
        def _(late=late):
            wait_block((b + late) % GATHER_BUFS)


def _experts(h2rows, slots, blk_exp, blk_valid, n_used, w_gate_up, b_gate_up, w_down, b_down):
    n_exp, d, f2 = w_gate_up.shape
    f = w_down.shape[1]
    rows = EXPERT_ROWS
    n_blocks = slots.shape[0] // rows
    assert n_blocks >= GATHER_BUFS
    buf_shape = (rows * ROW_CHUNKS, LANES)
    return pl.pallas_call(
        _experts_kernel,
        grid_spec=pltpu.PrefetchScalarGridSpec(
            num_scalar_prefetch=4,
            grid=(n_blocks,),
            in_specs=[pl.BlockSpec(memory_space=pl.ANY),
                      pl.BlockSpec((None, d, f2), lambda b, e, *_: (e[b], 0, 0)),
                      pl.BlockSpec((None, 1, f2), lambda b, e, *_: (e[b], 0, 0)),
                      pl.BlockSpec((None, f, d), lambda b, e, *_: (e[b], 0, 0)),
                      pl.BlockSpec((None, 1, d), lambda b, e, *_: (e[b], 0, 0))],
            out_specs=pl.BlockSpec(buf_shape, lambda b, e, *_: (b, 0)),
            scratch_shapes=[pltpu.VMEM((GATHER_BUFS,) + buf_shape, F32),
                            pltpu.VMEM((rows, d), BF16),
                            pltpu.VMEM((d, f2), BF16),
                            pltpu.VMEM((f, d), BF16),
                            pltpu.SemaphoreType.DMA((GATHER_BUFS,))]),
        out_shape=jax.ShapeDtypeStruct((slots.shape[0] * ROW_CHUNKS, LANES), F32),
        compiler_params=_params(("arbitrary",), 56 * 1024 * 1024),
        name="experts",
    )(blk_exp, blk_valid, n_used, slots, h2rows, w_gate_up, b_gate_up.reshape(n_exp, 1, f2), w_down,
      b_down.reshape(n_exp, 1, d))


def _combine_kernel(src_ref, cnt_ref, dst_ref, pos_ref, gate_ref, y_ref, x1_ref, gf_ref, g_ref, o_ref,
                    stage_ref, tok_ref, sem):
    i = pl.program_id(0)
    nt = pl.num_programs(0)
    tm = x1_ref.shape[0]
    n_exp = cnt_ref.shape[0] // nt

    def fetch(tile, buf):
        def per_expert(e, c):
            seg = tile * n_exp + e
            cnt, src, dst = cnt_ref[seg], src_ref[seg], dst_ref[seg]

            def piece(done, size):
                return pltpu.make_async_copy(
                    y_ref.at[pl.ds(pl.multiple_of((src + done) * ROW_CHUNKS, ROW_CHUNKS),
                                   size * ROW_CHUNKS), :],
                    stage_ref.at[buf, pl.ds(pl.multiple_of((dst + done) * ROW_CHUNKS, ROW_CHUNKS),
                                            size * ROW_CHUNKS), :],
                    sem.at[buf])

            def whole(j, c2):
                piece(j * COMBINE_PIECE, COMBINE_PIECE).start()
                return c2

            n_whole = cnt // COMBINE_PIECE
            lax.fori_loop(0, n_whole, whole, 0)
            done = n_whole * COMBINE_PIECE
            size = COMBINE_PIECE // 2
            while size >= 1:
                @pl.when((cnt & size) != 0)
                def _(size=size, done=done):
                    piece(done, size).start()

                done = done + (cnt & size)
                size //= 2
            return c
        lax.fori_loop(0, n_exp, per_expert, 0)

    @pl.when(i == 0)
    def _():
        fetch(0, 0)

    @pl.when(i + 1 < nt)
    def _():
        fetch(i + 1, (i + 1) % 2)

    buf = i % 2
    pltpu.make_async_copy(stage_ref.at[buf], stage_ref.at[buf], sem.at[buf]).wait()

    for t in range(tm):
        acc = None
        for kk in range(TOP_K):
            row = stage_ref[buf, pl.ds(pl.multiple_of(pos_ref[0, kk * tm + t], ROW_CHUNKS), ROW_CHUNKS), :]
            term = row * gate_ref[0, kk * tm + t]
            acc = term if acc is None else acc + term
        tok_ref[t * ROW_CHUNKS:(t + 1) * ROW_CHUNKS, :] = acc

    moe = jnp.concatenate([tok_ref[pl.ds(c, tm, stride=ROW_CHUNKS), :] for c in range(ROW_CHUNKS)],
                          axis=-1)
    x2 = x1_ref[...] + gf_ref[...] * moe
    o_ref[...] = (x2 * lax.rsqrt(jnp.mean(x2 * x2, axis=-1, keepdims=True) + NORM_EPS)) * g_ref[...]


def _combine(y, plan, x1, gate_f, g_final):
    bsz, s, d = x1.shape
    tm = COMBINE_ROWS
    nt = bsz * s // tm
    per_b = s // tm
    seg_src, seg_cnt, seg_dst, pos3, gates3 = plan
    per_tile = pl.BlockSpec((None, 1, TOP_K * tm), lambda i, *_: (i, 0, 0), memory_space=pltpu.SMEM)
    return pl.pallas_call(
        _combine_kernel,
        grid_spec=pltpu.PrefetchScalarGridSpec(
            num_scalar_prefetch=3,
            grid=(nt,),
            in_specs=[per_tile, per_tile,
                      pl.BlockSpec(memory_space=pl.ANY),
                      pl.BlockSpec((None, tm, d), lambda i, *_: (i // per_b, i % per_b, 0)),
                      pl.BlockSpec((None, 1, d), lambda i, *_: (i // per_b, 0, 0)),
                      pl.BlockSpec((1, d), lambda i, *_: (0, 0))],
            out_specs=pl.BlockSpec((None, tm, d), lambda i, *_: (i // per_b, i % per_b, 0)),
            scratch_shapes=[pltpu.VMEM((2, TOP_K * tm * ROW_CHUNKS, LANES), F32),
                            pltpu.VMEM((tm * ROW_CHUNKS, LANES), F32),
                            pltpu.SemaphoreType.DMA((2,))]),
        out_shape=jax.ShapeDtypeStruct((bsz, s, d), F32),
        compiler_params=_params(("arbitrary",)),
        name="combine",
    )(seg_src, seg_cnt, seg_dst, pos3, gates3, y, x1, gate_f, g_final.reshape(1, d))


def _combine_plan(idx, rank, gates, pstart):
    tm = COMBINE_ROWS
    t = idx.shape[1]
    nt = t // tm
    experts = jnp.arange(N_EXPERTS, dtype=I32)
    hot = idx.reshape(TOP_K, nt, tm)[..., None] == experts
    tile_cnt = jnp.sum(hot.astype(I32), axis=(0, 2))
    tile_base = jnp.cumsum(tile_cnt, axis=0) - tile_cnt
    stage_at = jnp.cumsum(tile_cnt, axis=1) - tile_cnt
    pos = rank.reshape(TOP_K, nt, tm) + jnp.sum(
        jnp.where(hot, (stage_at - tile_base)[None, :, None, :], 0), axis=-1)
    flat = lambda a: a.reshape(-1).astype(I32)
    return (flat(pstart[None, :] + tile_base), flat(tile_cnt), flat(stage_at),
            (pos * ROW_CHUNKS).transpose(1, 0, 2).reshape(nt, 1, TOP_K * tm).astype(I32),
            gates.reshape(TOP_K, nt, tm).transpose(1, 0, 2).reshape(nt, 1, TOP_K * tm))


def _routing_plan(idx, rank, counts, n_rows):
    rows = EXPERT_ROWS
    experts = jnp.arange(N_EXPERTS, dtype=I32)

    def lookup(table, e):
        return jnp.sum(jnp.where(e[..., None] == experts, table, 0), axis=-1)

    def segment_of(ends, pos):
        return jnp.minimum(jnp.sum((pos[..., None] >= ends).astype(I32), axis=-1), N_EXPERTS - 1)

    padded = ((counts + rows - 1) // rows) * rows
    pend = jnp.cumsum(padded)
    pstart = pend - padded
    dest = lookup(pstart, idx) + rank
    n_blocks = n_rows // rows
    blk_exp = segment_of(pend, jnp.arange(n_blocks, dtype=I32) * rows)
    n_used = (pend[-1] // rows).astype(I32).reshape(1)
    blk_first = jnp.arange(n_blocks, dtype=I32) * rows - lookup(pstart, blk_exp)
    blk_valid = jnp.clip(lookup(counts, blk_exp) - blk_first, 0, rows)
    return dest.astype(I32), blk_exp.astype(I32), blk_valid.astype(I32), n_used, pstart


def kernel(x, c, w_ada, b_ada, g_mix, w_in, conv_w, conv_b, w_rg_a, b_rg_a, w_rg_x, b_rg_x, lam, w_out,
           g_ffn, w_router, b_router, w_gate_up, b_gate_up, w_down, b_down, g_final):
    bsz, s, d = x.shape
    t = bsz * s
    depth = w_ada.shape[0]
    assert depth == 1, "the combine kernel applies the final norm; one layer only"
    for l in range(depth):
        mod = _ada(c, w_ada[l], b_ada[l]).reshape(bsz, 6, 1, d)
        shift_m, scale_m, gate_m, shift_f, scale_f, gate_f = (mod[:, j] for j in range(6))
        q, k, v, xr, gr = _inproj(x, g_mix[l], shift_m, scale_m, w_in[l].astype(BF16))
        attn = _attention(q, k, v)
        rec = _rglru(xr, gr, conv_w[l], conv_b[l], w_rg_a[l], b_rg_a[l], w_rg_x[l], b_rg_x[l], lam[l])
        x1, h2rows, idx, gates, rank, cnt = _outproj(
            attn, rec, w_out[l].astype(BF16), x, gate_m, g_ffn[l], shift_f, scale_f, w_router[l], b_router[l])
        n_rows = t * TOP_K + N_EXPERTS * EXPERT_ROWS
        dest, blk_exp, blk_valid, n_used, pstart = _routing_plan(idx, rank, cnt[:, 0], n_rows)
        slots = _slots(dest.reshape(-1), n_rows)
        y = _experts(h2rows, slots, blk_exp, blk_valid, n_used, w_gate_up[l], b_gate_up[l], w_down[l],
                     b_down[l])
        x = _combine(y, _combine_plan(idx, rank, gates, pstart), x1, gate_f, g_final)
    return x
```

```python
import jax
import jax.numpy as jnp
from jax import lax
from jax.experimental import pallas as pl
from jax.experimental.pallas import tpu as pltpu

F32 = jnp.float32
BF16 = jnp.bfloat16
I32 = jnp.int32

HEAD_DIM = 64
DILATED_PATTERNS = ((128, 1), (512, 4), (2048, 16))
REC_BLOCKS = 8
CONV_WIDTH = 4
RG_C = 8.0
N_EXPERTS = 32
TOP_K = 4
SWIGLU_LIMIT = 7.0
SWIGLU_ALPHA = 1.702
NORM_EPS = 1e-6
NEG_INF = -1e30
LOG2_E = 1.4426950408889634

LANES = 128
SUBLANES = 8
ROW_CHUNKS = 8
VMEM_LIMIT = 48 * 1024 * 1024

INPROJ_ROWS = 1024
RGLRU_ROWS = 1024
OUTPROJ_ROWS = 1024
ATTN_BLOCK = 128
ATTN_UNROLL = 16
EXPERT_ROWS = 512
COMBINE_ROWS = 512
COMBINE_PIECE = 64


def _params(sem, vmem=VMEM_LIMIT):
    return pltpu.CompilerParams(dimension_semantics=sem, vmem_limit_bytes=vmem)


def _halves(a):
    hi = a.astype(BF16)
    return hi, (a - hi.astype(F32)).astype(BF16)


def _ada_kernel(c_ref, w_ref, b_ref, o_ref):
    c_hi, c_lo = _halves(c_ref[...])
    w_hi, w_lo = _halves(w_ref[...])
    dot = lambda a, b: jnp.dot(a, b, preferred_element_type=F32)
    o_ref[...] = dot(c_hi, w_hi) + (dot(c_lo, w_hi) + dot(c_hi, w_lo)) + b_ref[...]


def _ada(c, w, b):
    bsz, d = c.shape
    n = w.shape[1]
    return pl.pallas_call(
        _ada_kernel,
        grid=(n // d,),
        in_specs=[pl.BlockSpec((bsz, d), lambda j: (0, 0)),
                  pl.BlockSpec((d, d), lambda j: (0, j)),
                  pl.BlockSpec((1, d), lambda j: (0, j))],
        out_specs=pl.BlockSpec((bsz, d), lambda j: (0, j)),
        out_shape=jax.ShapeDtypeStruct((bsz, n), F32),
        compiler_params=_params(("arbitrary",)),
        name="ada",
    )(c, w, b.reshape(1, n))


def _rms_modulate(x, g, shift, scale):
    y = x * lax.rsqrt(jnp.mean(x * x, axis=-1, keepdims=True) + NORM_EPS)
    return (y * g) * (1.0 + scale) + shift


def _inproj_kernel(x_ref, g_ref, sh_ref, sc_ref, w_ref, q_ref, k_ref, v_ref, xr_ref, gr_ref):
    h = _rms_modulate(x_ref[...], g_ref[...], sh_ref[...], sc_ref[...]).astype(BF16)
    outs = (q_ref, k_ref, v_ref, xr_ref, gr_ref)
    width = q_ref.shape[-1]
    for j, o_ref in enumerate(outs):
        z = jnp.dot(h, w_ref[:, j * width:(j + 1) * width], preferred_element_type=F32)
        o_ref[...] = z.astype(o_ref.dtype)


def _inproj(x, g, shift, scale, w_bf):
    bsz, s, d = x.shape
    n = w_bf.shape[1]
    width = n // 5
    tm = INPROJ_ROWS
    row = pl.BlockSpec((None, 1, d), lambda b, i: (b, 0, 0))
    out_blk = pl.BlockSpec((None, tm, width), lambda b, i: (b, i, 0))
    shp = lambda dt: jax.ShapeDtypeStruct((bsz, s, width), dt)
    return pl.pallas_call(
        _inproj_kernel,
        grid=(bsz, s // tm),
        in_specs=[pl.BlockSpec((None, tm, d), lambda b, i: (b, i, 0)),
                  pl.BlockSpec((1, d), lambda b, i: (0, 0)),
                  row, row,
                  pl.BlockSpec((d, n), lambda b, i: (0, 0))],
        out_specs=[out_blk] * 5,
        out_shape=[shp(BF16), shp(BF16), shp(BF16), shp(F32), shp(F32)],
        compiler_params=_params(("arbitrary", "arbitrary")),
        name="inproj",
    )(x, g.reshape(1, d), shift, scale, w_bf)


def _sigmoid(x):
    return 0.5 * jnp.tanh(0.5 * x) + 0.5


def _gelu_tanh(x):
    return 0.5 * x * (1.0 + jnp.tanh(0.7978845608028654 * (x + 0.044715 * (x * x * x))))


def _rglru_kernel(xr_ref, gr_ref, cw_ref, cb_ref, wa_ref, ba_ref, wx_ref, bx_ref, lam_ref,
                  o_ref, xe_ref, a_ref, u_ref, tail_ref, h_ref):
    ts, ch = xr_ref.shape
    pad = SUBLANES

    @pl.when(pl.program_id(1) == 0)
    def _():
        tail_ref[...] = jnp.zeros_like(tail_ref)
        h_ref[...] = jnp.zeros_like(h_ref)

    x = xr_ref[...]
    xe_ref[0:pad, :] = tail_ref[...]
    xe_ref[pad:pad + ts, :] = x
    tail_ref[...] = x[ts - pad:ts, :]
    xc = cb_ref[...] + jnp.zeros((ts, ch), F32)
    for j in range(CONV_WIDTH):
        off = pad - (CONV_WIDTH - 1) + j
        xc = xc + cw_ref[j:j + 1, :] * xe_ref[off:off + ts, :]

    xb = xc.astype(BF16)
    r = _sigmoid(jnp.dot(xb, wa_ref[...], preferred_element_type=F32) + ba_ref[...])
    i = _sigmoid(jnp.dot(xb, wx_ref[...], preferred_element_type=F32) + bx_ref[...])
    nl = -lam_ref[...]
    softplus = jnp.maximum(nl, 0.0) + jnp.log(1.0 + jnp.exp(-jnp.abs(nl)))
    log_a = (-RG_C) * r * softplus
    a = jnp.exp(log_a)
    mult = jnp.sqrt(1.0 - a * a)
    u = mult * (i * xc)

    ng = ts // SUBLANES
    a3 = a.reshape(ng, SUBLANES, ch)
    u3 = u.reshape(ng, SUBLANES, ch)
    rid = lax.broadcasted_iota(I32, (ng, SUBLANES, ch), 1)
    for sft in (1, 2, 4):
        a_s = pltpu.roll(a3, sft, 1)
        u_s = pltpu.roll(u3, sft, 1)
        keep = rid >= sft
        u3 = jnp.where(keep, a3 * u_s + u3, u3)
        a3 = jnp.where(keep, a3 * a_s, a3)
    a_ref[...] = a3.reshape(ts, ch)
    u_ref[...] = u3.reshape(ts, ch)

    inner = 8

    def body(gi, hprev):
        for jj in range(inner):
            base = pl.multiple_of((gi * inner + jj) * SUBLANES, SUBLANES)
            hcur = u_ref[pl.ds(base, SUBLANES), :] + a_ref[pl.ds(base, SUBLANES), :] * hprev
            u_ref[pl.ds(base, SUBLANES), :] = hcur
            hprev = hcur[SUBLANES - 1:SUBLANES, :]
        return hprev

    h_ref[...] = lax.fori_loop(0, ng // inner, body, h_ref[...])
    o_ref[...] = (u_ref[...] * _gelu_tanh(gr_ref[...])).astype(o_ref.dtype)


def _block_diag(w):
    nb, hin, hout = w.shape
    eye = jnp.eye(nb, dtype=w.dtype)
    return (eye[:, None, :, None] * w[:, :, None, :]).reshape(nb * hin, nb * hout)


def _rglru(xr, gr, conv_w, conv_b, w_a, b_a, w_x, b_x, lam):
    bsz, s, ch = xr.shape
    ts = RGLRU_ROWS
    blk = pl.BlockSpec((None, ts, ch), lambda b, t: (b, t, 0))
    full = lambda r, c: pl.BlockSpec((r, c), lambda b, t: (0, 0))
    return pl.pallas_call(
        _rglru_kernel,
        grid=(bsz, s // ts),
        in_specs=[blk, blk, full(CONV_WIDTH, ch), full(1, ch), full(ch, ch), full(1, ch),
                  full(ch, ch), full(1, ch), full(1, ch)],
        out_specs=blk,
        out_shape=jax.ShapeDtypeStruct((bsz, s, ch), BF16),
        scratch_shapes=[pltpu.VMEM((ts + SUBLANES, ch), F32),
                        pltpu.VMEM((ts, ch), F32),
                        pltpu.VMEM((ts, ch), F32),
                        pltpu.VMEM((SUBLANES, ch), F32),
                        pltpu.VMEM((1, ch), F32)],
        compiler_params=_params(("arbitrary", "arbitrary")),
        name="rglru",
    )(xr, gr, conv_w, conv_b.reshape(1, ch), _block_diag(w_a).astype(BF16), b_a.reshape(1, ch),
      _block_diag(w_x).astype(BF16), b_x.reshape(1, ch), lam.reshape(1, ch))


def _attn_kernel(q_ref, k_ref, v_ref, o_ref, f_ref, f4_ref, qs_ref, kp_ref, vp_ref, op_ref, lp_ref,
                 pc_ref, ml_ref):
    s = q_ref.shape[0]
    blk = ATTN_BLOCK
    n_blocks = s // blk
    unroll = ATTN_UNROLL

    lane = lax.broadcasted_iota(I32, (blk, LANES), 1)
    head0 = lane < HEAD_DIM
    head0_wide = lax.broadcasted_iota(I32, (blk, 2 * LANES), 1) % LANES < HEAD_DIM
    qi = lax.broadcasted_iota(I32, (2 * blk, blk), 0) % blk
    kj = lax.broadcasted_iota(I32, (2 * blk, blk), 1)
    prev_ok = kj >= qi
    cur_ok = kj <= qi
    dims = (((1,), (1,)), ((), ()))

    def block_rows(j, dil):
        per_res = (s // dil) // blk
        start = j // per_res + (j % per_res) * (dil * blk)
        if dil == 1:
            return pl.ds(pl.multiple_of(start, blk), blk)
        return pl.ds(start, blk, stride=dil)

    dils = tuple(d for _, d in DILATED_PATTERNS)
    assert dils == (1, 4, 16)

    def permute(src_ref, store):
        f_ref[...] = src_ref[...].astype(F32)

        def body0(j, c):
            store(0, j, f_ref[block_rows(j, 1), :])
            return c

        def body1(j, c):
            x = f_ref[block_rows(j, 4), :]
            f4_ref[pl.ds(pl.multiple_of(j * blk, blk), blk), :] = x
            store(1, j, x)
            return c

        def body2(j, c):
            per_res = (s // 16) // blk
            res, nblk = j // per_res, j % per_res
            start = (res % 4) * (s // 4) + res // 4 + nblk * (4 * blk)
            store(2, j, f4_ref[pl.ds(start, blk, stride=4), :])
            return c

        for body in (body0, body1, body2):
            for j in range(n_blocks):
                body(j, 0)

    def store_q(p, j, x):
        x = x * (HEAD_DIM ** -0.5 * LOG2_E)
        base = pl.multiple_of(j * (2 * blk), 2 * blk)
        qs_ref[p, pl.ds(base, blk), :] = jnp.where(head0, x, 0.0).astype(BF16)
        qs_ref[p, pl.ds(base + blk, blk), :] = jnp.where(head0, 0.0, x).astype(BF16)

    def store_kv(dst_ref):
        def store(p, j, x):
            dst_ref[p, pl.ds(pl.multiple_of((j + 1) * blk, blk), blk), 0:LANES] = x.astype(BF16)
        return store

    for p in range(len(DILATED_PATTERNS)):
        kp_ref[p, 0:blk, :] = jnp.zeros((blk, LANES), BF16)
        vp_ref[p, 0:blk, 0:LANES] = jnp.zeros((blk, LANES), BF16)
        vp_ref[p, :, LANES:] = jnp.ones((s + blk, LANES), BF16)
    permute(q_ref, store_q)
    permute(k_ref, store_kv(kp_ref))
    permute(v_ref, store_kv(vp_ref))

    def kv_rows(j):
        return pl.ds(pl.multiple_of(j * blk, blk), 2 * blk)

    def scores(p, g):
        per_res = (s // dils[p]) // blk
        for u in range(unroll):
            j = g * unroll + u
            no_prev = jnp.where(j % per_res > 0, 0.0, NEG_INF)
            qs = qs_ref[p, pl.ds(pl.multiple_of(j * (2 * blk), 2 * blk), 2 * blk), :]
            sc = lax.dot_general(qs, kp_ref[p, kv_rows(j), :], dims, preferred_element_type=F32)
            s_prev = jnp.where(prev_ok, sc[:, :blk], NEG_INF) + no_prev
            s_cur = jnp.where(cur_ok, sc[:, blk:], NEG_INF)
            m = jnp.max(jnp.maximum(s_prev, s_cur), axis=-1, keepdims=True)
            pc_ref[g % 2, u] = jnp.concatenate(
                [jnp.exp2(s_prev - m), jnp.exp2(s_cur - m)], axis=-1).astype(BF16)
            ml_ref[g % 2, u] = jnp.where(head0, m[:blk], m[blk:])

    def outputs(p, g):
        for u in range(unroll):
            j = g * unroll + u
            rl = jnp.dot(pc_ref[g % 2, u], vp_ref[p, kv_rows(j), :], preferred_element_type=F32)
            rl = jnp.where(head0_wide, rl[:blk], rl[blk:])
            l = rl[:, LANES:]
            if dils[p] == 16:
                per_res = (s // 16) // blk
                res, nblk = j // per_res, j % per_res
                rows = pl.ds((res % 4) * (s // 4) + res // 4 + nblk * (4 * blk), blk, stride=4)
            else:
                rows = block_rows(j, dils[p])
            op_ref[p, rows, :] = rl[:, :LANES] / l
            lp_ref[p, rows, :] = ml_ref[g % 2, u] + jnp.log2(l)

    n_groups = n_blocks // unroll
    assert n_groups % 2 == 0
    n_pat = len(DILATED_PATTERNS)
    scores(0, 0)
    for p, (window, dil) in enumerate(DILATED_PATTERNS):
        assert window // dil == blk

        def body(g, carry, p=p):
            outputs(p, g - 1)
            scores(p, g)
            return carry

        lax.fori_loop(1, n_groups, body, 0)
        outputs(p, n_groups - 1)
        if p + 1 < n_pat:
            scores(p + 1, 0)

    quarter, chunk = s // 4, 256
    for r4 in range(4):
        for l0 in range(0, quarter, chunk):
            tok = pl.ds(r4 + 4 * l0, chunk, stride=4)
            lay = pl.ds(r4 * quarter + l0, chunk)
            l0_, l1_, l2_ = lp_ref[0, tok, :], lp_ref[1, tok, :], lp_ref[2, lay, :]
            m = jnp.maximum(jnp.maximum(l0_, l1_), l2_)
            w0, w1, w2 = jnp.exp2(l0_ - m), jnp.exp2(l1_ - m), jnp.exp2(l2_ - m)
            num = w0 * op_ref[0, tok, :] + w1 * op_ref[1, tok, :] + w2 * op_ref[2, lay, :]
            f_ref[tok, :] = num / (w0 + w1 + w2)
    for c0 in range(0, s, 512):
        o_ref[c0:c0 + 512, :] = f_ref[c0:c0 + 512, :].astype(o_ref.dtype)


def _attention(q, k, v):
    bsz, s, da = q.shape
    blk = pl.BlockSpec((None, s, LANES), lambda b, h: (b, 0, h))
    n_pat = len(DILATED_PATTERNS)
    return pl.pallas_call(
        _attn_kernel,
        grid=(bsz, da // LANES),
        in_specs=[blk, blk, blk],
        out_specs=blk,
        out_shape=jax.ShapeDtypeStruct((bsz, s, da), BF16),
        scratch_shapes=[pltpu.VMEM((s, LANES), F32),
                        pltpu.VMEM((s, LANES), F32),
                        pltpu.VMEM((n_pat, 2 * s, LANES), BF16),
                        pltpu.VMEM((n_pat, s + ATTN_BLOCK, LANES), BF16),
                        pltpu.VMEM((n_pat, s + ATTN_BLOCK, 2 * LANES), BF16),
                        pltpu.VMEM((n_pat, s, LANES), F32),
                        pltpu.VMEM((n_pat, s, LANES), F32),
                        pltpu.VMEM((2, ATTN_UNROLL, 2 * ATTN_BLOCK, 2 * ATTN_BLOCK), BF16),
                        pltpu.VMEM((2, ATTN_UNROLL, ATTN_BLOCK, LANES), F32)],
        compiler_params=_params(("arbitrary", "arbitrary"), 56 * 1024 * 1024),
        name="attn",
    )(q, k, v)


def _outproj_kernel(attn_ref, rec_ref, w_ref, x_ref, gm_ref, g_ref, sh_ref, sc_ref, wr_ref, br_ref,
                    x1_ref, h2_ref, idx_ref, gate_ref, rank_ref, cnt_ref, base_ref, before_ref):
    tm = x_ref.shape[0]
    da = attn_ref.shape[1]

    @pl.when(jnp.logical_and(pl.program_id(0) == 0, pl.program_id(1) == 0))
    def _():
        base_ref[...] = jnp.zeros_like(base_ref)
        tr = lax.broadcasted_iota(I32, (tm, tm), 0)
        tc = lax.broadcasted_iota(I32, (tm, tm), 1)
        before_ref[...] = jnp.where(tr < tc, 1.0, 0.0).astype(BF16)

    y = (jnp.dot(attn_ref[...], w_ref[0:da, :], preferred_element_type=F32)
         + jnp.dot(rec_ref[...], w_ref[da:, :], preferred_element_type=F32))
    x1 = x_ref[...] + gm_ref[...] * y
    x1_ref[...] = x1
    h2 = _rms_modulate(x1, g_ref[...], sh_ref[...], sc_ref[...])
    for c in range(ROW_CHUNKS):
        h2_ref[pl.ds(c, tm, stride=ROW_CHUNKS), :] = h2[:, c * LANES:(c + 1) * LANES]

    def logits(w, h):
        return lax.dot_general(w, h, (((1,), (1,)), ((), ())), preferred_element_type=F32)

    w_hi, w_lo = _halves(wr_ref[...])
    h_hi, h_lo = _halves(h2)
    work = logits(w_hi, h_hi) + (logits(w_hi, h_lo) + logits(w_lo, h_hi)) + br_ref[...]
    n_exp = work.shape[0]
    eid = lax.broadcasted_iota(I32, (n_exp, tm), 0)
    vals, hots = [], []
    for kk in range(TOP_K):
        m = jnp.max(work, axis=0, keepdims=True)
        sel = jnp.min(jnp.where(work == m, eid, n_exp), axis=0, keepdims=True)
        hot = eid == sel
        idx_ref[kk:kk + 1, :] = sel
        vals.append(m)
        hots.append(hot)
        work = jnp.where(hot, -jnp.inf, work)
    exps = [jnp.exp(v - vals[0]) for v in vals]
    den = exps[0] + exps[1] + exps[2] + exps[3]
    for kk in range(TOP_K):
        gate_ref[kk:kk + 1, :] = exps[kk] / den

    cnt = jnp.zeros((n_exp, tm), F32)
    for hot in hots:
        cnt = cnt + hot.astype(F32)
    prior = jnp.dot(cnt.astype(BF16), before_ref[...], preferred_element_type=F32) + base_ref[:, 0:1]
    for kk in range(TOP_K):
        rank_ref[kk:kk + 1, :] = jnp.sum(jnp.where(hots[kk], prior, 0.0), axis=0,
                                         keepdims=True).astype(I32)
    base_ref[...] = base_ref[...] + jnp.sum(cnt, axis=1, keepdims=True)
    cnt_ref[...] = base_ref[...].astype(I32)


def _outproj(attn, rec, w_bf, x, gate_m, g_ffn, shift_f, scale_f, w_router, b_router):
    bsz, s, d = x.shape
    da = attn.shape[-1]
    n_exp = w_router.shape[1]
    tm = OUTPROJ_ROWS
    nt = s // tm
    t = bsz * s
    row = pl.BlockSpec((None, 1, d), lambda b, i: (b, 0, 0))
    small = pl.BlockSpec((TOP_K, tm), lambda b, i: (0, b * nt + i))
    return pl.pallas_call(
        _outproj_kernel,
        grid=(bsz, nt),
        in_specs=[pl.BlockSpec((None, tm, da), lambda b, i: (b, i, 0)),
                  pl.BlockSpec((None, tm, da), lambda b, i: (b, i, 0)),
                  pl.BlockSpec((d, d), lambda b, i: (0, 0)),
                  pl.BlockSpec((None, tm, d), lambda b, i: (b, i, 0)),
                  row,
                  pl.BlockSpec((1, d), lambda b, i: (0, 0)),
                  row, row,
                  pl.BlockSpec((n_exp, d), lambda b, i: (0, 0)),
                  pl.BlockSpec((n_exp, 1), lambda b, i: (0, 0))],
        out_specs=[pl.BlockSpec((None, tm, d), lambda b, i: (b, i, 0)),
                   pl.BlockSpec((tm * ROW_CHUNKS, LANES), lambda b, i: (b * nt + i, 0)),
                   small, small, small,
                   pl.BlockSpec((n_exp, LANES), lambda b, i: (0, 0))],
        out_shape=[jax.ShapeDtypeStruct((bsz, s, d), F32),
                   jax.ShapeDtypeStruct((t * ROW_CHUNKS, LANES), F32),
                   jax.ShapeDtypeStruct((TOP_K, t), I32),
                   jax.ShapeDtypeStruct((TOP_K, t), F32),
                   jax.ShapeDtypeStruct((TOP_K, t), I32),
                   jax.ShapeDtypeStruct((n_exp, LANES), I32)],
        scratch_shapes=[pltpu.VMEM((n_exp, LANES), F32), pltpu.VMEM((tm, tm), BF16)],
        compiler_params=_params(("arbitrary", "arbitrary")),
        name="outproj_router",
    )(attn, rec, w_bf, x, gate_m, g_ffn.reshape(1, d), shift_f, scale_f, w_router.T,
      b_router.reshape(n_exp, 1))


def _row_slice(ref, row):
    return ref.at[pl.ds(pl.multiple_of(row * ROW_CHUNKS, ROW_CHUNKS), ROW_CHUNKS), :]


def _slots_kernel(dest_ref, init_ref, slot_ref, sem):
    n_assign = dest_ref.shape[0]
    unroll = 32

    init = pltpu.make_async_copy(init_ref, slot_ref, sem)
    init.start()
    init.wait()

    def assign_body(i, c):
        for u in range(unroll):
            slot_ref[dest_ref[i * unroll + u]] = i * unroll + u
        return c

    lax.fori_loop(0, n_assign // unroll, assign_body, 0)


def _slots(dest_flat, n_slots):
    smem = pl.BlockSpec(memory_space=pltpu.SMEM)
    return pl.pallas_call(
        _slots_kernel,
        in_specs=[smem, pl.BlockSpec(memory_space=pltpu.VMEM)],
        out_specs=smem,
        out_shape=jax.ShapeDtypeStruct((n_slots,), I32),
        scratch_shapes=[pltpu.SemaphoreType.DMA(())],
        name="slots",
    )(dest_flat, jnp.arange(n_slots, dtype=I32))


GATHER_BUFS = 3


def _experts_kernel(bexp_ref, bvalid_ref, nused_ref, slot_ref, h2_ref, wgu_ref, bgu_ref, wd_ref, bd_ref,
                    y_ref, xg_ref, xs_ref, wgu_bf, wd_bf, gsem):
    b = pl.program_id(0)
    nb = pl.num_programs(0)
    rows = xs_ref.shape[0]
    f = wd_ref.shape[0]
    n_tok = h2_ref.shape[0] // ROW_CHUNKS
    assert n_tok & (n_tok - 1) == 0
    used = b < nused_ref[0]
    changed = jnp.logical_or(b == 0, bexp_ref[b] != bexp_ref[jnp.maximum(b - 1, 0)])

    def buf_rows(ref, buf, i):
        if isinstance(i, int):
            return ref.at[buf, pl.ds(i * ROW_CHUNKS, ROW_CHUNKS), :]
        return ref.at[buf, pl.ds(pl.multiple_of(i * ROW_CHUNKS, ROW_CHUNKS), ROW_CHUNKS), :]

    def fetch_row(base, buf, i):
        tok = slot_ref[base + i] & (n_tok - 1)
        return pltpu.make_async_copy(_row_slice(h2_ref, tok), buf_rows(xg_ref, buf, i), gsem.at[buf])

    def fetch_block(blk, buf, rolled=False):
        base = blk * rows
        if rolled:
            def body(i, c):
                fetch_row(base, buf, i).start()
                return c
            lax.fori_loop(0, rows, body, 0)
        else:
            for i in range(rows):
                fetch_row(base, buf, i).start(priority=i % 2)

    def wait_block(buf):
        pltpu.make_async_copy(xg_ref.at[buf], xg_ref.at[buf], gsem.at[buf]).wait()

    @pl.when(b == 0)
    def _():
        fetch_block(0, 0, rolled=True)
        fetch_block(1, 1, rolled=True)

    gbuf = b % GATHER_BUFS
    n_used = nused_ref[0]

    @pl.when(b < n_used + 2)
    def _():
        wait_block(gbuf)

    def fetch_ahead():
        fetch_block(jnp.minimum(b + 2, nb - 1), (b + 2) % GATHER_BUFS)

    @pl.when(jnp.logical_and(used, changed))
    def _():
        wgu_bf[...] = wgu_ref[...].astype(BF16)
        wd_bf[...] = wd_ref[...].astype(BF16)

    def run_expert(m):
        for c in range(ROW_CHUNKS):
            xs_ref[0:m, c * LANES:(c + 1) * LANES] = (
                xg_ref[gbuf, pl.ds(c, m, stride=ROW_CHUNKS), :].astype(BF16))
        fetch_ahead()
        x = xs_ref[0:m, :]
        y = bd_ref[...]
        fh = f // 2
        for c in range(2):
            lo, hi = c * fh, (c + 1) * fh
            gate = jnp.dot(x, wgu_bf[:, lo:hi], preferred_element_type=F32) + bgu_ref[:, lo:hi]
            up = jnp.dot(x, wgu_bf[:, f + lo:f + hi], preferred_element_type=F32) + bgu_ref[:, f + lo:f + hi]
            gate = jnp.minimum(gate, SWIGLU_LIMIT)
            up = jnp.clip(up, -SWIGLU_LIMIT, SWIGLU_LIMIT)
            act = ((up + 1.0) * (gate * _sigmoid(gate * SWIGLU_ALPHA))).astype(BF16)
            y = y + jnp.dot(act, wd_bf[lo:hi, :], preferred_element_type=F32)
        for c in range(ROW_CHUNKS):
            y_ref[pl.ds(c, m, stride=ROW_CHUNKS), :] = y[:, c * LANES:(c + 1) * LANES]
        if m < rows:
            y_ref[m * ROW_CHUNKS:, :] = jnp.zeros(((rows - m) * ROW_CHUNKS, LANES), F32)

    few = bvalid_ref[b] <= rows // 2

    @pl.when(jnp.logical_and(used, jnp.logical_not(few)))
    def _():
        run_expert(rows)

    @pl.when(jnp.logical_and(used, few))
    def _():
        run_expert(rows // 2)

    @pl.when(jnp.logical_not(used))
    def _():
        y_ref[...] = jnp.zeros_like(y_ref)

    for late in (1, 2):
        @pl.when(jnp.logical_and(b == nb - 1, n_used > nb - 3 + late))
        def _(late=late):
            wait_block((b + late) % GATHER_BUFS)


def _experts(h2rows, slots, blk_exp, blk_valid, n_used, w_gate_up, b_gate_up, w_down, b_down):
    n_exp, d, f2 = w_gate_up.shape
    f = w_down.shape[1]
    rows = EXPERT_ROWS
    n_blocks = slots.shape[0] // rows
    assert n_blocks >= GATHER_BUFS
    buf_shape = (rows * ROW_CHUNKS, LANES)
    return pl.pallas_call(
        _experts_kernel,
        grid_spec=pltpu.PrefetchScalarGridSpec(
            num_scalar_prefetch=4,
            grid=(n_blocks,),
            in_specs=[pl.BlockSpec(memory_space=pl.ANY),
                      pl.BlockSpec((None, d, f2), lambda b, e, *_: (e[b], 0, 0)),
                      pl.BlockSpec((None, 1, f2), lambda b, e, *_: (e[b], 0, 0)),
                      pl.BlockSpec((None, f, d), lambda b, e, *_: (e[b], 0, 0)),
                      pl.BlockSpec((None, 1, d), lambda b, e, *_: (e[b], 0, 0))],
            out_specs=pl.BlockSpec(buf_shape, lambda b, e, *_: (b, 0)),
            scratch_shapes=[pltpu.VMEM((GATHER_BUFS,) + buf_shape, F32),
                            pltpu.VMEM((rows, d), BF16),
                            pltpu.VMEM((d, f2), BF16),
                            pltpu.VMEM((f, d), BF16),
                            pltpu.SemaphoreType.DMA((GATHER_BUFS,))]),
        out_shape=jax.ShapeDtypeStruct((slots.shape[0] * ROW_CHUNKS, LANES), F32),
        compiler_params=_params(("arbitrary",), 56 * 1024 * 1024),
        name="experts",
    )(blk_exp, blk_valid, n_used, slots, h2rows, w_gate_up, b_gate_up.reshape(n_exp, 1, f2), w_down,
      b_down.reshape(n_exp, 1, d))


def _combine_kernel(src_ref, cnt_ref, dst_ref, pos_ref, gate_ref, y_ref, x1_ref, gf_ref, g_ref, o_ref,
                    stage_ref, tok_ref, sem):
    i = pl.program_id(0)
    nt = pl.num_programs(0)
    tm = x1_ref.shape[0]
    n_exp = cnt_ref.shape[0] // nt

    def fetch(tile, buf):
        def per_expert(e, c):
            seg = tile * n_exp + e
            cnt, src, dst = cnt_ref[seg], src_ref[seg], dst_ref[seg]

            def piece(done, size):
                return pltpu.make_async_copy(
                    y_ref.at[pl.ds(pl.multiple_of((src + done) * ROW_CHUNKS, ROW_CHUNKS),
                                   size * ROW_CHUNKS), :],
                    stage_ref.at[buf, pl.ds(pl.multiple_of((dst + done) * ROW_CHUNKS, ROW_CHUNKS),
                                            size * ROW_CHUNKS), :],
                    sem.at[buf])

            def whole(j, c2):
                piece(j * COMBINE_PIECE, COMBINE_PIECE).start()
                return c2

            n_whole = cnt // COMBINE_PIECE
            lax.fori_loop(0, n_whole, whole, 0)
            done = n_whole * COMBINE_PIECE
            size = COMBINE_PIECE // 2
            while size >= 1:
                @pl.when((cnt & size) != 0)
                def _(size=size, done=done):
                    piece(done, size).start()

                done = done + (cnt & size)
                size //= 2
            return c
        lax.fori_loop(0, n_exp, per_expert, 0)

    @pl.when(i == 0)
    def _():
        fetch(0, 0)

    @pl.when(i + 1 < nt)
    def _():
        fetch(i + 1, (i + 1) % 2)

    buf = i % 2
    pltpu.make_async_copy(stage_ref.at[buf], stage_ref.at[buf], sem.at[buf]).wait()

    for t in range(tm):
        acc = None
        for kk in range(TOP_K):
            row = stage_ref[buf, pl.ds(pl.multiple_of(pos_ref[0, kk * tm + t], ROW_CHUNKS), ROW_CHUNKS), :]
            term = row * gate_ref[0, kk * tm + t]
            acc = term if acc is None else acc + term
        tok_ref[t * ROW_CHUNKS:(t + 1) * ROW_CHUNKS, :] = acc

    moe = jnp.concatenate([tok_ref[pl.ds(c, tm, stride=ROW_CHUNKS), :] for c in range(ROW_CHUNKS)],
                          axis=-1)
    x2 = x1_ref[...] + gf_ref[...] * moe
    o_ref[...] = (x2 * lax.rsqrt(jnp.mean(x2 * x2, axis=-1, keepdims=True) + NORM_EPS)) * g_ref[...]


def _combine(y, plan, x1, gate_f, g_final):
    bsz, s, d = x1.shape
    tm = COMBINE_ROWS
    nt = bsz * s // tm
    per_b = s // tm
    seg_src, seg_cnt, seg_dst, pos3, gates3 = plan
    per_tile = pl.BlockSpec((None, 1, TOP_K * tm), lambda i, *_: (i, 0, 0), memory_space=pltpu.SMEM)
    return pl.pallas_call(
        _combine_kernel,
        grid_spec=pltpu.PrefetchScalarGridSpec(
            num_scalar_prefetch=3,
            grid=(nt,),
            in_specs=[per_tile, per_tile,
                      pl.BlockSpec(memory_space=pl.ANY),
                      pl.BlockSpec((None, tm, d), lambda i, *_: (i // per_b, i % per_b, 0)),
                      pl.BlockSpec((None, 1, d), lambda i, *_: (i // per_b, 0, 0)),
                      pl.BlockSpec((1, d), lambda i, *_: (0, 0))],
            out_specs=pl.BlockSpec((None, tm, d), lambda i, *_: (i // per_b, i % per_b, 0)),
            scratch_shapes=[pltpu.VMEM((2, TOP_K * tm * ROW_CHUNKS, LANES), F32),
                            pltpu.VMEM((tm * ROW_CHUNKS, LANES), F32),
                            pltpu.SemaphoreType.DMA((2,))]),
        out_shape=jax.ShapeDtypeStruct((bsz, s, d), F32),
        compiler_params=_params(("arbitrary",)),
        name="combine",
    )(seg_src, seg_cnt, seg_dst, pos3, gates3, y, x1, gate_f, g_final.reshape(1, d))


def _combine_plan(idx, rank, gates, pstart):
    tm = COMBINE_ROWS
    t = idx.shape[1]
    nt = t // tm
    experts = jnp.arange(N_EXPERTS, dtype=I32)
    hot = idx.reshape(TOP_K, nt, tm)[..., None] == experts
    tile_cnt = jnp.sum(hot.astype(I32), axis=(0, 2))
    tile_base = jnp.cumsum(tile_cnt, axis=0) - tile_cnt
    stage_at = jnp.cumsum(tile_cnt, axis=1) - tile_cnt
    pos = rank.reshape(TOP_K, nt, tm) + jnp.sum(
        jnp.where(hot, (stage_at - tile_base)[None, :, None, :], 0), axis=-1)
    flat = lambda a: a.reshape(-1).astype(I32)
    return (flat(pstart[None, :] + tile_base), flat(tile_cnt), flat(stage_at),
            (pos * ROW_CHUNKS).transpose(1, 0, 2).reshape(nt, 1, TOP_K * tm).astype(I32),
            gates.reshape(TOP_K, nt, tm).transpose(1, 0, 2).reshape(nt, 1, TOP_K * tm))


def _routing_plan(idx, rank, counts, n_rows):
    rows = EXPERT_ROWS
    experts = jnp.arange(N_EXPERTS, dtype=I32)

    def lookup(table, e):
        return jnp.sum(jnp.where(e[..., None] == experts, table, 0), axis=-1)

    def segment_of(ends, pos):
        return jnp.minimum(jnp.sum((pos[..., None] >= ends).astype(I32), axis=-1), N_EXPERTS - 1)

    padded = ((counts + rows - 1) // rows) * rows
    pend = jnp.cumsum(padded)
    pstart = pend - padded
    dest = lookup(pstart, idx) + rank
    n_blocks = n_rows // rows
    blk_exp = segment_of(pend, jnp.arange(n_blocks, dtype=I32) * rows)
    n_used = (pend[-1] // rows).astype(I32).reshape(1)
    blk_first = jnp.arange(n_blocks, dtype=I32) * rows - lookup(pstart, blk_exp)
    blk_valid = jnp.clip(lookup(counts, blk_exp) - blk_first, 0, rows)
    return dest.astype(I32), blk_exp.astype(I32), blk_valid.astype(I32), n_used, pstart


def kernel(x, c, w_ada, b_ada, g_mix, w_in, conv_w, conv_b, w_rg_a, b_rg_a, w_rg_x, b_rg_x, lam, w_out,
           g_ffn, w_router, b_router, w_gate_up, b_gate_up, w_down, b_down, g_final):
    bsz, s, d = x.shape
    t = bsz * s
    depth = w_ada.shape[0]
    assert depth == 1, "the combine kernel applies the final norm; one layer only"
    for l in range(depth):
        mod = _ada(c, w_ada[l], b_ada[l]).reshape(bsz, 6, 1, d)
        shift_m, scale_m, gate_m, shift_f, scale_f, gate_f = (mod[:, j] for j in range(6))
        q, k, v, xr, gr = _inproj(x, g_mix[l], shift_m, scale_m, w_in[l].astype(BF16))
        attn = _attention(q, k, v)
        rec = _rglru(xr, gr, conv_w[l], conv_b[l], w_rg_a[l], b_rg_a[l], w_rg_x[l], b_rg_x[l], lam[l])
        x1, h2rows, idx, gates, rank, cnt = _outproj(
            attn, rec, w_out[l].astype(BF16), x, gate_m, g_ffn[l], shift_f, scale_f, w_router[l], b_router[l])
        n_rows = t * TOP_K + N_EXPERTS * EXPERT_ROWS
        dest, blk_exp, blk_valid, n_used, pstart = _routing_plan(idx, rank, cnt[:, 0], n_rows)
        slots = _slots(dest.reshape(-1), n_rows)
        y = _experts(h2rows, slots, blk_exp, blk_valid, n_used, w_gate_up[l], b_gate_up[l], w_down[l],
                     b_down[l])
        x = _combine(y, _combine_plan(idx, rank, gates, pstart), x1, gate_f, g_final)
    return x
```
